```python
import jax, jax.numpy as jnp
from jax import lax
import numpy as np

D_MODEL = 1024
BATCH = 32
SEQ = 2048
DEPTH = 1

MEM_LEN = 256
CONV_WIDTH = 31
CONV_DIM = 1024
SGU_DIM = 1024
SGU_GROUPS = 8
SGU_CHUNK = 128
XATTN_HEADS = 4
XATTN_HEAD_DIM = D_MODEL // XATTN_HEADS
FFN_HIDDEN = ((-(-8 * D_MODEL // 3) + 255) // 256) * 256
IN_COLS = 2 * CONV_DIM + 2 * SGU_DIM + 2 * D_MODEL
RMS_EPS = 1e-6
LN_EPS = 1e-5

kernel_name = "hybrid_conv_sgu_gated_block"


def rmsnorm(x, g):
    xf = x.astype(jnp.float32)
    y = xf * lax.rsqrt(jnp.mean(xf * xf, axis=-1, keepdims=True) + RMS_EPS)
    return (y * g.astype(jnp.float32)).astype(x.dtype)


def layernorm(x, g, b):
    xf = x.astype(jnp.float32)
    mu = jnp.mean(xf, axis=-1, keepdims=True)
    var = jnp.mean(jnp.square(xf - mu), axis=-1, keepdims=True)
    y = (xf - mu) * lax.rsqrt(var + LN_EPS) * g.astype(jnp.float32) + b.astype(jnp.float32)
    return y.astype(x.dtype)


def causal_depthwise_conv(x, w, b):
    y = lax.conv_general_dilated(
        x, w[:, None, :], window_strides=(1,), padding=[(CONV_WIDTH - 1, 0)],
        dimension_numbers=("NWC", "WIO", "NWC"), feature_group_count=x.shape[-1])
    return y + b


def chunked_spatial_gating(u, v, w_s, b_s):
    B, S, C = v.shape
    n_chunks = S // SGU_CHUNK
    gd = C // SGU_GROUPS
    mask = jnp.tril(jnp.ones((SGU_CHUNK, SGU_CHUNK), dtype=bool))
    w = jnp.where(mask[None], w_s, jnp.zeros_like(w_s))
    vc = v.reshape(B, n_chunks, SGU_CHUNK, SGU_GROUPS, gd)
    z = jnp.einsum('gts,bnsgc->bntgc', w, vc) + jnp.transpose(b_s)[None, None, :, :, None]
    return u * z.reshape(B, S, C)


def mixer_block(h, w_in, b_gate, conv_w, conv_b, conv_ln_g, conv_ln_b, w_conv_out,
                sgu_ln_g, sgu_ln_b, sgu_w, sgu_b, w_sgu_out, w_mix_out):
    p = jnp.einsum('bsd,de->bse', h, w_in)
    a_val, a_gate, b_u, b_v, g_a, g_b = jnp.split(
        p, np.cumsum([CONV_DIM, CONV_DIM, SGU_DIM, SGU_DIM, D_MODEL]).tolist(), axis=-1)
    a = a_val * jax.nn.sigmoid(a_gate)
    a = causal_depthwise_conv(a, conv_w, conv_b)
    a = jax.nn.silu(layernorm(a, conv_ln_g, conv_ln_b))
    y_a = jnp.einsum('bsc,cd->bsd', a, w_conv_out)
    u = jax.nn.gelu(b_u)
    v = layernorm(jax.nn.gelu(b_v), sgu_ln_g, sgu_ln_b)
    y_b = jnp.einsum('bsc,cd->bsd', chunked_spatial_gating(u, v, sgu_w, sgu_b), w_sgu_out)
    merged = jax.nn.sigmoid(g_a + b_gate[0]) * y_a + jax.nn.sigmoid(g_b + b_gate[1]) * y_b
    return jnp.einsum('bsd,de->bse', merged, w_mix_out)


def memory_cross_attention(h, mem_n, w_q, w_kv, w_xo):
    B, S, _ = h.shape
    M = mem_n.shape[1]
    q = jnp.einsum('bsd,de->bse', h, w_q).reshape(B, S, XATTN_HEADS, XATTN_HEAD_DIM)
    kv = jnp.einsum('bmd,de->bme', mem_n, w_kv)
    k, v = jnp.split(kv, 2, axis=-1)
    k = k.reshape(B, M, XATTN_HEADS, XATTN_HEAD_DIM)
    v = v.reshape(B, M, XATTN_HEADS, XATTN_HEAD_DIM)
    s = jnp.einsum('bshd,bmhd->bhsm', q, k).astype(jnp.float32) * (XATTN_HEAD_DIM ** -0.5)
    pr = jax.nn.softmax(s, axis=-1).astype(v.dtype)
    o = jnp.einsum('bhsm,bmhd->bshd', pr, v).reshape(B, S, D_MODEL)
    return jnp.einsum('bsd,de->bse', o, w_xo)


def swiglu_ffn(h, w_gu, w_down):
    gu = jnp.einsum('bsd,df->bsf', h, w_gu)
    gt, up = jnp.split(gu, 2, axis=-1)
    return jnp.einsum('bsf,fd->bsd', jax.nn.silu(gt) * up, w_down)


def _fwd_setup_inputs(seed: int = 0) -> dict:
    key = jax.random.key(seed)
    ks = jax.random.split(key, 32)
    L, D = DEPTH, D_MODEL
    f32 = jnp.float32

    def nrm(k, shape, scale):
        return jax.random.normal(k, shape, f32) * scale

    def gain(k, shape):
        return 1.0 + 0.02 * jax.random.normal(k, shape, f32)

    return {
        "x": jax.random.normal(ks[0], (BATCH, SEQ, D), f32),
        "mem": jax.random.normal(ks[1], (BATCH, MEM_LEN, D), f32),
        "norm_mix": gain(ks[2], (L, D)),
        "w_in": nrm(ks[3], (L, D, IN_COLS), D ** -0.5),
        "b_gate": nrm(ks[4], (L, 2, D), 0.02),
        "conv_w": nrm(ks[5], (L, CONV_WIDTH, CONV_DIM), CONV_WIDTH ** -0.5),
        "conv_b": nrm(ks[6], (L, CONV_DIM), 0.02),
        "conv_ln_g": gain(ks[7], (L, CONV_DIM)),
        "conv_ln_b": nrm(ks[8], (L, CONV_DIM), 0.02),
        "w_conv_out": nrm(ks[9], (L, CONV_DIM, D), CONV_DIM ** -0.5),
        "sgu_ln_g": gain(ks[10], (L, SGU_DIM)),
        "sgu_ln_b": nrm(ks[11], (L, SGU_DIM), 0.02),
        "sgu_w": nrm(ks[12], (L, SGU_GROUPS, SGU_CHUNK, SGU_CHUNK), SGU_CHUNK ** -0.5),
        "sgu_b": gain(ks[13], (L, SGU_GROUPS, SGU_CHUNK)),
        "w_sgu_out": nrm(ks[14], (L, SGU_DIM, D), SGU_DIM ** -0.5),
        "w_mix_out": nrm(ks[15], (L, D, D), D ** -0.5),
        "norm_xattn": gain(ks[16], (L, D)),
        "norm_mem": gain(ks[17], (L, D)),
        "w_q": nrm(ks[18], (L, D, D), D ** -0.5),
        "w_kv": nrm(ks[19], (L, D, 2 * D), D ** -0.5),
        "w_xo": nrm(ks[20], (L, D, D), D ** -0.5),
        "norm_ffn": gain(ks[21], (L, D)),
        "w_gu": nrm(ks[22], (L, D, 2 * FFN_HIDDEN), D ** -0.5),
        "w_down": nrm(ks[23], (L, FFN_HIDDEN, D), FFN_HIDDEN ** -0.5),
        "norm_final": gain(ks[24], (D,)),
    }


def _fwd_reference(x, mem, norm_mix, w_in, b_gate, conv_w, conv_b, conv_ln_g, conv_ln_b, w_conv_out,
              sgu_ln_g, sgu_ln_b, sgu_w, sgu_b, w_sgu_out, w_mix_out,
              norm_xattn, norm_mem, w_q, w_kv, w_xo,
              norm_ffn, w_gu, w_down, norm_final):
    for l in range(DEPTH):
        h = rmsnorm(x, norm_mix[l])
        x = x + mixer_block(h, w_in[l], b_gate[l], conv_w[l], conv_b[l], conv_ln_g[l], conv_ln_b[l],
                            w_conv_out[l], sgu_ln_g[l], sgu_ln_b[l], sgu_w[l], sgu_b[l],
                            w_sgu_out[l], w_mix_out[l])
        h = rmsnorm(x, norm_xattn[l])
        mem_n = rmsnorm(mem, norm_mem[l])
        x = x + memory_cross_attention(h, mem_n, w_q[l], w_kv[l], w_xo[l])
        h = rmsnorm(x, norm_ffn[l])
        x = x + swiglu_ffn(h, w_gu[l], w_down[l])
    return rmsnorm(x, norm_final)


import jax as _jax
import jax.numpy as _jnp

TWIN_FORMAT = 'train_step'
FWD_PARAMS = ['x', 'mem', 'norm_mix', 'w_in', 'b_gate', 'conv_w', 'conv_b', 'conv_ln_g', 'conv_ln_b', 'w_conv_out', 'sgu_ln_g', 'sgu_ln_b', 'sgu_w', 'sgu_b', 'w_sgu_out', 'w_mix_out', 'norm_xattn', 'norm_mem', 'w_q', 'w_kv', 'w_xo', 'norm_ffn', 'w_gu', 'w_down', 'norm_final']
TWIN_WEIGHTS = ['norm_mix', 'w_in', 'b_gate', 'conv_w', 'conv_b', 'conv_ln_g', 'conv_ln_b', 'w_conv_out', 'sgu_ln_g', 'sgu_ln_b', 'sgu_w', 'sgu_b', 'w_sgu_out', 'w_mix_out', 'norm_xattn', 'norm_mem', 'w_q', 'w_kv', 'w_xo', 'norm_ffn', 'w_gu', 'w_down', 'norm_final']
TWIN_DIFF_INPUT = 'x'
TWIN_INPUTS = ['x', 'mem', 'norm_mix', 'w_in', 'b_gate', 'conv_w', 'conv_b', 'conv_ln_g', 'conv_ln_b', 'w_conv_out', 'sgu_ln_g', 'sgu_ln_b', 'sgu_w', 'sgu_b', 'w_sgu_out', 'w_mix_out', 'norm_xattn', 'norm_mem', 'w_q', 'w_kv', 'w_xo', 'norm_ffn', 'w_gu', 'w_down', 'norm_final', 'loss_target', 'm_norm_mix', 'm_w_in', 'm_b_gate', 'm_conv_w', 'm_conv_b', 'm_conv_ln_g', 'm_conv_ln_b', 'm_w_conv_out', 'm_sgu_ln_g', 'm_sgu_ln_b', 'm_sgu_w', 'm_sgu_b', 'm_w_sgu_out', 'm_w_mix_out', 'm_norm_xattn', 'm_norm_mem', 'm_w_q', 'm_w_kv', 'm_w_xo', 'm_norm_ffn', 'm_w_gu', 'm_w_down', 'm_norm_final', 'v_norm_mix', 'v_w_in', 'v_b_gate', 'v_conv_w', 'v_conv_b', 'v_conv_ln_g', 'v_conv_ln_b', 'v_w_conv_out', 'v_sgu_ln_g', 'v_sgu_ln_b', 'v_sgu_w', 'v_sgu_b', 'v_w_sgu_out', 'v_w_mix_out', 'v_norm_xattn', 'v_norm_mem', 'v_w_q', 'v_w_kv', 'v_w_xo', 'v_norm_ffn', 'v_w_gu', 'v_w_down', 'v_norm_final']
TWIN_OUTPUTS = ['loss', 'grad_x', 'grad_norm_mix', 'grad_w_in', 'grad_b_gate', 'grad_conv_w', 'grad_conv_b', 'grad_conv_ln_g', 'grad_conv_ln_b', 'grad_w_conv_out', 'grad_sgu_ln_g', 'grad_sgu_ln_b', 'grad_sgu_w', 'grad_sgu_b', 'grad_w_sgu_out', 'grad_w_mix_out', 'grad_norm_xattn', 'grad_norm_mem', 'grad_w_q', 'grad_w_kv', 'grad_w_xo', 'grad_norm_ffn', 'grad_w_gu', 'grad_w_down', 'grad_norm_final', 'delta_norm_mix', 'delta_w_in', 'delta_b_gate', 'delta_conv_w', 'delta_conv_b', 'delta_conv_ln_g', 'delta_conv_ln_b', 'delta_w_conv_out', 'delta_sgu_ln_g', 'delta_sgu_ln_b', 'delta_sgu_w', 'delta_sgu_b', 'delta_w_sgu_out', 'delta_w_mix_out', 'delta_norm_xattn', 'delta_norm_mem', 'delta_w_q', 'delta_w_kv', 'delta_w_xo', 'delta_norm_ffn', 'delta_w_gu', 'delta_w_down', 'delta_norm_final', 'new_m_norm_mix', 'new_m_w_in', 'new_m_b_gate', 'new_m_conv_w', 'new_m_conv_b', 'new_m_conv_ln_g', 'new_m_conv_ln_b', 'new_m_w_conv_out', 'new_m_sgu_ln_g', 'new_m_sgu_ln_b', 'new_m_sgu_w', 'new_m_sgu_b', 'new_m_w_sgu_out', 'new_m_w_mix_out', 'new_m_norm_xattn', 'new_m_norm_mem', 'new_m_w_q', 'new_m_w_kv', 'new_m_w_xo', 'new_m_norm_ffn', 'new_m_w_gu', 'new_m_w_down', 'new_m_norm_final', 'new_v_norm_mix', 'new_v_w_in', 'new_v_b_gate', 'new_v_conv_w', 'new_v_conv_b', 'new_v_conv_ln_g', 'new_v_conv_ln_b', 'new_v_w_conv_out', 'new_v_sgu_ln_g', 'new_v_sgu_ln_b', 'new_v_sgu_w', 'new_v_sgu_b', 'new_v_w_sgu_out', 'new_v_w_mix_out', 'new_v_norm_xattn', 'new_v_norm_mem', 'new_v_w_q', 'new_v_w_kv', 'new_v_w_xo', 'new_v_norm_ffn', 'new_v_w_gu', 'new_v_w_down', 'new_v_norm_final']
TWIN_LEAF_KINDS = {'loss': 'loss', 'grad_x': 'grad_x', 'grad_norm_mix': 'grad_w', 'grad_w_in': 'grad_w', 'grad_b_gate': 'grad_w', 'grad_conv_w': 'grad_w', 'grad_conv_b': 'grad_w', 'grad_conv_ln_g': 'grad_w', 'grad_conv_ln_b': 'grad_w', 'grad_w_conv_out': 'grad_w', 'grad_sgu_ln_g': 'grad_w', 'grad_sgu_ln_b': 'grad_w', 'grad_sgu_w': 'grad_w', 'grad_sgu_b': 'grad_w', 'grad_w_sgu_out': 'grad_w', 'grad_w_mix_out': 'grad_w', 'grad_norm_xattn': 'grad_w', 'grad_norm_mem': 'grad_w', 'grad_w_q': 'grad_w', 'grad_w_kv': 'grad_w', 'grad_w_xo': 'grad_w', 'grad_norm_ffn': 'grad_w', 'grad_w_gu': 'grad_w', 'grad_w_down': 'grad_w', 'grad_norm_final': 'grad_w', 'delta_norm_mix': 'delta_w', 'delta_w_in': 'delta_w', 'delta_b_gate': 'delta_w', 'delta_conv_w': 'delta_w', 'delta_conv_b': 'delta_w', 'delta_conv_ln_g': 'delta_w', 'delta_conv_ln_b': 'delta_w', 'delta_w_conv_out': 'delta_w', 'delta_sgu_ln_g': 'delta_w', 'delta_sgu_ln_b': 'delta_w', 'delta_sgu_w': 'delta_w', 'delta_sgu_b': 'delta_w', 'delta_w_sgu_out': 'delta_w', 'delta_w_mix_out': 'delta_w', 'delta_norm_xattn': 'delta_w', 'delta_norm_mem': 'delta_w', 'delta_w_q': 'delta_w', 'delta_w_kv': 'delta_w', 'delta_w_xo': 'delta_w', 'delta_norm_ffn': 'delta_w', 'delta_w_gu': 'delta_w', 'delta_w_down': 'delta_w', 'delta_norm_final': 'delta_w', 'new_m_norm_mix': 'new_m', 'new_m_w_in': 'new_m', 'new_m_b_gate': 'new_m', 'new_m_conv_w': 'new_m', 'new_m_conv_b': 'new_m', 'new_m_conv_ln_g': 'new_m', 'new_m_conv_ln_b': 'new_m', 'new_m_w_conv_out': 'new_m', 'new_m_sgu_ln_g': 'new_m', 'new_m_sgu_ln_b': 'new_m', 'new_m_sgu_w': 'new_m', 'new_m_sgu_b': 'new_m', 'new_m_w_sgu_out': 'new_m', 'new_m_w_mix_out': 'new_m', 'new_m_norm_xattn': 'new_m', 'new_m_norm_mem': 'new_m', 'new_m_w_q': 'new_m', 'new_m_w_kv': 'new_m', 'new_m_w_xo': 'new_m', 'new_m_norm_ffn': 'new_m', 'new_m_w_gu': 'new_m', 'new_m_w_down': 'new_m', 'new_m_norm_final': 'new_m', 'new_v_norm_mix': 'new_v', 'new_v_w_in': 'new_v', 'new_v_b_gate': 'new_v', 'new_v_conv_w': 'new_v', 'new_v_conv_b': 'new_v', 'new_v_conv_ln_g': 'new_v', 'new_v_conv_ln_b': 'new_v', 'new_v_w_conv_out': 'new_v', 'new_v_sgu_ln_g': 'new_v', 'new_v_sgu_ln_b': 'new_v', 'new_v_sgu_w': 'new_v', 'new_v_sgu_b': 'new_v', 'new_v_w_sgu_out': 'new_v', 'new_v_w_mix_out': 'new_v', 'new_v_norm_xattn': 'new_v', 'new_v_norm_mem': 'new_v', 'new_v_w_q': 'new_v', 'new_v_w_kv': 'new_v', 'new_v_w_xo': 'new_v', 'new_v_norm_ffn': 'new_v', 'new_v_w_gu': 'new_v', 'new_v_w_down': 'new_v', 'new_v_norm_final': 'new_v'}


def _forward(args):
    return _fwd_reference(*[args[k] for k in FWD_PARAMS])


def _output_shape():
    out = _jax.eval_shape(lambda: _forward(_fwd_setup_inputs(0)))
    return out.shape, out.dtype

N_MICROBATCH = 1
ADAM_LR = 0.001
ADAM_B1 = 0.9
ADAM_B2 = 0.999
ADAM_EPS = 1e-08
ADAM_WD = 0.01
ADAM_STEP = 10
PER_EXAMPLE_BATCH_AXIS = {'x': 0, 'mem': 0, 'loss_target': 0}
SHARED_INPUTS = []
_WEIGHT_DTYPES = {'norm_mix': _jnp.float32, 'w_in': _jnp.float32, 'b_gate': _jnp.float32, 'conv_w': _jnp.float32, 'conv_b': _jnp.float32, 'conv_ln_g': _jnp.float32, 'conv_ln_b': _jnp.float32, 'w_conv_out': _jnp.float32, 'sgu_ln_g': _jnp.float32, 'sgu_ln_b': _jnp.float32, 'sgu_w': _jnp.float32, 'sgu_b': _jnp.float32, 'w_sgu_out': _jnp.float32, 'w_mix_out': _jnp.float32, 'norm_xattn': _jnp.float32, 'norm_mem': _jnp.float32, 'w_q': _jnp.float32, 'w_kv': _jnp.float32, 'w_xo': _jnp.float32, 'norm_ffn': _jnp.float32, 'w_gu': _jnp.float32, 'w_down': _jnp.float32, 'norm_final': _jnp.float32}
MOMENT_SCALE = {'norm_mix': 1.592276e-01, 'w_in': 6.769307e-02, 'b_gate': 4.154733e-02, 'conv_w': 8.429042e-02, 'conv_b': 1.728083e-01, 'conv_ln_g': 9.728579e-02, 'conv_ln_b': 8.991834e-02, 'w_conv_out': 8.071084e-02, 'sgu_ln_g': 6.136825e-02, 'sgu_ln_b': 6.193188e-02, 'sgu_w': 6.130681e-02, 'sgu_b': 8.935798e-02, 'w_sgu_out': 1.063776e-01, 'w_mix_out': 1.343661e-01, 'norm_xattn': 2.860251e-02, 'norm_mem': 4.310456e-02, 'w_q': 2.655555e-02, 'w_kv': 2.624570e-02, 'w_xo': 2.628036e-02, 'norm_ffn': 1.679091e-01, 'w_gu': 7.147149e-02, 'w_down': 1.163147e-01, 'norm_final': 6.405989e+01}


def _to_microbatches(a, axis):
    t = _jnp.moveaxis(a, axis, 0)
    t = t.reshape((N_MICROBATCH, t.shape[0] // N_MICROBATCH) + t.shape[1:])
    return _jnp.moveaxis(t, 1, axis + 1)


def setup_inputs(seed: int = 0) -> dict:
    inp = _fwd_setup_inputs(seed)
    key = _jax.random.fold_in(_jax.random.key(seed), 7919)
    shape, _ = _output_shape()
    out = dict(inp)
    out["loss_target"] = _jax.random.normal(_jax.random.fold_in(key, 0), shape, _jnp.float32)
    for i, name in enumerate(TWIN_WEIGHTS):
        w = inp[name].astype(_jnp.float32)
        if MOMENT_SCALE is None:
            s = _jnp.sqrt(_jnp.mean(_jnp.square(w)) + 1e-30)
        else:
            s = MOMENT_SCALE[name]
        km, kv = _jax.random.split(_jax.random.fold_in(key, i + 1))
        out[name] = w
        out["m_" + name] = s * _jax.random.normal(km, w.shape, _jnp.float32)
        out["v_" + name] = (s * s) * _jax.random.uniform(kv, w.shape, _jnp.float32, 0.5, 1.5)
    if N_MICROBATCH > 1:
        for name, axis in PER_EXAMPLE_BATCH_AXIS.items():
            out[name] = _to_microbatches(out[name], axis)
    return {'x': out['x'], 'mem': out['mem'], 'norm_mix': out['norm_mix'], 'w_in': out['w_in'], 'b_gate': out['b_gate'], 'conv_w': out['conv_w'], 'conv_b': out['conv_b'], 'conv_ln_g': out['conv_ln_g'], 'conv_ln_b': out['conv_ln_b'], 'w_conv_out': out['w_conv_out'], 'sgu_ln_g': out['sgu_ln_g'], 'sgu_ln_b': out['sgu_ln_b'], 'sgu_w': out['sgu_w'], 'sgu_b': out['sgu_b'], 'w_sgu_out': out['w_sgu_out'], 'w_mix_out': out['w_mix_out'], 'norm_xattn': out['norm_xattn'], 'norm_mem': out['norm_mem'], 'w_q': out['w_q'], 'w_kv': out['w_kv'], 'w_xo': out['w_xo'], 'norm_ffn': out['norm_ffn'], 'w_gu': out['w_gu'], 'w_down': out['w_down'], 'norm_final': out['norm_final'], 'loss_target': out['loss_target'], 'm_norm_mix': out['m_norm_mix'], 'm_w_in': out['m_w_in'], 'm_b_gate': out['m_b_gate'], 'm_conv_w': out['m_conv_w'], 'm_conv_b': out['m_conv_b'], 'm_conv_ln_g': out['m_conv_ln_g'], 'm_conv_ln_b': out['m_conv_ln_b'], 'm_w_conv_out': out['m_w_conv_out'], 'm_sgu_ln_g': out['m_sgu_ln_g'], 'm_sgu_ln_b': out['m_sgu_ln_b'], 'm_sgu_w': out['m_sgu_w'], 'm_sgu_b': out['m_sgu_b'], 'm_w_sgu_out': out['m_w_sgu_out'], 'm_w_mix_out': out['m_w_mix_out'], 'm_norm_xattn': out['m_norm_xattn'], 'm_norm_mem': out['m_norm_mem'], 'm_w_q': out['m_w_q'], 'm_w_kv': out['m_w_kv'], 'm_w_xo': out['m_w_xo'], 'm_norm_ffn': out['m_norm_ffn'], 'm_w_gu': out['m_w_gu'], 'm_w_down': out['m_w_down'], 'm_norm_final': out['m_norm_final'], 'v_norm_mix': out['v_norm_mix'], 'v_w_in': out['v_w_in'], 'v_b_gate': out['v_b_gate'], 'v_conv_w': out['v_conv_w'], 'v_conv_b': out['v_conv_b'], 'v_conv_ln_g': out['v_conv_ln_g'], 'v_conv_ln_b': out['v_conv_ln_b'], 'v_w_conv_out': out['v_w_conv_out'], 'v_sgu_ln_g': out['v_sgu_ln_g'], 'v_sgu_ln_b': out['v_sgu_ln_b'], 'v_sgu_w': out['v_sgu_w'], 'v_sgu_b': out['v_sgu_b'], 'v_w_sgu_out': out['v_w_sgu_out'], 'v_w_mix_out': out['v_w_mix_out'], 'v_norm_xattn': out['v_norm_xattn'], 'v_norm_mem': out['v_norm_mem'], 'v_w_q': out['v_w_q'], 'v_w_kv': out['v_w_kv'], 'v_w_xo': out['v_w_xo'], 'v_norm_ffn': out['v_norm_ffn'], 'v_w_gu': out['v_w_gu'], 'v_w_down': out['v_w_down'], 'v_norm_final': out['v_norm_final']}


def _loss(weights, diff, rest, loss_target):
    with _jax.named_scope("forward"):
        args = {**rest, TWIN_DIFF_INPUT: diff, **{k: w.astype(_WEIGHT_DTYPES[k]) for k, w in weights.items()}}
        y = _forward(args)
    with _jax.named_scope("loss_head"):
        err = _jnp.square(y.astype(_jnp.float32) - loss_target)
        return 0.5 * _jnp.sum(_jnp.mean(err, axis=-1)) if err.ndim else 0.5 * err


def _adamw(w, g, m, v):
    m = ADAM_B1 * m + (1.0 - ADAM_B1) * g
    v = ADAM_B2 * v + (1.0 - ADAM_B2) * _jnp.square(g)
    m_hat = m / (1.0 - ADAM_B1 ** ADAM_STEP)
    v_hat = v / (1.0 - ADAM_B2 ** ADAM_STEP)
    delta = -ADAM_LR * (m_hat / (_jnp.sqrt(v_hat) + ADAM_EPS) + ADAM_WD * w)
    return delta, m, v


def reference(x, mem, norm_mix, w_in, b_gate, conv_w, conv_b, conv_ln_g, conv_ln_b, w_conv_out, sgu_ln_g, sgu_ln_b, sgu_w, sgu_b, w_sgu_out, w_mix_out, norm_xattn, norm_mem, w_q, w_kv, w_xo, norm_ffn, w_gu, w_down, norm_final, loss_target, m_norm_mix, m_w_in, m_b_gate, m_conv_w, m_conv_b, m_conv_ln_g, m_conv_ln_b, m_w_conv_out, m_sgu_ln_g, m_sgu_ln_b, m_sgu_w, m_sgu_b, m_w_sgu_out, m_w_mix_out, m_norm_xattn, m_norm_mem, m_w_q, m_w_kv, m_w_xo, m_norm_ffn, m_w_gu, m_w_down, m_norm_final, v_norm_mix, v_w_in, v_b_gate, v_conv_w, v_conv_b, v_conv_ln_g, v_conv_ln_b, v_w_conv_out, v_sgu_ln_g, v_sgu_ln_b, v_sgu_w, v_sgu_b, v_w_sgu_out, v_w_mix_out, v_norm_xattn, v_norm_mem, v_w_q, v_w_kv, v_w_xo, v_norm_ffn, v_w_gu, v_w_down, v_norm_final):
    given = dict(x=x, mem=mem, norm_mix=norm_mix, w_in=w_in, b_gate=b_gate, conv_w=conv_w, conv_b=conv_b, conv_ln_g=conv_ln_g, conv_ln_b=conv_ln_b, w_conv_out=w_conv_out, sgu_ln_g=sgu_ln_g, sgu_ln_b=sgu_ln_b, sgu_w=sgu_w, sgu_b=sgu_b, w_sgu_out=w_sgu_out, w_mix_out=w_mix_out, norm_xattn=norm_xattn, norm_mem=norm_mem, w_q=w_q, w_kv=w_kv, w_xo=w_xo, norm_ffn=norm_ffn, w_gu=w_gu, w_down=w_down, norm_final=norm_final, loss_target=loss_target, m_norm_mix=m_norm_mix, m_w_in=m_w_in, m_b_gate=m_b_gate, m_conv_w=m_conv_w, m_conv_b=m_conv_b, m_conv_ln_g=m_conv_ln_g, m_conv_ln_b=m_conv_ln_b, m_w_conv_out=m_w_conv_out, m_sgu_ln_g=m_sgu_ln_g, m_sgu_ln_b=m_sgu_ln_b, m_sgu_w=m_sgu_w, m_sgu_b=m_sgu_b, m_w_sgu_out=m_w_sgu_out, m_w_mix_out=m_w_mix_out, m_norm_xattn=m_norm_xattn, m_norm_mem=m_norm_mem, m_w_q=m_w_q, m_w_kv=m_w_kv, m_w_xo=m_w_xo, m_norm_ffn=m_norm_ffn, m_w_gu=m_w_gu, m_w_down=m_w_down, m_norm_final=m_norm_final, v_norm_mix=v_norm_mix, v_w_in=v_w_in, v_b_gate=v_b_gate, v_conv_w=v_conv_w, v_conv_b=v_conv_b, v_conv_ln_g=v_conv_ln_g, v_conv_ln_b=v_conv_ln_b, v_w_conv_out=v_w_conv_out, v_sgu_ln_g=v_sgu_ln_g, v_sgu_ln_b=v_sgu_ln_b, v_sgu_w=v_sgu_w, v_sgu_b=v_sgu_b, v_w_sgu_out=v_w_sgu_out, v_w_mix_out=v_w_mix_out, v_norm_xattn=v_norm_xattn, v_norm_mem=v_norm_mem, v_w_q=v_w_q, v_w_kv=v_w_kv, v_w_xo=v_w_xo, v_norm_ffn=v_norm_ffn, v_w_gu=v_w_gu, v_w_down=v_w_down, v_norm_final=v_norm_final)
    weights = {n: given[n] for n in TWIN_WEIGHTS}
    shared = {n: given[n] for n in SHARED_INPUTS}
    per_example = {n: given[n] for n in ['x', 'mem']}
    grad_fn = _jax.value_and_grad(_loss, argnums=(0, 1))

    def one_microbatch(ex, loss_target):
        ex = dict(ex)
        diff = ex.pop(TWIN_DIFF_INPUT)
        return grad_fn(weights, diff, {**shared, **ex}, loss_target)

    if N_MICROBATCH == 1:
        loss, (grad_w, grad_x) = one_microbatch(per_example, given["loss_target"])
    else:
        def body(carry, xs):
            loss_sum, grad_sum = carry
            l_k, (gw_k, gx_k) = one_microbatch(xs[0], xs[1])
            with _jax.named_scope("update"):
                return (loss_sum + l_k, _jax.tree.map(_jnp.add, grad_sum, gw_k)), gx_k

        init = (_jnp.zeros((), _jnp.float32), _jax.tree.map(_jnp.zeros_like, weights))
        (loss, grad_w), grad_x = _jax.lax.scan(body, init, (per_example, given["loss_target"]))
    with _jax.named_scope("update"):
        delta_w, new_m, new_v = {}, {}, {}
        for n in TWIN_WEIGHTS:
            delta_w[n], new_m[n], new_v[n] = _adamw(weights[n], grad_w[n], given["m_" + n], given["v_" + n])
    return (loss, grad_x, *[grad_w[n] for n in TWIN_WEIGHTS], *[delta_w[n] for n in TWIN_WEIGHTS],
            *[new_m[n] for n in TWIN_WEIGHTS], *[new_v[n] for n in TWIN_WEIGHTS])
```

```python
import functools
import math

import jax
import jax.numpy as jnp
from jax import lax
from jax.experimental import pallas as pl
from jax.experimental.pallas import tpu as pltpu

BF = jnp.bfloat16
F32 = jnp.float32
MESH = pl.DeviceIdType.MESH
ANY = pl.BlockSpec(memory_space=pl.ANY)

RMS_EPS = 1e-6
LN_EPS = 1e-5
HEADS = 4
SGU_GROUPS = 8
LANES = 128
ADAM_LR = 0.001
ADAM_B1 = 0.9
ADAM_B2 = 0.999
ADAM_EPS = 1e-08
ADAM_WD = 0.01
ADAM_STEP = 10
N_CHIPS = 4
N_DEV = 8
MIB = 1024 * 1024


def _sds(shape, dtype):
    return jax.ShapeDtypeStruct(tuple(shape), dtype)


def _cp(vmem_mib):
    return pltpu.CompilerParams(vmem_limit_bytes=vmem_mib * MIB)


def _const(shape):
    nd = len(shape)
    return pl.BlockSpec(tuple(shape), lambda *_: (0,) * nd, pipeline_mode=pl.Buffered(1))


def _dot(a, b):
    return jnp.dot(a.astype(BF), b.astype(BF), preferred_element_type=F32)


def _dot_nt(a, b):
    return lax.dot_general(a.astype(BF), b.astype(BF), (((1,), (1,)), ((), ())), preferred_element_type=F32)


def _dot_tn(a, b):
    return lax.dot_general(a.astype(BF), b.astype(BF), (((0,), (0,)), ((), ())), preferred_element_type=F32)


def _sig(x):
    return 1.0 / (1.0 + jnp.exp(-x))


def _dsilu(x, s):
    return s * (1.0 + x * (1.0 - s))


_GELU_C = math.sqrt(2.0 / math.pi)


def _gelu(x):
    t = jnp.tanh(_GELU_C * (x + 0.044715 * (x * x * x)))
    return 0.5 * x * (1.0 + t), t


def _dgelu(x, t):
    return 0.5 * (1.0 + t) + 0.5 * x * (1.0 - t * t) * (_GELU_C * (1.0 + 3.0 * 0.044715 * (x * x)))


def _rms(x, g):
    r = lax.rsqrt(jnp.mean(x * x, axis=-1, keepdims=True) + RMS_EPS)
    return x * r * g, r


def _rms_bwd(x, g, r, dh):
    xr = x * r
    dxh = dh * g
    dx = r * (dxh - xr * jnp.mean(dxh * xr, axis=-1, keepdims=True))
    return dx, jnp.sum(dh * xr, axis=0, keepdims=True)


def _ln(x, g, b):
    mu = jnp.mean(x, axis=-1, keepdims=True)
    xc = x - mu
    rstd = lax.rsqrt(jnp.mean(xc * xc, axis=-1, keepdims=True) + LN_EPS)
    xh = xc * rstd
    return xh * g + b, xh, rstd


def _ln_bwd(xh, rstd, g, dy):
    dxh = dy * g
    dx = rstd * (dxh - jnp.mean(dxh, axis=-1, keepdims=True) - xh * jnp.mean(dxh * xh, axis=-1, keepdims=True))
    return dx, jnp.sum(dy * xh, axis=0, keepdims=True), jnp.sum(dy, axis=0, keepdims=True)


def _acc(ref, val, first):
    @pl.when(first)
    def _():
        ref[...] = val

    @pl.when(jnp.logical_not(first))
    def _():
        ref[...] += val


def _in_proj(x, g1, w_in):
    T, D = x.shape
    N = w_in.shape[1]
    tm, tn = 512, 1024

    def body(x_ref, g_ref, w_ref, p_ref, h_ref):
        @pl.when(pl.program_id(1) == 0)
        def _():
            h, _ = _rms(x_ref[...], g_ref[...])
            h_ref[...] = h.astype(BF)

        p_ref[...] = jnp.dot(h_ref[...], w_ref[...], preferred_element_type=F32)

    return pl.pallas_call(
        body, grid=(T // tm, N // tn),
        in_specs=[pl.BlockSpec((tm, D), lambda i, j: (i, 0)), pl.BlockSpec((1, D), lambda i, j: (0, 0)),
                  pl.BlockSpec((D, tn), lambda i, j: (0, j))],
        out_specs=[pl.BlockSpec((tm, tn), lambda i, j: (i, j)), pl.BlockSpec((tm, D), lambda i, j: (i, 0))],
        out_shape=[_sds((T, N), F32), _sds((T, D), BF)],
        name="in_proj", compiler_params=_cp(32))(x, g1, w_in)


CONV_PAD = 32
CONV_ROWS = 256


def _conv_fwd(p, conv_w, conv_b, B, S):
    K, D = conv_w.shape
    nc = D // LANES

    def body(av_ref, ag_ref, w_ref, b_ref, c_ref, apad):
        apad[pl.ds(0, CONV_PAD), :] = jnp.zeros((CONV_PAD, LANES), F32)
        apad[pl.ds(CONV_PAD, S), :] = av_ref[...] * _sig(ag_ref[...])
        for r0 in range(0, S, CONV_ROWS):
            acc = jnp.zeros((CONV_ROWS, LANES), F32) + b_ref[...]
            for k in range(K):
                acc = acc + w_ref[pl.ds(k, 1), :] * apad[pl.ds(r0 + k + CONV_PAD - (K - 1), CONV_ROWS), :]
            c_ref[pl.ds(r0, CONV_ROWS), :] = acc

    return pl.pallas_call(
        body, grid=(B, nc),
        in_specs=[pl.BlockSpec((S, LANES), lambda b, j: (b, j)), pl.BlockSpec((S, LANES), lambda b, j: (b, nc + j)),
                  pl.BlockSpec((K, LANES), lambda b, j: (0, j)), pl.BlockSpec((1, LANES), lambda b, j: (0, j))],
        out_specs=pl.BlockSpec((S, LANES), lambda b, j: (b, j)),
        out_shape=_sds((B * S, D), F32),
        scratch_shapes=[pltpu.VMEM((S + CONV_PAD, LANES), F32)],
        name="conv_fwd", compiler_params=_cp(32))(p, p, conv_w, conv_b)


def _tril_mask():
    t = lax.broadcasted_iota(jnp.int32, (LANES, LANES), 0)
    s = lax.broadcasted_iota(jnp.int32, (LANES, LANES), 1)
    return t >= s


def _branch_a(c, g, b):
    ln_a, xh, rstd = _ln(c, g, b)
    s = _sig(ln_a)
    return ln_a * s, ln_a, s, xh, rstd


def _branch_b(bu, bv, g, b, wm_ref, bz_ref, z_scr, v_scr):
    tm, D = bu.shape
    u, tu = _gelu(bu)
    gv, tv = _gelu(bv)
    v, vh, rstd = _ln(gv, g, b)
    v_scr[...] = v.astype(BF)
    mask = _tril_mask()
    for gi in range(SGU_GROUPS):
        wm = jnp.where(mask, wm_ref[gi], 0.0).astype(BF)
        cols = pl.ds(gi * LANES, LANES)
        for n in range(tm // LANES):
            rows = pl.ds(n * LANES, LANES)
            z_scr[rows, cols] = jnp.dot(wm, v_scr[rows, cols], preferred_element_type=F32) + bz_ref[:, cols]
    z = z_scr[...]
    return u * z, u, tu, z, tv, vh, rstd


TM3 = 256


def _branch_fwd(c, p, prm, tm=TM3):
    T, D = c.shape

    def body(c_ref, bu_ref, bv_ref, ga_ref, gb_ref, wco_ref, wso_ref, lag_ref, lab_ref, lsg_ref, lsb_ref, wm_ref,
             bz_ref, bg_ref, mg_ref, sa_ref, sg_ref, z_scr, v_scr):
        s_a = _branch_a(c_ref[...], lag_ref[...], lab_ref[...])[0]
        sa_ref[...] = s_a.astype(BF)
        y_a = jnp.dot(sa_ref[...], wco_ref[...], preferred_element_type=F32)
        sg = _branch_b(bu_ref[...], bv_ref[...], lsg_ref[...], lsb_ref[...], wm_ref, bz_ref, z_scr, v_scr)[0]
        sg_ref[...] = sg.astype(BF)
        y_b = jnp.dot(sg_ref[...], wso_ref[...], preferred_element_type=F32)
        ga = _sig(ga_ref[...] + bg_ref[pl.ds(0, 1), :])
        gb = _sig(gb_ref[...] + bg_ref[pl.ds(1, 1), :])
        mg_ref[...] = (ga * y_a + gb * y_b).astype(BF)

    tile = lambda j: pl.BlockSpec((tm, D), lambda i: (i, j))
    return pl.pallas_call(
        body, grid=(T // tm,),
        in_specs=[tile(0), tile(2), tile(3), tile(4), tile(5), _const((D, D)), _const((D, D)),
                  _const((1, D)), _const((1, D)), _const((1, D)), _const((1, D)),
                  _const((SGU_GROUPS, LANES, LANES)), _const((LANES, D)), _const((2, D))],
        out_specs=[tile(0), tile(0), tile(0)],
        out_shape=[_sds((T, D), BF)] * 3,
        scratch_shapes=[pltpu.VMEM((tm, D), F32), pltpu.VMEM((tm, D), BF)],
        name="branch_fwd", compiler_params=_cp(48))(
            c, p, p, p, p, prm["w_co"], prm["w_so"], prm["la_g"], prm["la_b"], prm["ls_g"], prm["ls_b"],
            prm["sgu_w"], prm["bz"], prm["b_gate"])


def _kv_fwd(mem, gm, w_kv, B, M):
    D = mem.shape[1]
    N = w_kv.shape[1]

    def body(m_ref, g_ref, w_ref, mn_ref, kv_ref):
        h, _ = _rms(m_ref[...], g_ref[...])
        mn_ref[...] = h.astype(BF)
        kv_ref[...] = jnp.dot(mn_ref[...], w_ref[...], preferred_element_type=F32).astype(BF)

    return pl.pallas_call(
        body, grid=(B,),
        in_specs=[pl.BlockSpec((M, D), lambda b: (b, 0)), _const((1, D)), _const((D, N))],
        out_specs=[pl.BlockSpec((M, D), lambda b: (b, 0)), pl.BlockSpec((M, N), lambda b: (b, 0))],
        out_shape=[_sds((B * M, D), BF), _sds((B * M, N), BF)],
        name="kv_fwd", compiler_params=_cp(32))(mem, gm, w_kv)


def _softmax_rows(s):
    e = jnp.exp(s - jnp.max(s, axis=-1, keepdims=True))
    return e / jnp.sum(e, axis=-1, keepdims=True)


TM4 = 256


def _attn_fwd(x, merged, kv, prm, S, M, tm=TM4):
    T, D = x.shape
    hd = D // HEADS
    scale = hd ** -0.5
    tpb = S // tm

    def body(x_ref, mg_ref, kv_ref, wmo_ref, wq_ref, wxo_ref, g_ref, x1_ref, x2_ref, h2_ref, o_ref):
        x1 = x_ref[...] + jnp.dot(mg_ref[...], wmo_ref[...], preferred_element_type=F32)
        x1_ref[...] = x1
        h2, _ = _rms(x1, g_ref[...])
        h2_ref[...] = h2.astype(BF)
        qb = jnp.dot(h2_ref[...], wq_ref[...], preferred_element_type=F32).astype(BF)
        for h in range(HEADS):
            cs = pl.ds(h * hd, hd)
            s = _dot_nt(qb[:, h * hd:(h + 1) * hd], kv_ref[:, cs]) * scale
            pr = _softmax_rows(s)
            o_ref[:, cs] = _dot(pr, kv_ref[:, pl.ds(D + h * hd, hd)]).astype(BF)
        x2_ref[...] = x1 + jnp.dot(o_ref[...], wxo_ref[...], preferred_element_type=F32)

    tile = pl.BlockSpec((tm, D), lambda i: (i, 0))
    return pl.pallas_call(
        body, grid=(T // tm,),
        in_specs=[tile, tile, pl.BlockSpec((M, 2 * D), lambda i: (i // tpb, 0)),
                  _const((D, D)), _const((D, D)), _const((D, D)), _const((1, D))],
        out_specs=[tile, tile, tile, tile],
        out_shape=[_sds((T, D), F32), _sds((T, D), F32), _sds((T, D), BF), _sds((T, D), BF)],
        name="attn_fwd", compiler_params=_cp(40))(x, merged, kv, prm["w_mo"], prm["w_q"], prm["w_xo"], prm["g2"])


TM5 = 256


def _ffn_loss(x2, tgt, prm, tm=TM5):
    T, D = x2.shape
    F = prm["w_down"].shape[0]
    FC = F // 2

    def body(x2_ref, t_ref, wgu_ref, wd_ref, g3_ref, gf_ref, dx2_ref, dx3_ref, dgu_ref, h3_ref, f_ref, ls_ref,
             dg3_ref, dgf_ref, gu_scr):
        first = pl.program_id(0) == 0
        x2 = x2_ref[...]
        h3, r3 = _rms(x2, g3_ref[...])
        h3_ref[...] = h3.astype(BF)
        x3 = x2
        for ch in range(2):
            gc, uc = pl.ds(ch * FC, FC), pl.ds(F + ch * FC, FC)
            gt = jnp.dot(h3_ref[...], wgu_ref[:, gc], preferred_element_type=F32)
            up = jnp.dot(h3_ref[...], wgu_ref[:, uc], preferred_element_type=F32)
            gu_scr[:, gc] = gt
            gu_scr[:, uc] = up
            f_ref[:, gc] = (gt * _sig(gt) * up).astype(BF)
            x3 = x3 + jnp.dot(f_ref[:, gc], wd_ref[gc, :], preferred_element_type=F32)
        y, rf = _rms(x3, gf_ref[...])
        e = y - t_ref[...]
        _acc(ls_ref, jnp.sum(e * e, axis=0, keepdims=True), first)
        dx3, dgf = _rms_bwd(x3, gf_ref[...], rf, e * (1.0 / D))
        _acc(dgf_ref, dgf, first)
        dx3_ref[...] = dx3.astype(BF)
        dh3 = jnp.zeros((tm, D), F32)
        for ch in range(2):
            gc, uc = pl.ds(ch * FC, FC), pl.ds(F + ch * FC, FC)
            df = lax.dot_general(dx3_ref[...], wd_ref[gc, :], (((1,), (1,)), ((), ())), preferred_element_type=F32)
            gt, up = gu_scr[:, gc], gu_scr[:, uc]
            s = _sig(gt)
            dgu_ref[:, gc] = (df * up * _dsilu(gt, s)).astype(BF)
            dgu_ref[:, uc] = (df * gt * s).astype(BF)
            dh3 = dh3 + lax.dot_general(dgu_ref[:, gc], wgu_ref[:, gc], (((1,), (1,)), ((), ())), preferred_element_type=F32)
            dh3 = dh3 + lax.dot_general(dgu_ref[:, uc], wgu_ref[:, uc], (((1,), (1,)), ((), ())), preferred_element_type=F32)
        dxa, dg3 = _rms_bwd(x2, g3_ref[...], r3, dh3)
        _acc(dg3_ref, dg3, first)
        dx2_ref[...] = dx3 + dxa

    tile = lambda n: pl.BlockSpec((tm, n), lambda i: (i, 0))
    vec = pl.BlockSpec((1, D), lambda i: (0, 0))
    return pl.pallas_call(
        body, grid=(T // tm,),
        in_specs=[tile(D), tile(D), _const((D, 2 * F)), _const((F, D)), _const((1, D)), _const((1, D))],
        out_specs=[tile(D), tile(D), tile(2 * F), tile(D), tile(F), vec, vec, vec],
        out_shape=[_sds((T, D), F32), _sds((T, D), BF), _sds((T, 2 * F), BF), _sds((T, D), BF), _sds((T, F), BF),
                   _sds((1, D), F32), _sds((1, D), F32), _sds((1, D), F32)],
        scratch_shapes=[pltpu.VMEM((tm, 2 * F), F32)],
        name="ffn_loss", compiler_params=_cp(56))(x2, tgt, prm["w_gu"], prm["w_down"], prm["g3"], prm["gf"])


def _attn_bwd(x1, kv, dx2, prm, S, M, tm=TM4):
    T, D = x1.shape
    hd = D // HEADS
    scale = hd ** -0.5
    tpb = S // tm

    def body(x1_ref, kv_ref, dx2_ref, wmo_ref, wq_ref, wxo_ref, g_ref, dx1_ref, dmg_ref, dq_ref, dkv_ref, dg_ref,
             h2_scr, do_scr):
        i = pl.program_id(0)
        x1 = x1_ref[...]
        dx2 = dx2_ref[...]
        h2, r2 = _rms(x1, g_ref[...])
        h2_scr[...] = h2.astype(BF)
        qb = jnp.dot(h2_scr[...], wq_ref[...], preferred_element_type=F32).astype(BF)
        do_scr[...] = _dot_nt(dx2, wxo_ref[...]).astype(BF)
        for h in range(HEADS):
            cs, vs = pl.ds(h * hd, hd), pl.ds(D + h * hd, hd)
            qh = qb[:, h * hd:(h + 1) * hd]
            pr = _softmax_rows(_dot_nt(qh, kv_ref[:, cs]) * scale)
            dpr = _dot_nt(do_scr[:, cs], kv_ref[:, vs])
            dv = _dot_tn(pr, do_scr[:, cs])
            ds = (pr * (dpr - jnp.sum(dpr * pr, axis=-1, keepdims=True)) * scale).astype(BF)
            dq_ref[:, cs] = jnp.dot(ds, kv_ref[:, cs], preferred_element_type=F32).astype(BF)
            dk = _dot_tn(ds, qh)

            @pl.when(i % tpb == 0)
            def _():
                dkv_ref[:, cs] = dk
                dkv_ref[:, vs] = dv

            @pl.when(i % tpb != 0)
            def _():
                dkv_ref[:, cs] += dk
                dkv_ref[:, vs] += dv

        dh2 = _dot_nt(dq_ref[...], wq_ref[...])
        dxa, dg = _rms_bwd(x1, g_ref[...], r2, dh2)
        _acc(dg_ref, dg, i == 0)
        dx1 = dx2 + dxa
        dx1_ref[...] = dx1
        dmg_ref[...] = _dot_nt(dx1, wmo_ref[...])

    tile = pl.BlockSpec((tm, D), lambda i: (i, 0))
    kvb = pl.BlockSpec((M, 2 * D), lambda i: (i // tpb, 0))
    B = T // S
    return pl.pallas_call(
        body, grid=(T // tm,),
        in_specs=[tile, kvb, tile, _const((D, D)), _const((D, D)), _const((D, D)), _const((1, D))],
        out_specs=[tile, tile, tile, kvb, pl.BlockSpec((1, D), lambda i: (0, 0))],
        out_shape=[_sds((T, D), F32), _sds((T, D), F32), _sds((T, D), BF), _sds((B * M, 2 * D), F32), _sds((1, D), F32)],
        scratch_shapes=[pltpu.VMEM((tm, D), BF), pltpu.VMEM((tm, D), BF)],
        name="attn_bwd", compiler_params=_cp(48))(x1, kv, dx2, prm["w_mo"], prm["w_q"], prm["w_xo"], prm["g2"])


def _kv_bwd(mem, gm, w_kv, dkv, B, M):
    D = mem.shape[1]
    N = w_kv.shape[1]

    def body(m_ref, g_ref, w_ref, dkv_ref, dg_ref):
        mem_t = m_ref[...]
        _, r = _rms(mem_t, g_ref[...])
        dmn = _dot_nt(dkv_ref[...], w_ref[...])
        _acc(dg_ref, jnp.sum(dmn * (mem_t * r), axis=0, keepdims=True), pl.program_id(0) == 0)

    return pl.pallas_call(
        body, grid=(B,),
        in_specs=[pl.BlockSpec((M, D), lambda b: (b, 0)), _const((1, D)), _const((D, N)),
                  pl.BlockSpec((M, N), lambda b: (b, 0))],
        out_specs=pl.BlockSpec((1, D), lambda b: (0, 0)),
        out_shape=_sds((1, D), F32),
        name="kv_bwd", compiler_params=_cp(32))(mem, gm, w_kv, dkv)


def _branch_bwd(c, p, dmerged, prm, tm=TM3):
    T, D = c.shape

    def body(c_ref, bu_ref, bv_ref, ga_ref, gb_ref, dm_ref, wco_ref, wso_ref, lag_ref, lab_ref, lsg_ref, lsb_ref,
             wm_ref, bz_ref, bg_ref,
             dc_ref, dpb_ref, dya_ref, dyb_ref, dwm_ref, dbz_ref, dlag_ref, dlab_ref, dlsg_ref, dlsb_ref, dbg_ref,
             z_scr, v_scr, sa_scr, sg_scr, dv_scr):
        first = pl.program_id(0) == 0
        s_a, ln_a, sig_a, xh_a, rstd_a = _branch_a(c_ref[...], lag_ref[...], lab_ref[...])
        sa_scr[...] = s_a.astype(BF)
        y_a = jnp.dot(sa_scr[...], wco_ref[...], preferred_element_type=F32)
        bu, bv = bu_ref[...], bv_ref[...]
        sg, u, tu, z, tv, vh, rstd_v = _branch_b(bu, bv, lsg_ref[...], lsb_ref[...], wm_ref, bz_ref, z_scr, v_scr)
        sg_scr[...] = sg.astype(BF)
        y_b = jnp.dot(sg_scr[...], wso_ref[...], preferred_element_type=F32)
        ga = _sig(ga_ref[...] + bg_ref[pl.ds(0, 1), :])
        gb = _sig(gb_ref[...] + bg_ref[pl.ds(1, 1), :])
        dm = dm_ref[...]
        dga = dm * y_a * ga * (1.0 - ga)
        dgb = dm * y_b * gb * (1.0 - gb)
        dpb_ref[:, pl.ds(2 * D, D)] = dga.astype(BF)
        dpb_ref[:, pl.ds(3 * D, D)] = dgb.astype(BF)
        _acc(dbg_ref.at[pl.ds(0, 1), :], jnp.sum(dga, axis=0, keepdims=True), first)
        _acc(dbg_ref.at[pl.ds(1, 1), :], jnp.sum(dgb, axis=0, keepdims=True), first)
        dya_ref[...] = (dm * ga).astype(BF)
        dyb_ref[...] = (dm * gb).astype(BF)
        dln = _dot_nt(dya_ref[...], wco_ref[...]) * _dsilu(ln_a, sig_a)
        dc, dlag, dlab = _ln_bwd(xh_a, rstd_a, lag_ref[...], dln)
        dc_ref[...] = dc
        _acc(dlag_ref, dlag, first)
        _acc(dlab_ref, dlab, first)
        dsg = _dot_nt(dyb_ref[...], wso_ref[...])
        dpb_ref[:, pl.ds(0, D)] = (dsg * z * _dgelu(bu, tu)).astype(BF)
        dz = dsg * u
        z_scr[...] = dz
        mask = _tril_mask()

        @pl.when(first)
        def _():
            dwm_ref[...] = jnp.zeros_like(dwm_ref)
            dbz_ref[...] = jnp.zeros_like(dbz_ref)

        for gi in range(SGU_GROUPS):
            wm = jnp.where(mask, wm_ref[gi], 0.0).astype(BF)
            cols = pl.ds(gi * LANES, LANES)
            for n in range(tm // LANES):
                rows = pl.ds(n * LANES, LANES)
                dzb = z_scr[rows, cols].astype(BF)
                dv_scr[rows, cols] = lax.dot_general(wm, dzb, (((0,), (0,)), ((), ())), preferred_element_type=F32)
                dw = lax.dot_general(dzb, v_scr[rows, cols], (((1,), (1,)), ((), ())), preferred_element_type=F32)
                dwm_ref[gi] += jnp.where(mask, dw, 0.0)
                dbz_ref[:, cols] += z_scr[rows, cols]
        dgv, dlsg, dlsb = _ln_bwd(vh, rstd_v, lsg_ref[...], dv_scr[...])
        _acc(dlsg_ref, dlsg, first)
        _acc(dlsb_ref, dlsb, first)
        dpb_ref[:, pl.ds(D, D)] = (dgv * _dgelu(bv, tv)).astype(BF)

    tile = lambda j: pl.BlockSpec((tm, D), lambda i: (i, j))
    vec = pl.BlockSpec((1, D), lambda i: (0, 0))
    return pl.pallas_call(
        body, grid=(T // tm,),
        in_specs=[tile(0), tile(2), tile(3), tile(4), tile(5), tile(0), _const((D, D)), _const((D, D)),
                  _const((1, D)), _const((1, D)), _const((1, D)), _const((1, D)),
                  _const((SGU_GROUPS, LANES, LANES)), _const((LANES, D)), _const((2, D))],
        out_specs=[tile(0), pl.BlockSpec((tm, 4 * D), lambda i: (i, 0)), tile(0), tile(0),
                   pl.BlockSpec((SGU_GROUPS, LANES, LANES), lambda i: (0, 0, 0)),
                   pl.BlockSpec((LANES, D), lambda i: (0, 0)), vec, vec, vec, vec,
                   pl.BlockSpec((2, D), lambda i: (0, 0))],
        out_shape=[_sds((T, D), F32), _sds((T, 4 * D), BF), _sds((T, D), BF), _sds((T, D), BF),
                   _sds((SGU_GROUPS, LANES, LANES), F32), _sds((LANES, D), F32),
                   _sds((1, D), F32), _sds((1, D), F32), _sds((1, D), F32), _sds((1, D), F32), _sds((2, D), F32)],
        scratch_shapes=[pltpu.VMEM((tm, D), F32), pltpu.VMEM((tm, D), BF), pltpu.VMEM((tm, D), BF),
                        pltpu.VMEM((tm, D), BF), pltpu.VMEM((tm, D), F32)],
        name="branch_bwd", compiler_params=_cp(56))(
            c, p, p, p, p, dmerged, prm["w_co"], prm["w_so"], prm["la_g"], prm["la_b"], prm["ls_g"], prm["ls_b"],
            prm["sgu_w"], prm["bz"], prm["b_gate"])


def _conv_bwd(p, dc, conv_w, B, S):
    K, D = conv_w.shape
    nc = D // LANES

    def body(av_ref, ag_ref, dc_ref, w_ref, dav_ref, dag_ref, dw_ref, db_ref, apad, dpad):
        b = pl.program_id(1)
        av = av_ref[...]
        sg = _sig(ag_ref[...])
        apad[pl.ds(0, CONV_PAD), :] = jnp.zeros((CONV_PAD, LANES), F32)
        apad[pl.ds(CONV_PAD, S), :] = av * sg
        dpad[pl.ds(S, CONV_PAD), :] = jnp.zeros((CONV_PAD, LANES), F32)
        dpad[pl.ds(0, S), :] = dc_ref[...]

        @pl.when(b == 0)
        def _():
            dw_ref[...] = jnp.zeros_like(dw_ref)
            db_ref[...] = jnp.zeros_like(db_ref)

        db_ref[...] += jnp.sum(dc_ref[...], axis=0, keepdims=True)
        for k in range(K):
            tot = jnp.zeros((1, LANES), F32)
            for r0 in range(0, S, CONV_ROWS):
                tot = tot + jnp.sum(dpad[pl.ds(r0, CONV_ROWS), :] * apad[pl.ds(r0 + k + CONV_PAD - (K - 1), CONV_ROWS), :],
                                    axis=0, keepdims=True)
            dw_ref[pl.ds(k, 1), :] += tot
        for r0 in range(0, S, CONV_ROWS):
            da = jnp.zeros((CONV_ROWS, LANES), F32)
            for k in range(K):
                da = da + w_ref[pl.ds(k, 1), :] * dpad[pl.ds(r0 + (K - 1) - k, CONV_ROWS), :]
            rows = pl.ds(r0, CONV_ROWS)
            s = sg[r0:r0 + CONV_ROWS, :]
            a_v = av[r0:r0 + CONV_ROWS, :]
            dav_ref[rows, :] = (da * s).astype(BF)
            dag_ref[rows, :] = (da * a_v * s * (1.0 - s)).astype(BF)

    blk = lambda off: pl.BlockSpec((S, LANES), lambda j, b: (b, off + j))
    return pl.pallas_call(
        body, grid=(nc, B),
        in_specs=[blk(0), blk(nc), blk(0), pl.BlockSpec((K, LANES), lambda j, b: (0, j))],
        out_specs=[blk(0), blk(0), pl.BlockSpec((K, LANES), lambda j, b: (0, j)), pl.BlockSpec((1, LANES), lambda j, b: (0, j))],
        out_shape=[_sds((B * S, D), BF), _sds((B * S, D), BF), _sds((K, D), F32), _sds((1, D), F32)],
        scratch_shapes=[pltpu.VMEM((S + CONV_PAD, LANES), F32), pltpu.VMEM((S + CONV_PAD, LANES), F32)],
        name="conv_bwd", compiler_params=_cp(32))(p, p, dc, conv_w)


TM1 = 256


def _in_proj_bwd(x, dx1, dp, g1, w_in, tm=TM1):
    T, D = x.shape
    N = w_in.shape[1]

    def body(x_ref, dx1_ref, dp_ref, g_ref, w_ref, dx_ref, dg_ref):
        x_t = x_ref[...]
        _, r = _rms(x_t, g_ref[...])
        dh = lax.dot_general(dp_ref[...], w_ref[...], (((1,), (1,)), ((), ())), preferred_element_type=F32)
        dxa, dg = _rms_bwd(x_t, g_ref[...], r, dh)
        dx_ref[...] = dx1_ref[...] + dxa
        _acc(dg_ref, dg, pl.program_id(0) == 0)

    tile = pl.BlockSpec((tm, D), lambda i: (i, 0))
    return pl.pallas_call(
        body, grid=(T // tm,),
        in_specs=[tile, tile, pl.BlockSpec((tm, N), lambda i: (i, 0)), _const((1, D)), _const((D, N))],
        out_specs=[tile, pl.BlockSpec((1, D), lambda i: (0, 0))],
        out_shape=[_sds((T, D), F32), _sds((1, D), F32)],
        name="in_proj_bwd", compiler_params=_cp(48))(x, dx1, dp, g1, w_in)


def _pick(n, cands):
    for c in cands:
        if n % c == 0:
            return c
    raise ValueError(f"no tile of {cands} divides {n}")


def _mm_tn(x, dy, name):
    T, K = x.shape
    N = dy.shape[1]
    tm = _pick(T, (1024, 512, 256))
    tk = _pick(K, (1024, 1408, 512))
    tn = _pick(N, (1024, 1408, 512))
    nt = T // tm

    def body(x_ref, dy_ref, o_ref, ob_ref, acc):
        t = pl.program_id(2)

        @pl.when(t == 0)
        def _():
            acc[...] = jnp.zeros_like(acc)

        acc[...] += _dot_tn(x_ref[...], dy_ref[...])

        @pl.when(t == nt - 1)
        def _():
            o_ref[...] = acc[...]
            ob_ref[...] = acc[...].astype(BF)

    return pl.pallas_call(
        body, grid=(K // tk, N // tn, nt),
        in_specs=[pl.BlockSpec((tm, tk), lambda i, j, t: (t, i)), pl.BlockSpec((tm, tn), lambda i, j, t: (t, j))],
        out_specs=[pl.BlockSpec((tk, tn), lambda i, j, t: (i, j)), pl.BlockSpec((tk, tn), lambda i, j, t: (i, j))],
        out_shape=[_sds((K, N), F32), _sds((K, N), BF)],
        scratch_shapes=[pltpu.VMEM((tk, tn), F32)],
        name=name, compiler_params=_cp(48))(x, dy)


def _place():
    x, y, c = lax.axis_index("x"), lax.axis_index("y"), lax.axis_index("c")
    chips = [(1 - x, y), (x, 1 - y), (1 - x, 1 - y)]
    return x, y, c, chips


def _shard_of(ref, kind, k, n):
    if kind == "row":
        return ref.at[pl.ds(k * n, n), :]
    return ref.at[:, pl.ds(k * n, n)]


def _all_gather_weights(shards, kinds):
    n = len(shards)
    sizes = [s.shape[0] if kd == "row" else s.shape[1] for s, kd in zip(shards, kinds)]
    full = [_sds((s.shape[0] * N_CHIPS, s.shape[1]) if kd == "row" else (s.shape[0], s.shape[1] * N_CHIPS), s.dtype)
            for s, kd in zip(shards, kinds)]

    def body(*refs):
        ins, outs = refs[:n], refs[n:2 * n]
        send, recv, loc = refs[2 * n:]
        x, y, c, chips = _place()
        k = 2 * x + y
        for t in range(n):
            pltpu.make_async_copy(ins[t], _shard_of(outs[t], kinds[t], k, sizes[t]), loc.at[t]).start()
            for j, (px, py) in enumerate(chips):
                pltpu.make_async_remote_copy(
                    src_ref=ins[t], dst_ref=_shard_of(outs[t], kinds[t], k, sizes[t]),
                    send_sem=send.at[3 * t + j], recv_sem=recv.at[3 * t + j],
                    device_id=(px, py, c), device_id_type=MESH).start()
        for t in range(n):
            for j, (px, py) in enumerate(chips):
                cp = pltpu.make_async_remote_copy(
                    src_ref=ins[t], dst_ref=_shard_of(outs[t], kinds[t], 2 * px + py, sizes[t]),
                    send_sem=send.at[3 * t + j], recv_sem=recv.at[3 * t + j],
                    device_id=(px, py, c), device_id_type=MESH)
                cp.wait_recv()
                cp.wait_send()
            pltpu.make_async_copy(ins[t], _shard_of(outs[t], kinds[t], k, sizes[t]), loc.at[t]).wait()

    return pl.pallas_call(
        body, in_specs=[ANY] * n, out_specs=[ANY] * n, out_shape=full,
        scratch_shapes=[pltpu.SemaphoreType.DMA((3 * n,)), pltpu.SemaphoreType.DMA((3 * n,)), pltpu.SemaphoreType.DMA((n,))],
        name="all_gather_weights")(*shards)


def _scatter_grads(grads, kinds, sizes):
    n = len(grads)

    def shard_shape(g, kd, sz):
        return (sz, g.shape[1]) if kd == "row" else (g.shape[0], sz)

    land = [_sds((3,) + shard_shape(g, kd, sz), g.dtype) for g, kd, sz in zip(grads, kinds, sizes)]

    def body(*refs):
        ins, outs = refs[:n], refs[n:2 * n]
        send, recv = refs[2 * n:]
        x, y, c, chips = _place()

        def copy(t, j):
            px, py = chips[j]
            return pltpu.make_async_remote_copy(
                src_ref=_shard_of(ins[t], kinds[t], 2 * px + py, sizes[t]), dst_ref=outs[t].at[j],
                send_sem=send.at[3 * t + j], recv_sem=recv.at[3 * t + j],
                device_id=(px, py, c), device_id_type=MESH)

        for t in range(n):
            for j in range(3):
                copy(t, j).start()
        for t in range(n):
            for j in range(3):
                copy(t, j).wait_recv()
                copy(t, j).wait_send()

    return pl.pallas_call(
        body, in_specs=[ANY] * n, out_specs=[ANY] * n, out_shape=land,
        scratch_shapes=[pltpu.SemaphoreType.DMA((3 * n,)), pltpu.SemaphoreType.DMA((3 * n,))],
        name="scatter_grads")(*grads)


def _swap_cores(parts):
    n = len(parts)

    def body(*refs):
        ins, outs = refs[:n], refs[n:2 * n]
        send, recv = refs[2 * n:]
        x, y, c, _ = _place()

        def copy(t):
            return pltpu.make_async_remote_copy(
                src_ref=ins[t], dst_ref=outs[t], send_sem=send.at[t], recv_sem=recv.at[t],
                device_id=(x, y, 1 - c), device_id_type=MESH)

        for t in range(n):
            copy(t).start()
        for t in range(n):
            copy(t).wait_recv()
            copy(t).wait_send()

    return pl.pallas_call(
        body, in_specs=[ANY] * n, out_specs=[ANY] * n, out_shape=[_sds(p.shape, p.dtype) for p in parts],
        scratch_shapes=[pltpu.SemaphoreType.DMA((n,)), pltpu.SemaphoreType.DMA((n,))],
        name="swap_cores")(*parts)


def _all_reduce_small(pack):
    R, C = pack.shape

    def body(p_ref, sum_ref, land, send, recv):
        x, y, c, _ = _place()
        me = 4 * x + 2 * y + c
        land[me] = p_ref[...]
        copies = []
        for mask in range(1, N_DEV):
            peer = ((1 - x) if mask & 4 else x, (1 - y) if mask & 2 else y, (1 - c) if mask & 1 else c)
            src = peer[0] * 4 + peer[1] * 2 + peer[2]
            copies.append(pltpu.make_async_remote_copy(
                src_ref=p_ref, dst_ref=land.at[me], send_sem=send.at[mask], recv_sem=recv.at[mask],
                device_id=peer, device_id_type=MESH))
            copies[-1].start()
            copies[-1] = pltpu.make_async_remote_copy(
                src_ref=p_ref, dst_ref=land.at[src], send_sem=send.at[mask], recv_sem=recv.at[mask],
                device_id=peer, device_id_type=MESH)
        for cp in copies:
            cp.wait_recv()
            cp.wait_send()
        tot = land[0]
        for d in range(1, N_DEV):
            tot = tot + land[d]
        sum_ref[...] = tot

    vm = pl.BlockSpec(memory_space=pltpu.VMEM)
    return pl.pallas_call(
        body, in_specs=[vm], out_specs=vm, out_shape=_sds((R, C), F32),
        scratch_shapes=[pltpu.VMEM((N_DEV, R, C), F32), pltpu.SemaphoreType.DMA((N_DEV,)), pltpu.SemaphoreType.DMA((N_DEV,))],
        name="all_reduce_small", compiler_params=_cp(32))(pack)


def _row_tile(R):
    return _pick(R, (128, 64, 32, 16, 8)) if R % 8 == 0 else R


def _sum_landed(own, land, name):
    R, C = own.shape
    tr = _row_tile(R)

    def body(o_ref, l_ref, s_ref):
        s_ref[...] = ((o_ref[...] + l_ref[0].astype(F32)) + l_ref[1].astype(F32)) + l_ref[2].astype(F32)

    return pl.pallas_call(
        body, grid=(R // tr,),
        in_specs=[pl.BlockSpec((tr, C), lambda i: (i, 0)), pl.BlockSpec((3, tr, C), lambda i: (0, i, 0))],
        out_specs=pl.BlockSpec((tr, C), lambda i: (i, 0)), out_shape=_sds((R, C), F32),
        name=name, compiler_params=_cp(32))(own, land)


def _adamw(g, w, m, v):
    m = ADAM_B1 * m + (1.0 - ADAM_B1) * g
    v = ADAM_B2 * v + (1.0 - ADAM_B2) * (g * g)
    m_hat = m / (1.0 - ADAM_B1 ** ADAM_STEP)
    v_hat = v / (1.0 - ADAM_B2 ** ADAM_STEP)
    return -ADAM_LR * (m_hat / (jnp.sqrt(v_hat) + ADAM_EPS) + ADAM_WD * w), m, v


def _update(parts, w, m, v, name):
    R, C = w.shape
    tr = _row_tile(R)
    k = len(parts)

    def body(*refs):
        g = refs[0][...]
        for r in refs[1:k]:
            g = g + r[...]
        w_ref, m_ref, v_ref, g_out, d_out, m_out, v_out = refs[k:]
        d, m_new, v_new = _adamw(g, w_ref[...], m_ref[...], v_ref[...])
        g_out[...] = g
        d_out[...] = d
        m_out[...] = m_new
        v_out[...] = v_new

    blk = pl.BlockSpec((tr, C), lambda i: (i, 0))
    return pl.pallas_call(
        body, grid=(R // tr,), in_specs=[blk] * (k + 3), out_specs=[blk] * 4, out_shape=[_sds((R, C), F32)] * 4,
        name=name, compiler_params=_cp(40))(*parts, w, m, v)


BIG = ("w_in", "w_conv_out", "w_sgu_out", "w_mix_out", "w_q", "w_kv", "w_xo", "w_gu", "w_down")
BIG_KIND = {"w_in": "col", "w_conv_out": "row", "w_sgu_out": "row", "w_mix_out": "row", "w_q": "row",
            "w_kv": "col", "w_xo": "row", "w_gu": "col", "w_down": "row"}
VECS = ("norm_mix", "conv_b", "conv_ln_g", "conv_ln_b", "sgu_ln_g", "sgu_ln_b", "norm_xattn", "norm_mem", "norm_ffn",
        "norm_final")


def _step(a):
    x3d, mem3d, tgt3d = a["x"], a["mem"], a["loss_target"]
    B, S, D = x3d.shape
    M = mem3d.shape[1]
    T = B * S
    x = x3d.reshape(T, D)
    mem = mem3d.reshape(B * M, D)
    tgt = tgt3d.reshape(T, D)
    xi, yi = lax.axis_index("x"), lax.axis_index("y")
    chip = 2 * xi + yi

    names = list(BIG) + ["b_gate", "conv_w"]
    kinds = [BIG_KIND[nm] for nm in BIG] + ["col", "col"]
    shards = [a[nm][0].astype(BF) for nm in BIG] + [a["b_gate"][0], a["conv_w"][0]]
    full = dict(zip(names, _all_gather_weights(shards, kinds)))

    sgu_b = a["sgu_b"][0]
    bz = jnp.repeat(jnp.transpose(sgu_b), LANES, axis=1)
    prm = dict(w_co=full["w_conv_out"], w_so=full["w_sgu_out"], w_mo=full["w_mix_out"], w_q=full["w_q"],
               w_xo=full["w_xo"], w_gu=full["w_gu"], w_down=full["w_down"],
               la_g=a["conv_ln_g"], la_b=a["conv_ln_b"], ls_g=a["sgu_ln_g"], ls_b=a["sgu_ln_b"],
               sgu_w=a["sgu_w"][0], bz=bz, b_gate=full["b_gate"], g2=a["norm_xattn"], g3=a["norm_ffn"],
               gf=a["norm_final"].reshape(1, D))

    p, h1 = _in_proj(x, a["norm_mix"], full["w_in"])
    c = _conv_fwd(p, full["conv_w"], a["conv_b"], B, S)
    merged, s_a, sg = _branch_fwd(c, p, prm)
    mem_n, kv = _kv_fwd(mem, a["norm_mem"], full["w_kv"], B, M)
    x1, x2, h2, o = _attn_fwd(x, merged, kv, prm, S, M)
    dx2, dx3, dgu, h3, f, lsum, d_g3, d_gf = _ffn_loss(x2, tgt, prm)
    loss = lax.psum(0.5 * jnp.sum(lsum) / D, ("x", "y", "c"))

    gw = {}
    gw["w_down"] = _mm_tn(f, dx3, "dw_down")
    gw["w_gu"] = _mm_tn(h3, dgu, "dw_gu")
    dx1, dmerged, dq, dkv, d_g2 = _attn_bwd(x1, kv, dx2, prm, S, M)
    gw["w_xo"] = _mm_tn(o, dx2, "dw_xo")
    gw["w_q"] = _mm_tn(h2, dq, "dw_q")
    gw["w_kv"] = _mm_tn(mem_n, dkv, "dw_kv")
    d_gm = _kv_bwd(mem, a["norm_mem"], full["w_kv"], dkv, B, M)
    gw["w_mix_out"] = _mm_tn(merged, dx1, "dw_mix_out")
    dc, dpb, dya, dyb, d_wm, d_bz, d_lag, d_lab, d_lsg, d_lsb, d_bg = _branch_bwd(c, p, dmerged, prm)
    gw["w_conv_out"] = _mm_tn(s_a, dya, "dw_conv_out")
    gw["w_sgu_out"] = _mm_tn(sg, dyb, "dw_sgu_out")
    dav, dag, d_cw, d_cb = _conv_bwd(p, dc, full["conv_w"], B, S)
    dp = jnp.concatenate([dav, dag, dpb], axis=1)
    gw["w_in"] = _mm_tn(h1, dp, "dw_in")
    grad_x, d_g1 = _in_proj_bwd(x, dx1, dp, a["norm_mix"], full["w_in"])

    sizes = [a[nm].shape[1] if BIG_KIND[nm] == "row" else a[nm].shape[2] for nm in BIG]
    landed = _scatter_grads([gw[nm][1] for nm in BIG], [BIG_KIND[nm] for nm in BIG], sizes)
    part = []
    for nm, sz, ld in zip(BIG, sizes, landed):
        g32 = gw[nm][0]
        own = (lax.dynamic_slice_in_dim(g32, chip * sz, sz, axis=0) if BIG_KIND[nm] == "row"
               else lax.dynamic_slice_in_dim(g32, chip * sz, sz, axis=1))
        part.append(_sum_landed(own, ld, "sum_" + nm))
    other = _swap_cores(part)
    out = {}
    for nm, mine, theirs in zip(BIG, part, other):
        out[nm] = _update([mine, theirs], a[nm][0], a["m_" + nm][0], a["v_" + nm][0], "upd_" + nm)

    G = SGU_GROUPS
    d_sb = jnp.transpose(d_bz.reshape(LANES, G, LANES).sum(axis=-1))
    vec_g = dict(norm_mix=d_g1, conv_b=d_cb, conv_ln_g=d_lag, conv_ln_b=d_lab, sgu_ln_g=d_lsg, sgu_ln_b=d_lsb,
                 norm_xattn=d_g2, norm_mem=d_gm, norm_ffn=d_g3, norm_final=d_gf)
    rows = [vec_g[nm] for nm in VECS] + [d_sb.reshape(1, D), d_bg, d_cw, d_wm.reshape(G * LANES * LANES // D, D)]
    n_rows = sum(r.shape[0] for r in rows)
    pad = (-n_rows) % 8
    pack = jnp.concatenate(rows + [jnp.zeros((pad, D), F32)], axis=0)
    tot = _all_reduce_small(pack)

    def small(nm):
        arr = a[nm]
        return arr.reshape(-1, D) if nm != "sgu_b" else arr.reshape(1, D)

    nv = len(VECS)
    K = a["conv_w"].shape[1]
    rep_names = list(VECS) + ["sgu_b", "sgu_w"]
    rep_g = jnp.concatenate([tot[:nv + 1], tot[nv + 3 + K:n_rows]], axis=0)
    rep = [jnp.concatenate([small(pre + nm) for nm in rep_names], axis=0) for pre in ("", "m_", "v_")]
    rep_out = _update([rep_g], rep[0], rep[1], rep[2], "upd_replicated")
    Dq = D // N_CHIPS
    col_g = lax.dynamic_slice_in_dim(tot[nv + 1:nv + 3 + K], chip * Dq, Dq, axis=1)
    col = [jnp.concatenate([a[pre + "b_gate"][0], a[pre + "conv_w"][0]], axis=0) for pre in ("", "m_", "v_")]
    col_out = _update([col_g], col[0], col[1], col[2], "upd_columns")

    for q in range(4):
        for i, nm in enumerate(VECS):
            out.setdefault(nm, [None] * 4)[q] = rep_out[q][i:i + 1].reshape(a[nm].shape)
        out.setdefault("sgu_b", [None] * 4)[q] = rep_out[q][nv:nv + 1].reshape(a["sgu_b"].shape)
        out.setdefault("sgu_w", [None] * 4)[q] = rep_out[q][nv + 1:].reshape(a["sgu_w"].shape)
        out.setdefault("b_gate", [None] * 4)[q] = col_out[q][0:2][None]
        out.setdefault("conv_w", [None] * 4)[q] = col_out[q][2:][None]
    for nm in BIG:
        out[nm] = [o_[None] for o_ in out[nm]]
    return loss, grad_x.reshape(B, S, D), out


WEIGHTS = ("norm_mix", "w_in", "b_gate", "conv_w", "conv_b", "conv_ln_g", "conv_ln_b", "w_conv_out", "sgu_ln_g",
           "sgu_ln_b", "sgu_w", "sgu_b", "w_sgu_out", "w_mix_out", "norm_xattn", "norm_mem", "w_q", "w_kv", "w_xo",
           "norm_ffn", "w_gu", "w_down", "norm_final")


def kernel(x, mem, norm_mix, w_in, b_gate, conv_w, conv_b, conv_ln_g, conv_ln_b, w_conv_out, sgu_ln_g, sgu_ln_b, sgu_w, sgu_b, w_sgu_out, w_mix_out, norm_xattn, norm_mem, w_q, w_kv, w_xo, norm_ffn, w_gu, w_down, norm_final, loss_target, m_norm_mix, m_w_in, m_b_gate, m_conv_w, m_conv_b, m_conv_ln_g, m_conv_ln_b, m_w_conv_out, m_sgu_ln_g, m_sgu_ln_b, m_sgu_w, m_sgu_b, m_w_sgu_out, m_w_mix_out, m_norm_xattn, m_norm_mem, m_w_q, m_w_kv, m_w_xo, m_norm_ffn, m_w_gu, m_w_down, m_norm_final, v_norm_mix, v_w_in, v_b_gate, v_conv_w, v_conv_b, v_conv_ln_g, v_conv_ln_b, v_w_conv_out, v_sgu_ln_g, v_sgu_ln_b, v_sgu_w, v_sgu_b, v_w_sgu_out, v_w_mix_out, v_norm_xattn, v_norm_mem, v_w_q, v_w_kv, v_w_xo, v_norm_ffn, v_w_gu, v_w_down, v_norm_final):
    a = dict(locals())
    loss, grad_x, out = _step(a)
    res = [loss, grad_x]
    for q in range(4):
        res += [out[nm][q] for nm in WEIGHTS]
    return tuple(res)
```

```python
import functools
import math

import jax
import jax.numpy as jnp
from jax import lax
from jax.experimental import pallas as pl
from jax.experimental.pallas import tpu as pltpu

BF = jnp.bfloat16
F32 = jnp.float32
MESH = pl.DeviceIdType.MESH
ANY = pl.BlockSpec(memory_space=pl.ANY)

RMS_EPS = 1e-6
LN_EPS = 1e-5
HEADS = 4
SGU_GROUPS = 8
LANES = 128
ADAM_LR = 0.001
ADAM_B1 = 0.9
ADAM_B2 = 0.999
ADAM_EPS = 1e-08
ADAM_WD = 0.01
ADAM_STEP = 10
N_CHIPS = 4
N_DEV = 8
MIB = 1024 * 1024


def _sds(shape, dtype):
    return jax.ShapeDtypeStruct(tuple(shape), dtype)


def _cp(vmem_mib):
    return pltpu.CompilerParams(vmem_limit_bytes=vmem_mib * MIB)


def _const(shape):
    nd = len(shape)
    return pl.BlockSpec(tuple(shape), lambda *_: (0,) * nd, pipeline_mode=pl.Buffered(1))


def _dot(a, b):
    return jnp.dot(a.astype(BF), b.astype(BF), preferred_element_type=F32)


def _dot_nt(a, b):
    return lax.dot_general(a.astype(BF), b.astype(BF), (((1,), (1,)), ((), ())), preferred_element_type=F32)


def _dot_tn(a, b):
    return lax.dot_general(a.astype(BF), b.astype(BF), (((0,), (0,)), ((), ())), preferred_element_type=F32)


def _sig(x):
    return 1.0 / (1.0 + jnp.exp(-x))


def _dsilu(x, s):
    return s * (1.0 + x * (1.0 - s))


_GELU_C = math.sqrt(2.0 / math.pi)


def _gelu(x):
    t = jnp.tanh(_GELU_C * (x + 0.044715 * (x * x * x)))
    return 0.5 * x * (1.0 + t), t


def _dgelu(x, t):
    return 0.5 * (1.0 + t) + 0.5 * x * (1.0 - t * t) * (_GELU_C * (1.0 + 3.0 * 0.044715 * (x * x)))


def _rms(x, g):
    r = lax.rsqrt(jnp.mean(x * x, axis=-1, keepdims=True) + RMS_EPS)
    return x * r * g, r


def _rms_bwd(x, g, r, dh):
    xr = x * r
    dxh = dh * g
    dx = r * (dxh - xr * jnp.mean(dxh * xr, axis=-1, keepdims=True))
    return dx, jnp.sum(dh * xr, axis=0, keepdims=True)


def _ln(x, g, b):
    mu = jnp.mean(x, axis=-1, keepdims=True)
    xc = x - mu
    rstd = lax.rsqrt(jnp.mean(xc * xc, axis=-1, keepdims=True) + LN_EPS)
    xh = xc * rstd
    return xh * g + b, xh, rstd


def _ln_bwd(xh, rstd, g, dy):
    dxh = dy * g
    dx = rstd * (dxh - jnp.mean(dxh, axis=-1, keepdims=True) - xh * jnp.mean(dxh * xh, axis=-1, keepdims=True))
    return dx, jnp.sum(dy * xh, axis=0, keepdims=True), jnp.sum(dy, axis=0, keepdims=True)


def _acc(ref, val, first):
    @pl.when(first)
    def _():
        ref[...] = val

    @pl.when(jnp.logical_not(first))
    def _():
        ref[...] += val


def _place():
    x, y, c = lax.axis_index("x"), lax.axis_index("y"), lax.axis_index("c")
    chips = [(1 - x, y), (x, 1 - y), (1 - x, 1 - y)]
    return x, y, c, chips


def _shard_of(ref, kind, k, n):
    if kind == "row":
        return ref.at[pl.ds(k * n, n), :]
    return ref.at[:, pl.ds(k * n, n)]


class _Gather:
    def __init__(self, shards, kinds):
        n = len(shards)
        self.srcs, self.kinds = list(shards), list(kinds)
        self.sizes = [s.shape[0] if kd == "row" else s.shape[1] for s, kd in zip(shards, kinds)]
        self.out_shape = [
            _sds((s.shape[0] * N_CHIPS, s.shape[1]) if kd == "row" else (s.shape[0], s.shape[1] * N_CHIPS), s.dtype)
            for s, kd in zip(shards, kinds)]
        self.sems = [pltpu.SemaphoreType.DMA((3 * n,)), pltpu.SemaphoreType.DMA((3 * n,)), pltpu.SemaphoreType.DMA((n,))]

    def _copies(self, ins, outs, send, recv, loc):
        x, y, c, chips = _place()
        k = 2 * x + y
        local, remote = [], []
        for t in range(len(ins)):
            mine = _shard_of(outs[t], self.kinds[t], k, self.sizes[t])
            local.append(pltpu.make_async_copy(ins[t], mine, loc.at[t]))
            for j, (px, py) in enumerate(chips):
                sent = pltpu.make_async_remote_copy(
                    src_ref=ins[t], dst_ref=mine, send_sem=send.at[3 * t + j], recv_sem=recv.at[3 * t + j],
                    device_id=(px, py, c), device_id_type=MESH)
                landed = pltpu.make_async_remote_copy(
                    src_ref=ins[t], dst_ref=_shard_of(outs[t], self.kinds[t], 2 * px + py, self.sizes[t]),
                    send_sem=send.at[3 * t + j], recv_sem=recv.at[3 * t + j],
                    device_id=(px, py, c), device_id_type=MESH)
                remote.append((sent, landed))
        return local, remote

    def start(self, ins, outs, send, recv, loc):
        local, remote = self._copies(ins, outs, send, recv, loc)
        for cp in local:
            cp.start()
        for sent, _ in remote:
            sent.start()

    def wait(self, ins, outs, send, recv, loc):
        local, remote = self._copies(ins, outs, send, recv, loc)
        for sent, landed in remote:
            landed.wait_recv()
            sent.wait_send()
        for cp in local:
            cp.wait()


class _Scatter:
    def __init__(self, grads, kinds, sizes):
        n = len(grads)
        self.srcs, self.kinds, self.sizes = list(grads), list(kinds), list(sizes)
        self.out_shape = [_sds((3,) + ((sz, g.shape[1]) if kd == "row" else (g.shape[0], sz)), g.dtype)
                          for g, kd, sz in zip(grads, kinds, sizes)]
        self.sems = [pltpu.SemaphoreType.DMA((3 * n,)), pltpu.SemaphoreType.DMA((3 * n,))]

    def _copies(self, ins, outs, send, recv):
        x, y, c, chips = _place()
        return [pltpu.make_async_remote_copy(
            src_ref=_shard_of(ins[t], self.kinds[t], 2 * px + py, self.sizes[t]), dst_ref=outs[t].at[j],
            send_sem=send.at[3 * t + j], recv_sem=recv.at[3 * t + j], device_id=(px, py, c), device_id_type=MESH)
            for t in range(len(ins)) for j, (px, py) in enumerate(chips)]

    def start(self, ins, outs, send, recv):
        for cp in self._copies(ins, outs, send, recv):
            cp.start()

    def wait(self, ins, outs, send, recv):
        for cp in self._copies(ins, outs, send, recv):
            cp.wait_recv()
            cp.wait_send()


def _comm_call(comm, name):
    n, m = len(comm.srcs), len(comm.out_shape)

    def body(*refs):
        comm.start(refs[:n], refs[n:n + m], *refs[n + m:])
        comm.wait(refs[:n], refs[n:n + m], *refs[n + m:])

    return pl.pallas_call(body, in_specs=[ANY] * n, out_specs=[ANY] * m, out_shape=comm.out_shape,
                          scratch_shapes=comm.sems, name=name)(*comm.srcs)


def _pcall(body, args, *, grid, in_specs, out_specs, out_shape, name, vmem, scratch_shapes=(), comm=None):
    in_specs, out_specs, out_shape = list(in_specs), list(out_specs), list(out_shape)
    scratch_shapes = list(scratch_shapes)
    if comm is None:
        res = pl.pallas_call(body, grid=grid, in_specs=in_specs, out_specs=out_specs, out_shape=out_shape,
                             scratch_shapes=scratch_shapes, name=name, compiler_params=_cp(vmem))(*args)
        return list(res), []
    ni, no, ns = len(in_specs), len(out_specs), len(scratch_shapes)
    ci, co = len(comm.srcs), len(comm.out_shape)

    def carried(*refs):
        c_in = refs[ni:ni + ci]
        c_out = refs[ni + ci + no:ni + ci + no + co]
        sems = refs[ni + ci + no + co + ns:]
        ids = [pl.program_id(d) for d in range(len(grid))]
        first = functools.reduce(jnp.logical_and, [i == 0 for i in ids])
        last = functools.reduce(jnp.logical_and, [i == g - 1 for i, g in zip(ids, grid)])

        @pl.when(first)
        def _():
            comm.start(c_in, c_out, *sems)

        body(*refs[:ni], *refs[ni + ci:ni + ci + no], *refs[ni + ci + no + co:ni + ci + no + co + ns])

        @pl.when(last)
        def _():
            comm.wait(c_in, c_out, *sems)

    res = pl.pallas_call(carried, grid=grid, in_specs=in_specs + [ANY] * ci, out_specs=out_specs + [ANY] * co,
                         out_shape=out_shape + list(comm.out_shape), scratch_shapes=scratch_shapes + list(comm.sems),
                         name=name, compiler_params=_cp(vmem))(*args, *comm.srcs)
    return list(res[:no]), list(res[no:])


def _in_proj(x, g1, w_in, comm=None):
    T, D = x.shape
    N = w_in.shape[1]
    tm, tn = 512, 1024

    def body(x_ref, g_ref, w_ref, p_ref, h_ref):
        @pl.when(pl.program_id(1) == 0)
        def _():
            h, _ = _rms(x_ref[...], g_ref[...])
            h_ref[...] = h.astype(BF)

        p_ref[...] = jnp.dot(h_ref[...], w_ref[...], preferred_element_type=F32)

    return _pcall(
        body, (x, g1, w_in), grid=(T // tm, N // tn),
        in_specs=[pl.BlockSpec((tm, D), lambda i, j: (i, 0)), pl.BlockSpec((1, D), lambda i, j: (0, 0)),
                  pl.BlockSpec((D, tn), lambda i, j: (0, j))],
        out_specs=[pl.BlockSpec((tm, tn), lambda i, j: (i, j)), pl.BlockSpec((tm, D), lambda i, j: (i, 0))],
        out_shape=[_sds((T, N), F32), _sds((T, D), BF)],
        name="in_proj", vmem=32, comm=comm)


CONV_PAD = 32
CONV_ROWS = 256


def _conv_fwd(p, conv_w, conv_b, B, S, comm=None):
    K, D = conv_w.shape
    nc = D // LANES

    def body(av_ref, ag_ref, w_ref, b_ref, c_ref, apad):
        apad[pl.ds(0, CONV_PAD), :] = jnp.zeros((CONV_PAD, LANES), F32)
        apad[pl.ds(CONV_PAD, S), :] = av_ref[...] * _sig(ag_ref[...])
        for r0 in range(0, S, CONV_ROWS):
            acc = jnp.zeros((CONV_ROWS, LANES), F32) + b_ref[...]
            for k in range(K):
                acc = acc + w_ref[pl.ds(k, 1), :] * apad[pl.ds(r0 + k + CONV_PAD - (K - 1), CONV_ROWS), :]
            c_ref[pl.ds(r0, CONV_ROWS), :] = acc

    return _pcall(
        body, (p, p, conv_w, conv_b), grid=(B, nc),
        in_specs=[pl.BlockSpec((S, LANES), lambda b, j: (b, j)), pl.BlockSpec((S, LANES), lambda b, j: (b, nc + j)),
                  pl.BlockSpec((K, LANES), lambda b, j: (0, j)), pl.BlockSpec((1, LANES), lambda b, j: (0, j))],
        out_specs=[pl.BlockSpec((S, LANES), lambda b, j: (b, j))],
        out_shape=[_sds((B * S, D), F32)],
        scratch_shapes=[pltpu.VMEM((S + CONV_PAD, LANES), F32)],
        name="conv_fwd", vmem=32, comm=comm)


def _tril_mask():
    t = lax.broadcasted_iota(jnp.int32, (LANES, LANES), 0)
    s = lax.broadcasted_iota(jnp.int32, (LANES, LANES), 1)
    return t >= s


def _branch_a(c, g, b):
    ln_a, xh, rstd = _ln(c, g, b)
    s = _sig(ln_a)
    return ln_a * s, ln_a, s, xh, rstd


def _branch_b(bu, bv, g, b, wm_ref, bz_ref, z_scr, v_scr):
    tm, D = bu.shape
    u, tu = _gelu(bu)
    gv, tv = _gelu(bv)
    v, vh, rstd = _ln(gv, g, b)
    v_scr[...] = v.astype(BF)
    mask = _tril_mask()
    for gi in range(SGU_GROUPS):
        wm = jnp.where(mask, wm_ref[gi], 0.0).astype(BF)
        cols = pl.ds(gi * LANES, LANES)
        for n in range(tm // LANES):
            rows = pl.ds(n * LANES, LANES)
            z_scr[rows, cols] = jnp.dot(wm, v_scr[rows, cols], preferred_element_type=F32) + bz_ref[:, cols]
    z = z_scr[...]
    return u * z, u, tu, z, tv, vh, rstd


TM3 = 256


def _branch_fwd(c, p, prm, tm=TM3, comm=None):
    T, D = c.shape

    def body(c_ref, bu_ref, bv_ref, ga_ref, gb_ref, wco_ref, wso_ref, lag_ref, lab_ref, lsg_ref, lsb_ref, wm_ref,
             bz_ref, bg_ref, mg_ref, sa_ref, sg_ref, z_scr, v_scr):
        s_a = _branch_a(c_ref[...], lag_ref[...], lab_ref[...])[0]
        sa_ref[...] = s_a.astype(BF)
        y_a = jnp.dot(sa_ref[...], wco_ref[...], preferred_element_type=F32)
        sg = _branch_b(bu_ref[...], bv_ref[...], lsg_ref[...], lsb_ref[...], wm_ref, bz_ref, z_scr, v_scr)[0]
        sg_ref[...] = sg.astype(BF)
        y_b = jnp.dot(sg_ref[...], wso_ref[...], preferred_element_type=F32)
        ga = _sig(ga_ref[...] + bg_ref[pl.ds(0, 1), :])
        gb = _sig(gb_ref[...] + bg_ref[pl.ds(1, 1), :])
        mg_ref[...] = (ga * y_a + gb * y_b).astype(BF)

    tile = lambda j: pl.BlockSpec((tm, D), lambda i: (i, j))
    return _pcall(
        body, (c, p, p, p, p, prm["w_co"], prm["w_so"], prm["la_g"], prm["la_b"], prm["ls_g"], prm["ls_b"],
               prm["sgu_w"], prm["bz"], prm["b_gate"]),
        grid=(T // tm,),
        in_specs=[tile(0), tile(2), tile(3), tile(4), tile(5), _const((D, D)), _const((D, D)),
                  _const((1, D)), _const((1, D)), _const((1, D)), _const((1, D)),
                  _const((SGU_GROUPS, LANES, LANES)), _const((LANES, D)), _const((2, D))],
        out_specs=[tile(0), tile(0), tile(0)],
        out_shape=[_sds((T, D), BF)] * 3,
        scratch_shapes=[pltpu.VMEM((tm, D), F32), pltpu.VMEM((tm, D), BF)],
        name="branch_fwd", vmem=48, comm=comm)


def _kv_fwd(mem, gm, w_kv, B, M):
    D = mem.shape[1]
    N = w_kv.shape[1]

    def body(m_ref, g_ref, w_ref, mn_ref, kv_ref):
        h, _ = _rms(m_ref[...], g_ref[...])
        mn_ref[...] = h.astype(BF)
        kv_ref[...] = jnp.dot(mn_ref[...], w_ref[...], preferred_element_type=F32).astype(BF)

    return pl.pallas_call(
        body, grid=(B,),
        in_specs=[pl.BlockSpec((M, D), lambda b: (b, 0)), _const((1, D)), _const((D, N))],
        out_specs=[pl.BlockSpec((M, D), lambda b: (b, 0)), pl.BlockSpec((M, N), lambda b: (b, 0))],
        out_shape=[_sds((B * M, D), BF), _sds((B * M, N), BF)],
        name="kv_fwd", compiler_params=_cp(32))(mem, gm, w_kv)


def _softmax_rows(s):
    e = jnp.exp(s - jnp.max(s, axis=-1, keepdims=True))
    return e / jnp.sum(e, axis=-1, keepdims=True)


TM4 = 256


def _attn_fwd(x, merged, kv, prm, S, M, tm=TM4):
    T, D = x.shape
    hd = D // HEADS
    scale = hd ** -0.5
    tpb = S // tm

    def body(x_ref, mg_ref, kv_ref, wmo_ref, wq_ref, wxo_ref, g_ref, x1_ref, x2_ref, h2_ref, o_ref):
        x1 = x_ref[...] + jnp.dot(mg_ref[...], wmo_ref[...], preferred_element_type=F32)
        x1_ref[...] = x1
        h2, _ = _rms(x1, g_ref[...])
        h2_ref[...] = h2.astype(BF)
        qb = jnp.dot(h2_ref[...], wq_ref[...], preferred_element_type=F32).astype(BF)
        for h in range(HEADS):
            cs = pl.ds(h * hd, hd)
            s = _dot_nt(qb[:, h * hd:(h + 1) * hd], kv_ref[:, cs]) * scale
            pr = _softmax_rows(s)
            o_ref[:, cs] = _dot(pr, kv_ref[:, pl.ds(D + h * hd, hd)]).astype(BF)
        x2_ref[...] = x1 + jnp.dot(o_ref[...], wxo_ref[...], preferred_element_type=F32)

    tile = pl.BlockSpec((tm, D), lambda i: (i, 0))
    return pl.pallas_call(
        body, grid=(T // tm,),
        in_specs=[tile, tile, pl.BlockSpec((M, 2 * D), lambda i: (i // tpb, 0)),
                  _const((D, D)), _const((D, D)), _const((D, D)), _const((1, D))],
        out_specs=[tile, tile, tile, tile],
        out_shape=[_sds((T, D), F32), _sds((T, D), F32), _sds((T, D), BF), _sds((T, D), BF)],
        name="attn_fwd", compiler_params=_cp(40))(x, merged, kv, prm["w_mo"], prm["w_q"], prm["w_xo"], prm["g2"])


TM5 = 256


def _ffn_loss(x2, tgt, prm, tm=TM5):
    T, D = x2.shape
    F = prm["w_down"].shape[0]
    FC = F // 2

    def body(x2_ref, t_ref, wgu_ref, wd_ref, g3_ref, gf_ref, dx2_ref, dx3_ref, dgu_ref, h3_ref, f_ref, ls_ref,
             dg3_ref, dgf_ref, gu_scr):
        first = pl.program_id(0) == 0
        x2 = x2_ref[...]
        h3, r3 = _rms(x2, g3_ref[...])
        h3_ref[...] = h3.astype(BF)
        x3 = x2
        for ch in range(2):
            gc, uc = pl.ds(ch * FC, FC), pl.ds(F + ch * FC, FC)
            gt = jnp.dot(h3_ref[...], wgu_ref[:, gc], preferred_element_type=F32)
            up = jnp.dot(h3_ref[...], wgu_ref[:, uc], preferred_element_type=F32)
            gu_scr[:, gc] = gt
            gu_scr[:, uc] = up
            f_ref[:, gc] = (gt * _sig(gt) * up).astype(BF)
            x3 = x3 + jnp.dot(f_ref[:, gc], wd_ref[gc, :], preferred_element_type=F32)
        y, rf = _rms(x3, gf_ref[...])
        e = y - t_ref[...]
        _acc(ls_ref, jnp.sum(e * e, axis=0, keepdims=True), first)
        dx3, dgf = _rms_bwd(x3, gf_ref[...], rf, e * (1.0 / D))
        _acc(dgf_ref, dgf, first)
        dx3_ref[...] = dx3.astype(BF)
        dh3 = jnp.zeros((tm, D), F32)
        for ch in range(2):
            gc, uc = pl.ds(ch * FC, FC), pl.ds(F + ch * FC, FC)
            df = lax.dot_general(dx3_ref[...], wd_ref[gc, :], (((1,), (1,)), ((), ())), preferred_element_type=F32)
            gt, up = gu_scr[:, gc], gu_scr[:, uc]
            s = _sig(gt)
            dgu_ref[:, gc] = (df * up * _dsilu(gt, s)).astype(BF)
            dgu_ref[:, uc] = (df * gt * s).astype(BF)
            dh3 = dh3 + lax.dot_general(dgu_ref[:, gc], wgu_ref[:, gc], (((1,), (1,)), ((), ())), preferred_element_type=F32)
            dh3 = dh3 + lax.dot_general(dgu_ref[:, uc], wgu_ref[:, uc], (((1,), (1,)), ((), ())), preferred_element_type=F32)
        dxa, dg3 = _rms_bwd(x2, g3_ref[...], r3, dh3)
        _acc(dg3_ref, dg3, first)
        dx2_ref[...] = dx3 + dxa

    tile = lambda n: pl.BlockSpec((tm, n), lambda i: (i, 0))
    vec = pl.BlockSpec((1, D), lambda i: (0, 0))
    return pl.pallas_call(
        body, grid=(T // tm,),
        in_specs=[tile(D), tile(D), _const((D, 2 * F)), _const((F, D)), _const((1, D)), _const((1, D))],
        out_specs=[tile(D), tile(D), tile(2 * F), tile(D), tile(F), vec, vec, vec],
        out_shape=[_sds((T, D), F32), _sds((T, D), BF), _sds((T, 2 * F), BF), _sds((T, D), BF), _sds((T, F), BF),
                   _sds((1, D), F32), _sds((1, D), F32), _sds((1, D), F32)],
        scratch_shapes=[pltpu.VMEM((tm, 2 * F), F32)],
        name="ffn_loss", compiler_params=_cp(56))(x2, tgt, prm["w_gu"], prm["w_down"], prm["g3"], prm["gf"])


def _attn_bwd(x1, kv, dx2, prm, S, M, tm=TM4, comm=None):
    T, D = x1.shape
    hd = D // HEADS
    scale = hd ** -0.5
    tpb = S // tm

    def body(x1_ref, kv_ref, dx2_ref, wmo_ref, wq_ref, wxo_ref, g_ref, dx1_ref, dmg_ref, dq_ref, dkv_ref, dg_ref,
             h2_scr, do_scr):
        i = pl.program_id(0)
        x1 = x1_ref[...]
        dx2 = dx2_ref[...]
        h2, r2 = _rms(x1, g_ref[...])
        h2_scr[...] = h2.astype(BF)
        qb = jnp.dot(h2_scr[...], wq_ref[...], preferred_element_type=F32).astype(BF)
        do_scr[...] = _dot_nt(dx2, wxo_ref[...]).astype(BF)
        for h in range(HEADS):
            cs, vs = pl.ds(h * hd, hd), pl.ds(D + h * hd, hd)
            qh = qb[:, h * hd:(h + 1) * hd]
            pr = _softmax_rows(_dot_nt(qh, kv_ref[:, cs]) * scale)
            dpr = _dot_nt(do_scr[:, cs], kv_ref[:, vs])
            dv = _dot_tn(pr, do_scr[:, cs])
            ds = (pr * (dpr - jnp.sum(dpr * pr, axis=-1, keepdims=True)) * scale).astype(BF)
            dq_ref[:, cs] = jnp.dot(ds, kv_ref[:, cs], preferred_element_type=F32).astype(BF)
            dk = _dot_tn(ds, qh)

            @pl.when(i % tpb == 0)
            def _():
                dkv_ref[:, cs] = dk
                dkv_ref[:, vs] = dv

            @pl.when(i % tpb != 0)
            def _():
                dkv_ref[:, cs] += dk
                dkv_ref[:, vs] += dv

        dh2 = _dot_nt(dq_ref[...], wq_ref[...])
        dxa, dg = _rms_bwd(x1, g_ref[...], r2, dh2)
        _acc(dg_ref, dg, i == 0)
        dx1 = dx2 + dxa
        dx1_ref[...] = dx1
        dmg_ref[...] = _dot_nt(dx1, wmo_ref[...])

    tile = pl.BlockSpec((tm, D), lambda i: (i, 0))
    kvb = pl.BlockSpec((M, 2 * D), lambda i: (i // tpb, 0))
    B = T // S
    return _pcall(
        body, (x1, kv, dx2, prm["w_mo"], prm["w_q"], prm["w_xo"], prm["g2"]), grid=(T // tm,),
        in_specs=[tile, kvb, tile, _const((D, D)), _const((D, D)), _const((D, D)), _const((1, D))],
        out_specs=[tile, tile, tile, kvb, pl.BlockSpec((1, D), lambda i: (0, 0))],
        out_shape=[_sds((T, D), F32), _sds((T, D), F32), _sds((T, D), BF), _sds((B * M, 2 * D), F32), _sds((1, D), F32)],
        scratch_shapes=[pltpu.VMEM((tm, D), BF), pltpu.VMEM((tm, D), BF)],
        name="attn_bwd", vmem=48, comm=comm)


def _kv_bwd(mem, gm, w_kv, dkv, B, M):
    D = mem.shape[1]
    N = w_kv.shape[1]

    def body(m_ref, g_ref, w_ref, dkv_ref, dg_ref):
        mem_t = m_ref[...]
        _, r = _rms(mem_t, g_ref[...])
        dmn = _dot_nt(dkv_ref[...], w_ref[...])
        _acc(dg_ref, jnp.sum(dmn * (mem_t * r), axis=0, keepdims=True), pl.program_id(0) == 0)

    return pl.pallas_call(
        body, grid=(B,),
        in_specs=[pl.BlockSpec((M, D), lambda b: (b, 0)), _const((1, D)), _const((D, N)),
                  pl.BlockSpec((M, N), lambda b: (b, 0))],
        out_specs=pl.BlockSpec((1, D), lambda b: (0, 0)),
        out_shape=_sds((1, D), F32),
        name="kv_bwd", compiler_params=_cp(32))(mem, gm, w_kv, dkv)


def _branch_bwd(c, p, dmerged, prm, tm=TM3, comm=None):
    T, D = c.shape

    def body(c_ref, bu_ref, bv_ref, ga_ref, gb_ref, dm_ref, wco_ref, wso_ref, lag_ref, lab_ref, lsg_ref, lsb_ref,
             wm_ref, bz_ref, bg_ref,
             dc_ref, dpb_ref, dya_ref, dyb_ref, dwm_ref, dbz_ref, dlag_ref, dlab_ref, dlsg_ref, dlsb_ref, dbg_ref,
             z_scr, v_scr, sa_scr, sg_scr, dv_scr):
        first = pl.program_id(0) == 0
        s_a, ln_a, sig_a, xh_a, rstd_a = _branch_a(c_ref[...], lag_ref[...], lab_ref[...])
        sa_scr[...] = s_a.astype(BF)
        y_a = jnp.dot(sa_scr[...], wco_ref[...], preferred_element_type=F32)
        bu, bv = bu_ref[...], bv_ref[...]
        sg, u, tu, z, tv, vh, rstd_v = _branch_b(bu, bv, lsg_ref[...], lsb_ref[...], wm_ref, bz_ref, z_scr, v_scr)
        sg_scr[...] = sg.astype(BF)
        y_b = jnp.dot(sg_scr[...], wso_ref[...], preferred_element_type=F32)
        ga = _sig(ga_ref[...] + bg_ref[pl.ds(0, 1), :])
        gb = _sig(gb_ref[...] + bg_ref[pl.ds(1, 1), :])
        dm = dm_ref[...]
        dga = dm * y_a * ga * (1.0 - ga)
        dgb = dm * y_b * gb * (1.0 - gb)
        dpb_ref[:, pl.ds(2 * D, D)] = dga.astype(BF)
        dpb_ref[:, pl.ds(3 * D, D)] = dgb.astype(BF)
        _acc(dbg_ref.at[pl.ds(0, 1), :], jnp.sum(dga, axis=0, keepdims=True), first)
        _acc(dbg_ref.at[pl.ds(1, 1), :], jnp.sum(dgb, axis=0, keepdims=True), first)
        dya_ref[...] = (dm * ga).astype(BF)
        dyb_ref[...] = (dm * gb).astype(BF)
        dln = _dot_nt(dya_ref[...], wco_ref[...]) * _dsilu(ln_a, sig_a)
        dc, dlag, dlab = _ln_bwd(xh_a, rstd_a, lag_ref[...], dln)
        dc_ref[...] = dc
        _acc(dlag_ref, dlag, first)
        _acc(dlab_ref, dlab, first)
        dsg = _dot_nt(dyb_ref[...], wso_ref[...])
        dpb_ref[:, pl.ds(0, D)] = (dsg * z * _dgelu(bu, tu)).astype(BF)
        dz = dsg * u
        z_scr[...] = dz
        mask = _tril_mask()

        @pl.when(first)
        def _():
            dwm_ref[...] = jnp.zeros_like(dwm_ref)
            dbz_ref[...] = jnp.zeros_like(dbz_ref)

        for gi in range(SGU_GROUPS):
            wm = jnp.where(mask, wm_ref[gi], 0.0).astype(BF)
            cols = pl.ds(gi * LANES, LANES)
            for n in range(tm // LANES):
                rows = pl.ds(n * LANES, LANES)
                dzb = z_scr[rows, cols].astype(BF)
                dv_scr[rows, cols] = lax.dot_general(wm, dzb, (((0,), (0,)), ((), ())), preferred_element_type=F32)
                dw = lax.dot_general(dzb, v_scr[rows, cols], (((1,), (1,)), ((), ())), preferred_element_type=F32)
                dwm_ref[gi] += jnp.where(mask, dw, 0.0)
                dbz_ref[:, cols] += z_scr[rows, cols]
        dgv, dlsg, dlsb = _ln_bwd(vh, rstd_v, lsg_ref[...], dv_scr[...])
        _acc(dlsg_ref, dlsg, first)
        _acc(dlsb_ref, dlsb, first)
        dpb_ref[:, pl.ds(D, D)] = (dgv * _dgelu(bv, tv)).astype(BF)

    tile = lambda j: pl.BlockSpec((tm, D), lambda i: (i, j))
    vec = pl.BlockSpec((1, D), lambda i: (0, 0))
    return _pcall(
        body, (c, p, p, p, p, dmerged, prm["w_co"], prm["w_so"], prm["la_g"], prm["la_b"], prm["ls_g"], prm["ls_b"],
               prm["sgu_w"], prm["bz"], prm["b_gate"]),
        grid=(T // tm,),
        in_specs=[tile(0), tile(2), tile(3), tile(4), tile(5), tile(0), _const((D, D)), _const((D, D)),
                  _const((1, D)), _const((1, D)), _const((1, D)), _const((1, D)),
                  _const((SGU_GROUPS, LANES, LANES)), _const((LANES, D)), _const((2, D))],
        out_specs=[tile(0), pl.BlockSpec((tm, 4 * D), lambda i: (i, 0)), tile(0), tile(0),
                   pl.BlockSpec((SGU_GROUPS, LANES, LANES), lambda i: (0, 0, 0)),
                   pl.BlockSpec((LANES, D), lambda i: (0, 0)), vec, vec, vec, vec,
                   pl.BlockSpec((2, D), lambda i: (0, 0))],
        out_shape=[_sds((T, D), F32), _sds((T, 4 * D), BF), _sds((T, D), BF), _sds((T, D), BF),
                   _sds((SGU_GROUPS, LANES, LANES), F32), _sds((LANES, D), F32),
                   _sds((1, D), F32), _sds((1, D), F32), _sds((1, D), F32), _sds((1, D), F32), _sds((2, D), F32)],
        scratch_shapes=[pltpu.VMEM((tm, D), F32), pltpu.VMEM((tm, D), BF), pltpu.VMEM((tm, D), BF),
                        pltpu.VMEM((tm, D), BF), pltpu.VMEM((tm, D), F32)],
        name="branch_bwd", vmem=56, comm=comm)


def _conv_bwd(p, dc, conv_w, B, S, comm=None):
    K, D = conv_w.shape
    nc = D // LANES

    def body(av_ref, ag_ref, dc_ref, w_ref, dav_ref, dag_ref, dw_ref, db_ref, apad, dpad):
        b = pl.program_id(1)
        av = av_ref[...]
        sg = _sig(ag_ref[...])
        apad[pl.ds(0, CONV_PAD), :] = jnp.zeros((CONV_PAD, LANES), F32)
        apad[pl.ds(CONV_PAD, S), :] = av * sg
        dpad[pl.ds(S, CONV_PAD), :] = jnp.zeros((CONV_PAD, LANES), F32)
        dpad[pl.ds(0, S), :] = dc_ref[...]

        @pl.when(b == 0)
        def _():
            dw_ref[...] = jnp.zeros_like(dw_ref)
            db_ref[...] = jnp.zeros_like(db_ref)

        db_ref[...] += jnp.sum(dc_ref[...], axis=0, keepdims=True)
        for k in range(K):
            tot = jnp.zeros((1, LANES), F32)
            for r0 in range(0, S, CONV_ROWS):
                tot = tot + jnp.sum(dpad[pl.ds(r0, CONV_ROWS), :] * apad[pl.ds(r0 + k + CONV_PAD - (K - 1), CONV_ROWS), :],
                                    axis=0, keepdims=True)
            dw_ref[pl.ds(k, 1), :] += tot
        for r0 in range(0, S, CONV_ROWS):
            da = jnp.zeros((CONV_ROWS, LANES), F32)
            for k in range(K):
                da = da + w_ref[pl.ds(k, 1), :] * dpad[pl.ds(r0 + (K - 1) - k, CONV_ROWS), :]
            rows = pl.ds(r0, CONV_ROWS)
            s = sg[r0:r0 + CONV_ROWS, :]
            a_v = av[r0:r0 + CONV_ROWS, :]
            dav_ref[rows, :] = (da * s).astype(BF)
            dag_ref[rows, :] = (da * a_v * s * (1.0 - s)).astype(BF)

    blk = lambda off: pl.BlockSpec((S, LANES), lambda j, b: (b, off + j))
    return _pcall(
        body, (p, p, dc, conv_w), grid=(nc, B),
        in_specs=[blk(0), blk(nc), blk(0), pl.BlockSpec((K, LANES), lambda j, b: (0, j))],
        out_specs=[blk(0), blk(0), pl.BlockSpec((K, LANES), lambda j, b: (0, j)), pl.BlockSpec((1, LANES), lambda j, b: (0, j))],
        out_shape=[_sds((B * S, D), BF), _sds((B * S, D), BF), _sds((K, D), F32), _sds((1, D), F32)],
        scratch_shapes=[pltpu.VMEM((S + CONV_PAD, LANES), F32), pltpu.VMEM((S + CONV_PAD, LANES), F32)],
        name="conv_bwd", vmem=32, comm=comm)


TM1 = 256


def _in_proj_bwd(x, dx1, dp, g1, w_in, tm=TM1, comm=None):
    T, D = x.shape
    N = w_in.shape[1]

    def body(x_ref, dx1_ref, dp_ref, g_ref, w_ref, dx_ref, dg_ref):
        x_t = x_ref[...]
        _, r = _rms(x_t, g_ref[...])
        dh = lax.dot_general(dp_ref[...], w_ref[...], (((1,), (1,)), ((), ())), preferred_element_type=F32)
        dxa, dg = _rms_bwd(x_t, g_ref[...], r, dh)
        dx_ref[...] = dx1_ref[...] + dxa
        _acc(dg_ref, dg, pl.program_id(0) == 0)

    tile = pl.BlockSpec((tm, D), lambda i: (i, 0))
    return _pcall(
        body, (x, dx1, dp, g1, w_in), grid=(T // tm,),
        in_specs=[tile, tile, pl.BlockSpec((tm, N), lambda i: (i, 0)), _const((1, D)), _const((D, N))],
        out_specs=[tile, pl.BlockSpec((1, D), lambda i: (0, 0))],
        out_shape=[_sds((T, D), F32), _sds((1, D), F32)],
        name="in_proj_bwd", vmem=48, comm=comm)


def _pick(n, cands):
    for c in cands:
        if n % c == 0:
            return c
    raise ValueError(f"no tile of {cands} divides {n}")


def _mm_tn(x, dy, name):
    T, K = x.shape
    N = dy.shape[1]
    tm = _pick(T, (1024, 512, 256))
    tk = _pick(K, (1024, 1408, 512))
    tn = _pick(N, (1024, 1408, 512))
    nt = T // tm

    def body(x_ref, dy_ref, o_ref, ob_ref, acc):
        t = pl.program_id(2)

        @pl.when(t == 0)
        def _():
            acc[...] = jnp.zeros_like(acc)

        acc[...] += _dot_tn(x_ref[...], dy_ref[...])

        @pl.when(t == nt - 1)
        def _():
            o_ref[...] = acc[...]
            ob_ref[...] = acc[...].astype(BF)

    return pl.pallas_call(
        body, grid=(K // tk, N // tn, nt),
        in_specs=[pl.BlockSpec((tm, tk), lambda i, j, t: (t, i)), pl.BlockSpec((tm, tn), lambda i, j, t: (t, j))],
        out_specs=[pl.BlockSpec((tk, tn), lambda i, j, t: (i, j)), pl.BlockSpec((tk, tn), lambda i, j, t: (i, j))],
        out_shape=[_sds((K, N), F32), _sds((K, N), BF)],
        scratch_shapes=[pltpu.VMEM((tk, tn), F32)],
        name=name, compiler_params=_cp(48))(x, dy)


def _swap_cores(parts):
    n = len(parts)

    def body(*refs):
        ins, outs = refs[:n], refs[n:2 * n]
        send, recv = refs[2 * n:]
        x, y, c, _ = _place()

        def copy(t):
            return pltpu.make_async_remote_copy(
                src_ref=ins[t], dst_ref=outs[t], send_sem=send.at[t], recv_sem=recv.at[t],
                device_id=(x, y, 1 - c), device_id_type=MESH)

        for t in range(n):
            copy(t).start()
        for t in range(n):
            copy(t).wait_recv()
            copy(t).wait_send()

    return pl.pallas_call(
        body, in_specs=[ANY] * n, out_specs=[ANY] * n, out_shape=[_sds(p.shape, p.dtype) for p in parts],
        scratch_shapes=[pltpu.SemaphoreType.DMA((n,)), pltpu.SemaphoreType.DMA((n,))],
        name="swap_cores")(*parts)


def _all_reduce_small(pack):
    R, C = pack.shape

    def body(p_ref, sum_ref, land, send, recv):
        x, y, c, _ = _place()
        me = 4 * x + 2 * y + c
        land[me] = p_ref[...]
        copies = []
        for mask in range(1, N_DEV):
            peer = ((1 - x) if mask & 4 else x, (1 - y) if mask & 2 else y, (1 - c) if mask & 1 else c)
            src = peer[0] * 4 + peer[1] * 2 + peer[2]
            copies.append(pltpu.make_async_remote_copy(
                src_ref=p_ref, dst_ref=land.at[me], send_sem=send.at[mask], recv_sem=recv.at[mask],
                device_id=peer, device_id_type=MESH))
            copies[-1].start()
            copies[-1] = pltpu.make_async_remote_copy(
                src_ref=p_ref, dst_ref=land.at[src], send_sem=send.at[mask], recv_sem=recv.at[mask],
                device_id=peer, device_id_type=MESH)
        for cp in copies:
            cp.wait_recv()
            cp.wait_send()
        tot = land[0]
        for d in range(1, N_DEV):
            tot = tot + land[d]
        sum_ref[...] = tot

    vm = pl.BlockSpec(memory_space=pltpu.VMEM)
    return pl.pallas_call(
        body, in_specs=[vm], out_specs=vm, out_shape=_sds((R, C), F32),
        scratch_shapes=[pltpu.VMEM((N_DEV, R, C), F32), pltpu.SemaphoreType.DMA((N_DEV,)), pltpu.SemaphoreType.DMA((N_DEV,))],
        name="all_reduce_small", compiler_params=_cp(32))(pack)


def _row_tile(R):
    return _pick(R, (128, 64, 32, 16, 8)) if R % 8 == 0 else R


def _sum_landed(own, land, name):
    R, C = own.shape
    tr = _row_tile(R)

    def body(o_ref, l_ref, s_ref):
        s_ref[...] = ((o_ref[...] + l_ref[0].astype(F32)) + l_ref[1].astype(F32)) + l_ref[2].astype(F32)

    return pl.pallas_call(
        body, grid=(R // tr,),
        in_specs=[pl.BlockSpec((tr, C), lambda i: (i, 0)), pl.BlockSpec((3, tr, C), lambda i: (0, i, 0))],
        out_specs=pl.BlockSpec((tr, C), lambda i: (i, 0)), out_shape=_sds((R, C), F32),
        name=name, compiler_params=_cp(32))(own, land)


def _adamw(g, w, m, v):
    m = ADAM_B1 * m + (1.0 - ADAM_B1) * g
    v = ADAM_B2 * v + (1.0 - ADAM_B2) * (g * g)
    m_hat = m / (1.0 - ADAM_B1 ** ADAM_STEP)
    v_hat = v / (1.0 - ADAM_B2 ** ADAM_STEP)
    return -ADAM_LR * (m_hat / (jnp.sqrt(v_hat) + ADAM_EPS) + ADAM_WD * w), m, v


def _update(parts, w, m, v, name):
    R, C = w.shape
    tr = _row_tile(R)
    k = len(parts)

    def body(*refs):
        g = refs[0][...]
        for r in refs[1:k]:
            g = g + r[...]
        w_ref, m_ref, v_ref, g_out, d_out, m_out, v_out = refs[k:]
        d, m_new, v_new = _adamw(g, w_ref[...], m_ref[...], v_ref[...])
        g_out[...] = g
        d_out[...] = d
        m_out[...] = m_new
        v_out[...] = v_new

    blk = pl.BlockSpec((tr, C), lambda i: (i, 0))
    return pl.pallas_call(
        body, grid=(R // tr,), in_specs=[blk] * (k + 3), out_specs=[blk] * 4, out_shape=[_sds((R, C), F32)] * 4,
        name=name, compiler_params=_cp(40))(*parts, w, m, v)


BIG = ("w_in", "w_conv_out", "w_sgu_out", "w_mix_out", "w_q", "w_kv", "w_xo", "w_gu", "w_down")
BIG_KIND = {"w_in": "col", "w_conv_out": "row", "w_sgu_out": "row", "w_mix_out": "row", "w_q": "row",
            "w_kv": "col", "w_xo": "row", "w_gu": "col", "w_down": "row"}
VECS = ("norm_mix", "conv_b", "conv_ln_g", "conv_ln_b", "sgu_ln_g", "sgu_ln_b", "norm_xattn", "norm_mem", "norm_ffn",
        "norm_final")


def _step(a):
    x3d, mem3d, tgt3d = a["x"], a["mem"], a["loss_target"]
    B, S, D = x3d.shape
    M = mem3d.shape[1]
    T = B * S
    x = x3d.reshape(T, D)
    mem = mem3d.reshape(B * M, D)
    tgt = tgt3d.reshape(T, D)
    xi, yi = lax.axis_index("x"), lax.axis_index("y")
    chip = 2 * xi + yi

    def gather(names):
        return _Gather([a[nm][0] if nm in ("b_gate", "conv_w") else a[nm][0].astype(BF) for nm in names],
                       [BIG_KIND.get(nm, "col") for nm in names])

    first = ("w_in", "b_gate", "conv_w")
    on_in_proj = ("w_conv_out", "w_sgu_out", "w_kv", "w_mix_out", "w_q", "w_xo")
    full = dict(zip(first, _comm_call(gather(first), "gather_w_in")))
    (p, h1), got = _in_proj(x, a["norm_mix"], full["w_in"], comm=gather(on_in_proj))
    full.update(zip(on_in_proj, got))
    (c,), got = _conv_fwd(p, full["conv_w"], a["conv_b"], B, S, comm=gather(("w_down",)))
    full["w_down"] = got[0]

    sgu_b = a["sgu_b"][0]
    bz = jnp.repeat(jnp.transpose(sgu_b), LANES, axis=1)
    prm = dict(w_co=full["w_conv_out"], w_so=full["w_sgu_out"], w_mo=full["w_mix_out"], w_q=full["w_q"],
               w_xo=full["w_xo"], w_down=full["w_down"],
               la_g=a["conv_ln_g"], la_b=a["conv_ln_b"], ls_g=a["sgu_ln_g"], ls_b=a["sgu_ln_b"],
               sgu_w=a["sgu_w"][0], bz=bz, b_gate=full["b_gate"], g2=a["norm_xattn"], g3=a["norm_ffn"],
               gf=a["norm_final"].reshape(1, D))

    (merged, s_a, sg), got = _branch_fwd(c, p, prm, comm=gather(("w_gu",)))
    prm["w_gu"] = got[0]
    mem_n, kv = _kv_fwd(mem, a["norm_mem"], full["w_kv"], B, M)
    x1, x2, h2, o = _attn_fwd(x, merged, kv, prm, S, M)
    dx2, dx3, dgu, h3, f, lsum, d_g3, d_gf = _ffn_loss(x2, tgt, prm)
    loss = lax.psum(0.5 * jnp.sum(lsum) / D, ("x", "y", "c"))

    size_of = {nm: a[nm].shape[1] if BIG_KIND[nm] == "row" else a[nm].shape[2] for nm in BIG}
    landed = {}

    def scatter(names):
        return _Scatter([gw[nm][1] for nm in names], [BIG_KIND[nm] for nm in names], [size_of[nm] for nm in names])

    gw = {}
    gw["w_down"] = _mm_tn(f, dx3, "dw_down")
    gw["w_gu"] = _mm_tn(h3, dgu, "dw_gu")
    (dx1, dmerged, dq, dkv, d_g2), got = _attn_bwd(x1, kv, dx2, prm, S, M, comm=scatter(("w_gu", "w_down")))
    landed.update(zip(("w_gu", "w_down"), got))
    gw["w_xo"] = _mm_tn(o, dx2, "dw_xo")
    gw["w_q"] = _mm_tn(h2, dq, "dw_q")
    gw["w_kv"] = _mm_tn(mem_n, dkv, "dw_kv")
    d_gm = _kv_bwd(mem, a["norm_mem"], full["w_kv"], dkv, B, M)
    gw["w_mix_out"] = _mm_tn(merged, dx1, "dw_mix_out")
    group = ("w_xo", "w_q", "w_kv", "w_mix_out")
    (dc, dpb, dya, dyb, d_wm, d_bz, d_lag, d_lab, d_lsg, d_lsb, d_bg), got = _branch_bwd(
        c, p, dmerged, prm, comm=scatter(group))
    landed.update(zip(group, got))
    gw["w_conv_out"] = _mm_tn(s_a, dya, "dw_conv_out")
    gw["w_sgu_out"] = _mm_tn(sg, dyb, "dw_sgu_out")
    group = ("w_conv_out", "w_sgu_out")
    (dav, dag, d_cw, d_cb), got = _conv_bwd(p, dc, full["conv_w"], B, S, comm=scatter(group))
    landed.update(zip(group, got))
    dp = jnp.concatenate([dav, dag, dpb], axis=1)
    gw["w_in"] = _mm_tn(h1, dp, "dw_in")
    (grad_x, d_g1), got = _in_proj_bwd(x, dx1, dp, a["norm_mix"], full["w_in"], comm=scatter(("w_in",)))
    landed["w_in"] = got[0]

    sizes = [size_of[nm] for nm in BIG]
    part = []
    for nm, sz, ld in zip(BIG, sizes, [landed[nm] for nm in BIG]):
        g32 = gw[nm][0]
        own = (lax.dynamic_slice_in_dim(g32, chip * sz, sz, axis=0) if BIG_KIND[nm] == "row"
               else lax.dynamic_slice_in_dim(g32, chip * sz, sz, axis=1))
        part.append(_sum_landed(own, ld, "sum_" + nm))
    other = _swap_cores(part)
    out = {}
    for nm, mine, theirs in zip(BIG, part, other):
        out[nm] = _update([mine, theirs], a[nm][0], a["m_" + nm][0], a["v_" + nm][0], "upd_" + nm)

    G = SGU_GROUPS
    d_sb = jnp.transpose(d_bz.reshape(LANES, G, LANES).sum(axis=-1))
    vec_g = dict(norm_mix=d_g1, conv_b=d_cb, conv_ln_g=d_lag, conv_ln_b=d_lab, sgu_ln_g=d_lsg, sgu_ln_b=d_lsb,
                 norm_xattn=d_g2, norm_mem=d_gm, norm_ffn=d_g3, norm_final=d_gf)
    rows = [vec_g[nm] for nm in VECS] + [d_sb.reshape(1, D), d_bg, d_cw, d_wm.reshape(G * LANES * LANES // D, D)]
    n_rows = sum(r.shape[0] for r in rows)
    pad = (-n_rows) % 8
    pack = jnp.concatenate(rows + [jnp.zeros((pad, D), F32)], axis=0)
    tot = _all_reduce_small(pack)

    def small(nm):
        arr = a[nm]
        return arr.reshape(-1, D) if nm != "sgu_b" else arr.reshape(1, D)

    nv = len(VECS)
    K = a["conv_w"].shape[1]
    rep_names = list(VECS) + ["sgu_b", "sgu_w"]
    rep_g = jnp.concatenate([tot[:nv + 1], tot[nv + 3 + K:n_rows]], axis=0)
    rep = [jnp.concatenate([small(pre + nm) for nm in rep_names], axis=0) for pre in ("", "m_", "v_")]
    rep_out = _update([rep_g], rep[0], rep[1], rep[2], "upd_replicated")
    Dq = D // N_CHIPS
    col_g = lax.dynamic_slice_in_dim(tot[nv + 1:nv + 3 + K], chip * Dq, Dq, axis=1)
    col = [jnp.concatenate([a[pre + "b_gate"][0], a[pre + "conv_w"][0]], axis=0) for pre in ("", "m_", "v_")]
    col_out = _update([col_g], col[0], col[1], col[2], "upd_columns")

    for q in range(4):
        for i, nm in enumerate(VECS):
            out.setdefault(nm, [None] * 4)[q] = rep_out[q][i:i + 1].reshape(a[nm].shape)
        out.setdefault("sgu_b", [None] * 4)[q] = rep_out[q][nv:nv + 1].reshape(a["sgu_b"].shape)
        out.setdefault("sgu_w", [None] * 4)[q] = rep_out[q][nv + 1:].reshape(a["sgu_w"].shape)
        out.setdefault("b_gate", [None] * 4)[q] = col_out[q][0:2][None]
        out.setdefault("conv_w", [None] * 4)[q] = col_out[q][2:][None]
    for nm in BIG:
        out[nm] = [o_[None] for o_ in out[nm]]
    return loss, grad_x.reshape(B, S, D), out


WEIGHTS = ("norm_mix", "w_in", "b_gate", "conv_w", "conv_b", "conv_ln_g", "conv_ln_b", "w_conv_out", "sgu_ln_g",
           "sgu_ln_b", "sgu_w", "sgu_b", "w_sgu_out", "w_mix_out", "norm_xattn", "norm_mem", "w_q", "w_kv", "w_xo",
           "norm_ffn", "w_gu", "w_down", "norm_final")


def kernel(x, mem, norm_mix, w_in, b_gate, conv_w, conv_b, conv_ln_g, conv_ln_b, w_conv_out, sgu_ln_g, sgu_ln_b, sgu_w, sgu_b, w_sgu_out, w_mix_out, norm_xattn, norm_mem, w_q, w_kv, w_xo, norm_ffn, w_gu, w_down, norm_final, loss_target, m_norm_mix, m_w_in, m_b_gate, m_conv_w, m_conv_b, m_conv_ln_g, m_conv_ln_b, m_w_conv_out, m_sgu_ln_g, m_sgu_ln_b, m_sgu_w, m_sgu_b, m_w_sgu_out, m_w_mix_out, m_norm_xattn, m_norm_mem, m_w_q, m_w_kv, m_w_xo, m_norm_ffn, m_w_gu, m_w_down, m_norm_final, v_norm_mix, v_w_in, v_b_gate, v_conv_w, v_conv_b, v_conv_ln_g, v_conv_ln_b, v_w_conv_out, v_sgu_ln_g, v_sgu_ln_b, v_sgu_w, v_sgu_b, v_w_sgu_out, v_w_mix_out, v_norm_xattn, v_norm_mem, v_w_q, v_w_kv, v_w_xo, v_norm_ffn, v_w_gu, v_w_down, v_norm_final):
    a = dict(locals())
    loss, grad_x, out = _step(a)
    res = [loss, grad_x]
    for q in range(4):
        res += [out[nm][q] for nm in WEIGHTS]
    return tuple(res)
```

```python
import functools
import math

import jax
import jax.numpy as jnp
from jax import lax
from jax.experimental import pallas as pl
from jax.experimental.pallas import tpu as pltpu

BF = jnp.bfloat16
F32 = jnp.float32
MESH = pl.DeviceIdType.MESH
ANY = pl.BlockSpec(memory_space=pl.ANY)

RMS_EPS = 1e-6
LN_EPS = 1e-5
HEADS = 4
SGU_GROUPS = 8
LANES = 128
ADAM_LR = 0.001
ADAM_B1 = 0.9
ADAM_B2 = 0.999
ADAM_EPS = 1e-08
ADAM_WD = 0.01
ADAM_STEP = 10
N_CHIPS = 4
N_DEV = 8
MIB = 1024 * 1024


def _sds(shape, dtype):
    return jax.ShapeDtypeStruct(tuple(shape), dtype)


def _cp(vmem_mib):
    return pltpu.CompilerParams(vmem_limit_bytes=vmem_mib * MIB)


def _const(shape):
    nd = len(shape)
    return pl.BlockSpec(tuple(shape), lambda *_: (0,) * nd, pipeline_mode=pl.Buffered(1))


def _dot(a, b):
    return jnp.dot(a.astype(BF), b.astype(BF), preferred_element_type=F32)


def _dot_nt(a, b):
    return lax.dot_general(a.astype(BF), b.astype(BF), (((1,), (1,)), ((), ())), preferred_element_type=F32)


def _dot_tn(a, b):
    return lax.dot_general(a.astype(BF), b.astype(BF), (((0,), (0,)), ((), ())), preferred_element_type=F32)


def _sig(x):
    return 1.0 / (1.0 + jnp.exp(-x))


def _dsilu(x, s):
    return s * (1.0 + x * (1.0 - s))


_GELU_C = math.sqrt(2.0 / math.pi)


def _gelu(x):
    t = jnp.tanh(_GELU_C * (x + 0.044715 * (x * x * x)))
    return 0.5 * x * (1.0 + t), t


def _dgelu(x, t):
    return 0.5 * (1.0 + t) + 0.5 * x * (1.0 - t * t) * (_GELU_C * (1.0 + 3.0 * 0.044715 * (x * x)))


def _rms(x, g):
    r = lax.rsqrt(jnp.mean(x * x, axis=-1, keepdims=True) + RMS_EPS)
    return x * r * g, r


def _rms_bwd(x, g, r, dh):
    xr = x * r
    dxh = dh * g
    dx = r * (dxh - xr * jnp.mean(dxh * xr, axis=-1, keepdims=True))
    return dx, jnp.sum(dh * xr, axis=0, keepdims=True)


def _ln(x, g, b):
    mu = jnp.mean(x, axis=-1, keepdims=True)
    xc = x - mu
    rstd = lax.rsqrt(jnp.mean(xc * xc, axis=-1, keepdims=True) + LN_EPS)
    xh = xc * rstd
    return xh * g + b, xh, rstd


def _ln_bwd(xh, rstd, g, dy):
    dxh = dy * g
    dx = rstd * (dxh - jnp.mean(dxh, axis=-1, keepdims=True) - xh * jnp.mean(dxh * xh, axis=-1, keepdims=True))
    return dx, jnp.sum(dy * xh, axis=0, keepdims=True), jnp.sum(dy, axis=0, keepdims=True)


def _acc(ref, val, first):
    @pl.when(first)
    def _():
        ref[...] = val

    @pl.when(jnp.logical_not(first))
    def _():
        ref[...] += val


def _place():
    x, y, c = lax.axis_index("x"), lax.axis_index("y"), lax.axis_index("c")
    chips = [(1 - x, y), (x, 1 - y), (1 - x, 1 - y)]
    return x, y, c, chips


def _shard_of(ref, kind, k, n):
    if kind == "row":
        return ref.at[pl.ds(k * n, n), :]
    return ref.at[:, pl.ds(k * n, n)]


class _Gather:
    def __init__(self, shards, kinds):
        n = len(shards)
        self.srcs, self.kinds = list(shards), list(kinds)
        self.sizes = [s.shape[0] if kd == "row" else s.shape[1] for s, kd in zip(shards, kinds)]
        self.halves = [s.shape[0] // 2 if s.shape[0] % 32 == 0 else None for s in shards]
        self.out_shape = [
            _sds((s.shape[0] * N_CHIPS, s.shape[1]) if kd == "row" else (s.shape[0], s.shape[1] * N_CHIPS), s.dtype)
            for s, kd in zip(shards, kinds)]
        dma = pltpu.SemaphoreType.DMA
        self.sems = [dma((3 * n,)), dma((3 * n,)), dma((n,)), dma((3 * n,)), dma((3 * n,))]

    def _part(self, ref, t, core):
        h = self.halves[t]
        return ref if h is None else ref.at[pl.ds(core * h, h), :]

    def _copies(self, ins, outs, send, recv, loc, fsend, frecv):
        x, y, c, chips = _place()
        k = 2 * x + y
        local, remote = [], []
        for t in range(len(ins)):
            block = lambda q: _shard_of(outs[t], self.kinds[t], q, self.sizes[t])
            local.append(pltpu.make_async_copy(ins[t], block(k), loc.at[t]))
            for j, (px, py) in enumerate(chips):
                sems = dict(send_sem=send.at[3 * t + j], recv_sem=recv.at[3 * t + j])
                there = dict(device_id=(px, py, c), device_id_type=MESH)
                sent = pltpu.make_async_remote_copy(
                    src_ref=self._part(ins[t], t, c), dst_ref=self._part(block(k), t, c), **sems, **there)
                got = self._part(block(2 * px + py), t, c)
                landed = pltpu.make_async_remote_copy(src_ref=self._part(ins[t], t, c), dst_ref=got, **sems, **there)
                passed = handed = None
                if self.halves[t] is not None:
                    fsems = dict(send_sem=fsend.at[3 * t + j], recv_sem=frecv.at[3 * t + j])
                    sibling = dict(device_id=(x, y, 1 - c), device_id_type=MESH)
                    passed = pltpu.make_async_remote_copy(src_ref=got, dst_ref=got, **fsems, **sibling)
                    other = self._part(block(2 * px + py), t, 1 - c)
                    handed = pltpu.make_async_remote_copy(src_ref=got, dst_ref=other, **fsems, **sibling)
                remote.append((sent, landed, passed, handed))
        return local, remote

    def start(self, ins, outs, *sems):
        local, remote = self._copies(ins, outs, *sems)
        for cp in local:
            cp.start()
        for sent, _, _, _ in remote:
            sent.start()

    def wait(self, ins, outs, *sems):
        local, remote = self._copies(ins, outs, *sems)
        for sent, landed, passed, handed in remote:
            landed.wait_recv()
            if passed is not None:
                passed.start()
        for sent, landed, passed, handed in remote:
            if passed is not None:
                handed.wait_recv()
                passed.wait_send()
            sent.wait_send()
        for cp in local:
            cp.wait()


class _Scatter:
    def __init__(self, grads, kinds, sizes):
        n = len(grads)
        self.srcs, self.kinds, self.sizes = list(grads), list(kinds), list(sizes)
        self.out_shape = [_sds((3,) + ((sz, g.shape[1]) if kd == "row" else (g.shape[0], sz)), g.dtype)
                          for g, kd, sz in zip(grads, kinds, sizes)]
        self.sems = [pltpu.SemaphoreType.DMA((3 * n,)), pltpu.SemaphoreType.DMA((3 * n,))]

    def _copies(self, ins, outs, send, recv):
        x, y, c, chips = _place()
        return [pltpu.make_async_remote_copy(
            src_ref=_shard_of(ins[t], self.kinds[t], 2 * px + py, self.sizes[t]), dst_ref=outs[t].at[j],
            send_sem=send.at[3 * t + j], recv_sem=recv.at[3 * t + j], device_id=(px, py, c), device_id_type=MESH)
            for t in range(len(ins)) for j, (px, py) in enumerate(chips)]

    def start(self, ins, outs, send, recv):
        for cp in self._copies(ins, outs, send, recv):
            cp.start()

    def wait(self, ins, outs, send, recv):
        for cp in self._copies(ins, outs, send, recv):
            cp.wait_recv()
            cp.wait_send()


def _comm_call(comm, name):
    n, m = len(comm.srcs), len(comm.out_shape)

    def body(*refs):
        comm.start(refs[:n], refs[n:n + m], *refs[n + m:])
        comm.wait(refs[:n], refs[n:n + m], *refs[n + m:])

    return pl.pallas_call(body, in_specs=[ANY] * n, out_specs=[ANY] * m, out_shape=comm.out_shape,
                          scratch_shapes=comm.sems, name=name)(*comm.srcs)


def _pcall(body, args, *, grid, in_specs, out_specs, out_shape, name, vmem, scratch_shapes=(), comm=None):
    in_specs, out_specs, out_shape = list(in_specs), list(out_specs), list(out_shape)
    scratch_shapes = list(scratch_shapes)
    if comm is None:
        res = pl.pallas_call(body, grid=grid, in_specs=in_specs, out_specs=out_specs, out_shape=out_shape,
                             scratch_shapes=scratch_shapes, name=name, compiler_params=_cp(vmem))(*args)
        return list(res), []
    ni, no, ns = len(in_specs), len(out_specs), len(scratch_shapes)
    ci, co = len(comm.srcs), len(comm.out_shape)

    def carried(*refs):
        c_in = refs[ni:ni + ci]
        c_out = refs[ni + ci + no:ni + ci + no + co]
        sems = refs[ni + ci + no + co + ns:]
        ids = [pl.program_id(d) for d in range(len(grid))]
        first = functools.reduce(jnp.logical_and, [i == 0 for i in ids])
        last = functools.reduce(jnp.logical_and, [i == g - 1 for i, g in zip(ids, grid)])

        @pl.when(first)
        def _():
            comm.start(c_in, c_out, *sems)

        body(*refs[:ni], *refs[ni + ci:ni + ci + no], *refs[ni + ci + no + co:ni + ci + no + co + ns])

        @pl.when(last)
        def _():
            comm.wait(c_in, c_out, *sems)

    res = pl.pallas_call(carried, grid=grid, in_specs=in_specs + [ANY] * ci, out_specs=out_specs + [ANY] * co,
                         out_shape=out_shape + list(comm.out_shape), scratch_shapes=scratch_shapes + list(comm.sems),
                         name=name, compiler_params=_cp(vmem))(*args, *comm.srcs)
    return list(res[:no]), list(res[no:])


def _in_proj(x, g1, w_in, comm=None):
    T, D = x.shape
    N = w_in.shape[1]
    tm, tn = 512, 1024

    def body(x_ref, g_ref, w_ref, p_ref, h_ref):
        @pl.when(pl.program_id(1) == 0)
        def _():
            h, _ = _rms(x_ref[...], g_ref[...])
            h_ref[...] = h.astype(BF)

        p_ref[...] = jnp.dot(h_ref[...], w_ref[...], preferred_element_type=F32)

    return _pcall(
        body, (x, g1, w_in), grid=(T // tm, N // tn),
        in_specs=[pl.BlockSpec((tm, D), lambda i, j: (i, 0)), pl.BlockSpec((1, D), lambda i, j: (0, 0)),
                  pl.BlockSpec((D, tn), lambda i, j: (0, j))],
        out_specs=[pl.BlockSpec((tm, tn), lambda i, j: (i, j)), pl.BlockSpec((tm, D), lambda i, j: (i, 0))],
        out_shape=[_sds((T, N), F32), _sds((T, D), BF)],
        name="in_proj", vmem=32, comm=comm)


CONV_PAD = 32
CONV_ROWS = 256


def _conv_fwd(p, conv_w, conv_b, B, S, comm=None):
    K, D = conv_w.shape
    nc = D // LANES

    def body(av_ref, ag_ref, w_ref, b_ref, c_ref, apad):
        apad[pl.ds(0, CONV_PAD), :] = jnp.zeros((CONV_PAD, LANES), F32)
        apad[pl.ds(CONV_PAD, S), :] = av_ref[...] * _sig(ag_ref[...])
        for r0 in range(0, S, CONV_ROWS):
            acc = jnp.zeros((CONV_ROWS, LANES), F32) + b_ref[...]
            for k in range(K):
                acc = acc + w_ref[pl.ds(k, 1), :] * apad[pl.ds(r0 + k + CONV_PAD - (K - 1), CONV_ROWS), :]
            c_ref[pl.ds(r0, CONV_ROWS), :] = acc

    return _pcall(
        body, (p, p, conv_w, conv_b), grid=(B, nc),
        in_specs=[pl.BlockSpec((S, LANES), lambda b, j: (b, j)), pl.BlockSpec((S, LANES), lambda b, j: (b, nc + j)),
                  pl.BlockSpec((K, LANES), lambda b, j: (0, j)), pl.BlockSpec((1, LANES), lambda b, j: (0, j))],
        out_specs=[pl.BlockSpec((S, LANES), lambda b, j: (b, j))],
        out_shape=[_sds((B * S, D), F32)],
        scratch_shapes=[pltpu.VMEM((S + CONV_PAD, LANES), F32)],
        name="conv_fwd", vmem=32, comm=comm)


def _tril_mask():
    t = lax.broadcasted_iota(jnp.int32, (LANES, LANES), 0)
    s = lax.broadcasted_iota(jnp.int32, (LANES, LANES), 1)
    return t >= s


def _branch_a(c, g, b):
    ln_a, xh, rstd = _ln(c, g, b)
    s = _sig(ln_a)
    return ln_a * s, ln_a, s, xh, rstd


def _branch_b(bu, bv, g, b, wm_ref, bz_ref, z_scr, v_scr):
    tm, D = bu.shape
    u, tu = _gelu(bu)
    gv, tv = _gelu(bv)
    v, vh, rstd = _ln(gv, g, b)
    v_scr[...] = v.astype(BF)
    mask = _tril_mask()
    for gi in range(SGU_GROUPS):
        wm = jnp.where(mask, wm_ref[gi], 0.0).astype(BF)
        cols = pl.ds(gi * LANES, LANES)
        for n in range(tm // LANES):
            rows = pl.ds(n * LANES, LANES)
            z_scr[rows, cols] = jnp.dot(wm, v_scr[rows, cols], preferred_element_type=F32) + bz_ref[:, cols]
    z = z_scr[...]
    return u * z, u, tu, z, tv, vh, rstd


TM3 = 256
TM3_FWD = 512


def _branch_fwd(c, p, prm, tm=TM3_FWD, comm=None):
    T, D = c.shape

    def body(c_ref, bu_ref, bv_ref, ga_ref, gb_ref, wco_ref, wso_ref, lag_ref, lab_ref, lsg_ref, lsb_ref, wm_ref,
             bz_ref, bg_ref, mg_ref, sa_ref, sg_ref, z_scr, v_scr):
        s_a = _branch_a(c_ref[...], lag_ref[...], lab_ref[...])[0]
        sa_ref[...] = s_a.astype(BF)
        y_a = jnp.dot(sa_ref[...], wco_ref[...], preferred_element_type=F32)
        sg = _branch_b(bu_ref[...], bv_ref[...], lsg_ref[...], lsb_ref[...], wm_ref, bz_ref, z_scr, v_scr)[0]
        sg_ref[...] = sg.astype(BF)
        y_b = jnp.dot(sg_ref[...], wso_ref[...], preferred_element_type=F32)
        ga = _sig(ga_ref[...] + bg_ref[pl.ds(0, 1), :])
        gb = _sig(gb_ref[...] + bg_ref[pl.ds(1, 1), :])
        mg_ref[...] = (ga * y_a + gb * y_b).astype(BF)

    tile = lambda j: pl.BlockSpec((tm, D), lambda i: (i, j))
    return _pcall(
        body, (c, p, p, p, p, prm["w_co"], prm["w_so"], prm["la_g"], prm["la_b"], prm["ls_g"], prm["ls_b"],
               prm["sgu_w"], prm["bz"], prm["b_gate"]),
        grid=(T // tm,),
        in_specs=[tile(0), tile(2), tile(3), tile(4), tile(5), _const((D, D)), _const((D, D)),
                  _const((1, D)), _const((1, D)), _const((1, D)), _const((1, D)),
                  _const((SGU_GROUPS, LANES, LANES)), _const((LANES, D)), _const((2, D))],
        out_specs=[tile(0), tile(0), tile(0)],
        out_shape=[_sds((T, D), BF)] * 3,
        scratch_shapes=[pltpu.VMEM((tm, D), F32), pltpu.VMEM((tm, D), BF)],
        name="branch_fwd", vmem=48, comm=comm)


def _kv_fwd(mem, gm, w_kv, B, M):
    D = mem.shape[1]
    N = w_kv.shape[1]

    def body(m_ref, g_ref, w_ref, mn_ref, kv_ref):
        h, _ = _rms(m_ref[...], g_ref[...])
        mn_ref[...] = h.astype(BF)
        kv_ref[...] = jnp.dot(mn_ref[...], w_ref[...], preferred_element_type=F32).astype(BF)

    return pl.pallas_call(
        body, grid=(B,),
        in_specs=[pl.BlockSpec((M, D), lambda b: (b, 0)), _const((1, D)), _const((D, N))],
        out_specs=[pl.BlockSpec((M, D), lambda b: (b, 0)), pl.BlockSpec((M, N), lambda b: (b, 0))],
        out_shape=[_sds((B * M, D), BF), _sds((B * M, N), BF)],
        name="kv_fwd", compiler_params=_cp(32))(mem, gm, w_kv)


def _softmax_rows(s):
    e = jnp.exp(s - jnp.max(s, axis=-1, keepdims=True))
    return e / jnp.sum(e, axis=-1, keepdims=True)


TM4 = 512


def _attn_fwd(x, merged, kv, prm, S, M, tm=TM4):
    T, D = x.shape
    hd = D // HEADS
    scale = hd ** -0.5
    tpb = S // tm

    def body(x_ref, mg_ref, kv_ref, wmo_ref, wq_ref, wxo_ref, g_ref, x1_ref, x2_ref, h2_ref, o_ref):
        x1 = x_ref[...] + jnp.dot(mg_ref[...], wmo_ref[...], preferred_element_type=F32)
        x1_ref[...] = x1
        h2, _ = _rms(x1, g_ref[...])
        h2_ref[...] = h2.astype(BF)
        qb = jnp.dot(h2_ref[...], wq_ref[...], preferred_element_type=F32).astype(BF)
        for h in range(HEADS):
            cs = pl.ds(h * hd, hd)
            s = _dot_nt(qb[:, h * hd:(h + 1) * hd], kv_ref[:, cs]) * scale
            pr = _softmax_rows(s)
            o_ref[:, cs] = _dot(pr, kv_ref[:, pl.ds(D + h * hd, hd)]).astype(BF)
        x2_ref[...] = x1 + jnp.dot(o_ref[...], wxo_ref[...], preferred_element_type=F32)

    tile = pl.BlockSpec((tm, D), lambda i: (i, 0))
    return pl.pallas_call(
        body, grid=(T // tm,),
        in_specs=[tile, tile, pl.BlockSpec((M, 2 * D), lambda i: (i // tpb, 0)),
                  _const((D, D)), _const((D, D)), _const((D, D)), _const((1, D))],
        out_specs=[tile, tile, tile, tile],
        out_shape=[_sds((T, D), F32), _sds((T, D), F32), _sds((T, D), BF), _sds((T, D), BF)],
        name="attn_fwd", compiler_params=_cp(40))(x, merged, kv, prm["w_mo"], prm["w_q"], prm["w_xo"], prm["g2"])


TM5 = 256


def _ffn_loss(x2, tgt, prm, tm=TM5):
    T, D = x2.shape
    F = prm["w_down"].shape[0]
    FC = F // 2

    def body(x2_ref, t_ref, wgu_ref, wd_ref, g3_ref, gf_ref, dx2_ref, dx3_ref, dgu_ref, h3_ref, f_ref, ls_ref,
             dg3_ref, dgf_ref, gu_scr):
        first = pl.program_id(0) == 0
        x2 = x2_ref[...]
        h3, r3 = _rms(x2, g3_ref[...])
        h3_ref[...] = h3.astype(BF)
        x3 = x2
        for ch in range(2):
            gc, uc = pl.ds(ch * FC, FC), pl.ds(F + ch * FC, FC)
            gt = jnp.dot(h3_ref[...], wgu_ref[:, gc], preferred_element_type=F32)
            up = jnp.dot(h3_ref[...], wgu_ref[:, uc], preferred_element_type=F32)
            gu_scr[:, gc] = gt
            gu_scr[:, uc] = up
            f_ref[:, gc] = (gt * _sig(gt) * up).astype(BF)
            x3 = x3 + jnp.dot(f_ref[:, gc], wd_ref[gc, :], preferred_element_type=F32)
        y, rf = _rms(x3, gf_ref[...])
        e = y - t_ref[...]
        _acc(ls_ref, jnp.sum(e * e, axis=0, keepdims=True), first)
        dx3, dgf = _rms_bwd(x3, gf_ref[...], rf, e * (1.0 / D))
        _acc(dgf_ref, dgf, first)
        dx3_ref[...] = dx3.astype(BF)
        dh3 = jnp.zeros((tm, D), F32)
        for ch in range(2):
            gc, uc = pl.ds(ch * FC, FC), pl.ds(F + ch * FC, FC)
            df = lax.dot_general(dx3_ref[...], wd_ref[gc, :], (((1,), (1,)), ((), ())), preferred_element_type=F32)
            gt, up = gu_scr[:, gc], gu_scr[:, uc]
            s = _sig(gt)
            dgu_ref[:, gc] = (df * up * _dsilu(gt, s)).astype(BF)
            dgu_ref[:, uc] = (df * gt * s).astype(BF)
            dh3 = dh3 + lax.dot_general(dgu_ref[:, gc], wgu_ref[:, gc], (((1,), (1,)), ((), ())), preferred_element_type=F32)
            dh3 = dh3 + lax.dot_general(dgu_ref[:, uc], wgu_ref[:, uc], (((1,), (1,)), ((), ())), preferred_element_type=F32)
        dxa, dg3 = _rms_bwd(x2, g3_ref[...], r3, dh3)
        _acc(dg3_ref, dg3, first)
        dx2_ref[...] = dx3 + dxa

    tile = lambda n: pl.BlockSpec((tm, n), lambda i: (i, 0))
    vec = pl.BlockSpec((1, D), lambda i: (0, 0))
    return pl.pallas_call(
        body, grid=(T // tm,),
        in_specs=[tile(D), tile(D), _const((D, 2 * F)), _const((F, D)), _const((1, D)), _const((1, D))],
        out_specs=[tile(D), tile(D), tile(2 * F), tile(D), tile(F), vec, vec, vec],
        out_shape=[_sds((T, D), F32), _sds((T, D), BF), _sds((T, 2 * F), BF), _sds((T, D), BF), _sds((T, F), BF),
                   _sds((1, D), F32), _sds((1, D), F32), _sds((1, D), F32)],
        scratch_shapes=[pltpu.VMEM((tm, 2 * F), F32)],
        name="ffn_loss", compiler_params=_cp(56))(x2, tgt, prm["w_gu"], prm["w_down"], prm["g3"], prm["gf"])


def _attn_bwd(x1, kv, dx2, prm, S, M, tm=TM4, comm=None):
    T, D = x1.shape
    hd = D // HEADS
    scale = hd ** -0.5
    tpb = S // tm

    def body(x1_ref, kv_ref, dx2_ref, wmo_ref, wq_ref, wxo_ref, g_ref, dx1_ref, dmg_ref, dq_ref, dkv_ref, dg_ref,
             h2_scr, do_scr):
        i = pl.program_id(0)
        x1 = x1_ref[...]
        dx2 = dx2_ref[...]
        h2, r2 = _rms(x1, g_ref[...])
        h2_scr[...] = h2.astype(BF)
        qb = jnp.dot(h2_scr[...], wq_ref[...], preferred_element_type=F32).astype(BF)
        do_scr[...] = _dot_nt(dx2, wxo_ref[...]).astype(BF)
        for h in range(HEADS):
            cs, vs = pl.ds(h * hd, hd), pl.ds(D + h * hd, hd)
            qh = qb[:, h * hd:(h + 1) * hd]
            pr = _softmax_rows(_dot_nt(qh, kv_ref[:, cs]) * scale)
            dpr = _dot_nt(do_scr[:, cs], kv_ref[:, vs])
            dv = _dot_tn(pr, do_scr[:, cs])
            ds = (pr * (dpr - jnp.sum(dpr * pr, axis=-1, keepdims=True)) * scale).astype(BF)
            dq_ref[:, cs] = jnp.dot(ds, kv_ref[:, cs], preferred_element_type=F32).astype(BF)
            dk = _dot_tn(ds, qh)

            @pl.when(i % tpb == 0)
            def _():
                dkv_ref[:, cs] = dk
                dkv_ref[:, vs] = dv

            @pl.when(i % tpb != 0)
            def _():
                dkv_ref[:, cs] += dk
                dkv_ref[:, vs] += dv

        dh2 = _dot_nt(dq_ref[...], wq_ref[...])
        dxa, dg = _rms_bwd(x1, g_ref[...], r2, dh2)
        _acc(dg_ref, dg, i == 0)
        dx1 = dx2 + dxa
        dx1_ref[...] = dx1
        dmg_ref[...] = _dot_nt(dx1, wmo_ref[...])

    tile = pl.BlockSpec((tm, D), lambda i: (i, 0))
    kvb = pl.BlockSpec((M, 2 * D), lambda i: (i // tpb, 0))
    B = T // S
    return _pcall(
        body, (x1, kv, dx2, prm["w_mo"], prm["w_q"], prm["w_xo"], prm["g2"]), grid=(T // tm,),
        in_specs=[tile, kvb, tile, _const((D, D)), _const((D, D)), _const((D, D)), _const((1, D))],
        out_specs=[tile, tile, tile, kvb, pl.BlockSpec((1, D), lambda i: (0, 0))],
        out_shape=[_sds((T, D), F32), _sds((T, D), F32), _sds((T, D), BF), _sds((B * M, 2 * D), F32), _sds((1, D), F32)],
        scratch_shapes=[pltpu.VMEM((tm, D), BF), pltpu.VMEM((tm, D), BF)],
        name="attn_bwd", vmem=48, comm=comm)


def _kv_bwd(mem, gm, w_kv, dkv, B, M):
    D = mem.shape[1]
    N = w_kv.shape[1]

    def body(m_ref, g_ref, w_ref, dkv_ref, dg_ref):
        mem_t = m_ref[...]
        _, r = _rms(mem_t, g_ref[...])
        dmn = _dot_nt(dkv_ref[...], w_ref[...])
        _acc(dg_ref, jnp.sum(dmn * (mem_t * r), axis=0, keepdims=True), pl.program_id(0) == 0)

    return pl.pallas_call(
        body, grid=(B,),
        in_specs=[pl.BlockSpec((M, D), lambda b: (b, 0)), _const((1, D)), _const((D, N)),
                  pl.BlockSpec((M, N), lambda b: (b, 0))],
        out_specs=pl.BlockSpec((1, D), lambda b: (0, 0)),
        out_shape=_sds((1, D), F32),
        name="kv_bwd", compiler_params=_cp(32))(mem, gm, w_kv, dkv)


def _branch_bwd(c, p, dmerged, prm, tm=TM3, comm=None):
    T, D = c.shape

    def body(c_ref, bu_ref, bv_ref, ga_ref, gb_ref, dm_ref, wco_ref, wso_ref, lag_ref, lab_ref, lsg_ref, lsb_ref,
             wm_ref, bz_ref, bg_ref,
             dc_ref, dpb_ref, dya_ref, dyb_ref, dwm_ref, dbz_ref, dlag_ref, dlab_ref, dlsg_ref, dlsb_ref, dbg_ref,
             z_scr, v_scr, sa_scr, sg_scr, dv_scr):
        first = pl.program_id(0) == 0
        s_a, ln_a, sig_a, xh_a, rstd_a = _branch_a(c_ref[...], lag_ref[...], lab_ref[...])
        sa_scr[...] = s_a.astype(BF)
        y_a = jnp.dot(sa_scr[...], wco_ref[...], preferred_element_type=F32)
        bu, bv = bu_ref[...], bv_ref[...]
        sg, u, tu, z, tv, vh, rstd_v = _branch_b(bu, bv, lsg_ref[...], lsb_ref[...], wm_ref, bz_ref, z_scr, v_scr)
        sg_scr[...] = sg.astype(BF)
        y_b = jnp.dot(sg_scr[...], wso_ref[...], preferred_element_type=F32)
        ga = _sig(ga_ref[...] + bg_ref[pl.ds(0, 1), :])
        gb = _sig(gb_ref[...] + bg_ref[pl.ds(1, 1), :])
        dm = dm_ref[...]
        dga = dm * y_a * ga * (1.0 - ga)
        dgb = dm * y_b * gb * (1.0 - gb)
        dpb_ref[:, pl.ds(2 * D, D)] = dga.astype(BF)
        dpb_ref[:, pl.ds(3 * D, D)] = dgb.astype(BF)
        _acc(dbg_ref.at[pl.ds(0, 1), :], jnp.sum(dga, axis=0, keepdims=True), first)
        _acc(dbg_ref.at[pl.ds(1, 1), :], jnp.sum(dgb, axis=0, keepdims=True), first)
        dya_ref[...] = (dm * ga).astype(BF)
        dyb_ref[...] = (dm * gb).astype(BF)
        dln = _dot_nt(dya_ref[...], wco_ref[...]) * _dsilu(ln_a, sig_a)
        dc, dlag, dlab = _ln_bwd(xh_a, rstd_a, lag_ref[...], dln)
        dc_ref[...] = dc
        _acc(dlag_ref, dlag, first)
        _acc(dlab_ref, dlab, first)
        dsg = _dot_nt(dyb_ref[...], wso_ref[...])
        dpb_ref[:, pl.ds(0, D)] = (dsg * z * _dgelu(bu, tu)).astype(BF)
        dz = dsg * u
        z_scr[...] = dz
        mask = _tril_mask()

        @pl.when(first)
        def _():
            dwm_ref[...] = jnp.zeros_like(dwm_ref)
            dbz_ref[...] = jnp.zeros_like(dbz_ref)

        for gi in range(SGU_GROUPS):
            wm = jnp.where(mask, wm_ref[gi], 0.0).astype(BF)
            cols = pl.ds(gi * LANES, LANES)
            for n in range(tm // LANES):
                rows = pl.ds(n * LANES, LANES)
                dzb = z_scr[rows, cols].astype(BF)
                dv_scr[rows, cols] = lax.dot_general(wm, dzb, (((0,), (0,)), ((), ())), preferred_element_type=F32)
                dw = lax.dot_general(dzb, v_scr[rows, cols], (((1,), (1,)), ((), ())), preferred_element_type=F32)
                dwm_ref[gi] += jnp.where(mask, dw, 0.0)
                dbz_ref[:, cols] += z_scr[rows, cols]
        dgv, dlsg, dlsb = _ln_bwd(vh, rstd_v, lsg_ref[...], dv_scr[...])
        _acc(dlsg_ref, dlsg, first)
        _acc(dlsb_ref, dlsb, first)
        dpb_ref[:, pl.ds(D, D)] = (dgv * _dgelu(bv, tv)).astype(BF)

    tile = lambda j: pl.BlockSpec((tm, D), lambda i: (i, j))
    vec = pl.BlockSpec((1, D), lambda i: (0, 0))
    return _pcall(
        body, (c, p, p, p, p, dmerged, prm["w_co"], prm["w_so"], prm["la_g"], prm["la_b"], prm["ls_g"], prm["ls_b"],
               prm["sgu_w"], prm["bz"], prm["b_gate"]),
        grid=(T // tm,),
        in_specs=[tile(0), tile(2), tile(3), tile(4), tile(5), tile(0), _const((D, D)), _const((D, D)),
                  _const((1, D)), _const((1, D)), _const((1, D)), _const((1, D)),
                  _const((SGU_GROUPS, LANES, LANES)), _const((LANES, D)), _const((2, D))],
        out_specs=[tile(0), pl.BlockSpec((tm, 4 * D), lambda i: (i, 0)), tile(0), tile(0),
                   pl.BlockSpec((SGU_GROUPS, LANES, LANES), lambda i: (0, 0, 0)),
                   pl.BlockSpec((LANES, D), lambda i: (0, 0)), vec, vec, vec, vec,
                   pl.BlockSpec((2, D), lambda i: (0, 0))],
        out_shape=[_sds((T, D), F32), _sds((T, 4 * D), BF), _sds((T, D), BF), _sds((T, D), BF),
                   _sds((SGU_GROUPS, LANES, LANES), F32), _sds((LANES, D), F32),
                   _sds((1, D), F32), _sds((1, D), F32), _sds((1, D), F32), _sds((1, D), F32), _sds((2, D), F32)],
        scratch_shapes=[pltpu.VMEM((tm, D), F32), pltpu.VMEM((tm, D), BF), pltpu.VMEM((tm, D), BF),
                        pltpu.VMEM((tm, D), BF), pltpu.VMEM((tm, D), F32)],
        name="branch_bwd", vmem=56, comm=comm)


def _conv_bwd(p, dc, conv_w, B, S, comm=None):
    K, D = conv_w.shape
    nc = D // LANES

    def body(av_ref, ag_ref, dc_ref, w_ref, dav_ref, dag_ref, dw_ref, db_ref, apad, dpad):
        b = pl.program_id(1)
        av = av_ref[...]
        sg = _sig(ag_ref[...])
        apad[pl.ds(0, CONV_PAD), :] = jnp.zeros((CONV_PAD, LANES), F32)
        apad[pl.ds(CONV_PAD, S), :] = av * sg
        dpad[pl.ds(S, CONV_PAD), :] = jnp.zeros((CONV_PAD, LANES), F32)
        dpad[pl.ds(0, S), :] = dc_ref[...]

        @pl.when(b == 0)
        def _():
            dw_ref[...] = jnp.zeros_like(dw_ref)
            db_ref[...] = jnp.zeros_like(db_ref)

        db_ref[...] += jnp.sum(dc_ref[...], axis=0, keepdims=True)
        for k in range(K):
            tot = jnp.zeros((1, LANES), F32)
            for r0 in range(0, S, CONV_ROWS):
                tot = tot + jnp.sum(dpad[pl.ds(r0, CONV_ROWS), :] * apad[pl.ds(r0 + k + CONV_PAD - (K - 1), CONV_ROWS), :],
                                    axis=0, keepdims=True)
            dw_ref[pl.ds(k, 1), :] += tot
        for r0 in range(0, S, CONV_ROWS):
            da = jnp.zeros((CONV_ROWS, LANES), F32)
            for k in range(K):
                da = da + w_ref[pl.ds(k, 1), :] * dpad[pl.ds(r0 + (K - 1) - k, CONV_ROWS), :]
            rows = pl.ds(r0, CONV_ROWS)
            s = sg[r0:r0 + CONV_ROWS, :]
            a_v = av[r0:r0 + CONV_ROWS, :]
            dav_ref[rows, :] = (da * s).astype(BF)
            dag_ref[rows, :] = (da * a_v * s * (1.0 - s)).astype(BF)

    blk = lambda off: pl.BlockSpec((S, LANES), lambda j, b: (b, off + j))
    return _pcall(
        body, (p, p, dc, conv_w), grid=(nc, B),
        in_specs=[blk(0), blk(nc), blk(0), pl.BlockSpec((K, LANES), lambda j, b: (0, j))],
        out_specs=[blk(0), blk(0), pl.BlockSpec((K, LANES), lambda j, b: (0, j)), pl.BlockSpec((1, LANES), lambda j, b: (0, j))],
        out_shape=[_sds((B * S, D), BF), _sds((B * S, D), BF), _sds((K, D), F32), _sds((1, D), F32)],
        scratch_shapes=[pltpu.VMEM((S + CONV_PAD, LANES), F32), pltpu.VMEM((S + CONV_PAD, LANES), F32)],
        name="conv_bwd", vmem=32, comm=comm)


TM1 = 512


def _in_proj_bwd(x, dx1, dps, g1, w_in, tm=TM1, comm=None):
    T, D = x.shape
    N = w_in.shape[1]
    widths = [d.shape[1] for d in dps]

    def body(x_ref, dx1_ref, *refs):
        dp_refs, (g_ref, w_ref, dx_ref, dg_ref) = refs[:len(dps)], refs[len(dps):]
        x_t = x_ref[...]
        _, r = _rms(x_t, g_ref[...])
        dh = jnp.zeros((tm, D), F32)
        for q, dp_ref in enumerate(dp_refs):
            cols = pl.ds(sum(widths[:q]), widths[q])
            dh = dh + lax.dot_general(dp_ref[...], w_ref[:, cols], (((1,), (1,)), ((), ())), preferred_element_type=F32)
        dxa, dg = _rms_bwd(x_t, g_ref[...], r, dh)
        dx_ref[...] = dx1_ref[...] + dxa
        _acc(dg_ref, dg, pl.program_id(0) == 0)

    tile = pl.BlockSpec((tm, D), lambda i: (i, 0))
    return _pcall(
        body, (x, dx1, *dps, g1, w_in), grid=(T // tm,),
        in_specs=[tile, tile] + [pl.BlockSpec((tm, w), lambda i: (i, 0)) for w in widths] + [_const((1, D)), _const((D, N))],
        out_specs=[tile, pl.BlockSpec((1, D), lambda i: (0, 0))],
        out_shape=[_sds((T, D), F32), _sds((1, D), F32)],
        name="in_proj_bwd", vmem=48, comm=comm)


def _pick(n, cands):
    for c in cands:
        if n % c == 0:
            return c
    raise ValueError(f"no tile of {cands} divides {n}")


def _mm_tn(x, dys, name):
    T, K = x.shape
    dys = list(dys) if isinstance(dys, (list, tuple)) else [dys]
    widths = [d.shape[1] for d in dys]
    N = sum(widths)
    tm = _pick(T, (1024, 512, 256))
    tk = _pick(K, (1024, 1408, 512))
    tn = _pick(math.gcd(*widths), (1024, 1408, 512))
    nt = T // tm
    first = [sum(widths[:q]) // tn for q in range(len(dys))]
    count = [w // tn for w in widths]

    def body(x_ref, *refs):
        dy_refs, (o_ref, ob_ref, acc) = refs[:len(dys)], refs[len(dys):]
        j, t = pl.program_id(1), pl.program_id(2)

        @pl.when(t == 0)
        def _():
            acc[...] = jnp.zeros_like(acc)

        for q, dy_ref in enumerate(dy_refs):
            @pl.when(jnp.logical_and(j >= first[q], j < first[q] + count[q]))
            def _():
                acc[...] += _dot_tn(x_ref[...], dy_ref[...])

        @pl.when(t == nt - 1)
        def _():
            o_ref[...] = acc[...]
            ob_ref[...] = acc[...].astype(BF)

    def dy_spec(q):
        def index(i, j, t):
            mine = jnp.logical_and(j >= first[q], j < first[q] + count[q])
            return jnp.where(mine, t, 0), jnp.clip(j - first[q], 0, count[q] - 1)
        return pl.BlockSpec((tm, tn), index)

    return pl.pallas_call(
        body, grid=(K // tk, N // tn, nt),
        in_specs=[pl.BlockSpec((tm, tk), lambda i, j, t: (t, i))] + [dy_spec(q) for q in range(len(dys))],
        out_specs=[pl.BlockSpec((tk, tn), lambda i, j, t: (i, j)), pl.BlockSpec((tk, tn), lambda i, j, t: (i, j))],
        out_shape=[_sds((K, N), F32), _sds((K, N), BF)],
        scratch_shapes=[pltpu.VMEM((tk, tn), F32)],
        name=name, compiler_params=_cp(48))(x, *dys)


def _swap_cores(parts):
    n = len(parts)

    def body(*refs):
        ins, outs = refs[:n], refs[n:2 * n]
        send, recv = refs[2 * n:]
        x, y, c, _ = _place()

        def copy(t):
            return pltpu.make_async_remote_copy(
                src_ref=ins[t], dst_ref=outs[t], send_sem=send.at[t], recv_sem=recv.at[t],
                device_id=(x, y, 1 - c), device_id_type=MESH)

        for t in range(n):
            copy(t).start()
        for t in range(n):
            copy(t).wait_recv()
            copy(t).wait_send()

    return pl.pallas_call(
        body, in_specs=[ANY] * n, out_specs=[ANY] * n, out_shape=[_sds(p.shape, p.dtype) for p in parts],
        scratch_shapes=[pltpu.SemaphoreType.DMA((n,)), pltpu.SemaphoreType.DMA((n,))],
        name="swap_cores")(*parts)


def _all_reduce_small(pack):
    R, C = pack.shape

    def body(p_ref, sum_ref, land, send, recv):
        x, y, c, _ = _place()
        me = 4 * x + 2 * y + c
        land[me] = p_ref[...]
        copies = []
        for mask in range(1, N_DEV):
            peer = ((1 - x) if mask & 4 else x, (1 - y) if mask & 2 else y, (1 - c) if mask & 1 else c)
            src = peer[0] * 4 + peer[1] * 2 + peer[2]
            copies.append(pltpu.make_async_remote_copy(
                src_ref=p_ref, dst_ref=land.at[me], send_sem=send.at[mask], recv_sem=recv.at[mask],
                device_id=peer, device_id_type=MESH))
            copies[-1].start()
            copies[-1] = pltpu.make_async_remote_copy(
                src_ref=p_ref, dst_ref=land.at[src], send_sem=send.at[mask], recv_sem=recv.at[mask],
                device_id=peer, device_id_type=MESH)
        for cp in copies:
            cp.wait_recv()
            cp.wait_send()
        tot = land[0]
        for d in range(1, N_DEV):
            tot = tot + land[d]
        sum_ref[...] = tot

    vm = pl.BlockSpec(memory_space=pltpu.VMEM)
    return pl.pallas_call(
        body, in_specs=[vm], out_specs=vm, out_shape=_sds((R, C), F32),
        scratch_shapes=[pltpu.VMEM((N_DEV, R, C), F32), pltpu.SemaphoreType.DMA((N_DEV,)), pltpu.SemaphoreType.DMA((N_DEV,))],
        name="all_reduce_small", compiler_params=_cp(32))(pack)


def _row_tile(R):
    return _pick(R, (128, 64, 32, 16, 8)) if R % 8 == 0 else R


def _sum_landed(own, land, name):
    R, C = own.shape
    tr = _row_tile(R)

    def body(o_ref, l_ref, s_ref):
        s_ref[...] = ((o_ref[...] + l_ref[0].astype(F32)) + l_ref[1].astype(F32)) + l_ref[2].astype(F32)

    return pl.pallas_call(
        body, grid=(R // tr,),
        in_specs=[pl.BlockSpec((tr, C), lambda i: (i, 0)), pl.BlockSpec((3, tr, C), lambda i: (0, i, 0))],
        out_specs=pl.BlockSpec((tr, C), lambda i: (i, 0)), out_shape=_sds((R, C), F32),
        name=name, compiler_params=_cp(32))(own, land)


def _adamw(g, w, m, v):
    m = ADAM_B1 * m + (1.0 - ADAM_B1) * g
    v = ADAM_B2 * v + (1.0 - ADAM_B2) * (g * g)
    m_hat = m / (1.0 - ADAM_B1 ** ADAM_STEP)
    v_hat = v / (1.0 - ADAM_B2 ** ADAM_STEP)
    return -ADAM_LR * (m_hat / (jnp.sqrt(v_hat) + ADAM_EPS) + ADAM_WD * w), m, v


def _update(parts, w, m, v, name):
    R, C = w.shape
    tr = _row_tile(R)
    k = len(parts)

    def body(*refs):
        g = refs[0][...]
        for r in refs[1:k]:
            g = g + r[...]
        w_ref, m_ref, v_ref, g_out, d_out, m_out, v_out = refs[k:]
        d, m_new, v_new = _adamw(g, w_ref[...], m_ref[...], v_ref[...])
        g_out[...] = g
        d_out[...] = d
        m_out[...] = m_new
        v_out[...] = v_new

    blk = pl.BlockSpec((tr, C), lambda i: (i, 0))
    return pl.pallas_call(
        body, grid=(R // tr,), in_specs=[blk] * (k + 3), out_specs=[blk] * 4, out_shape=[_sds((R, C), F32)] * 4,
        name=name, compiler_params=_cp(40))(*parts, w, m, v)


BIG = ("w_in", "w_conv_out", "w_sgu_out", "w_mix_out", "w_q", "w_kv", "w_xo", "w_gu", "w_down")
BIG_KIND = {"w_in": "col", "w_conv_out": "row", "w_sgu_out": "row", "w_mix_out": "row", "w_q": "row",
            "w_kv": "col", "w_xo": "row", "w_gu": "col", "w_down": "row"}
VECS = ("norm_mix", "conv_b", "conv_ln_g", "conv_ln_b", "sgu_ln_g", "sgu_ln_b", "norm_xattn", "norm_mem", "norm_ffn",
        "norm_final")


def _step(a):
    x3d, mem3d, tgt3d = a["x"], a["mem"], a["loss_target"]
    B, S, D = x3d.shape
    M = mem3d.shape[1]
    T = B * S
    x = x3d.reshape(T, D)
    mem = mem3d.reshape(B * M, D)
    tgt = tgt3d.reshape(T, D)
    xi, yi = lax.axis_index("x"), lax.axis_index("y")
    chip = 2 * xi + yi

    def gather(names):
        return _Gather([a[nm][0] if nm in ("b_gate", "conv_w") else a[nm][0].astype(BF) for nm in names],
                       [BIG_KIND.get(nm, "col") for nm in names])

    first = ("w_in", "b_gate", "conv_w")
    on_in_proj = ("w_conv_out", "w_sgu_out", "w_kv", "w_mix_out", "w_q", "w_xo")
    full = dict(zip(first, _comm_call(gather(first), "gather_w_in")))
    (p, h1), got = _in_proj(x, a["norm_mix"], full["w_in"], comm=gather(on_in_proj))
    full.update(zip(on_in_proj, got))
    (c,), got = _conv_fwd(p, full["conv_w"], a["conv_b"], B, S, comm=gather(("w_down",)))
    full["w_down"] = got[0]

    sgu_b = a["sgu_b"][0]
    bz = jnp.repeat(jnp.transpose(sgu_b), LANES, axis=1)
    prm = dict(w_co=full["w_conv_out"], w_so=full["w_sgu_out"], w_mo=full["w_mix_out"], w_q=full["w_q"],
               w_xo=full["w_xo"], w_down=full["w_down"],
               la_g=a["conv_ln_g"], la_b=a["conv_ln_b"], ls_g=a["sgu_ln_g"], ls_b=a["sgu_ln_b"],
               sgu_w=a["sgu_w"][0], bz=bz, b_gate=full["b_gate"], g2=a["norm_xattn"], g3=a["norm_ffn"],
               gf=a["norm_final"].reshape(1, D))

    (merged, s_a, sg), got = _branch_fwd(c, p, prm, comm=gather(("w_gu",)))
    prm["w_gu"] = got[0]
    mem_n, kv = _kv_fwd(mem, a["norm_mem"], full["w_kv"], B, M)
    x1, x2, h2, o = _attn_fwd(x, merged, kv, prm, S, M)
    dx2, dx3, dgu, h3, f, lsum, d_g3, d_gf = _ffn_loss(x2, tgt, prm)
    loss = lax.psum(0.5 * jnp.sum(lsum) / D, ("x", "y", "c"))

    size_of = {nm: a[nm].shape[1] if BIG_KIND[nm] == "row" else a[nm].shape[2] for nm in BIG}
    landed = {}

    def scatter(names):
        return _Scatter([gw[nm][1] for nm in names], [BIG_KIND[nm] for nm in names], [size_of[nm] for nm in names])

    gw = {}
    gw["w_down"] = _mm_tn(f, dx3, "dw_down")
    gw["w_gu"] = _mm_tn(h3, dgu, "dw_gu")
    (dx1, dmerged, dq, dkv, d_g2), got = _attn_bwd(x1, kv, dx2, prm, S, M, comm=scatter(("w_gu", "w_down")))
    landed.update(zip(("w_gu", "w_down"), got))
    gw["w_xo"] = _mm_tn(o, dx2, "dw_xo")
    gw["w_q"] = _mm_tn(h2, dq, "dw_q")
    gw["w_kv"] = _mm_tn(mem_n, dkv, "dw_kv")
    d_gm = _kv_bwd(mem, a["norm_mem"], full["w_kv"], dkv, B, M)
    gw["w_mix_out"] = _mm_tn(merged, dx1, "dw_mix_out")
    group = ("w_xo", "w_q", "w_kv", "w_mix_out")
    (dc, dpb, dya, dyb, d_wm, d_bz, d_lag, d_lab, d_lsg, d_lsb, d_bg), got = _branch_bwd(
        c, p, dmerged, prm, comm=scatter(group))
    landed.update(zip(group, got))
    gw["w_conv_out"] = _mm_tn(s_a, dya, "dw_conv_out")
    gw["w_sgu_out"] = _mm_tn(sg, dyb, "dw_sgu_out")
    group = ("w_conv_out", "w_sgu_out")
    (dav, dag, d_cw, d_cb), got = _conv_bwd(p, dc, full["conv_w"], B, S, comm=scatter(group))
    landed.update(zip(group, got))
    dp = [dav, dag, dpb]
    gw["w_in"] = _mm_tn(h1, dp, "dw_in")
    (grad_x, d_g1), got = _in_proj_bwd(x, dx1, dp, a["norm_mix"], full["w_in"], comm=scatter(("w_in",)))
    landed["w_in"] = got[0]

    sizes = [size_of[nm] for nm in BIG]
    part = []
    for nm, sz, ld in zip(BIG, sizes, [landed[nm] for nm in BIG]):
        g32 = gw[nm][0]
        own = (lax.dynamic_slice_in_dim(g32, chip * sz, sz, axis=0) if BIG_KIND[nm] == "row"
               else lax.dynamic_slice_in_dim(g32, chip * sz, sz, axis=1))
        part.append(_sum_landed(own, ld, "sum_" + nm))
    other = _swap_cores(part)
    out = {}
    for nm, mine, theirs in zip(BIG, part, other):
        out[nm] = _update([mine, theirs], a[nm][0], a["m_" + nm][0], a["v_" + nm][0], "upd_" + nm)

    G = SGU_GROUPS
    d_sb = jnp.transpose(d_bz.reshape(LANES, G, LANES).sum(axis=-1))
    vec_g = dict(norm_mix=d_g1, conv_b=d_cb, conv_ln_g=d_lag, conv_ln_b=d_lab, sgu_ln_g=d_lsg, sgu_ln_b=d_lsb,
                 norm_xattn=d_g2, norm_mem=d_gm, norm_ffn=d_g3, norm_final=d_gf)
    rows = [vec_g[nm] for nm in VECS] + [d_sb.reshape(1, D), d_bg, d_cw, d_wm.reshape(G * LANES * LANES // D, D)]
    n_rows = sum(r.shape[0] for r in rows)
    pad = (-n_rows) % 8
    pack = jnp.concatenate(rows + [jnp.zeros((pad, D), F32)], axis=0)
    tot = _all_reduce_small(pack)

    def small(nm):
        arr = a[nm]
        return arr.reshape(-1, D) if nm != "sgu_b" else arr.reshape(1, D)

    nv = len(VECS)
    K = a["conv_w"].shape[1]
    rep_names = list(VECS) + ["sgu_b", "sgu_w"]
    rep_g = jnp.concatenate([tot[:nv + 1], tot[nv + 3 + K:n_rows]], axis=0)
    rep = [jnp.concatenate([small(pre + nm) for nm in rep_names], axis=0) for pre in ("", "m_", "v_")]
    rep_out = _update([rep_g], rep[0], rep[1], rep[2], "upd_replicated")
    Dq = D // N_CHIPS
    col_g = lax.dynamic_slice_in_dim(tot[nv + 1:nv + 3 + K], chip * Dq, Dq, axis=1)
    col = [jnp.concatenate([a[pre + "b_gate"][0], a[pre + "conv_w"][0]], axis=0) for pre in ("", "m_", "v_")]
    col_out = _update([col_g], col[0], col[1], col[2], "upd_columns")

    for q in range(4):
        for i, nm in enumerate(VECS):
            out.setdefault(nm, [None] * 4)[q] = rep_out[q][i:i + 1].reshape(a[nm].shape)
        out.setdefault("sgu_b", [None] * 4)[q] = rep_out[q][nv:nv + 1].reshape(a["sgu_b"].shape)
        out.setdefault("sgu_w", [None] * 4)[q] = rep_out[q][nv + 1:].reshape(a["sgu_w"].shape)
        out.setdefault("b_gate", [None] * 4)[q] = col_out[q][0:2][None]
        out.setdefault("conv_w", [None] * 4)[q] = col_out[q][2:][None]
    for nm in BIG:
        out[nm] = [o_[None] for o_ in out[nm]]
    return loss, grad_x.reshape(B, S, D), out


WEIGHTS = ("norm_mix", "w_in", "b_gate", "conv_w", "conv_b", "conv_ln_g", "conv_ln_b", "w_conv_out", "sgu_ln_g",
           "sgu_ln_b", "sgu_w", "sgu_b", "w_sgu_out", "w_mix_out", "norm_xattn", "norm_mem", "w_q", "w_kv", "w_xo",
           "norm_ffn", "w_gu", "w_down", "norm_final")


def kernel(x, mem, norm_mix, w_in, b_gate, conv_w, conv_b, conv_ln_g, conv_ln_b, w_conv_out, sgu_ln_g, sgu_ln_b, sgu_w, sgu_b, w_sgu_out, w_mix_out, norm_xattn, norm_mem, w_q, w_kv, w_xo, norm_ffn, w_gu, w_down, norm_final, loss_target, m_norm_mix, m_w_in, m_b_gate, m_conv_w, m_conv_b, m_conv_ln_g, m_conv_ln_b, m_w_conv_out, m_sgu_ln_g, m_sgu_ln_b, m_sgu_w, m_sgu_b, m_w_sgu_out, m_w_mix_out, m_norm_xattn, m_norm_mem, m_w_q, m_w_kv, m_w_xo, m_norm_ffn, m_w_gu, m_w_down, m_norm_final, v_norm_mix, v_w_in, v_b_gate, v_conv_w, v_conv_b, v_conv_ln_g, v_conv_ln_b, v_w_conv_out, v_sgu_ln_g, v_sgu_ln_b, v_sgu_w, v_sgu_b, v_w_sgu_out, v_w_mix_out, v_norm_xattn, v_norm_mem, v_w_q, v_w_kv, v_w_xo, v_norm_ffn, v_w_gu, v_w_down, v_norm_final):
    a = dict(locals())
    loss, grad_x, out = _step(a)
    res = [loss, grad_x]
    for q in range(4):
        res += [out[nm][q] for nm in WEIGHTS]
    return tuple(res)
```

```python
import functools
import math

import jax
import jax.numpy as jnp
from jax import lax
from jax.experimental import pallas as pl
from jax.experimental.pallas import tpu as pltpu

BF = jnp.bfloat16
F32 = jnp.float32
MESH = pl.DeviceIdType.MESH
ANY = pl.BlockSpec(memory_space=pl.ANY)

RMS_EPS = 1e-6
LN_EPS = 1e-5
HEADS = 4
SGU_GROUPS = 8
LANES = 128
ADAM_LR = 0.001
ADAM_B1 = 0.9
ADAM_B2 = 0.999
ADAM_EPS = 1e-08
ADAM_WD = 0.01
ADAM_STEP = 10
N_CHIPS = 4
N_DEV = 8
MIB = 1024 * 1024


def _sds(shape, dtype):
    return jax.ShapeDtypeStruct(tuple(shape), dtype)


def _cp(vmem_mib):
    return pltpu.CompilerParams(vmem_limit_bytes=vmem_mib * MIB)


def _const(shape):
    nd = len(shape)
    return pl.BlockSpec(tuple(shape), lambda *_: (0,) * nd, pipeline_mode=pl.Buffered(1))


def _dot(a, b):
    return jnp.dot(a.astype(BF), b.astype(BF), preferred_element_type=F32)


def _dot_nt(a, b):
    return lax.dot_general(a.astype(BF), b.astype(BF), (((1,), (1,)), ((), ())), preferred_element_type=F32)


def _dot_tn(a, b):
    return lax.dot_general(a.astype(BF), b.astype(BF), (((0,), (0,)), ((), ())), preferred_element_type=F32)


def _sig(x):
    return 1.0 / (1.0 + jnp.exp(-x))


def _dsilu(x, s):
    return s * (1.0 + x * (1.0 - s))


_GELU_C = math.sqrt(2.0 / math.pi)


def _gelu(x):
    t = jnp.tanh(_GELU_C * (x + 0.044715 * (x * x * x)))
    return 0.5 * x * (1.0 + t), t


def _dgelu(x, t):
    return 0.5 * (1.0 + t) + 0.5 * x * (1.0 - t * t) * (_GELU_C * (1.0 + 3.0 * 0.044715 * (x * x)))


def _rms(x, g):
    r = lax.rsqrt(jnp.mean(x * x, axis=-1, keepdims=True) + RMS_EPS)
    return x * r * g, r


def _rms_bwd(x, g, r, dh):
    xr = x * r
    dxh = dh * g
    dx = r * (dxh - xr * jnp.mean(dxh * xr, axis=-1, keepdims=True))
    return dx, jnp.sum(dh * xr, axis=0, keepdims=True)


def _ln(x, g, b):
    mu = jnp.mean(x, axis=-1, keepdims=True)
    xc = x - mu
    rstd = lax.rsqrt(jnp.mean(xc * xc, axis=-1, keepdims=True) + LN_EPS)
    xh = xc * rstd
    return xh * g + b, xh, rstd


def _ln_bwd(xh, rstd, g, dy):
    dxh = dy * g
    dx = rstd * (dxh - jnp.mean(dxh, axis=-1, keepdims=True) - xh * jnp.mean(dxh * xh, axis=-1, keepdims=True))
    return dx, jnp.sum(dy * xh, axis=0, keepdims=True), jnp.sum(dy, axis=0, keepdims=True)


def _acc(ref, val, first):
    @pl.when(first)
    def _():
        ref[...] = val

    @pl.when(jnp.logical_not(first))
    def _():
        ref[...] += val


def _place():
    x, y, c = lax.axis_index("x"), lax.axis_index("y"), lax.axis_index("c")
    chips = [(1 - x, y), (x, 1 - y), (1 - x, 1 - y)]
    return x, y, c, chips


def _shard_of(ref, kind, k, n):
    if kind == "row":
        return ref.at[pl.ds(k * n, n), :]
    return ref.at[:, pl.ds(k * n, n)]


class _Gather:
    def __init__(self, shards, kinds):
        n = len(shards)
        self.srcs, self.kinds = list(shards), list(kinds)
        self.sizes = [s.shape[0] if kd == "row" else s.shape[1] for s, kd in zip(shards, kinds)]
        self.halves = [s.shape[0] // 2 if s.shape[0] % 32 == 0 else None for s in shards]
        self.out_shape = [
            _sds((s.shape[0] * N_CHIPS, s.shape[1]) if kd == "row" else (s.shape[0], s.shape[1] * N_CHIPS), s.dtype)
            for s, kd in zip(shards, kinds)]
        dma = pltpu.SemaphoreType.DMA
        self.sems = [dma((3 * n,)), dma((3 * n,)), dma((n,)), dma((3 * n,)), dma((3 * n,))]

    def _part(self, ref, t, core):
        h = self.halves[t]
        return ref if h is None else ref.at[pl.ds(core * h, h), :]

    def _copies(self, ins, outs, send, recv, loc, fsend, frecv):
        x, y, c, chips = _place()
        k = 2 * x + y
        local, remote = [], []
        for t in range(len(ins)):
            block = lambda q: _shard_of(outs[t], self.kinds[t], q, self.sizes[t])
            local.append(pltpu.make_async_copy(ins[t], block(k), loc.at[t]))
            for j, (px, py) in enumerate(chips):
                sems = dict(send_sem=send.at[3 * t + j], recv_sem=recv.at[3 * t + j])
                there = dict(device_id=(px, py, c), device_id_type=MESH)
                sent = pltpu.make_async_remote_copy(
                    src_ref=self._part(ins[t], t, c), dst_ref=self._part(block(k), t, c), **sems, **there)
                got = self._part(block(2 * px + py), t, c)
                landed = pltpu.make_async_remote_copy(src_ref=self._part(ins[t], t, c), dst_ref=got, **sems, **there)
                passed = handed = None
                if self.halves[t] is not None:
                    fsems = dict(send_sem=fsend.at[3 * t + j], recv_sem=frecv.at[3 * t + j])
                    sibling = dict(device_id=(x, y, 1 - c), device_id_type=MESH)
                    passed = pltpu.make_async_remote_copy(src_ref=got, dst_ref=got, **fsems, **sibling)
                    other = self._part(block(2 * px + py), t, 1 - c)
                    handed = pltpu.make_async_remote_copy(src_ref=got, dst_ref=other, **fsems, **sibling)
                remote.append((sent, landed, passed, handed))
        return local, remote

    def start(self, ins, outs, *sems):
        local, remote = self._copies(ins, outs, *sems)
        for cp in local:
            cp.start()
        for sent, _, _, _ in remote:
            sent.start()

    def wait(self, ins, outs, *sems):
        local, remote = self._copies(ins, outs, *sems)
        for sent, landed, passed, handed in remote:
            landed.wait_recv()
            if passed is not None:
                passed.start()
        for sent, landed, passed, handed in remote:
            if passed is not None:
                handed.wait_recv()
                passed.wait_send()
            sent.wait_send()
        for cp in local:
            cp.wait()


class _Scatter:
    def __init__(self, grads, kinds, sizes):
        n = len(grads)
        self.srcs, self.kinds, self.sizes = list(grads), list(kinds), list(sizes)
        self.out_shape = [_sds((3,) + ((sz, g.shape[1]) if kd == "row" else (g.shape[0], sz)), g.dtype)
                          for g, kd, sz in zip(grads, kinds, sizes)]
        self.sems = [pltpu.SemaphoreType.DMA((3 * n,)), pltpu.SemaphoreType.DMA((3 * n,))]

    def _copies(self, ins, outs, send, recv):
        x, y, c, chips = _place()
        return [pltpu.make_async_remote_copy(
            src_ref=_shard_of(ins[t], self.kinds[t], 2 * px + py, self.sizes[t]), dst_ref=outs[t].at[j],
            send_sem=send.at[3 * t + j], recv_sem=recv.at[3 * t + j], device_id=(px, py, c), device_id_type=MESH)
            for t in range(len(ins)) for j, (px, py) in enumerate(chips)]

    def start(self, ins, outs, send, recv):
        for cp in self._copies(ins, outs, send, recv):
            cp.start()

    def wait(self, ins, outs, send, recv):
        for cp in self._copies(ins, outs, send, recv):
            cp.wait_recv()
            cp.wait_send()


def _comm_call(comm, name):
    n, m = len(comm.srcs), len(comm.out_shape)

    def body(*refs):
        comm.start(refs[:n], refs[n:n + m], *refs[n + m:])
        comm.wait(refs[:n], refs[n:n + m], *refs[n + m:])

    return pl.pallas_call(body, in_specs=[ANY] * n, out_specs=[ANY] * m, out_shape=comm.out_shape,
                          scratch_shapes=comm.sems, name=name)(*comm.srcs)


def _pcall(body, args, *, grid, in_specs, out_specs, out_shape, name, vmem, scratch_shapes=(), comm=None):
    in_specs, out_specs, out_shape = list(in_specs), list(out_specs), list(out_shape)
    scratch_shapes = list(scratch_shapes)
    if comm is None:
        res = pl.pallas_call(body, grid=grid, in_specs=in_specs, out_specs=out_specs, out_shape=out_shape,
                             scratch_shapes=scratch_shapes, name=name, compiler_params=_cp(vmem))(*args)
        return list(res), []
    ni, no, ns = len(in_specs), len(out_specs), len(scratch_shapes)
    ci, co = len(comm.srcs), len(comm.out_shape)

    def carried(*refs):
        c_in = refs[ni:ni + ci]
        c_out = refs[ni + ci + no:ni + ci + no + co]
        sems = refs[ni + ci + no + co + ns:]
        ids = [pl.program_id(d) for d in range(len(grid))]
        first = functools.reduce(jnp.logical_and, [i == 0 for i in ids])
        last = functools.reduce(jnp.logical_and, [i == g - 1 for i, g in zip(ids, grid)])

        @pl.when(first)
        def _():
            comm.start(c_in, c_out, *sems)

        body(*refs[:ni], *refs[ni + ci:ni + ci + no], *refs[ni + ci + no + co:ni + ci + no + co + ns])

        @pl.when(last)
        def _():
            comm.wait(c_in, c_out, *sems)

    res = pl.pallas_call(carried, grid=grid, in_specs=in_specs + [ANY] * ci, out_specs=out_specs + [ANY] * co,
                         out_shape=out_shape + list(comm.out_shape), scratch_shapes=scratch_shapes + list(comm.sems),
                         name=name, compiler_params=_cp(vmem))(*args, *comm.srcs)
    return list(res[:no]), list(res[no:])


def _in_proj(x, g1, w_in, comm=None):
    T, D = x.shape
    N = w_in.shape[1]
    tm, tn = 512, 1024

    def body(x_ref, g_ref, w_ref, p_ref, h_ref):
        h, _ = _rms(x_ref[...], g_ref[...])
        h_ref[...] = h.astype(BF)
        for j in range(N // tn):
            cols = pl.ds(j * tn, tn)
            p_ref[:, cols] = jnp.dot(h_ref[...], w_ref[:, cols], preferred_element_type=F32)

    return _pcall(
        body, (x, g1, w_in), grid=(T // tm,),
        in_specs=[pl.BlockSpec((tm, D), lambda i: (i, 0)), _const((1, D)), _const((D, N))],
        out_specs=[pl.BlockSpec((tm, N), lambda i: (i, 0)), pl.BlockSpec((tm, D), lambda i: (i, 0))],
        out_shape=[_sds((T, N), F32), _sds((T, D), BF)],
        name="in_proj", vmem=56, comm=comm)


CONV_PAD = 32
CONV_ROWS = 256


def _conv_fwd(p, conv_w, conv_b, B, S, comm=None):
    K, D = conv_w.shape
    nc = D // LANES

    def body(av_ref, ag_ref, w_ref, b_ref, c_ref, apad):
        apad[pl.ds(0, CONV_PAD), :] = jnp.zeros((CONV_PAD, LANES), F32)
        apad[pl.ds(CONV_PAD, S), :] = av_ref[...] * _sig(ag_ref[...])
        for r0 in range(0, S, CONV_ROWS):
            acc = jnp.zeros((CONV_ROWS, LANES), F32) + b_ref[...]
            for k in range(K):
                acc = acc + w_ref[pl.ds(k, 1), :] * apad[pl.ds(r0 + k + CONV_PAD - (K - 1), CONV_ROWS), :]
            c_ref[pl.ds(r0, CONV_ROWS), :] = acc

    return _pcall(
        body, (p, p, conv_w, conv_b), grid=(B, nc),
        in_specs=[pl.BlockSpec((S, LANES), lambda b, j: (b, j)), pl.BlockSpec((S, LANES), lambda b, j: (b, nc + j)),
                  pl.BlockSpec((K, LANES), lambda b, j: (0, j)), pl.BlockSpec((1, LANES), lambda b, j: (0, j))],
        out_specs=[pl.BlockSpec((S, LANES), lambda b, j: (b, j))],
        out_shape=[_sds((B * S, D), F32)],
        scratch_shapes=[pltpu.VMEM((S + CONV_PAD, LANES), F32)],
        name="conv_fwd", vmem=32, comm=comm)


def _tril_mask():
    t = lax.broadcasted_iota(jnp.int32, (LANES, LANES), 0)
    s = lax.broadcasted_iota(jnp.int32, (LANES, LANES), 1)
    return t >= s


def _branch_a(c, g, b):
    ln_a, xh, rstd = _ln(c, g, b)
    s = _sig(ln_a)
    return ln_a * s, ln_a, s, xh, rstd


def _branch_b(bu, bv, g, b, wm_ref, bz_ref, z_scr, v_scr):
    tm, D = bu.shape
    u, tu = _gelu(bu)
    gv, tv = _gelu(bv)
    v, vh, rstd = _ln(gv, g, b)
    v_scr[...] = v.astype(BF)
    mask = _tril_mask()
    for gi in range(SGU_GROUPS):
        wm = jnp.where(mask, wm_ref[gi], 0.0).astype(BF)
        cols = pl.ds(gi * LANES, LANES)
        for n in range(tm // LANES):
            rows = pl.ds(n * LANES, LANES)
            z_scr[rows, cols] = jnp.dot(wm, v_scr[rows, cols], preferred_element_type=F32) + bz_ref[:, cols]
    z = z_scr[...]
    return u * z, u, tu, z, tv, vh, rstd


TM3 = 256
TM3_FWD = 512


def _branch_fwd(c, p, prm, tm=TM3_FWD, comm=None):
    T, D = c.shape

    def body(c_ref, bu_ref, bv_ref, ga_ref, gb_ref, wco_ref, wso_ref, lag_ref, lab_ref, lsg_ref, lsb_ref, wm_ref,
             bz_ref, bg_ref, mg_ref, sa_ref, sg_ref, z_scr, v_scr):
        s_a = _branch_a(c_ref[...], lag_ref[...], lab_ref[...])[0]
        sa_ref[...] = s_a.astype(BF)
        y_a = jnp.dot(sa_ref[...], wco_ref[...], preferred_element_type=F32)
        sg = _branch_b(bu_ref[...], bv_ref[...], lsg_ref[...], lsb_ref[...], wm_ref, bz_ref, z_scr, v_scr)[0]
        sg_ref[...] = sg.astype(BF)
        y_b = jnp.dot(sg_ref[...], wso_ref[...], preferred_element_type=F32)
        ga = _sig(ga_ref[...] + bg_ref[pl.ds(0, 1), :])
        gb = _sig(gb_ref[...] + bg_ref[pl.ds(1, 1), :])
        mg_ref[...] = (ga * y_a + gb * y_b).astype(BF)

    tile = lambda j: pl.BlockSpec((tm, D), lambda i: (i, j))
    return _pcall(
        body, (c, p, p, p, p, prm["w_co"], prm["w_so"], prm["la_g"], prm["la_b"], prm["ls_g"], prm["ls_b"],
               prm["sgu_w"], prm["bz"], prm["b_gate"]),
        grid=(T // tm,),
        in_specs=[tile(0), tile(2), tile(3), tile(4), tile(5), _const((D, D)), _const((D, D)),
                  _const((1, D)), _const((1, D)), _const((1, D)), _const((1, D)),
                  _const((SGU_GROUPS, LANES, LANES)), _const((LANES, D)), _const((2, D))],
        out_specs=[tile(0), tile(0), tile(0)],
        out_shape=[_sds((T, D), BF)] * 3,
        scratch_shapes=[pltpu.VMEM((tm, D), F32), pltpu.VMEM((tm, D), BF)],
        name="branch_fwd", vmem=48, comm=comm)


def _kv_fwd(mem, gm, w_kv, B, M):
    D = mem.shape[1]
    N = w_kv.shape[1]

    def body(m_ref, g_ref, w_ref, mn_ref, kv_ref):
        h, _ = _rms(m_ref[...], g_ref[...])
        mn_ref[...] = h.astype(BF)
        kv_ref[...] = jnp.dot(mn_ref[...], w_ref[...], preferred_element_type=F32).astype(BF)

    return pl.pallas_call(
        body, grid=(B,),
        in_specs=[pl.BlockSpec((M, D), lambda b: (b, 0)), _const((1, D)), _const((D, N))],
        out_specs=[pl.BlockSpec((M, D), lambda b: (b, 0)), pl.BlockSpec((M, N), lambda b: (b, 0))],
        out_shape=[_sds((B * M, D), BF), _sds((B * M, N), BF)],
        name="kv_fwd", compiler_params=_cp(32))(mem, gm, w_kv)


def _softmax_rows(s):
    e = jnp.exp(s - jnp.max(s, axis=-1, keepdims=True))
    return e / jnp.sum(e, axis=-1, keepdims=True)


TM4 = 512


def _attn_fwd(x, merged, kv, prm, S, M, tm=TM4):
    T, D = x.shape
    hd = D // HEADS
    scale = hd ** -0.5
    tpb = S // tm

    def body(x_ref, mg_ref, kv_ref, wmo_ref, wq_ref, wxo_ref, g_ref, x1_ref, x2_ref, h2_ref, o_ref):
        x1 = x_ref[...] + jnp.dot(mg_ref[...], wmo_ref[...], preferred_element_type=F32)
        x1_ref[...] = x1
        h2, _ = _rms(x1, g_ref[...])
        h2_ref[...] = h2.astype(BF)
        qb = jnp.dot(h2_ref[...], wq_ref[...], preferred_element_type=F32).astype(BF)
        for h in range(HEADS):
            cs = pl.ds(h * hd, hd)
            s = _dot_nt(qb[:, h * hd:(h + 1) * hd], kv_ref[:, cs]) * scale
            pr = _softmax_rows(s)
            o_ref[:, cs] = _dot(pr, kv_ref[:, pl.ds(D + h * hd, hd)]).astype(BF)
        x2_ref[...] = x1 + jnp.dot(o_ref[...], wxo_ref[...], preferred_element_type=F32)

    tile = pl.BlockSpec((tm, D), lambda i: (i, 0))
    return pl.pallas_call(
        body, grid=(T // tm,),
        in_specs=[tile, tile, pl.BlockSpec((M, 2 * D), lambda i: (i // tpb, 0)),
                  _const((D, D)), _const((D, D)), _const((D, D)), _const((1, D))],
        out_specs=[tile, tile, tile, tile],
        out_shape=[_sds((T, D), F32), _sds((T, D), F32), _sds((T, D), BF), _sds((T, D), BF)],
        name="attn_fwd", compiler_params=_cp(40))(x, merged, kv, prm["w_mo"], prm["w_q"], prm["w_xo"], prm["g2"])


TM5 = 256


def _ffn_loss(x2, tgt, prm, tm=TM5):
    T, D = x2.shape
    F = prm["w_down"].shape[0]
    FC = F // 2

    def body(x2_ref, t_ref, wgu_ref, wd_ref, g3_ref, gf_ref, dx2_ref, dx3_ref, dgu_ref, h3_ref, f_ref, ls_ref,
             dg3_ref, dgf_ref, gu_scr):
        first = pl.program_id(0) == 0
        x2 = x2_ref[...]
        h3, r3 = _rms(x2, g3_ref[...])
        h3_ref[...] = h3.astype(BF)
        x3 = x2
        for ch in range(2):
            gc, uc = pl.ds(ch * FC, FC), pl.ds(F + ch * FC, FC)
            gt = jnp.dot(h3_ref[...], wgu_ref[:, gc], preferred_element_type=F32)
            up = jnp.dot(h3_ref[...], wgu_ref[:, uc], preferred_element_type=F32)
            gu_scr[:, gc] = gt
            gu_scr[:, uc] = up
            f_ref[:, gc] = (gt * _sig(gt) * up).astype(BF)
            x3 = x3 + jnp.dot(f_ref[:, gc], wd_ref[gc, :], preferred_element_type=F32)
        y, rf = _rms(x3, gf_ref[...])
        e = y - t_ref[...]
        _acc(ls_ref, jnp.sum(e * e, axis=0, keepdims=True), first)
        dx3, dgf = _rms_bwd(x3, gf_ref[...], rf, e * (1.0 / D))
        _acc(dgf_ref, dgf, first)
        dx3_ref[...] = dx3.astype(BF)
        dh3 = jnp.zeros((tm, D), F32)
        for ch in range(2):
            gc, uc = pl.ds(ch * FC, FC), pl.ds(F + ch * FC, FC)
            df = lax.dot_general(dx3_ref[...], wd_ref[gc, :], (((1,), (1,)), ((), ())), preferred_element_type=F32)
            gt, up = gu_scr[:, gc], gu_scr[:, uc]
            s = _sig(gt)
            dgu_ref[:, gc] = (df * up * _dsilu(gt, s)).astype(BF)
            dgu_ref[:, uc] = (df * gt * s).astype(BF)
            dh3 = dh3 + lax.dot_general(dgu_ref[:, gc], wgu_ref[:, gc], (((1,), (1,)), ((), ())), preferred_element_type=F32)
            dh3 = dh3 + lax.dot_general(dgu_ref[:, uc], wgu_ref[:, uc], (((1,), (1,)), ((), ())), preferred_element_type=F32)
        dxa, dg3 = _rms_bwd(x2, g3_ref[...], r3, dh3)
        _acc(dg3_ref, dg3, first)
        dx2_ref[...] = dx3 + dxa

    tile = lambda n: pl.BlockSpec((tm, n), lambda i: (i, 0))
    vec = pl.BlockSpec((1, D), lambda i: (0, 0))
    return pl.pallas_call(
        body, grid=(T // tm,),
        in_specs=[tile(D), tile(D), _const((D, 2 * F)), _const((F, D)), _const((1, D)), _const((1, D))],
        out_specs=[tile(D), tile(D), tile(2 * F), tile(D), tile(F), vec, vec, vec],
        out_shape=[_sds((T, D), F32), _sds((T, D), BF), _sds((T, 2 * F), BF), _sds((T, D), BF), _sds((T, F), BF),
                   _sds((1, D), F32), _sds((1, D), F32), _sds((1, D), F32)],
        scratch_shapes=[pltpu.VMEM((tm, 2 * F), F32)],
        name="ffn_loss", compiler_params=_cp(56))(x2, tgt, prm["w_gu"], prm["w_down"], prm["g3"], prm["gf"])


def _attn_bwd(x1, kv, dx2, prm, S, M, tm=TM4, comm=None):
    T, D = x1.shape
    hd = D // HEADS
    scale = hd ** -0.5
    tpb = S // tm

    def body(x1_ref, kv_ref, dx2_ref, wmo_ref, wq_ref, wxo_ref, g_ref, dx1_ref, dmg_ref, dq_ref, dkv_ref, dg_ref,
             h2_scr, do_scr):
        i = pl.program_id(0)
        x1 = x1_ref[...]
        dx2 = dx2_ref[...]
        h2, r2 = _rms(x1, g_ref[...])
        h2_scr[...] = h2.astype(BF)
        qb = jnp.dot(h2_scr[...], wq_ref[...], preferred_element_type=F32).astype(BF)
        do_scr[...] = _dot_nt(dx2, wxo_ref[...]).astype(BF)
        for h in range(HEADS):
            cs, vs = pl.ds(h * hd, hd), pl.ds(D + h * hd, hd)
            qh = qb[:, h * hd:(h + 1) * hd]
            pr = _softmax_rows(_dot_nt(qh, kv_ref[:, cs]) * scale)
            dpr = _dot_nt(do_scr[:, cs], kv_ref[:, vs])
            dv = _dot_tn(pr, do_scr[:, cs])
            ds = (pr * (dpr - jnp.sum(dpr * pr, axis=-1, keepdims=True)) * scale).astype(BF)
            dq_ref[:, cs] = jnp.dot(ds, kv_ref[:, cs], preferred_element_type=F32).astype(BF)
            dk = _dot_tn(ds, qh)

            @pl.when(i % tpb == 0)
            def _():
                dkv_ref[:, cs] = dk
                dkv_ref[:, vs] = dv

            @pl.when(i % tpb != 0)
            def _():
                dkv_ref[:, cs] += dk
                dkv_ref[:, vs] += dv

        dh2 = _dot_nt(dq_ref[...], wq_ref[...])
        dxa, dg = _rms_bwd(x1, g_ref[...], r2, dh2)
        _acc(dg_ref, dg, i == 0)
        dx1 = dx2 + dxa
        dx1_ref[...] = dx1
        dmg_ref[...] = _dot_nt(dx1, wmo_ref[...])

    tile = pl.BlockSpec((tm, D), lambda i: (i, 0))
    kvb = pl.BlockSpec((M, 2 * D), lambda i: (i // tpb, 0))
    B = T // S
    return _pcall(
        body, (x1, kv, dx2, prm["w_mo"], prm["w_q"], prm["w_xo"], prm["g2"]), grid=(T // tm,),
        in_specs=[tile, kvb, tile, _const((D, D)), _const((D, D)), _const((D, D)), _const((1, D))],
        out_specs=[tile, tile, tile, kvb, pl.BlockSpec((1, D), lambda i: (0, 0))],
        out_shape=[_sds((T, D), F32), _sds((T, D), F32), _sds((T, D), BF), _sds((B * M, 2 * D), F32), _sds((1, D), F32)],
        scratch_shapes=[pltpu.VMEM((tm, D), BF), pltpu.VMEM((tm, D), BF)],
        name="attn_bwd", vmem=48, comm=comm)


def _kv_bwd(mem, gm, w_kv, dkv, B, M):
    D = mem.shape[1]
    N = w_kv.shape[1]

    def body(m_ref, g_ref, w_ref, dkv_ref, dg_ref):
        mem_t = m_ref[...]
        _, r = _rms(mem_t, g_ref[...])
        dmn = _dot_nt(dkv_ref[...], w_ref[...])
        _acc(dg_ref, jnp.sum(dmn * (mem_t * r), axis=0, keepdims=True), pl.program_id(0) == 0)

    return pl.pallas_call(
        body, grid=(B,),
        in_specs=[pl.BlockSpec((M, D), lambda b: (b, 0)), _const((1, D)), _const((D, N)),
                  pl.BlockSpec((M, N), lambda b: (b, 0))],
        out_specs=pl.BlockSpec((1, D), lambda b: (0, 0)),
        out_shape=_sds((1, D), F32),
        name="kv_bwd", compiler_params=_cp(32))(mem, gm, w_kv, dkv)


def _branch_bwd(c, p, dmerged, prm, tm=TM3, comm=None):
    T, D = c.shape

    def body(c_ref, bu_ref, bv_ref, ga_ref, gb_ref, dm_ref, wco_ref, wso_ref, lag_ref, lab_ref, lsg_ref, lsb_ref,
             wm_ref, bz_ref, bg_ref,
             dc_ref, dpb_ref, dya_ref, dyb_ref, dwm_ref, dbz_ref, dlag_ref, dlab_ref, dlsg_ref, dlsb_ref, dbg_ref,
             z_scr, v_scr, sa_scr, sg_scr, dv_scr):
        first = pl.program_id(0) == 0
        s_a, ln_a, sig_a, xh_a, rstd_a = _branch_a(c_ref[...], lag_ref[...], lab_ref[...])
        sa_scr[...] = s_a.astype(BF)
        y_a = jnp.dot(sa_scr[...], wco_ref[...], preferred_element_type=F32)
        bu, bv = bu_ref[...], bv_ref[...]
        sg, u, tu, z, tv, vh, rstd_v = _branch_b(bu, bv, lsg_ref[...], lsb_ref[...], wm_ref, bz_ref, z_scr, v_scr)
        sg_scr[...] = sg.astype(BF)
        y_b = jnp.dot(sg_scr[...], wso_ref[...], preferred_element_type=F32)
        ga = _sig(ga_ref[...] + bg_ref[pl.ds(0, 1), :])
        gb = _sig(gb_ref[...] + bg_ref[pl.ds(1, 1), :])
        dm = dm_ref[...]
        dga = dm * y_a * ga * (1.0 - ga)
        dgb = dm * y_b * gb * (1.0 - gb)
        dpb_ref[:, pl.ds(2 * D, D)] = dga.astype(BF)
        dpb_ref[:, pl.ds(3 * D, D)] = dgb.astype(BF)
        _acc(dbg_ref.at[pl.ds(0, 1), :], jnp.sum(dga, axis=0, keepdims=True), first)
        _acc(dbg_ref.at[pl.ds(1, 1), :], jnp.sum(dgb, axis=0, keepdims=True), first)
        dya_ref[...] = (dm * ga).astype(BF)
        dyb_ref[...] = (dm * gb).astype(BF)
        dln = _dot_nt(dya_ref[...], wco_ref[...]) * _dsilu(ln_a, sig_a)
        dc, dlag, dlab = _ln_bwd(xh_a, rstd_a, lag_ref[...], dln)
        dc_ref[...] = dc
        _acc(dlag_ref, dlag, first)
        _acc(dlab_ref, dlab, first)
        dsg = _dot_nt(dyb_ref[...], wso_ref[...])
        dpb_ref[:, pl.ds(0, D)] = (dsg * z * _dgelu(bu, tu)).astype(BF)
        dz = dsg * u
        z_scr[...] = dz
        mask = _tril_mask()

        @pl.when(first)
        def _():
            dwm_ref[...] = jnp.zeros_like(dwm_ref)
            dbz_ref[...] = jnp.zeros_like(dbz_ref)

        for gi in range(SGU_GROUPS):
            wm = jnp.where(mask, wm_ref[gi], 0.0).astype(BF)
            cols = pl.ds(gi * LANES, LANES)
            for n in range(tm // LANES):
                rows = pl.ds(n * LANES, LANES)
                dzb = z_scr[rows, cols].astype(BF)
                dv_scr[rows, cols] = lax.dot_general(wm, dzb, (((0,), (0,)), ((), ())), preferred_element_type=F32)
                dw = lax.dot_general(dzb, v_scr[rows, cols], (((1,), (1,)), ((), ())), preferred_element_type=F32)
                dwm_ref[gi] += jnp.where(mask, dw, 0.0)
                dbz_ref[:, cols] += z_scr[rows, cols]
        dgv, dlsg, dlsb = _ln_bwd(vh, rstd_v, lsg_ref[...], dv_scr[...])
        _acc(dlsg_ref, dlsg, first)
        _acc(dlsb_ref, dlsb, first)
        dpb_ref[:, pl.ds(D, D)] = (dgv * _dgelu(bv, tv)).astype(BF)

    tile = lambda j: pl.BlockSpec((tm, D), lambda i: (i, j))
    vec = pl.BlockSpec((1, D), lambda i: (0, 0))
    return _pcall(
        body, (c, p, p, p, p, dmerged, prm["w_co"], prm["w_so"], prm["la_g"], prm["la_b"], prm["ls_g"], prm["ls_b"],
               prm["sgu_w"], prm["bz"], prm["b_gate"]),
        grid=(T // tm,),
        in_specs=[tile(0), tile(2), tile(3), tile(4), tile(5), tile(0), _const((D, D)), _const((D, D)),
                  _const((1, D)), _const((1, D)), _const((1, D)), _const((1, D)),
                  _const((SGU_GROUPS, LANES, LANES)), _const((LANES, D)), _const((2, D))],
        out_specs=[tile(0), pl.BlockSpec((tm, 4 * D), lambda i: (i, 0)), tile(0), tile(0),
                   pl.BlockSpec((SGU_GROUPS, LANES, LANES), lambda i: (0, 0, 0)),
                   pl.BlockSpec((LANES, D), lambda i: (0, 0)), vec, vec, vec, vec,
                   pl.BlockSpec((2, D), lambda i: (0, 0))],
        out_shape=[_sds((T, D), F32), _sds((T, 4 * D), BF), _sds((T, D), BF), _sds((T, D), BF),
                   _sds((SGU_GROUPS, LANES, LANES), F32), _sds((LANES, D), F32),
                   _sds((1, D), F32), _sds((1, D), F32), _sds((1, D), F32), _sds((1, D), F32), _sds((2, D), F32)],
        scratch_shapes=[pltpu.VMEM((tm, D), F32), pltpu.VMEM((tm, D), BF), pltpu.VMEM((tm, D), BF),
                        pltpu.VMEM((tm, D), BF), pltpu.VMEM((tm, D), F32)],
        name="branch_bwd", vmem=56, comm=comm)


def _conv_bwd(p, dc, conv_w, B, S, comm=None):
    K, D = conv_w.shape
    nc = D // LANES

    def body(av_ref, ag_ref, dc_ref, w_ref, dav_ref, dag_ref, dw_ref, db_ref, apad, dpad):
        b = pl.program_id(1)
        av = av_ref[...]
        sg = _sig(ag_ref[...])
        apad[pl.ds(0, CONV_PAD), :] = jnp.zeros((CONV_PAD, LANES), F32)
        apad[pl.ds(CONV_PAD, S), :] = av * sg
        dpad[pl.ds(S, CONV_PAD), :] = jnp.zeros((CONV_PAD, LANES), F32)
        dpad[pl.ds(0, S), :] = dc_ref[...]

        @pl.when(b == 0)
        def _():
            dw_ref[...] = jnp.zeros_like(dw_ref)
            db_ref[...] = jnp.zeros_like(db_ref)

        db_ref[...] += jnp.sum(dc_ref[...], axis=0, keepdims=True)
        for k in range(K):
            tot = jnp.zeros((1, LANES), F32)
            for r0 in range(0, S, CONV_ROWS):
                tot = tot + jnp.sum(dpad[pl.ds(r0, CONV_ROWS), :] * apad[pl.ds(r0 + k + CONV_PAD - (K - 1), CONV_ROWS), :],
                                    axis=0, keepdims=True)
            dw_ref[pl.ds(k, 1), :] += tot
        for r0 in range(0, S, CONV_ROWS):
            da = jnp.zeros((CONV_ROWS, LANES), F32)
            for k in range(K):
                da = da + w_ref[pl.ds(k, 1), :] * dpad[pl.ds(r0 + (K - 1) - k, CONV_ROWS), :]
            rows = pl.ds(r0, CONV_ROWS)
            s = sg[r0:r0 + CONV_ROWS, :]
            a_v = av[r0:r0 + CONV_ROWS, :]
            dav_ref[rows, :] = (da * s).astype(BF)
            dag_ref[rows, :] = (da * a_v * s * (1.0 - s)).astype(BF)

    blk = lambda off: pl.BlockSpec((S, LANES), lambda j, b: (b, off + j))
    return _pcall(
        body, (p, p, dc, conv_w), grid=(nc, B),
        in_specs=[blk(0), blk(nc), blk(0), pl.BlockSpec((K, LANES), lambda j, b: (0, j))],
        out_specs=[blk(0), blk(0), pl.BlockSpec((K, LANES), lambda j, b: (0, j)), pl.BlockSpec((1, LANES), lambda j, b: (0, j))],
        out_shape=[_sds((B * S, D), BF), _sds((B * S, D), BF), _sds((K, D), F32), _sds((1, D), F32)],
        scratch_shapes=[pltpu.VMEM((S + CONV_PAD, LANES), F32), pltpu.VMEM((S + CONV_PAD, LANES), F32)],
        name="conv_bwd", vmem=32, comm=comm)


TM1 = 512


def _in_proj_bwd(x, dx1, dps, g1, w_in, tm=TM1, comm=None):
    T, D = x.shape
    N = w_in.shape[1]
    widths = [d.shape[1] for d in dps]

    def body(x_ref, dx1_ref, *refs):
        dp_refs, (g_ref, w_ref, dx_ref, dg_ref) = refs[:len(dps)], refs[len(dps):]
        x_t = x_ref[...]
        _, r = _rms(x_t, g_ref[...])
        dh = jnp.zeros((tm, D), F32)
        for q, dp_ref in enumerate(dp_refs):
            cols = pl.ds(sum(widths[:q]), widths[q])
            dh = dh + lax.dot_general(dp_ref[...], w_ref[:, cols], (((1,), (1,)), ((), ())), preferred_element_type=F32)
        dxa, dg = _rms_bwd(x_t, g_ref[...], r, dh)
        dx_ref[...] = dx1_ref[...] + dxa
        _acc(dg_ref, dg, pl.program_id(0) == 0)

    tile = pl.BlockSpec((tm, D), lambda i: (i, 0))
    return _pcall(
        body, (x, dx1, *dps, g1, w_in), grid=(T // tm,),
        in_specs=[tile, tile] + [pl.BlockSpec((tm, w), lambda i: (i, 0)) for w in widths] + [_const((1, D)), _const((D, N))],
        out_specs=[tile, pl.BlockSpec((1, D), lambda i: (0, 0))],
        out_shape=[_sds((T, D), F32), _sds((1, D), F32)],
        name="in_proj_bwd", vmem=48, comm=comm)


def _pick(n, cands):
    for c in cands:
        if n % c == 0:
            return c
    raise ValueError(f"no tile of {cands} divides {n}")


def _mm_tn(x, dys, name):
    T, K = x.shape
    dys = list(dys) if isinstance(dys, (list, tuple)) else [dys]
    widths = [d.shape[1] for d in dys]
    N = sum(widths)
    tm = _pick(T, (1024, 512, 256))
    tk = _pick(K, (1024, 1408, 512))
    tn = _pick(math.gcd(*widths), (1024, 1408, 512))
    nt = T // tm
    first = [sum(widths[:q]) // tn for q in range(len(dys))]
    count = [w // tn for w in widths]

    def body(x_ref, *refs):
        dy_refs, (o_ref, ob_ref, acc) = refs[:len(dys)], refs[len(dys):]
        j, t = pl.program_id(1), pl.program_id(2)

        @pl.when(t == 0)
        def _():
            acc[...] = jnp.zeros_like(acc)

        for q, dy_ref in enumerate(dy_refs):
            @pl.when(jnp.logical_and(j >= first[q], j < first[q] + count[q]))
            def _():
                acc[...] += _dot_tn(x_ref[...], dy_ref[...])

        @pl.when(t == nt - 1)
        def _():
            o_ref[...] = acc[...]
            ob_ref[...] = acc[...].astype(BF)

    def dy_spec(q):
        def index(i, j, t):
            mine = jnp.logical_and(j >= first[q], j < first[q] + count[q])
            return jnp.where(mine, t, 0), jnp.clip(j - first[q], 0, count[q] - 1)
        return pl.BlockSpec((tm, tn), index)

    return pl.pallas_call(
        body, grid=(K // tk, N // tn, nt),
        in_specs=[pl.BlockSpec((tm, tk), lambda i, j, t: (t, i))] + [dy_spec(q) for q in range(len(dys))],
        out_specs=[pl.BlockSpec((tk, tn), lambda i, j, t: (i, j)), pl.BlockSpec((tk, tn), lambda i, j, t: (i, j))],
        out_shape=[_sds((K, N), F32), _sds((K, N), BF)],
        scratch_shapes=[pltpu.VMEM((tk, tn), F32)],
        name=name, compiler_params=_cp(48))(x, *dys)


def _swap_cores(parts):
    n = len(parts)

    def body(*refs):
        ins, outs = refs[:n], refs[n:2 * n]
        send, recv = refs[2 * n:]
        x, y, c, _ = _place()

        def copy(t):
            return pltpu.make_async_remote_copy(
                src_ref=ins[t], dst_ref=outs[t], send_sem=send.at[t], recv_sem=recv.at[t],
                device_id=(x, y, 1 - c), device_id_type=MESH)

        for t in range(n):
            copy(t).start()
        for t in range(n):
            copy(t).wait_recv()
            copy(t).wait_send()

    return pl.pallas_call(
        body, in_specs=[ANY] * n, out_specs=[ANY] * n, out_shape=[_sds(p.shape, p.dtype) for p in parts],
        scratch_shapes=[pltpu.SemaphoreType.DMA((n,)), pltpu.SemaphoreType.DMA((n,))],
        name="swap_cores")(*parts)


def _all_reduce_small(pack):
    R, C = pack.shape

    def body(p_ref, sum_ref, land, send, recv):
        x, y, c, _ = _place()
        me = 4 * x + 2 * y + c
        land[me] = p_ref[...]
        copies = []
        for mask in range(1, N_DEV):
            peer = ((1 - x) if mask & 4 else x, (1 - y) if mask & 2 else y, (1 - c) if mask & 1 else c)
            src = peer[0] * 4 + peer[1] * 2 + peer[2]
            copies.append(pltpu.make_async_remote_copy(
                src_ref=p_ref, dst_ref=land.at[me], send_sem=send.at[mask], recv_sem=recv.at[mask],
                device_id=peer, device_id_type=MESH))
            copies[-1].start()
            copies[-1] = pltpu.make_async_remote_copy(
                src_ref=p_ref, dst_ref=land.at[src], send_sem=send.at[mask], recv_sem=recv.at[mask],
                device_id=peer, device_id_type=MESH)
        for cp in copies:
            cp.wait_recv()
            cp.wait_send()
        tot = land[0]
        for d in range(1, N_DEV):
            tot = tot + land[d]
        sum_ref[...] = tot

    vm = pl.BlockSpec(memory_space=pltpu.VMEM)
    return pl.pallas_call(
        body, in_specs=[vm], out_specs=vm, out_shape=_sds((R, C), F32),
        scratch_shapes=[pltpu.VMEM((N_DEV, R, C), F32), pltpu.SemaphoreType.DMA((N_DEV,)), pltpu.SemaphoreType.DMA((N_DEV,))],
        name="all_reduce_small", compiler_params=_cp(32))(pack)


def _row_tile(R):
    return _pick(R, (128, 64, 32, 16, 8)) if R % 8 == 0 else R


def _sum_landed(own, land, name):
    R, C = own.shape
    tr = _row_tile(R)

    def body(o_ref, l_ref, s_ref):
        s_ref[...] = ((o_ref[...] + l_ref[0].astype(F32)) + l_ref[1].astype(F32)) + l_ref[2].astype(F32)

    return pl.pallas_call(
        body, grid=(R // tr,),
        in_specs=[pl.BlockSpec((tr, C), lambda i: (i, 0)), pl.BlockSpec((3, tr, C), lambda i: (0, i, 0))],
        out_specs=pl.BlockSpec((tr, C), lambda i: (i, 0)), out_shape=_sds((R, C), F32),
        name=name, compiler_params=_cp(32))(own, land)


def _adamw(g, w, m, v):
    m = ADAM_B1 * m + (1.0 - ADAM_B1) * g
    v = ADAM_B2 * v + (1.0 - ADAM_B2) * (g * g)
    m_hat = m / (1.0 - ADAM_B1 ** ADAM_STEP)
    v_hat = v / (1.0 - ADAM_B2 ** ADAM_STEP)
    return -ADAM_LR * (m_hat / (jnp.sqrt(v_hat) + ADAM_EPS) + ADAM_WD * w), m, v


def _update(parts, w, m, v, name):
    R, C = w.shape
    tr = _row_tile(R)
    k = len(parts)

    def body(*refs):
        g = refs[0][...]
        for r in refs[1:k]:
            g = g + r[...]
        w_ref, m_ref, v_ref, g_out, d_out, m_out, v_out = refs[k:]
        d, m_new, v_new = _adamw(g, w_ref[...], m_ref[...], v_ref[...])
        g_out[...] = g
        d_out[...] = d
        m_out[...] = m_new
        v_out[...] = v_new

    blk = pl.BlockSpec((tr, C), lambda i: (i, 0))
    return pl.pallas_call(
        body, grid=(R // tr,), in_specs=[blk] * (k + 3), out_specs=[blk] * 4, out_shape=[_sds((R, C), F32)] * 4,
        name=name, compiler_params=_cp(40))(*parts, w, m, v)


BIG = ("w_in", "w_conv_out", "w_sgu_out", "w_mix_out", "w_q", "w_kv", "w_xo", "w_gu", "w_down")
BIG_KIND = {"w_in": "col", "w_conv_out": "row", "w_sgu_out": "row", "w_mix_out": "row", "w_q": "row",
            "w_kv": "col", "w_xo": "row", "w_gu": "col", "w_down": "row"}
VECS = ("norm_mix", "conv_b", "conv_ln_g", "conv_ln_b", "sgu_ln_g", "sgu_ln_b", "norm_xattn", "norm_mem", "norm_ffn",
        "norm_final")


def _step(a):
    x3d, mem3d, tgt3d = a["x"], a["mem"], a["loss_target"]
    B, S, D = x3d.shape
    M = mem3d.shape[1]
    T = B * S
    x = x3d.reshape(T, D)
    mem = mem3d.reshape(B * M, D)
    tgt = tgt3d.reshape(T, D)
    xi, yi = lax.axis_index("x"), lax.axis_index("y")
    chip = 2 * xi + yi

    def gather(names):
        return _Gather([a[nm][0] if nm in ("b_gate", "conv_w") else a[nm][0].astype(BF) for nm in names],
                       [BIG_KIND.get(nm, "col") for nm in names])

    first = ("w_in", "b_gate", "conv_w")
    on_in_proj = ("w_conv_out", "w_sgu_out", "w_kv", "w_mix_out", "w_q", "w_xo")
    full = dict(zip(first, _comm_call(gather(first), "gather_w_in")))
    (p, h1), got = _in_proj(x, a["norm_mix"], full["w_in"], comm=gather(on_in_proj))
    full.update(zip(on_in_proj, got))
    (c,), got = _conv_fwd(p, full["conv_w"], a["conv_b"], B, S, comm=gather(("w_down",)))
    full["w_down"] = got[0]

    sgu_b = a["sgu_b"][0]
    bz = jnp.repeat(jnp.transpose(sgu_b), LANES, axis=1)
    prm = dict(w_co=full["w_conv_out"], w_so=full["w_sgu_out"], w_mo=full["w_mix_out"], w_q=full["w_q"],
               w_xo=full["w_xo"], w_down=full["w_down"],
               la_g=a["conv_ln_g"], la_b=a["conv_ln_b"], ls_g=a["sgu_ln_g"], ls_b=a["sgu_ln_b"],
               sgu_w=a["sgu_w"][0], bz=bz, b_gate=full["b_gate"], g2=a["norm_xattn"], g3=a["norm_ffn"],
               gf=a["norm_final"].reshape(1, D))

    (merged, s_a, sg), got = _branch_fwd(c, p, prm, comm=gather(("w_gu",)))
    prm["w_gu"] = got[0]
    mem_n, kv = _kv_fwd(mem, a["norm_mem"], full["w_kv"], B, M)
    x1, x2, h2, o = _attn_fwd(x, merged, kv, prm, S, M)
    dx2, dx3, dgu, h3, f, lsum, d_g3, d_gf = _ffn_loss(x2, tgt, prm)
    loss = lax.psum(0.5 * jnp.sum(lsum) / D, ("x", "y", "c"))

    size_of = {nm: a[nm].shape[1] if BIG_KIND[nm] == "row" else a[nm].shape[2] for nm in BIG}
    landed = {}

    def scatter(names):
        return _Scatter([gw[nm][1] for nm in names], [BIG_KIND[nm] for nm in names], [size_of[nm] for nm in names])

    gw = {}
    gw["w_down"] = _mm_tn(f, dx3, "dw_down")
    gw["w_gu"] = _mm_tn(h3, dgu, "dw_gu")
    (dx1, dmerged, dq, dkv, d_g2), got = _attn_bwd(x1, kv, dx2, prm, S, M, comm=scatter(("w_gu",)))
    landed["w_gu"] = got[0]
    gw["w_xo"] = _mm_tn(o, dx2, "dw_xo")
    gw["w_q"] = _mm_tn(h2, dq, "dw_q")
    gw["w_kv"] = _mm_tn(mem_n, dkv, "dw_kv")
    d_gm = _kv_bwd(mem, a["norm_mem"], full["w_kv"], dkv, B, M)
    gw["w_mix_out"] = _mm_tn(merged, dx1, "dw_mix_out")
    group = ("w_down", "w_xo", "w_q", "w_kv", "w_mix_out")
    (dc, dpb, dya, dyb, d_wm, d_bz, d_lag, d_lab, d_lsg, d_lsb, d_bg), got = _branch_bwd(
        c, p, dmerged, prm, comm=scatter(group))
    landed.update(zip(group, got))
    gw["w_conv_out"] = _mm_tn(s_a, dya, "dw_conv_out")
    gw["w_sgu_out"] = _mm_tn(sg, dyb, "dw_sgu_out")
    group = ("w_conv_out", "w_sgu_out")
    (dav, dag, d_cw, d_cb), got = _conv_bwd(p, dc, full["conv_w"], B, S, comm=scatter(group))
    landed.update(zip(group, got))
    dp = [dav, dag, dpb]
    gw["w_in"] = _mm_tn(h1, dp, "dw_in")
    (grad_x, d_g1), got = _in_proj_bwd(x, dx1, dp, a["norm_mix"], full["w_in"], comm=scatter(("w_in",)))
    landed["w_in"] = got[0]

    sizes = [size_of[nm] for nm in BIG]
    part = []
    for nm, sz, ld in zip(BIG, sizes, [landed[nm] for nm in BIG]):
        g32 = gw[nm][0]
        own = (lax.dynamic_slice_in_dim(g32, chip * sz, sz, axis=0) if BIG_KIND[nm] == "row"
               else lax.dynamic_slice_in_dim(g32, chip * sz, sz, axis=1))
        part.append(_sum_landed(own, ld, "sum_" + nm))
    other = _swap_cores(part)
    out = {}
    for nm, mine, theirs in zip(BIG, part, other):
        out[nm] = _update([mine, theirs], a[nm][0], a["m_" + nm][0], a["v_" + nm][0], "upd_" + nm)

    G = SGU_GROUPS
    d_sb = jnp.transpose(d_bz.reshape(LANES, G, LANES).sum(axis=-1))
    vec_g = dict(norm_mix=d_g1, conv_b=d_cb, conv_ln_g=d_lag, conv_ln_b=d_lab, sgu_ln_g=d_lsg, sgu_ln_b=d_lsb,
                 norm_xattn=d_g2, norm_mem=d_gm, norm_ffn=d_g3, norm_final=d_gf)
    rows = [vec_g[nm] for nm in VECS] + [d_sb.reshape(1, D), d_bg, d_cw, d_wm.reshape(G * LANES * LANES // D, D)]
    n_rows = sum(r.shape[0] for r in rows)
    pad = (-n_rows) % 8
    pack = jnp.concatenate(rows + [jnp.zeros((pad, D), F32)], axis=0)
    tot = _all_reduce_small(pack)

    def small(nm):
        arr = a[nm]
        return arr.reshape(-1, D) if nm != "sgu_b" else arr.reshape(1, D)

    nv = len(VECS)
    K = a["conv_w"].shape[1]
    rep_names = list(VECS) + ["sgu_b", "sgu_w"]
    rep_g = jnp.concatenate([tot[:nv + 1], tot[nv + 3 + K:n_rows]], axis=0)
    rep = [jnp.concatenate([small(pre + nm) for nm in rep_names], axis=0) for pre in ("", "m_", "v_")]
    rep_out = _update([rep_g], rep[0], rep[1], rep[2], "upd_replicated")
    Dq = D // N_CHIPS
    col_g = lax.dynamic_slice_in_dim(tot[nv + 1:nv + 3 + K], chip * Dq, Dq, axis=1)
    col = [jnp.concatenate([a[pre + "b_gate"][0], a[pre + "conv_w"][0]], axis=0) for pre in ("", "m_", "v_")]
    col_out = _update([col_g], col[0], col[1], col[2], "upd_columns")

    for q in range(4):
        for i, nm in enumerate(VECS):
            out.setdefault(nm, [None] * 4)[q] = rep_out[q][i:i + 1].reshape(a[nm].shape)
        out.setdefault("sgu_b", [None] * 4)[q] = rep_out[q][nv:nv + 1].reshape(a["sgu_b"].shape)
        out.setdefault("sgu_w", [None] * 4)[q] = rep_out[q][nv + 1:].reshape(a["sgu_w"].shape)
        out.setdefault("b_gate", [None] * 4)[q] = col_out[q][0:2][None]
        out.setdefault("conv_w", [None] * 4)[q] = col_out[q][2:][None]
    for nm in BIG:
        out[nm] = [o_[None] for o_ in out[nm]]
    return loss, grad_x.reshape(B, S, D), out


WEIGHTS = ("norm_mix", "w_in", "b_gate", "conv_w", "conv_b", "conv_ln_g", "conv_ln_b", "w_conv_out", "sgu_ln_g",
           "sgu_ln_b", "sgu_w", "sgu_b", "w_sgu_out", "w_mix_out", "norm_xattn", "norm_mem", "w_q", "w_kv", "w_xo",
           "norm_ffn", "w_gu", "w_down", "norm_final")


def kernel(x, mem, norm_mix, w_in, b_gate, conv_w, conv_b, conv_ln_g, conv_ln_b, w_conv_out, sgu_ln_g, sgu_ln_b, sgu_w, sgu_b, w_sgu_out, w_mix_out, norm_xattn, norm_mem, w_q, w_kv, w_xo, norm_ffn, w_gu, w_down, norm_final, loss_target, m_norm_mix, m_w_in, m_b_gate, m_conv_w, m_conv_b, m_conv_ln_g, m_conv_ln_b, m_w_conv_out, m_sgu_ln_g, m_sgu_ln_b, m_sgu_w, m_sgu_b, m_w_sgu_out, m_w_mix_out, m_norm_xattn, m_norm_mem, m_w_q, m_w_kv, m_w_xo, m_norm_ffn, m_w_gu, m_w_down, m_norm_final, v_norm_mix, v_w_in, v_b_gate, v_conv_w, v_conv_b, v_conv_ln_g, v_conv_ln_b, v_w_conv_out, v_sgu_ln_g, v_sgu_ln_b, v_sgu_w, v_sgu_b, v_w_sgu_out, v_w_mix_out, v_norm_xattn, v_norm_mem, v_w_q, v_w_kv, v_w_xo, v_norm_ffn, v_w_gu, v_w_down, v_norm_final):
    a = dict(locals())
    loss, grad_x, out = _step(a)
    res = [loss, grad_x]
    for q in range(4):
        res += [out[nm][q] for nm in WEIGHTS]
    return tuple(res)
```

```python
import functools
import math

import jax
import jax.numpy as jnp
from jax import lax
from jax.experimental import pallas as pl
from jax.experimental.pallas import tpu as pltpu

BF = jnp.bfloat16
F32 = jnp.float32
MESH = pl.DeviceIdType.MESH
ANY = pl.BlockSpec(memory_space=pl.ANY)

RMS_EPS = 1e-6
LN_EPS = 1e-5
HEADS = 4
SGU_GROUPS = 8
LANES = 128
ADAM_LR = 0.001
ADAM_B1 = 0.9
ADAM_B2 = 0.999
ADAM_EPS = 1e-08
ADAM_WD = 0.01
ADAM_STEP = 10
N_CHIPS = 4
N_DEV = 8
MIB = 1024 * 1024


def _sds(shape, dtype):
    return jax.ShapeDtypeStruct(tuple(shape), dtype)


def _cp(vmem_mib):
    return pltpu.CompilerParams(vmem_limit_bytes=vmem_mib * MIB)


def _const(shape):
    nd = len(shape)
    return pl.BlockSpec(tuple(shape), lambda *_: (0,) * nd, pipeline_mode=pl.Buffered(1))


def _dot(a, b):
    return jnp.dot(a.astype(BF), b.astype(BF), preferred_element_type=F32)


def _dot_nt(a, b):
    return lax.dot_general(a.astype(BF), b.astype(BF), (((1,), (1,)), ((), ())), preferred_element_type=F32)


def _dot_tn(a, b):
    return lax.dot_general(a.astype(BF), b.astype(BF), (((0,), (0,)), ((), ())), preferred_element_type=F32)


def _sig(x):
    return 1.0 / (1.0 + jnp.exp(-x))


def _dsilu(x, s):
    return s * (1.0 + x * (1.0 - s))


_GELU_C = math.sqrt(2.0 / math.pi)


def _gelu(x):
    t = jnp.tanh(_GELU_C * (x + 0.044715 * (x * x * x)))
    return 0.5 * x * (1.0 + t), t


def _dgelu(x, t):
    return 0.5 * (1.0 + t) + 0.5 * x * (1.0 - t * t) * (_GELU_C * (1.0 + 3.0 * 0.044715 * (x * x)))


def _rms(x, g):
    r = lax.rsqrt(jnp.mean(x * x, axis=-1, keepdims=True) + RMS_EPS)
    return x * r * g, r


def _rms_bwd(x, g, r, dh):
    xr = x * r
    dxh = dh * g
    dx = r * (dxh - xr * jnp.mean(dxh * xr, axis=-1, keepdims=True))
    return dx, jnp.sum(dh * xr, axis=0, keepdims=True)


def _ln(x, g, b):
    mu = jnp.mean(x, axis=-1, keepdims=True)
    xc = x - mu
    rstd = lax.rsqrt(jnp.mean(xc * xc, axis=-1, keepdims=True) + LN_EPS)
    xh = xc * rstd
    return xh * g + b, xh, rstd


def _ln_bwd(xh, rstd, g, dy):
    dxh = dy * g
    dx = rstd * (dxh - jnp.mean(dxh, axis=-1, keepdims=True) - xh * jnp.mean(dxh * xh, axis=-1, keepdims=True))
    return dx, jnp.sum(dy * xh, axis=0, keepdims=True), jnp.sum(dy, axis=0, keepdims=True)


def _acc(ref, val, first):
    @pl.when(first)
    def _():
        ref[...] = val

    @pl.when(jnp.logical_not(first))
    def _():
        ref[...] += val


def _place():
    x, y, c = lax.axis_index("x"), lax.axis_index("y"), lax.axis_index("c")
    chips = [(1 - x, y), (x, 1 - y), (1 - x, 1 - y)]
    return x, y, c, chips


def _shard_of(ref, kind, k, n):
    if kind == "row":
        return ref.at[pl.ds(k * n, n), :]
    return ref.at[:, pl.ds(k * n, n)]


class _Gather:
    def __init__(self, shards, kinds):
        n = len(shards)
        self.srcs, self.kinds = list(shards), list(kinds)
        self.sizes = [s.shape[0] if kd == "row" else s.shape[1] for s, kd in zip(shards, kinds)]
        self.halves = [s.shape[0] // 2 if s.shape[0] % 32 == 0 else None for s in shards]
        self.out_shape = [
            _sds((s.shape[0] * N_CHIPS, s.shape[1]) if kd == "row" else (s.shape[0], s.shape[1] * N_CHIPS), s.dtype)
            for s, kd in zip(shards, kinds)]
        dma = pltpu.SemaphoreType.DMA
        self.sems = [dma((3 * n,)), dma((3 * n,)), dma((n,)), dma((3 * n,)), dma((3 * n,))]

    def _part(self, ref, t, core):
        h = self.halves[t]
        return ref if h is None else ref.at[pl.ds(core * h, h), :]

    def _copies(self, ins, outs, send, recv, loc, fsend, frecv):
        x, y, c, chips = _place()
        k = 2 * x + y
        local, remote = [], []
        for t in range(len(ins)):
            block = lambda q: _shard_of(outs[t], self.kinds[t], q, self.sizes[t])
            local.append(pltpu.make_async_copy(ins[t], block(k), loc.at[t]))
            for j, (px, py) in enumerate(chips):
                sems = dict(send_sem=send.at[3 * t + j], recv_sem=recv.at[3 * t + j])
                there = dict(device_id=(px, py, c), device_id_type=MESH)
                sent = pltpu.make_async_remote_copy(
                    src_ref=self._part(ins[t], t, c), dst_ref=self._part(block(k), t, c), **sems, **there)
                got = self._part(block(2 * px + py), t, c)
                landed = pltpu.make_async_remote_copy(src_ref=self._part(ins[t], t, c), dst_ref=got, **sems, **there)
                passed = handed = None
                if self.halves[t] is not None:
                    fsems = dict(send_sem=fsend.at[3 * t + j], recv_sem=frecv.at[3 * t + j])
                    sibling = dict(device_id=(x, y, 1 - c), device_id_type=MESH)
                    passed = pltpu.make_async_remote_copy(src_ref=got, dst_ref=got, **fsems, **sibling)
                    other = self._part(block(2 * px + py), t, 1 - c)
                    handed = pltpu.make_async_remote_copy(src_ref=got, dst_ref=other, **fsems, **sibling)
                remote.append((sent, landed, passed, handed))
        return local, remote

    def start(self, ins, outs, *sems):
        local, remote = self._copies(ins, outs, *sems)
        for cp in local:
            cp.start()
        for sent, _, _, _ in remote:
            sent.start()

    def wait(self, ins, outs, *sems):
        local, remote = self._copies(ins, outs, *sems)
        for sent, landed, passed, handed in remote:
            landed.wait_recv()
            if passed is not None:
                passed.start()
        for sent, landed, passed, handed in remote:
            if passed is not None:
                handed.wait_recv()
                passed.wait_send()
            sent.wait_send()
        for cp in local:
            cp.wait()


class _Scatter:
    def __init__(self, grads, kinds, sizes):
        n = len(grads)
        self.srcs, self.kinds, self.sizes = list(grads), list(kinds), list(sizes)
        self.out_shape = [_sds((3,) + ((sz, g.shape[1]) if kd == "row" else (g.shape[0], sz)), g.dtype)
                          for g, kd, sz in zip(grads, kinds, sizes)]
        self.sems = [pltpu.SemaphoreType.DMA((3 * n,)), pltpu.SemaphoreType.DMA((3 * n,))]

    def _copies(self, ins, outs, send, recv):
        x, y, c, chips = _place()
        return [pltpu.make_async_remote_copy(
            src_ref=_shard_of(ins[t], self.kinds[t], 2 * px + py, self.sizes[t]), dst_ref=outs[t].at[j],
            send_sem=send.at[3 * t + j], recv_sem=recv.at[3 * t + j], device_id=(px, py, c), device_id_type=MESH)
            for t in range(len(ins)) for j, (px, py) in enumerate(chips)]

    def start(self, ins, outs, send, recv):
        for cp in self._copies(ins, outs, send, recv):
            cp.start()

    def wait(self, ins, outs, send, recv):
        for cp in self._copies(ins, outs, send, recv):
            cp.wait_recv()
            cp.wait_send()


def _comm_call(comm, name):
    n, m = len(comm.srcs), len(comm.out_shape)

    def body(*refs):
        comm.start(refs[:n], refs[n:n + m], *refs[n + m:])
        comm.wait(refs[:n], refs[n:n + m], *refs[n + m:])

    return pl.pallas_call(body, in_specs=[ANY] * n, out_specs=[ANY] * m, out_shape=comm.out_shape,
                          scratch_shapes=comm.sems, name=name)(*comm.srcs)


def _pcall(body, args, *, grid, in_specs, out_specs, out_shape, name, vmem, scratch_shapes=(), comm=None):
    in_specs, out_specs, out_shape = list(in_specs), list(out_specs), list(out_shape)
    scratch_shapes = list(scratch_shapes)
    if comm is None:
        res = pl.pallas_call(body, grid=grid, in_specs=in_specs, out_specs=out_specs, out_shape=out_shape,
                             scratch_shapes=scratch_shapes, name=name, compiler_params=_cp(vmem))(*args)
        return list(res), []
    ni, no, ns = len(in_specs), len(out_specs), len(scratch_shapes)
    ci, co = len(comm.srcs), len(comm.out_shape)

    def carried(*refs):
        c_in = refs[ni:ni + ci]
        c_out = refs[ni + ci + no:ni + ci + no + co]
        sems = refs[ni + ci + no + co + ns:]
        ids = [pl.program_id(d) for d in range(len(grid))]
        first = functools.reduce(jnp.logical_and, [i == 0 for i in ids])
        last = functools.reduce(jnp.logical_and, [i == g - 1 for i, g in zip(ids, grid)])

        @pl.when(first)
        def _():
            comm.start(c_in, c_out, *sems)

        body(*refs[:ni], *refs[ni + ci:ni + ci + no], *refs[ni + ci + no + co:ni + ci + no + co + ns])

        @pl.when(last)
        def _():
            comm.wait(c_in, c_out, *sems)

    res = pl.pallas_call(carried, grid=grid, in_specs=in_specs + [ANY] * ci, out_specs=out_specs + [ANY] * co,
                         out_shape=out_shape + list(comm.out_shape), scratch_shapes=scratch_shapes + list(comm.sems),
                         name=name, compiler_params=_cp(vmem))(*args, *comm.srcs)
    return list(res[:no]), list(res[no:])


def _in_proj(x, g1, w_in, comm=None):
    T, D = x.shape
    N = w_in.shape[1]
    tm, tn = 512, 1024

    def body(x_ref, g_ref, w_ref, p_ref, h_ref):
        h, _ = _rms(x_ref[...], g_ref[...])
        h_ref[...] = h.astype(BF)
        for j in range(N // tn):
            cols = pl.ds(j * tn, tn)
            p_ref[:, cols] = jnp.dot(h_ref[...], w_ref[:, cols], preferred_element_type=F32)

    return _pcall(
        body, (x, g1, w_in), grid=(T // tm,),
        in_specs=[pl.BlockSpec((tm, D), lambda i: (i, 0)), _const((1, D)), _const((D, N))],
        out_specs=[pl.BlockSpec((tm, N), lambda i: (i, 0)), pl.BlockSpec((tm, D), lambda i: (i, 0))],
        out_shape=[_sds((T, N), F32), _sds((T, D), BF)],
        name="in_proj", vmem=56, comm=comm)


CONV_PAD = 32
CONV_ROWS = 256


def _conv_fwd(p, conv_w, conv_b, B, S, comm=None):
    K, D = conv_w.shape
    nc = D // LANES

    def body(av_ref, ag_ref, w_ref, b_ref, c_ref, apad):
        apad[pl.ds(0, CONV_PAD), :] = jnp.zeros((CONV_PAD, LANES), F32)
        apad[pl.ds(CONV_PAD, S), :] = av_ref[...] * _sig(ag_ref[...])
        for r0 in range(0, S, CONV_ROWS):
            acc = jnp.zeros((CONV_ROWS, LANES), F32) + b_ref[...]
            for k in range(K):
                acc = acc + w_ref[pl.ds(k, 1), :] * apad[pl.ds(r0 + k + CONV_PAD - (K - 1), CONV_ROWS), :]
            c_ref[pl.ds(r0, CONV_ROWS), :] = acc

    return _pcall(
        body, (p, p, conv_w, conv_b), grid=(B, nc),
        in_specs=[pl.BlockSpec((S, LANES), lambda b, j: (b, j)), pl.BlockSpec((S, LANES), lambda b, j: (b, nc + j)),
                  pl.BlockSpec((K, LANES), lambda b, j: (0, j)), pl.BlockSpec((1, LANES), lambda b, j: (0, j))],
        out_specs=[pl.BlockSpec((S, LANES), lambda b, j: (b, j))],
        out_shape=[_sds((B * S, D), F32)],
        scratch_shapes=[pltpu.VMEM((S + CONV_PAD, LANES), F32)],
        name="conv_fwd", vmem=32, comm=comm)


def _tril_mask():
    t = lax.broadcasted_iota(jnp.int32, (LANES, LANES), 0)
    s = lax.broadcasted_iota(jnp.int32, (LANES, LANES), 1)
    return t >= s


def _branch_a(c, g, b):
    ln_a, xh, rstd = _ln(c, g, b)
    s = _sig(ln_a)
    return ln_a * s, ln_a, s, xh, rstd


def _branch_b(bu, bv, g, b, wm_ref, bz_ref, z_scr, v_scr):
    tm, D = bu.shape
    u, tu = _gelu(bu)
    gv, tv = _gelu(bv)
    v, vh, rstd = _ln(gv, g, b)
    v_scr[...] = v.astype(BF)
    mask = _tril_mask()
    for gi in range(SGU_GROUPS):
        wm = jnp.where(mask, wm_ref[gi], 0.0).astype(BF)
        cols = pl.ds(gi * LANES, LANES)
        for n in range(tm // LANES):
            rows = pl.ds(n * LANES, LANES)
            z_scr[rows, cols] = jnp.dot(wm, v_scr[rows, cols], preferred_element_type=F32) + bz_ref[:, cols]
    z = z_scr[...]
    return u * z, u, tu, z, tv, vh, rstd


TM3 = 256
TM3_FWD = 512


def _branch_fwd(c, p, prm, tm=TM3_FWD, comm=None):
    T, D = c.shape

    def body(c_ref, bu_ref, bv_ref, ga_ref, gb_ref, wco_ref, wso_ref, lag_ref, lab_ref, lsg_ref, lsb_ref, wm_ref,
             bz_ref, bg_ref, mg_ref, sa_ref, sg_ref, z_scr, v_scr):
        s_a = _branch_a(c_ref[...], lag_ref[...], lab_ref[...])[0]
        sa_ref[...] = s_a.astype(BF)
        y_a = jnp.dot(sa_ref[...], wco_ref[...], preferred_element_type=F32)
        sg = _branch_b(bu_ref[...], bv_ref[...], lsg_ref[...], lsb_ref[...], wm_ref, bz_ref, z_scr, v_scr)[0]
        sg_ref[...] = sg.astype(BF)
        y_b = jnp.dot(sg_ref[...], wso_ref[...], preferred_element_type=F32)
        ga = _sig(ga_ref[...] + bg_ref[pl.ds(0, 1), :])
        gb = _sig(gb_ref[...] + bg_ref[pl.ds(1, 1), :])
        mg_ref[...] = (ga * y_a + gb * y_b).astype(BF)

    tile = lambda j: pl.BlockSpec((tm, D), lambda i: (i, j))
    return _pcall(
        body, (c, p, p, p, p, prm["w_co"], prm["w_so"], prm["la_g"], prm["la_b"], prm["ls_g"], prm["ls_b"],
               prm["sgu_w"], prm["bz"], prm["b_gate"]),
        grid=(T // tm,),
        in_specs=[tile(0), tile(2), tile(3), tile(4), tile(5), _const((D, D)), _const((D, D)),
                  _const((1, D)), _const((1, D)), _const((1, D)), _const((1, D)),
                  _const((SGU_GROUPS, LANES, LANES)), _const((LANES, D)), _const((2, D))],
        out_specs=[tile(0), tile(0), tile(0)],
        out_shape=[_sds((T, D), BF)] * 3,
        scratch_shapes=[pltpu.VMEM((tm, D), F32), pltpu.VMEM((tm, D), BF)],
        name="branch_fwd", vmem=48, comm=comm)


def _kv_fwd(mem, gm, w_kv, B, M):
    D = mem.shape[1]
    N = w_kv.shape[1]

    def body(m_ref, g_ref, w_ref, mn_ref, kv_ref):
        h, _ = _rms(m_ref[...], g_ref[...])
        mn_ref[...] = h.astype(BF)
        kv_ref[...] = jnp.dot(mn_ref[...], w_ref[...], preferred_element_type=F32).astype(BF)

    return pl.pallas_call(
        body, grid=(B,),
        in_specs=[pl.BlockSpec((M, D), lambda b: (b, 0)), _const((1, D)), _const((D, N))],
        out_specs=[pl.BlockSpec((M, D), lambda b: (b, 0)), pl.BlockSpec((M, N), lambda b: (b, 0))],
        out_shape=[_sds((B * M, D), BF), _sds((B * M, N), BF)],
        name="kv_fwd", compiler_params=_cp(32))(mem, gm, w_kv)


def _softmax_rows(s):
    e = jnp.exp(s - jnp.max(s, axis=-1, keepdims=True))
    return e / jnp.sum(e, axis=-1, keepdims=True)


TM4 = 512


def _attn_fwd(x, merged, kv, prm, S, M, tm=TM4):
    T, D = x.shape
    hd = D // HEADS
    scale = hd ** -0.5
    tpb = S // tm

    def body(x_ref, mg_ref, kv_ref, wmo_ref, wq_ref, wxo_ref, g_ref, x1_ref, x2_ref, h2_ref, o_ref):
        x1 = x_ref[...] + jnp.dot(mg_ref[...], wmo_ref[...], preferred_element_type=F32)
        x1_ref[...] = x1
        h2, _ = _rms(x1, g_ref[...])
        h2_ref[...] = h2.astype(BF)
        qb = jnp.dot(h2_ref[...], wq_ref[...], preferred_element_type=F32).astype(BF)
        for h in range(HEADS):
            cs = pl.ds(h * hd, hd)
            s = _dot_nt(qb[:, h * hd:(h + 1) * hd], kv_ref[:, cs]) * scale
            pr = _softmax_rows(s)
            o_ref[:, cs] = _dot(pr, kv_ref[:, pl.ds(D + h * hd, hd)]).astype(BF)
        x2_ref[...] = x1 + jnp.dot(o_ref[...], wxo_ref[...], preferred_element_type=F32)

    tile = pl.BlockSpec((tm, D), lambda i: (i, 0))
    return pl.pallas_call(
        body, grid=(T // tm,),
        in_specs=[tile, tile, pl.BlockSpec((M, 2 * D), lambda i: (i // tpb, 0)),
                  _const((D, D)), _const((D, D)), _const((D, D)), _const((1, D))],
        out_specs=[tile, tile, tile, tile],
        out_shape=[_sds((T, D), F32), _sds((T, D), F32), _sds((T, D), BF), _sds((T, D), BF)],
        name="attn_fwd", compiler_params=_cp(40))(x, merged, kv, prm["w_mo"], prm["w_q"], prm["w_xo"], prm["g2"])


TM5 = 256


def _ffn_loss(x2, tgt, prm, tm=TM5):
    T, D = x2.shape
    F = prm["w_down"].shape[0]
    FC = F // 2

    def body(x2_ref, t_ref, wgu_ref, wd_ref, g3_ref, gf_ref, dx2_ref, dx3_ref, dgu_ref, h3_ref, f_ref, ls_ref,
             dg3_ref, dgf_ref, gu_scr):
        first = pl.program_id(0) == 0
        x2 = x2_ref[...]
        h3, r3 = _rms(x2, g3_ref[...])
        h3_ref[...] = h3.astype(BF)
        x3 = x2
        for ch in range(2):
            gc, uc = pl.ds(ch * FC, FC), pl.ds(F + ch * FC, FC)
            gt = jnp.dot(h3_ref[...], wgu_ref[:, gc], preferred_element_type=F32)
            up = jnp.dot(h3_ref[...], wgu_ref[:, uc], preferred_element_type=F32)
            gu_scr[:, gc] = gt
            gu_scr[:, uc] = up
            f_ref[:, gc] = (gt * _sig(gt) * up).astype(BF)
            x3 = x3 + jnp.dot(f_ref[:, gc], wd_ref[gc, :], preferred_element_type=F32)
        y, rf = _rms(x3, gf_ref[...])
        e = y - t_ref[...]
        _acc(ls_ref, jnp.sum(e * e, axis=0, keepdims=True), first)
        dx3, dgf = _rms_bwd(x3, gf_ref[...], rf, e * (1.0 / D))
        _acc(dgf_ref, dgf, first)
        dx3_ref[...] = dx3.astype(BF)
        dh3 = jnp.zeros((tm, D), F32)
        for ch in range(2):
            gc, uc = pl.ds(ch * FC, FC), pl.ds(F + ch * FC, FC)
            df = lax.dot_general(dx3_ref[...], wd_ref[gc, :], (((1,), (1,)), ((), ())), preferred_element_type=F32)
            gt, up = gu_scr[:, gc], gu_scr[:, uc]
            s = _sig(gt)
            dgu_ref[:, gc] = (df * up * _dsilu(gt, s)).astype(BF)
            dgu_ref[:, uc] = (df * gt * s).astype(BF)
            dh3 = dh3 + lax.dot_general(dgu_ref[:, gc], wgu_ref[:, gc], (((1,), (1,)), ((), ())), preferred_element_type=F32)
            dh3 = dh3 + lax.dot_general(dgu_ref[:, uc], wgu_ref[:, uc], (((1,), (1,)), ((), ())), preferred_element_type=F32)
        dxa, dg3 = _rms_bwd(x2, g3_ref[...], r3, dh3)
        _acc(dg3_ref, dg3, first)
        dx2_ref[...] = dx3 + dxa

    tile = lambda n: pl.BlockSpec((tm, n), lambda i: (i, 0))
    vec = pl.BlockSpec((1, D), lambda i: (0, 0))
    return pl.pallas_call(
        body, grid=(T // tm,),
        in_specs=[tile(D), tile(D), _const((D, 2 * F)), _const((F, D)), _const((1, D)), _const((1, D))],
        out_specs=[tile(D), tile(D), tile(2 * F), tile(D), tile(F), vec, vec, vec],
        out_shape=[_sds((T, D), F32), _sds((T, D), BF), _sds((T, 2 * F), BF), _sds((T, D), BF), _sds((T, F), BF),
                   _sds((1, D), F32), _sds((1, D), F32), _sds((1, D), F32)],
        scratch_shapes=[pltpu.VMEM((tm, 2 * F), F32)],
        name="ffn_loss", compiler_params=_cp(56))(x2, tgt, prm["w_gu"], prm["w_down"], prm["g3"], prm["gf"])


def _attn_bwd(x1, kv, dx2, prm, S, M, tm=TM4, comm=None):
    T, D = x1.shape
    hd = D // HEADS
    scale = hd ** -0.5
    tpb = S // tm

    def body(x1_ref, kv_ref, dx2_ref, wmo_ref, wq_ref, wxo_ref, g_ref, dx1_ref, dmg_ref, dq_ref, dkv_ref, dg_ref,
             h2_scr, do_scr):
        i = pl.program_id(0)
        x1 = x1_ref[...]
        dx2 = dx2_ref[...]
        h2, r2 = _rms(x1, g_ref[...])
        h2_scr[...] = h2.astype(BF)
        qb = jnp.dot(h2_scr[...], wq_ref[...], preferred_element_type=F32).astype(BF)
        do_scr[...] = _dot_nt(dx2, wxo_ref[...]).astype(BF)
        for h in range(HEADS):
            cs, vs = pl.ds(h * hd, hd), pl.ds(D + h * hd, hd)
            qh = qb[:, h * hd:(h + 1) * hd]
            pr = _softmax_rows(_dot_nt(qh, kv_ref[:, cs]) * scale)
            dpr = _dot_nt(do_scr[:, cs], kv_ref[:, vs])
            dv = _dot_tn(pr, do_scr[:, cs])
            ds = (pr * (dpr - jnp.sum(dpr * pr, axis=-1, keepdims=True)) * scale).astype(BF)
            dq_ref[:, cs] = jnp.dot(ds, kv_ref[:, cs], preferred_element_type=F32).astype(BF)
            dk = _dot_tn(ds, qh)

            @pl.when(i % tpb == 0)
            def _():
                dkv_ref[:, cs] = dk
                dkv_ref[:, vs] = dv

            @pl.when(i % tpb != 0)
            def _():
                dkv_ref[:, cs] += dk
                dkv_ref[:, vs] += dv

        dh2 = _dot_nt(dq_ref[...], wq_ref[...])
        dxa, dg = _rms_bwd(x1, g_ref[...], r2, dh2)
        _acc(dg_ref, dg, i == 0)
        dx1 = dx2 + dxa
        dx1_ref[...] = dx1
        dmg_ref[...] = _dot_nt(dx1, wmo_ref[...])

    tile = pl.BlockSpec((tm, D), lambda i: (i, 0))
    kvb = pl.BlockSpec((M, 2 * D), lambda i: (i // tpb, 0))
    B = T // S
    return _pcall(
        body, (x1, kv, dx2, prm["w_mo"], prm["w_q"], prm["w_xo"], prm["g2"]), grid=(T // tm,),
        in_specs=[tile, kvb, tile, _const((D, D)), _const((D, D)), _const((D, D)), _const((1, D))],
        out_specs=[tile, tile, tile, kvb, pl.BlockSpec((1, D), lambda i: (0, 0))],
        out_shape=[_sds((T, D), F32), _sds((T, D), F32), _sds((T, D), BF), _sds((B * M, 2 * D), F32), _sds((1, D), F32)],
        scratch_shapes=[pltpu.VMEM((tm, D), BF), pltpu.VMEM((tm, D), BF)],
        name="attn_bwd", vmem=48, comm=comm)


def _kv_bwd(mem, gm, w_kv, dkv, B, M):
    D = mem.shape[1]
    N = w_kv.shape[1]

    def body(m_ref, g_ref, w_ref, dkv_ref, dg_ref):
        mem_t = m_ref[...]
        _, r = _rms(mem_t, g_ref[...])
        dmn = _dot_nt(dkv_ref[...], w_ref[...])
        _acc(dg_ref, jnp.sum(dmn * (mem_t * r), axis=0, keepdims=True), pl.program_id(0) == 0)

    return pl.pallas_call(
        body, grid=(B,),
        in_specs=[pl.BlockSpec((M, D), lambda b: (b, 0)), _const((1, D)), _const((D, N)),
                  pl.BlockSpec((M, N), lambda b: (b, 0))],
        out_specs=pl.BlockSpec((1, D), lambda b: (0, 0)),
        out_shape=_sds((1, D), F32),
        name="kv_bwd", compiler_params=_cp(32))(mem, gm, w_kv, dkv)


def _branch_bwd(c, p, dmerged, prm, tm=TM3, comm=None):
    T, D = c.shape

    def body(c_ref, bu_ref, bv_ref, ga_ref, gb_ref, dm_ref, wco_ref, wso_ref, lag_ref, lab_ref, lsg_ref, lsb_ref,
             wm_ref, bz_ref, bg_ref,
             dc_ref, dpb_ref, dya_ref, dyb_ref, dwm_ref, dbz_ref, dlag_ref, dlab_ref, dlsg_ref, dlsb_ref, dbg_ref,
             z_scr, v_scr, sa_scr, sg_scr, dv_scr):
        first = pl.program_id(0) == 0
        s_a, ln_a, sig_a, xh_a, rstd_a = _branch_a(c_ref[...], lag_ref[...], lab_ref[...])
        sa_scr[...] = s_a.astype(BF)
        y_a = jnp.dot(sa_scr[...], wco_ref[...], preferred_element_type=F32)
        bu, bv = bu_ref[...], bv_ref[...]
        sg, u, tu, z, tv, vh, rstd_v = _branch_b(bu, bv, lsg_ref[...], lsb_ref[...], wm_ref, bz_ref, z_scr, v_scr)
        sg_scr[...] = sg.astype(BF)
        y_b = jnp.dot(sg_scr[...], wso_ref[...], preferred_element_type=F32)
        ga = _sig(ga_ref[...] + bg_ref[pl.ds(0, 1), :])
        gb = _sig(gb_ref[...] + bg_ref[pl.ds(1, 1), :])
        dm = dm_ref[...]
        dga = dm * y_a * ga * (1.0 - ga)
        dgb = dm * y_b * gb * (1.0 - gb)
        dpb_ref[:, pl.ds(2 * D, D)] = dga.astype(BF)
        dpb_ref[:, pl.ds(3 * D, D)] = dgb.astype(BF)
        _acc(dbg_ref.at[pl.ds(0, 1), :], jnp.sum(dga, axis=0, keepdims=True), first)
        _acc(dbg_ref.at[pl.ds(1, 1), :], jnp.sum(dgb, axis=0, keepdims=True), first)
        dya_ref[...] = (dm * ga).astype(BF)
        dyb_ref[...] = (dm * gb).astype(BF)
        dln = _dot_nt(dya_ref[...], wco_ref[...]) * _dsilu(ln_a, sig_a)
        dc, dlag, dlab = _ln_bwd(xh_a, rstd_a, lag_ref[...], dln)
        dc_ref[...] = dc
        _acc(dlag_ref, dlag, first)
        _acc(dlab_ref, dlab, first)
        dsg = _dot_nt(dyb_ref[...], wso_ref[...])
        dpb_ref[:, pl.ds(0, D)] = (dsg * z * _dgelu(bu, tu)).astype(BF)
        dz = dsg * u
        z_scr[...] = dz
        mask = _tril_mask()

        @pl.when(first)
        def _():
            dwm_ref[...] = jnp.zeros_like(dwm_ref)
            dbz_ref[...] = jnp.zeros_like(dbz_ref)

        for gi in range(SGU_GROUPS):
            wm = jnp.where(mask, wm_ref[gi], 0.0).astype(BF)
            cols = pl.ds(gi * LANES, LANES)
            for n in range(tm // LANES):
                rows = pl.ds(n * LANES, LANES)
                dzb = z_scr[rows, cols].astype(BF)
                dv_scr[rows, cols] = lax.dot_general(wm, dzb, (((0,), (0,)), ((), ())), preferred_element_type=F32)
                dw = lax.dot_general(dzb, v_scr[rows, cols], (((1,), (1,)), ((), ())), preferred_element_type=F32)
                dwm_ref[gi] += jnp.where(mask, dw, 0.0)
                dbz_ref[:, cols] += z_scr[rows, cols]
        dgv, dlsg, dlsb = _ln_bwd(vh, rstd_v, lsg_ref[...], dv_scr[...])
        _acc(dlsg_ref, dlsg, first)
        _acc(dlsb_ref, dlsb, first)
        dpb_ref[:, pl.ds(D, D)] = (dgv * _dgelu(bv, tv)).astype(BF)

    tile = lambda j: pl.BlockSpec((tm, D), lambda i: (i, j))
    vec = pl.BlockSpec((1, D), lambda i: (0, 0))
    return _pcall(
        body, (c, p, p, p, p, dmerged, prm["w_co"], prm["w_so"], prm["la_g"], prm["la_b"], prm["ls_g"], prm["ls_b"],
               prm["sgu_w"], prm["bz"], prm["b_gate"]),
        grid=(T // tm,),
        in_specs=[tile(0), tile(2), tile(3), tile(4), tile(5), tile(0), _const((D, D)), _const((D, D)),
                  _const((1, D)), _const((1, D)), _const((1, D)), _const((1, D)),
                  _const((SGU_GROUPS, LANES, LANES)), _const((LANES, D)), _const((2, D))],
        out_specs=[tile(0), pl.BlockSpec((tm, 4 * D), lambda i: (i, 0)), tile(0), tile(0),
                   pl.BlockSpec((SGU_GROUPS, LANES, LANES), lambda i: (0, 0, 0)),
                   pl.BlockSpec((LANES, D), lambda i: (0, 0)), vec, vec, vec, vec,
                   pl.BlockSpec((2, D), lambda i: (0, 0))],
        out_shape=[_sds((T, D), F32), _sds((T, 4 * D), BF), _sds((T, D), BF), _sds((T, D), BF),
                   _sds((SGU_GROUPS, LANES, LANES), F32), _sds((LANES, D), F32),
                   _sds((1, D), F32), _sds((1, D), F32), _sds((1, D), F32), _sds((1, D), F32), _sds((2, D), F32)],
        scratch_shapes=[pltpu.VMEM((tm, D), F32), pltpu.VMEM((tm, D), BF), pltpu.VMEM((tm, D), BF),
                        pltpu.VMEM((tm, D), BF), pltpu.VMEM((tm, D), F32)],
        name="branch_bwd", vmem=56, comm=comm)


def _conv_bwd(p, dc, conv_w, B, S, comm=None):
    K, D = conv_w.shape
    nc = D // LANES

    def body(av_ref, ag_ref, dc_ref, w_ref, dav_ref, dag_ref, dw_ref, db_ref, apad, dpad):
        b = pl.program_id(1)
        av = av_ref[...]
        sg = _sig(ag_ref[...])
        apad[pl.ds(0, CONV_PAD), :] = jnp.zeros((CONV_PAD, LANES), F32)
        apad[pl.ds(CONV_PAD, S), :] = av * sg
        dpad[pl.ds(S, CONV_PAD), :] = jnp.zeros((CONV_PAD, LANES), F32)
        dpad[pl.ds(0, S), :] = dc_ref[...]

        @pl.when(b == 0)
        def _():
            dw_ref[...] = jnp.zeros_like(dw_ref)
            db_ref[...] = jnp.zeros_like(db_ref)

        db_ref[...] += jnp.sum(dc_ref[...], axis=0, keepdims=True)
        for k in range(K):
            tot = jnp.zeros((1, LANES), F32)
            for r0 in range(0, S, CONV_ROWS):
                tot = tot + jnp.sum(dpad[pl.ds(r0, CONV_ROWS), :] * apad[pl.ds(r0 + k + CONV_PAD - (K - 1), CONV_ROWS), :],
                                    axis=0, keepdims=True)
            dw_ref[pl.ds(k, 1), :] += tot
        for r0 in range(0, S, CONV_ROWS):
            da = jnp.zeros((CONV_ROWS, LANES), F32)
            for k in range(K):
                da = da + w_ref[pl.ds(k, 1), :] * dpad[pl.ds(r0 + (K - 1) - k, CONV_ROWS), :]
            rows = pl.ds(r0, CONV_ROWS)
            s = sg[r0:r0 + CONV_ROWS, :]
            a_v = av[r0:r0 + CONV_ROWS, :]
            dav_ref[rows, :] = (da * s).astype(BF)
            dag_ref[rows, :] = (da * a_v * s * (1.0 - s)).astype(BF)

    blk = lambda off: pl.BlockSpec((S, LANES), lambda j, b: (b, off + j))
    return _pcall(
        body, (p, p, dc, conv_w), grid=(nc, B),
        in_specs=[blk(0), blk(nc), blk(0), pl.BlockSpec((K, LANES), lambda j, b: (0, j))],
        out_specs=[blk(0), blk(0), pl.BlockSpec((K, LANES), lambda j, b: (0, j)), pl.BlockSpec((1, LANES), lambda j, b: (0, j))],
        out_shape=[_sds((B * S, D), BF), _sds((B * S, D), BF), _sds((K, D), F32), _sds((1, D), F32)],
        scratch_shapes=[pltpu.VMEM((S + CONV_PAD, LANES), F32), pltpu.VMEM((S + CONV_PAD, LANES), F32)],
        name="conv_bwd", vmem=32, comm=comm)


TM1 = 512


def _in_proj_bwd(x, dx1, dps, g1, w_in, tm=TM1, comm=None):
    T, D = x.shape
    N = w_in.shape[1]
    widths = [d.shape[1] for d in dps]

    def body(x_ref, dx1_ref, *refs):
        dp_refs, (g_ref, w_ref, dx_ref, dg_ref) = refs[:len(dps)], refs[len(dps):]
        x_t = x_ref[...]
        _, r = _rms(x_t, g_ref[...])
        dh = jnp.zeros((tm, D), F32)
        for q, dp_ref in enumerate(dp_refs):
            cols = pl.ds(sum(widths[:q]), widths[q])
            dh = dh + lax.dot_general(dp_ref[...], w_ref[:, cols], (((1,), (1,)), ((), ())), preferred_element_type=F32)
        dxa, dg = _rms_bwd(x_t, g_ref[...], r, dh)
        dx_ref[...] = dx1_ref[...] + dxa
        _acc(dg_ref, dg, pl.program_id(0) == 0)

    tile = pl.BlockSpec((tm, D), lambda i: (i, 0))
    return _pcall(
        body, (x, dx1, *dps, g1, w_in), grid=(T // tm,),
        in_specs=[tile, tile] + [pl.BlockSpec((tm, w), lambda i: (i, 0)) for w in widths] + [_const((1, D)), _const((D, N))],
        out_specs=[tile, pl.BlockSpec((1, D), lambda i: (0, 0))],
        out_shape=[_sds((T, D), F32), _sds((1, D), F32)],
        name="in_proj_bwd", vmem=48, comm=comm)


def _pick(n, cands):
    for c in cands:
        if n % c == 0:
            return c
    raise ValueError(f"no tile of {cands} divides {n}")


def _mm_tn(x, dys, name):
    T, K = x.shape
    dys = list(dys) if isinstance(dys, (list, tuple)) else [dys]
    widths = [d.shape[1] for d in dys]
    N = sum(widths)
    tm = _pick(T, (1024, 512, 256))
    tk = _pick(K, (1024, 1408, 512))
    tn = _pick(math.gcd(*widths), (1024, 1408, 512))
    nt = T // tm
    first = [sum(widths[:q]) // tn for q in range(len(dys))]
    count = [w // tn for w in widths]

    def body(x_ref, *refs):
        dy_refs, (o_ref, ob_ref, acc) = refs[:len(dys)], refs[len(dys):]
        j, t = pl.program_id(1), pl.program_id(2)

        @pl.when(t == 0)
        def _():
            acc[...] = jnp.zeros_like(acc)

        for q, dy_ref in enumerate(dy_refs):
            @pl.when(jnp.logical_and(j >= first[q], j < first[q] + count[q]))
            def _():
                acc[...] += _dot_tn(x_ref[...], dy_ref[...])

        @pl.when(t == nt - 1)
        def _():
            o_ref[...] = acc[...]
            ob_ref[...] = acc[...].astype(BF)

    def dy_spec(q):
        def index(i, j, t):
            mine = jnp.logical_and(j >= first[q], j < first[q] + count[q])
            return jnp.where(mine, t, 0), jnp.clip(j - first[q], 0, count[q] - 1)
        return pl.BlockSpec((tm, tn), index)

    return pl.pallas_call(
        body, grid=(K // tk, N // tn, nt),
        in_specs=[pl.BlockSpec((tm, tk), lambda i, j, t: (t, i))] + [dy_spec(q) for q in range(len(dys))],
        out_specs=[pl.BlockSpec((tk, tn), lambda i, j, t: (i, j)), pl.BlockSpec((tk, tn), lambda i, j, t: (i, j))],
        out_shape=[_sds((K, N), F32), _sds((K, N), BF)],
        scratch_shapes=[pltpu.VMEM((tk, tn), F32)],
        name=name, compiler_params=_cp(48))(x, *dys)


def _swap_cores(parts):
    n = len(parts)

    def body(*refs):
        ins, outs = refs[:n], refs[n:2 * n]
        send, recv = refs[2 * n:]
        x, y, c, _ = _place()

        def copy(t):
            return pltpu.make_async_remote_copy(
                src_ref=ins[t], dst_ref=outs[t], send_sem=send.at[t], recv_sem=recv.at[t],
                device_id=(x, y, 1 - c), device_id_type=MESH)

        for t in range(n):
            copy(t).start()
        for t in range(n):
            copy(t).wait_recv()
            copy(t).wait_send()

    return pl.pallas_call(
        body, in_specs=[ANY] * n, out_specs=[ANY] * n, out_shape=[_sds(p.shape, p.dtype) for p in parts],
        scratch_shapes=[pltpu.SemaphoreType.DMA((n,)), pltpu.SemaphoreType.DMA((n,))],
        name="swap_cores")(*parts)


def _all_reduce_small(packs):
    n = len(packs)

    def body(*refs):
        p_refs, sum_refs, lands = refs[:n], refs[n:2 * n], refs[2 * n:3 * n]
        send, recv = refs[3 * n:]
        x, y, c, _ = _place()
        me = 4 * x + 2 * y + c
        copies = []
        for t in range(n):
            lands[t][me] = p_refs[t][...]
            for mask in range(1, N_DEV):
                peer = ((1 - x) if mask & 4 else x, (1 - y) if mask & 2 else y, (1 - c) if mask & 1 else c)
                src = peer[0] * 4 + peer[1] * 2 + peer[2]
                sems = dict(send_sem=send.at[N_DEV * t + mask], recv_sem=recv.at[N_DEV * t + mask])
                sent = pltpu.make_async_remote_copy(
                    src_ref=p_refs[t], dst_ref=lands[t].at[me], device_id=peer, device_id_type=MESH, **sems)
                sent.start()
                landed = pltpu.make_async_remote_copy(
                    src_ref=p_refs[t], dst_ref=lands[t].at[src], device_id=peer, device_id_type=MESH, **sems)
                copies.append((sent, landed))
        for sent, landed in copies:
            landed.wait_recv()
            sent.wait_send()
        for t in range(n):
            tot = lands[t][0]
            for d in range(1, N_DEV):
                tot = tot + lands[t][d]
            sum_refs[t][...] = tot

    vm = pl.BlockSpec(memory_space=pltpu.VMEM)
    dma = pltpu.SemaphoreType.DMA
    return pl.pallas_call(
        body, in_specs=[vm] * n, out_specs=[vm] * n, out_shape=[_sds(p.shape, F32) for p in packs],
        scratch_shapes=[pltpu.VMEM((N_DEV,) + p.shape, F32) for p in packs] + [dma((N_DEV * n,)), dma((N_DEV * n,))],
        name="all_reduce_small", compiler_params=_cp(32))(*packs)


def _row_tile(R):
    return _pick(R, (128, 64, 32, 16, 8)) if R % 8 == 0 else R


def _sum_landed(full, land, kind, chip, name):
    _, R, C = land.shape
    tr = _row_tile(R)
    nb = R // tr

    def body(k_ref, o_ref, l_ref, s_ref):
        s_ref[...] = ((o_ref[...] + l_ref[0].astype(F32)) + l_ref[1].astype(F32)) + l_ref[2].astype(F32)

    own = (pl.BlockSpec((tr, C), lambda i, k: (k[0] * nb + i, 0)) if kind == "row"
           else pl.BlockSpec((tr, C), lambda i, k: (i, k[0])))
    return pl.pallas_call(
        body,
        grid_spec=pltpu.PrefetchScalarGridSpec(
            num_scalar_prefetch=1, grid=(nb,),
            in_specs=[own, pl.BlockSpec((3, tr, C), lambda i, k: (0, i, 0))],
            out_specs=pl.BlockSpec((tr, C), lambda i, k: (i, 0))),
        out_shape=_sds((R, C), F32), name=name, compiler_params=_cp(32))(chip, full, land)


def _adamw(g, w, m, v):
    m = ADAM_B1 * m + (1.0 - ADAM_B1) * g
    v = ADAM_B2 * v + (1.0 - ADAM_B2) * (g * g)
    m_hat = m / (1.0 - ADAM_B1 ** ADAM_STEP)
    v_hat = v / (1.0 - ADAM_B2 ** ADAM_STEP)
    return -ADAM_LR * (m_hat / (jnp.sqrt(v_hat) + ADAM_EPS) + ADAM_WD * w), m, v


def _update(parts, w, m, v, name):
    R, C = w.shape
    tr = _row_tile(R)
    k = len(parts)

    def body(*refs):
        g = refs[0][...]
        for r in refs[1:k]:
            g = g + r[...]
        w_ref, m_ref, v_ref, g_out, d_out, m_out, v_out = refs[k:]
        d, m_new, v_new = _adamw(g, w_ref[...], m_ref[...], v_ref[...])
        g_out[...] = g
        d_out[...] = d
        m_out[...] = m_new
        v_out[...] = v_new

    blk = pl.BlockSpec((tr, C), lambda i: (i, 0))
    return pl.pallas_call(
        body, grid=(R // tr,), in_specs=[blk] * (k + 3), out_specs=[blk] * 4, out_shape=[_sds((R, C), F32)] * 4,
        name=name, compiler_params=_cp(40))(*parts, w, m, v)


VECS = ("norm_mix", "conv_b", "conv_ln_g", "conv_ln_b", "sgu_ln_g", "sgu_ln_b", "norm_xattn", "norm_mem", "norm_ffn",
        "norm_final")
SMALL = VECS + ("b_gate", "conv_w", "sgu_w", "sgu_b")


def _update_small(tot_vec, tot_w, chip, a):
    R, D = tot_vec.shape
    Dq, G, K, nv = D // N_CHIPS, SGU_GROUPS, a["conv_w"].shape[1], len(VECS)

    def as2d(nm, arr):
        if nm in VECS:
            return arr.reshape(1, D)
        if nm == "sgu_w":
            return arr.reshape(G * LANES, LANES)
        return arr.reshape(G, LANES) if nm == "sgu_b" else arr[0]

    params = [as2d(nm, a[pre + nm]) for nm in SMALL for pre in ("", "m_", "v_")]

    def body(k_ref, tv, tvc, tw, *refs):
        prm, outs = refs[:3 * len(SMALL)], refs[3 * len(SMALL):]
        grads = [tv[pl.ds(i, 1), :] for i in range(nv)]
        grads += [tvc[pl.ds(nv, 2), :], tvc[pl.ds(nv + 2, K), :], tw[pl.ds(0, G * LANES), :], tw[pl.ds(G * LANES, G), :]]
        for i, g in enumerate(grads):
            d, m_new, v_new = _adamw(g, prm[3 * i][...], prm[3 * i + 1][...], prm[3 * i + 2][...])
            for o_ref, val in zip(outs[4 * i:4 * i + 4], (g, d, m_new, v_new)):
                o_ref[...] = val

    whole = lambda shape: pl.BlockSpec(tuple(shape), lambda i, k: (0,) * len(shape))
    res = pl.pallas_call(
        body,
        grid_spec=pltpu.PrefetchScalarGridSpec(
            num_scalar_prefetch=1, grid=(1,),
            in_specs=[whole(tot_vec.shape), pl.BlockSpec((R, Dq), lambda i, k: (0, k[0])), whole(tot_w.shape)]
            + [whole(p.shape) for p in params],
            out_specs=[whole(params[3 * i].shape) for i in range(len(SMALL)) for _ in range(4)]),
        out_shape=[_sds(params[3 * i].shape, F32) for i in range(len(SMALL)) for _ in range(4)],
        name="upd_small", compiler_params=_cp(32))(chip, tot_vec, tot_vec, tot_w, *params)
    return {nm: list(res[4 * i:4 * i + 4]) for i, nm in enumerate(SMALL)}


BIG = ("w_in", "w_conv_out", "w_sgu_out", "w_mix_out", "w_q", "w_kv", "w_xo", "w_gu", "w_down")
BIG_KIND = {"w_in": "col", "w_conv_out": "row", "w_sgu_out": "row", "w_mix_out": "row", "w_q": "row",
            "w_kv": "col", "w_xo": "row", "w_gu": "col", "w_down": "row"}

def _step(a):
    x3d, mem3d, tgt3d = a["x"], a["mem"], a["loss_target"]
    B, S, D = x3d.shape
    M = mem3d.shape[1]
    T = B * S
    x = x3d.reshape(T, D)
    mem = mem3d.reshape(B * M, D)
    tgt = tgt3d.reshape(T, D)
    xi, yi = lax.axis_index("x"), lax.axis_index("y")
    chip = 2 * xi + yi

    def gather(names):
        return _Gather([a[nm][0] if nm in ("b_gate", "conv_w") else a[nm][0].astype(BF) for nm in names],
                       [BIG_KIND.get(nm, "col") for nm in names])

    first = ("w_in", "b_gate", "conv_w")
    on_in_proj = ("w_conv_out", "w_sgu_out", "w_kv", "w_mix_out", "w_q", "w_xo")
    full = dict(zip(first, _comm_call(gather(first), "gather_w_in")))
    (p, h1), got = _in_proj(x, a["norm_mix"], full["w_in"], comm=gather(on_in_proj))
    full.update(zip(on_in_proj, got))
    (c,), got = _conv_fwd(p, full["conv_w"], a["conv_b"], B, S, comm=gather(("w_down",)))
    full["w_down"] = got[0]

    sgu_b = a["sgu_b"][0]
    bz = jnp.repeat(jnp.transpose(sgu_b), LANES, axis=1)
    prm = dict(w_co=full["w_conv_out"], w_so=full["w_sgu_out"], w_mo=full["w_mix_out"], w_q=full["w_q"],
               w_xo=full["w_xo"], w_down=full["w_down"],
               la_g=a["conv_ln_g"], la_b=a["conv_ln_b"], ls_g=a["sgu_ln_g"], ls_b=a["sgu_ln_b"],
               sgu_w=a["sgu_w"][0], bz=bz, b_gate=full["b_gate"], g2=a["norm_xattn"], g3=a["norm_ffn"],
               gf=a["norm_final"].reshape(1, D))

    (merged, s_a, sg), got = _branch_fwd(c, p, prm, comm=gather(("w_gu",)))
    prm["w_gu"] = got[0]
    mem_n, kv = _kv_fwd(mem, a["norm_mem"], full["w_kv"], B, M)
    x1, x2, h2, o = _attn_fwd(x, merged, kv, prm, S, M)
    dx2, dx3, dgu, h3, f, lsum, d_g3, d_gf = _ffn_loss(x2, tgt, prm)
    loss = lax.psum(0.5 * jnp.sum(lsum) / D, ("x", "y", "c"))

    size_of = {nm: a[nm].shape[1] if BIG_KIND[nm] == "row" else a[nm].shape[2] for nm in BIG}
    landed = {}

    def scatter(names):
        return _Scatter([gw[nm][1] for nm in names], [BIG_KIND[nm] for nm in names], [size_of[nm] for nm in names])

    gw = {}
    gw["w_down"] = _mm_tn(f, dx3, "dw_down")
    gw["w_gu"] = _mm_tn(h3, dgu, "dw_gu")
    (dx1, dmerged, dq, dkv, d_g2), got = _attn_bwd(x1, kv, dx2, prm, S, M, comm=scatter(("w_gu",)))
    landed["w_gu"] = got[0]
    gw["w_xo"] = _mm_tn(o, dx2, "dw_xo")
    gw["w_q"] = _mm_tn(h2, dq, "dw_q")
    gw["w_kv"] = _mm_tn(mem_n, dkv, "dw_kv")
    d_gm = _kv_bwd(mem, a["norm_mem"], full["w_kv"], dkv, B, M)
    gw["w_mix_out"] = _mm_tn(merged, dx1, "dw_mix_out")
    group = ("w_down", "w_xo", "w_q", "w_kv", "w_mix_out")
    (dc, dpb, dya, dyb, d_wm, d_bz, d_lag, d_lab, d_lsg, d_lsb, d_bg), got = _branch_bwd(
        c, p, dmerged, prm, comm=scatter(group))
    landed.update(zip(group, got))
    gw["w_conv_out"] = _mm_tn(s_a, dya, "dw_conv_out")
    gw["w_sgu_out"] = _mm_tn(sg, dyb, "dw_sgu_out")
    group = ("w_conv_out", "w_sgu_out")
    (dav, dag, d_cw, d_cb), got = _conv_bwd(p, dc, full["conv_w"], B, S, comm=scatter(group))
    landed.update(zip(group, got))
    dp = [dav, dag, dpb]
    gw["w_in"] = _mm_tn(h1, dp, "dw_in")
    (grad_x, d_g1), got = _in_proj_bwd(x, dx1, dp, a["norm_mix"], full["w_in"], comm=scatter(("w_in",)))
    landed["w_in"] = got[0]

    chip_arr = jnp.reshape(chip, (1,)).astype(jnp.int32)
    part = [_sum_landed(gw[nm][0], landed[nm], BIG_KIND[nm], chip_arr, "sum_" + nm) for nm in BIG]
    other = _swap_cores(part)
    out = {}
    for nm, mine, theirs in zip(BIG, part, other):
        res = _update([mine, theirs], a[nm][0], a["m_" + nm][0], a["v_" + nm][0], "upd_" + nm)
        out[nm] = [r[None] for r in res]

    G = SGU_GROUPS
    d_sb = jnp.transpose(d_bz.reshape(LANES, G, LANES).sum(axis=-1))
    vec_g = dict(norm_mix=d_g1, conv_b=d_cb, conv_ln_g=d_lag, conv_ln_b=d_lab, sgu_ln_g=d_lsg, sgu_ln_b=d_lsb,
                 norm_xattn=d_g2, norm_mem=d_gm, norm_ffn=d_g3, norm_final=d_gf)
    rows = [vec_g[nm] for nm in VECS] + [d_bg, d_cw]
    pad = (-sum(r.shape[0] for r in rows)) % 8
    g_vec = jnp.concatenate(rows + [jnp.zeros((pad, D), F32)], axis=0)
    g_w = jnp.concatenate([d_wm.reshape(G * LANES, LANES), d_sb], axis=0)
    tot_vec, tot_w = _all_reduce_small([g_vec, g_w])
    for nm, res in _update_small(tot_vec, tot_w, chip_arr, a).items():
        out[nm] = [r.reshape(a[nm].shape) for r in res]
    return loss, grad_x.reshape(B, S, D), out


WEIGHTS = ("norm_mix", "w_in", "b_gate", "conv_w", "conv_b", "conv_ln_g", "conv_ln_b", "w_conv_out", "sgu_ln_g",
           "sgu_ln_b", "sgu_w", "sgu_b", "w_sgu_out", "w_mix_out", "norm_xattn", "norm_mem", "w_q", "w_kv", "w_xo",
           "norm_ffn", "w_gu", "w_down", "norm_final")


def kernel(x, mem, norm_mix, w_in, b_gate, conv_w, conv_b, conv_ln_g, conv_ln_b, w_conv_out, sgu_ln_g, sgu_ln_b, sgu_w, sgu_b, w_sgu_out, w_mix_out, norm_xattn, norm_mem, w_q, w_kv, w_xo, norm_ffn, w_gu, w_down, norm_final, loss_target, m_norm_mix, m_w_in, m_b_gate, m_conv_w, m_conv_b, m_conv_ln_g, m_conv_ln_b, m_w_conv_out, m_sgu_ln_g, m_sgu_ln_b, m_sgu_w, m_sgu_b, m_w_sgu_out, m_w_mix_out, m_norm_xattn, m_norm_mem, m_w_q, m_w_kv, m_w_xo, m_norm_ffn, m_w_gu, m_w_down, m_norm_final, v_norm_mix, v_w_in, v_b_gate, v_conv_w, v_conv_b, v_conv_ln_g, v_conv_ln_b, v_w_conv_out, v_sgu_ln_g, v_sgu_ln_b, v_sgu_w, v_sgu_b, v_w_sgu_out, v_w_mix_out, v_norm_xattn, v_norm_mem, v_w_q, v_w_kv, v_w_xo, v_norm_ffn, v_w_gu, v_w_down, v_norm_final):
    a = dict(locals())
    loss, grad_x, out = _step(a)
    res = [loss, grad_x]
    for q in range(4):
        res += [out[nm][q] for nm in WEIGHTS]
    return tuple(res)
```

```python
import functools
import math

import jax
import jax.numpy as jnp
from jax import lax
from jax.experimental import pallas as pl
from jax.experimental.pallas import tpu as pltpu

BF = jnp.bfloat16
F32 = jnp.float32
MESH = pl.DeviceIdType.MESH
ANY = pl.BlockSpec(memory_space=pl.ANY)

RMS_EPS = 1e-6
LN_EPS = 1e-5
HEADS = 4
SGU_GROUPS = 8
LANES = 128
ADAM_LR = 0.001
ADAM_B1 = 0.9
ADAM_B2 = 0.999
ADAM_EPS = 1e-08
ADAM_WD = 0.01
ADAM_STEP = 10
N_CHIPS = 4
N_DEV = 8
MIB = 1024 * 1024


def _sds(shape, dtype):
    return jax.ShapeDtypeStruct(tuple(shape), dtype)


def _cp(vmem_mib):
    return pltpu.CompilerParams(vmem_limit_bytes=vmem_mib * MIB)


def _const(shape):
    nd = len(shape)
    return pl.BlockSpec(tuple(shape), lambda *_: (0,) * nd, pipeline_mode=pl.Buffered(1))


def _dot(a, b):
    return jnp.dot(a.astype(BF), b.astype(BF), preferred_element_type=F32)


def _dot_nt(a, b):
    return lax.dot_general(a.astype(BF), b.astype(BF), (((1,), (1,)), ((), ())), preferred_element_type=F32)


def _dot_tn(a, b):
    return lax.dot_general(a.astype(BF), b.astype(BF), (((0,), (0,)), ((), ())), preferred_element_type=F32)


def _sig(x):
    return 1.0 / (1.0 + jnp.exp(-x))


def _dsilu(x, s):
    return s * (1.0 + x * (1.0 - s))


_GELU_C = math.sqrt(2.0 / math.pi)


def _gelu(x):
    t = jnp.tanh(_GELU_C * (x + 0.044715 * (x * x * x)))
    return 0.5 * x * (1.0 + t), t


def _dgelu(x, t):
    return 0.5 * (1.0 + t) + 0.5 * x * (1.0 - t * t) * (_GELU_C * (1.0 + 3.0 * 0.044715 * (x * x)))


def _rms(x, g):
    r = lax.rsqrt(jnp.mean(x * x, axis=-1, keepdims=True) + RMS_EPS)
    return x * r * g, r


def _rms_bwd(x, g, r, dh):
    xr = x * r
    dxh = dh * g
    dx = r * (dxh - xr * jnp.mean(dxh * xr, axis=-1, keepdims=True))
    return dx, jnp.sum(dh * xr, axis=0, keepdims=True)


def _ln(x, g, b):
    mu = jnp.mean(x, axis=-1, keepdims=True)
    xc = x - mu
    rstd = lax.rsqrt(jnp.mean(xc * xc, axis=-1, keepdims=True) + LN_EPS)
    xh = xc * rstd
    return xh * g + b, xh, rstd


def _ln_bwd(xh, rstd, g, dy):
    dxh = dy * g
    dx = rstd * (dxh - jnp.mean(dxh, axis=-1, keepdims=True) - xh * jnp.mean(dxh * xh, axis=-1, keepdims=True))
    return dx, jnp.sum(dy * xh, axis=0, keepdims=True), jnp.sum(dy, axis=0, keepdims=True)


def _acc(ref, val, first):
    @pl.when(first)
    def _():
        ref[...] = val

    @pl.when(jnp.logical_not(first))
    def _():
        ref[...] += val


def _place():
    x, y, c = lax.axis_index("x"), lax.axis_index("y"), lax.axis_index("c")
    chips = [(1 - x, y), (x, 1 - y), (1 - x, 1 - y)]
    return x, y, c, chips


def _shard_of(ref, kind, k, n):
    if kind == "row":
        return ref.at[pl.ds(k * n, n), :]
    return ref.at[:, pl.ds(k * n, n)]


class _Gather:
    def __init__(self, shards, kinds):
        n = len(shards)
        self.srcs, self.kinds = list(shards), list(kinds)
        self.sizes = [s.shape[0] if kd == "row" else s.shape[1] for s, kd in zip(shards, kinds)]
        self.halves = [s.shape[0] // 2 if s.shape[0] % 32 == 0 else None for s in shards]
        self.out_shape = [
            _sds((s.shape[0] * N_CHIPS, s.shape[1]) if kd == "row" else (s.shape[0], s.shape[1] * N_CHIPS), s.dtype)
            for s, kd in zip(shards, kinds)]
        dma = pltpu.SemaphoreType.DMA
        self.sems = [dma((3 * n,)), dma((3 * n,)), dma((n,)), dma((3 * n,)), dma((3 * n,))]

    def _part(self, ref, t, core):
        h = self.halves[t]
        return ref if h is None else ref.at[pl.ds(core * h, h), :]

    def _copies(self, ins, outs, send, recv, loc, fsend, frecv):
        x, y, c, chips = _place()
        k = 2 * x + y
        local, remote = [], []
        for t in range(len(ins)):
            block = lambda q: _shard_of(outs[t], self.kinds[t], q, self.sizes[t])
            local.append(pltpu.make_async_copy(ins[t], block(k), loc.at[t]))
            for j, (px, py) in enumerate(chips):
                sems = dict(send_sem=send.at[3 * t + j], recv_sem=recv.at[3 * t + j])
                there = dict(device_id=(px, py, c), device_id_type=MESH)
                sent = pltpu.make_async_remote_copy(
                    src_ref=self._part(ins[t], t, c), dst_ref=self._part(block(k), t, c), **sems, **there)
                got = self._part(block(2 * px + py), t, c)
                landed = pltpu.make_async_remote_copy(src_ref=self._part(ins[t], t, c), dst_ref=got, **sems, **there)
                passed = handed = None
                if self.halves[t] is not None:
                    fsems = dict(send_sem=fsend.at[3 * t + j], recv_sem=frecv.at[3 * t + j])
                    sibling = dict(device_id=(x, y, 1 - c), device_id_type=MESH)
                    passed = pltpu.make_async_remote_copy(src_ref=got, dst_ref=got, **fsems, **sibling)
                    other = self._part(block(2 * px + py), t, 1 - c)
                    handed = pltpu.make_async_remote_copy(src_ref=got, dst_ref=other, **fsems, **sibling)
                remote.append((sent, landed, passed, handed))
        return local, remote

    def start(self, ins, outs, *sems):
        local, remote = self._copies(ins, outs, *sems)
        for cp in local:
            cp.start()
        for sent, _, _, _ in remote:
            sent.start()

    def wait(self, ins, outs, *sems):
        local, remote = self._copies(ins, outs, *sems)
        for sent, landed, passed, handed in remote:
            landed.wait_recv()
            if passed is not None:
                passed.start()
        for sent, landed, passed, handed in remote:
            if passed is not None:
                handed.wait_recv()
                passed.wait_send()
            sent.wait_send()
        for cp in local:
            cp.wait()


class _Scatter:
    def __init__(self, grads, kinds, sizes):
        n = len(grads)
        self.srcs, self.kinds, self.sizes = list(grads), list(kinds), list(sizes)
        self.out_shape = [_sds((3,) + ((sz, g.shape[1]) if kd == "row" else (g.shape[0], sz)), g.dtype)
                          for g, kd, sz in zip(grads, kinds, sizes)]
        self.sems = [pltpu.SemaphoreType.DMA((3 * n,)), pltpu.SemaphoreType.DMA((3 * n,))]

    def _copies(self, ins, outs, send, recv):
        x, y, c, chips = _place()
        return [pltpu.make_async_remote_copy(
            src_ref=_shard_of(ins[t], self.kinds[t], 2 * px + py, self.sizes[t]), dst_ref=outs[t].at[j],
            send_sem=send.at[3 * t + j], recv_sem=recv.at[3 * t + j], device_id=(px, py, c), device_id_type=MESH)
            for t in range(len(ins)) for j, (px, py) in enumerate(chips)]

    def start(self, ins, outs, send, recv):
        for cp in self._copies(ins, outs, send, recv):
            cp.start()

    def wait(self, ins, outs, send, recv):
        for cp in self._copies(ins, outs, send, recv):
            cp.wait_recv()
            cp.wait_send()


class _Swap:
    def __init__(self, parts):
        n = len(parts)
        self.srcs = list(parts)
        self.out_shape = [_sds(p.shape, p.dtype) for p in parts]
        self.sems = [pltpu.SemaphoreType.DMA((n,)), pltpu.SemaphoreType.DMA((n,))]

    def _copies(self, ins, outs, send, recv):
        x, y, c, _ = _place()
        return [pltpu.make_async_remote_copy(
            src_ref=ins[t], dst_ref=outs[t], send_sem=send.at[t], recv_sem=recv.at[t],
            device_id=(x, y, 1 - c), device_id_type=MESH) for t in range(len(ins))]

    def start(self, ins, outs, send, recv):
        for cp in self._copies(ins, outs, send, recv):
            cp.start()

    def wait(self, ins, outs, send, recv):
        for cp in self._copies(ins, outs, send, recv):
            cp.wait_recv()
            cp.wait_send()


class _Spread:
    def __init__(self, packs):
        n = len(packs)
        self.srcs = list(packs)
        self.out_shape = [_sds((N_DEV,) + p.shape, p.dtype) for p in packs]
        dma = pltpu.SemaphoreType.DMA
        self.sems = [dma((N_DEV * n,)), dma((N_DEV * n,)), dma((n,))]

    def _copies(self, ins, outs, send, recv, loc):
        x, y, c, _ = _place()
        me = 4 * x + 2 * y + c
        local = [pltpu.make_async_copy(ins[t], outs[t].at[me], loc.at[t]) for t in range(len(ins))]
        remote = []
        for t in range(len(ins)):
            for mask in range(1, N_DEV):
                peer = ((1 - x) if mask & 4 else x, (1 - y) if mask & 2 else y, (1 - c) if mask & 1 else c)
                src = peer[0] * 4 + peer[1] * 2 + peer[2]
                sems = dict(send_sem=send.at[N_DEV * t + mask], recv_sem=recv.at[N_DEV * t + mask])
                sent = pltpu.make_async_remote_copy(
                    src_ref=ins[t], dst_ref=outs[t].at[me], device_id=peer, device_id_type=MESH, **sems)
                landed = pltpu.make_async_remote_copy(
                    src_ref=ins[t], dst_ref=outs[t].at[src], device_id=peer, device_id_type=MESH, **sems)
                remote.append((sent, landed))
        return local, remote

    def start(self, ins, outs, send, recv, loc):
        local, remote = self._copies(ins, outs, send, recv, loc)
        for cp in local:
            cp.start()
        for sent, _ in remote:
            sent.start()

    def wait(self, ins, outs, send, recv, loc):
        local, remote = self._copies(ins, outs, send, recv, loc)
        for sent, landed in remote:
            landed.wait_recv()
            sent.wait_send()
        for cp in local:
            cp.wait()


class _Both:
    def __init__(self, *comms):
        self.comms = comms
        self.srcs = [s for cm in comms for s in cm.srcs]
        self.out_shape = [s for cm in comms for s in cm.out_shape]
        self.sems = [s for cm in comms for s in cm.sems]

    def _each(self, ins, outs, sems):
        i = o = k = 0
        for cm in self.comms:
            ni, no, nk = len(cm.srcs), len(cm.out_shape), len(cm.sems)
            yield cm, ins[i:i + ni], outs[o:o + no], sems[k:k + nk]
            i, o, k = i + ni, o + no, k + nk

    def start(self, ins, outs, *sems):
        for cm, i, o, s in self._each(ins, outs, sems):
            cm.start(i, o, *s)

    def wait(self, ins, outs, *sems):
        for cm, i, o, s in self._each(ins, outs, sems):
            cm.wait(i, o, *s)


def _comm_call(comm, name):
    n, m = len(comm.srcs), len(comm.out_shape)

    def body(*refs):
        comm.start(refs[:n], refs[n:n + m], *refs[n + m:])
        comm.wait(refs[:n], refs[n:n + m], *refs[n + m:])

    return pl.pallas_call(body, in_specs=[ANY] * n, out_specs=[ANY] * m, out_shape=comm.out_shape,
                          scratch_shapes=comm.sems, name=name)(*comm.srcs)


def _pcall(body, args, *, grid, in_specs, out_specs, out_shape, name, vmem, scratch_shapes=(), comm=None):
    in_specs, out_specs, out_shape = list(in_specs), list(out_specs), list(out_shape)
    scratch_shapes = list(scratch_shapes)
    if comm is None:
        res = pl.pallas_call(body, grid=grid, in_specs=in_specs, out_specs=out_specs, out_shape=out_shape,
                             scratch_shapes=scratch_shapes, name=name, compiler_params=_cp(vmem))(*args)
        return list(res), []
    ni, no, ns = len(in_specs), len(out_specs), len(scratch_shapes)
    ci, co = len(comm.srcs), len(comm.out_shape)

    def carried(*refs):
        c_in = refs[ni:ni + ci]
        c_out = refs[ni + ci + no:ni + ci + no + co]
        sems = refs[ni + ci + no + co + ns:]
        ids = [pl.program_id(d) for d in range(len(grid))]
        first = functools.reduce(jnp.logical_and, [i == 0 for i in ids])
        last = functools.reduce(jnp.logical_and, [i == g - 1 for i, g in zip(ids, grid)])

        @pl.when(first)
        def _():
            comm.start(c_in, c_out, *sems)

        body(*refs[:ni], *refs[ni + ci:ni + ci + no], *refs[ni + ci + no + co:ni + ci + no + co + ns])

        @pl.when(last)
        def _():
            comm.wait(c_in, c_out, *sems)

    res = pl.pallas_call(carried, grid=grid, in_specs=in_specs + [ANY] * ci, out_specs=out_specs + [ANY] * co,
                         out_shape=out_shape + list(comm.out_shape), scratch_shapes=scratch_shapes + list(comm.sems),
                         name=name, compiler_params=_cp(vmem))(*args, *comm.srcs)
    return list(res[:no]), list(res[no:])


def _in_proj(x, g1, w_in, comm=None):
    T, D = x.shape
    N = w_in.shape[1]
    tm, tn = 512, 1024

    def body(x_ref, g_ref, w_ref, p_ref, h_ref):
        h, _ = _rms(x_ref[...], g_ref[...])
        h_ref[...] = h.astype(BF)
        for j in range(N // tn):
            cols = pl.ds(j * tn, tn)
            p_ref[:, cols] = jnp.dot(h_ref[...], w_ref[:, cols], preferred_element_type=F32)

    return _pcall(
        body, (x, g1, w_in), grid=(T // tm,),
        in_specs=[pl.BlockSpec((tm, D), lambda i: (i, 0)), _const((1, D)), _const((D, N))],
        out_specs=[pl.BlockSpec((tm, N), lambda i: (i, 0)), pl.BlockSpec((tm, D), lambda i: (i, 0))],
        out_shape=[_sds((T, N), F32), _sds((T, D), BF)],
        name="in_proj", vmem=56, comm=comm)


CONV_PAD = 32
CONV_ROWS = 256


def _conv_fwd(p, conv_w, conv_b, B, S, comm=None):
    K, D = conv_w.shape
    nc = D // LANES

    def body(av_ref, ag_ref, w_ref, b_ref, c_ref, apad):
        apad[pl.ds(0, CONV_PAD), :] = jnp.zeros((CONV_PAD, LANES), F32)
        apad[pl.ds(CONV_PAD, S), :] = av_ref[...] * _sig(ag_ref[...])
        for r0 in range(0, S, CONV_ROWS):
            acc = jnp.zeros((CONV_ROWS, LANES), F32) + b_ref[...]
            for k in range(K):
                acc = acc + w_ref[pl.ds(k, 1), :] * apad[pl.ds(r0 + k + CONV_PAD - (K - 1), CONV_ROWS), :]
            c_ref[pl.ds(r0, CONV_ROWS), :] = acc

    return _pcall(
        body, (p, p, conv_w, conv_b), grid=(B, nc),
        in_specs=[pl.BlockSpec((S, LANES), lambda b, j: (b, j)), pl.BlockSpec((S, LANES), lambda b, j: (b, nc + j)),
                  pl.BlockSpec((K, LANES), lambda b, j: (0, j)), pl.BlockSpec((1, LANES), lambda b, j: (0, j))],
        out_specs=[pl.BlockSpec((S, LANES), lambda b, j: (b, j))],
        out_shape=[_sds((B * S, D), F32)],
        scratch_shapes=[pltpu.VMEM((S + CONV_PAD, LANES), F32)],
        name="conv_fwd", vmem=32, comm=comm)


def _tril_mask():
    t = lax.broadcasted_iota(jnp.int32, (LANES, LANES), 0)
    s = lax.broadcasted_iota(jnp.int32, (LANES, LANES), 1)
    return t >= s


def _branch_a(c, g, b):
    ln_a, xh, rstd = _ln(c, g, b)
    s = _sig(ln_a)
    return ln_a * s, ln_a, s, xh, rstd


def _branch_b(bu, bv, g, b, wm_ref, bz_ref, z_scr, v_scr):
    tm, D = bu.shape
    u, tu = _gelu(bu)
    gv, tv = _gelu(bv)
    v, vh, rstd = _ln(gv, g, b)
    v_scr[...] = v.astype(BF)
    mask = _tril_mask()
    for gi in range(SGU_GROUPS):
        wm = jnp.where(mask, wm_ref[gi], 0.0).astype(BF)
        cols = pl.ds(gi * LANES, LANES)
        for n in range(tm // LANES):
            rows = pl.ds(n * LANES, LANES)
            z_scr[rows, cols] = jnp.dot(wm, v_scr[rows, cols], preferred_element_type=F32) + bz_ref[:, cols]
    z = z_scr[...]
    return u * z, u, tu, z, tv, vh, rstd


TM3 = 256
TM3_FWD = 512


def _branch_fwd(c, p, prm, tm=TM3_FWD, comm=None):
    T, D = c.shape

    def body(c_ref, bu_ref, bv_ref, ga_ref, gb_ref, wco_ref, wso_ref, lag_ref, lab_ref, lsg_ref, lsb_ref, wm_ref,
             bz_ref, bg_ref, mg_ref, sa_ref, sg_ref, z_scr, v_scr):
        s_a = _branch_a(c_ref[...], lag_ref[...], lab_ref[...])[0]
        sa_ref[...] = s_a.astype(BF)
        y_a = jnp.dot(sa_ref[...], wco_ref[...], preferred_element_type=F32)
        sg = _branch_b(bu_ref[...], bv_ref[...], lsg_ref[...], lsb_ref[...], wm_ref, bz_ref, z_scr, v_scr)[0]
        sg_ref[...] = sg.astype(BF)
        y_b = jnp.dot(sg_ref[...], wso_ref[...], preferred_element_type=F32)
        ga = _sig(ga_ref[...] + bg_ref[pl.ds(0, 1), :])
        gb = _sig(gb_ref[...] + bg_ref[pl.ds(1, 1), :])
        mg_ref[...] = (ga * y_a + gb * y_b).astype(BF)

    tile = lambda j: pl.BlockSpec((tm, D), lambda i: (i, j))
    return _pcall(
        body, (c, p, p, p, p, prm["w_co"], prm["w_so"], prm["la_g"], prm["la_b"], prm["ls_g"], prm["ls_b"],
               prm["sgu_w"], prm["bz"], prm["b_gate"]),
        grid=(T // tm,),
        in_specs=[tile(0), tile(2), tile(3), tile(4), tile(5), _const((D, D)), _const((D, D)),
                  _const((1, D)), _const((1, D)), _const((1, D)), _const((1, D)),
                  _const((SGU_GROUPS, LANES, LANES)), _const((LANES, D)), _const((2, D))],
        out_specs=[tile(0), tile(0), tile(0)],
        out_shape=[_sds((T, D), BF)] * 3,
        scratch_shapes=[pltpu.VMEM((tm, D), F32), pltpu.VMEM((tm, D), BF)],
        name="branch_fwd", vmem=48, comm=comm)


def _kv_fwd(mem, gm, w_kv, B, M):
    D = mem.shape[1]
    N = w_kv.shape[1]

    def body(m_ref, g_ref, w_ref, mn_ref, kv_ref):
        h, _ = _rms(m_ref[...], g_ref[...])
        mn_ref[...] = h.astype(BF)
        kv_ref[...] = jnp.dot(mn_ref[...], w_ref[...], preferred_element_type=F32).astype(BF)

    return pl.pallas_call(
        body, grid=(B,),
        in_specs=[pl.BlockSpec((M, D), lambda b: (b, 0)), _const((1, D)), _const((D, N))],
        out_specs=[pl.BlockSpec((M, D), lambda b: (b, 0)), pl.BlockSpec((M, N), lambda b: (b, 0))],
        out_shape=[_sds((B * M, D), BF), _sds((B * M, N), BF)],
        name="kv_fwd", compiler_params=_cp(32))(mem, gm, w_kv)


def _softmax_rows(s):
    e = jnp.exp(s - jnp.max(s, axis=-1, keepdims=True))
    return e / jnp.sum(e, axis=-1, keepdims=True)


TM4 = 512


def _attn_fwd(x, merged, kv, prm, S, M, tm=TM4):
    T, D = x.shape
    hd = D // HEADS
    scale = hd ** -0.5
    tpb = S // tm

    def body(x_ref, mg_ref, kv_ref, wmo_ref, wq_ref, wxo_ref, g_ref, x1_ref, x2_ref, h2_ref, o_ref):
        x1 = x_ref[...] + jnp.dot(mg_ref[...], wmo_ref[...], preferred_element_type=F32)
        x1_ref[...] = x1
        h2, _ = _rms(x1, g_ref[...])
        h2_ref[...] = h2.astype(BF)
        qb = jnp.dot(h2_ref[...], wq_ref[...], preferred_element_type=F32).astype(BF)
        for h in range(HEADS):
            cs = pl.ds(h * hd, hd)
            s = _dot_nt(qb[:, h * hd:(h + 1) * hd], kv_ref[:, cs]) * scale
            pr = _softmax_rows(s)
            o_ref[:, cs] = _dot(pr, kv_ref[:, pl.ds(D + h * hd, hd)]).astype(BF)
        x2_ref[...] = x1 + jnp.dot(o_ref[...], wxo_ref[...], preferred_element_type=F32)

    tile = pl.BlockSpec((tm, D), lambda i: (i, 0))
    return pl.pallas_call(
        body, grid=(T // tm,),
        in_specs=[tile, tile, pl.BlockSpec((M, 2 * D), lambda i: (i // tpb, 0)),
                  _const((D, D)), _const((D, D)), _const((D, D)), _const((1, D))],
        out_specs=[tile, tile, tile, tile],
        out_shape=[_sds((T, D), F32), _sds((T, D), F32), _sds((T, D), BF), _sds((T, D), BF)],
        name="attn_fwd", compiler_params=_cp(40))(x, merged, kv, prm["w_mo"], prm["w_q"], prm["w_xo"], prm["g2"])


TM5 = 256


def _ffn_loss(x2, tgt, prm, tm=TM5):
    T, D = x2.shape
    F = prm["w_down"].shape[0]
    FC = F // 2

    def body(x2_ref, t_ref, wgu_ref, wd_ref, g3_ref, gf_ref, dx2_ref, dx3_ref, dgu_ref, h3_ref, f_ref, ls_ref,
             dg3_ref, dgf_ref, gu_scr):
        first = pl.program_id(0) == 0
        x2 = x2_ref[...]
        h3, r3 = _rms(x2, g3_ref[...])
        h3_ref[...] = h3.astype(BF)
        x3 = x2
        for ch in range(2):
            gc, uc = pl.ds(ch * FC, FC), pl.ds(F + ch * FC, FC)
            gt = jnp.dot(h3_ref[...], wgu_ref[:, gc], preferred_element_type=F32)
            up = jnp.dot(h3_ref[...], wgu_ref[:, uc], preferred_element_type=F32)
            gu_scr[:, gc] = gt
            gu_scr[:, uc] = up
            f_ref[:, gc] = (gt * _sig(gt) * up).astype(BF)
            x3 = x3 + jnp.dot(f_ref[:, gc], wd_ref[gc, :], preferred_element_type=F32)
        y, rf = _rms(x3, gf_ref[...])
        e = y - t_ref[...]
        _acc(ls_ref, jnp.sum(e * e, axis=0, keepdims=True), first)
        dx3, dgf = _rms_bwd(x3, gf_ref[...], rf, e * (1.0 / D))
        _acc(dgf_ref, dgf, first)
        dx3_ref[...] = dx3.astype(BF)
        dh3 = jnp.zeros((tm, D), F32)
        for ch in range(2):
            gc, uc = pl.ds(ch * FC, FC), pl.ds(F + ch * FC, FC)
            df = lax.dot_general(dx3_ref[...], wd_ref[gc, :], (((1,), (1,)), ((), ())), preferred_element_type=F32)
            gt, up = gu_scr[:, gc], gu_scr[:, uc]
            s = _sig(gt)
            dgu_ref[:, gc] = (df * up * _dsilu(gt, s)).astype(BF)
            dgu_ref[:, uc] = (df * gt * s).astype(BF)
            dh3 = dh3 + lax.dot_general(dgu_ref[:, gc], wgu_ref[:, gc], (((1,), (1,)), ((), ())), preferred_element_type=F32)
            dh3 = dh3 + lax.dot_general(dgu_ref[:, uc], wgu_ref[:, uc], (((1,), (1,)), ((), ())), preferred_element_type=F32)
        dxa, dg3 = _rms_bwd(x2, g3_ref[...], r3, dh3)
        _acc(dg3_ref, dg3, first)
        dx2_ref[...] = dx3 + dxa

    tile = lambda n: pl.BlockSpec((tm, n), lambda i: (i, 0))
    vec = pl.BlockSpec((1, D), lambda i: (0, 0))
    return pl.pallas_call(
        body, grid=(T // tm,),
        in_specs=[tile(D), tile(D), _const((D, 2 * F)), _const((F, D)), _const((1, D)), _const((1, D))],
        out_specs=[tile(D), tile(D), tile(2 * F), tile(D), tile(F), vec, vec, vec],
        out_shape=[_sds((T, D), F32), _sds((T, D), BF), _sds((T, 2 * F), BF), _sds((T, D), BF), _sds((T, F), BF),
                   _sds((1, D), F32), _sds((1, D), F32), _sds((1, D), F32)],
        scratch_shapes=[pltpu.VMEM((tm, 2 * F), F32)],
        name="ffn_loss", compiler_params=_cp(56))(x2, tgt, prm["w_gu"], prm["w_down"], prm["g3"], prm["gf"])


def _attn_bwd(x1, kv, dx2, prm, S, M, tm=TM4, comm=None):
    T, D = x1.shape
    hd = D // HEADS
    scale = hd ** -0.5
    tpb = S // tm

    def body(x1_ref, kv_ref, dx2_ref, wmo_ref, wq_ref, wxo_ref, g_ref, dx1_ref, dmg_ref, dq_ref, dkv_ref, dg_ref,
             h2_scr, do_scr):
        i = pl.program_id(0)
        x1 = x1_ref[...]
        dx2 = dx2_ref[...]
        h2, r2 = _rms(x1, g_ref[...])
        h2_scr[...] = h2.astype(BF)
        qb = jnp.dot(h2_scr[...], wq_ref[...], preferred_element_type=F32).astype(BF)
        do_scr[...] = _dot_nt(dx2, wxo_ref[...]).astype(BF)
        for h in range(HEADS):
            cs, vs = pl.ds(h * hd, hd), pl.ds(D + h * hd, hd)
            qh = qb[:, h * hd:(h + 1) * hd]
            pr = _softmax_rows(_dot_nt(qh, kv_ref[:, cs]) * scale)
            dpr = _dot_nt(do_scr[:, cs], kv_ref[:, vs])
            dv = _dot_tn(pr, do_scr[:, cs])
            ds = (pr * (dpr - jnp.sum(dpr * pr, axis=-1, keepdims=True)) * scale).astype(BF)
            dq_ref[:, cs] = jnp.dot(ds, kv_ref[:, cs], preferred_element_type=F32).astype(BF)
            dk = _dot_tn(ds, qh)

            @pl.when(i % tpb == 0)
            def _():
                dkv_ref[:, cs] = dk
                dkv_ref[:, vs] = dv

            @pl.when(i % tpb != 0)
            def _():
                dkv_ref[:, cs] += dk
                dkv_ref[:, vs] += dv

        dh2 = _dot_nt(dq_ref[...], wq_ref[...])
        dxa, dg = _rms_bwd(x1, g_ref[...], r2, dh2)
        _acc(dg_ref, dg, i == 0)
        dx1 = dx2 + dxa
        dx1_ref[...] = dx1
        dmg_ref[...] = _dot_nt(dx1, wmo_ref[...])

    tile = pl.BlockSpec((tm, D), lambda i: (i, 0))
    kvb = pl.BlockSpec((M, 2 * D), lambda i: (i // tpb, 0))
    B = T // S
    return _pcall(
        body, (x1, kv, dx2, prm["w_mo"], prm["w_q"], prm["w_xo"], prm["g2"]), grid=(T // tm,),
        in_specs=[tile, kvb, tile, _const((D, D)), _const((D, D)), _const((D, D)), _const((1, D))],
        out_specs=[tile, tile, tile, kvb, pl.BlockSpec((1, D), lambda i: (0, 0))],
        out_shape=[_sds((T, D), F32), _sds((T, D), F32), _sds((T, D), BF), _sds((B * M, 2 * D), F32), _sds((1, D), F32)],
        scratch_shapes=[pltpu.VMEM((tm, D), BF), pltpu.VMEM((tm, D), BF)],
        name="attn_bwd", vmem=48, comm=comm)


def _kv_bwd(mem, gm, w_kv, dkv, B, M):
    D = mem.shape[1]
    N = w_kv.shape[1]

    def body(m_ref, g_ref, w_ref, dkv_ref, dg_ref):
        mem_t = m_ref[...]
        _, r = _rms(mem_t, g_ref[...])
        dmn = _dot_nt(dkv_ref[...], w_ref[...])
        _acc(dg_ref, jnp.sum(dmn * (mem_t * r), axis=0, keepdims=True), pl.program_id(0) == 0)

    return pl.pallas_call(
        body, grid=(B,),
        in_specs=[pl.BlockSpec((M, D), lambda b: (b, 0)), _const((1, D)), _const((D, N)),
                  pl.BlockSpec((M, N), lambda b: (b, 0))],
        out_specs=pl.BlockSpec((1, D), lambda b: (0, 0)),
        out_shape=_sds((1, D), F32),
        name="kv_bwd", compiler_params=_cp(32))(mem, gm, w_kv, dkv)


def _branch_bwd(c, p, dmerged, prm, tm=TM3, comm=None):
    T, D = c.shape

    def body(c_ref, bu_ref, bv_ref, ga_ref, gb_ref, dm_ref, wco_ref, wso_ref, lag_ref, lab_ref, lsg_ref, lsb_ref,
             wm_ref, bz_ref, bg_ref,
             dc_ref, dpb_ref, dya_ref, dyb_ref, dwm_ref, dbz_ref, dlag_ref, dlab_ref, dlsg_ref, dlsb_ref, dbg_ref,
             z_scr, v_scr, sa_scr, sg_scr, dv_scr):
        first = pl.program_id(0) == 0
        s_a, ln_a, sig_a, xh_a, rstd_a = _branch_a(c_ref[...], lag_ref[...], lab_ref[...])
        sa_scr[...] = s_a.astype(BF)
        y_a = jnp.dot(sa_scr[...], wco_ref[...], preferred_element_type=F32)
        bu, bv = bu_ref[...], bv_ref[...]
        sg, u, tu, z, tv, vh, rstd_v = _branch_b(bu, bv, lsg_ref[...], lsb_ref[...], wm_ref, bz_ref, z_scr, v_scr)
        sg_scr[...] = sg.astype(BF)
        y_b = jnp.dot(sg_scr[...], wso_ref[...], preferred_element_type=F32)
        ga = _sig(ga_ref[...] + bg_ref[pl.ds(0, 1), :])
        gb = _sig(gb_ref[...] + bg_ref[pl.ds(1, 1), :])
        dm = dm_ref[...]
        dga = dm * y_a * ga * (1.0 - ga)
        dgb = dm * y_b * gb * (1.0 - gb)
        dpb_ref[:, pl.ds(2 * D, D)] = dga.astype(BF)
        dpb_ref[:, pl.ds(3 * D, D)] = dgb.astype(BF)
        _acc(dbg_ref.at[pl.ds(0, 1), :], jnp.sum(dga, axis=0, keepdims=True), first)
        _acc(dbg_ref.at[pl.ds(1, 1), :], jnp.sum(dgb, axis=0, keepdims=True), first)
        dya_ref[...] = (dm * ga).astype(BF)
        dyb_ref[...] = (dm * gb).astype(BF)
        dln = _dot_nt(dya_ref[...], wco_ref[...]) * _dsilu(ln_a, sig_a)
        dc, dlag, dlab = _ln_bwd(xh_a, rstd_a, lag_ref[...], dln)
        dc_ref[...] = dc
        _acc(dlag_ref, dlag, first)
        _acc(dlab_ref, dlab, first)
        dsg = _dot_nt(dyb_ref[...], wso_ref[...])
        dpb_ref[:, pl.ds(0, D)] = (dsg * z * _dgelu(bu, tu)).astype(BF)
        dz = dsg * u
        z_scr[...] = dz
        mask = _tril_mask()

        @pl.when(first)
        def _():
            dwm_ref[...] = jnp.zeros_like(dwm_ref)
            dbz_ref[...] = jnp.zeros_like(dbz_ref)

        for gi in range(SGU_GROUPS):
            wm = jnp.where(mask, wm_ref[gi], 0.0).astype(BF)
            cols = pl.ds(gi * LANES, LANES)
            for n in range(tm // LANES):
                rows = pl.ds(n * LANES, LANES)
                dzb = z_scr[rows, cols].astype(BF)
                dv_scr[rows, cols] = lax.dot_general(wm, dzb, (((0,), (0,)), ((), ())), preferred_element_type=F32)
                dw = lax.dot_general(dzb, v_scr[rows, cols], (((1,), (1,)), ((), ())), preferred_element_type=F32)
                dwm_ref[gi] += jnp.where(mask, dw, 0.0)
                dbz_ref[:, cols] += z_scr[rows, cols]
        dgv, dlsg, dlsb = _ln_bwd(vh, rstd_v, lsg_ref[...], dv_scr[...])
        _acc(dlsg_ref, dlsg, first)
        _acc(dlsb_ref, dlsb, first)
        dpb_ref[:, pl.ds(D, D)] = (dgv * _dgelu(bv, tv)).astype(BF)

    tile = lambda j: pl.BlockSpec((tm, D), lambda i: (i, j))
    vec = pl.BlockSpec((1, D), lambda i: (0, 0))
    return _pcall(
        body, (c, p, p, p, p, dmerged, prm["w_co"], prm["w_so"], prm["la_g"], prm["la_b"], prm["ls_g"], prm["ls_b"],
               prm["sgu_w"], prm["bz"], prm["b_gate"]),
        grid=(T // tm,),
        in_specs=[tile(0), tile(2), tile(3), tile(4), tile(5), tile(0), _const((D, D)), _const((D, D)),
                  _const((1, D)), _const((1, D)), _const((1, D)), _const((1, D)),
                  _const((SGU_GROUPS, LANES, LANES)), _const((LANES, D)), _const((2, D))],
        out_specs=[tile(0), pl.BlockSpec((tm, 4 * D), lambda i: (i, 0)), tile(0), tile(0),
                   pl.BlockSpec((SGU_GROUPS, LANES, LANES), lambda i: (0, 0, 0)),
                   pl.BlockSpec((LANES, D), lambda i: (0, 0)), vec, vec, vec, vec,
                   pl.BlockSpec((2, D), lambda i: (0, 0))],
        out_shape=[_sds((T, D), F32), _sds((T, 4 * D), BF), _sds((T, D), BF), _sds((T, D), BF),
                   _sds((SGU_GROUPS, LANES, LANES), F32), _sds((LANES, D), F32),
                   _sds((1, D), F32), _sds((1, D), F32), _sds((1, D), F32), _sds((1, D), F32), _sds((2, D), F32)],
        scratch_shapes=[pltpu.VMEM((tm, D), F32), pltpu.VMEM((tm, D), BF), pltpu.VMEM((tm, D), BF),
                        pltpu.VMEM((tm, D), BF), pltpu.VMEM((tm, D), F32)],
        name="branch_bwd", vmem=56, comm=comm)


def _conv_bwd(p, dc, conv_w, B, S, comm=None):
    K, D = conv_w.shape
    nc = D // LANES

    def body(av_ref, ag_ref, dc_ref, w_ref, dav_ref, dag_ref, dw_ref, db_ref, apad, dpad):
        b = pl.program_id(1)
        av = av_ref[...]
        sg = _sig(ag_ref[...])
        apad[pl.ds(0, CONV_PAD), :] = jnp.zeros((CONV_PAD, LANES), F32)
        apad[pl.ds(CONV_PAD, S), :] = av * sg
        dpad[pl.ds(S, CONV_PAD), :] = jnp.zeros((CONV_PAD, LANES), F32)
        dpad[pl.ds(0, S), :] = dc_ref[...]

        @pl.when(b == 0)
        def _():
            dw_ref[...] = jnp.zeros_like(dw_ref)
            db_ref[...] = jnp.zeros_like(db_ref)

        db_ref[...] += jnp.sum(dc_ref[...], axis=0, keepdims=True)
        for k in range(K):
            tot = jnp.zeros((1, LANES), F32)
            for r0 in range(0, S, CONV_ROWS):
                tot = tot + jnp.sum(dpad[pl.ds(r0, CONV_ROWS), :] * apad[pl.ds(r0 + k + CONV_PAD - (K - 1), CONV_ROWS), :],
                                    axis=0, keepdims=True)
            dw_ref[pl.ds(k, 1), :] += tot
        for r0 in range(0, S, CONV_ROWS):
            da = jnp.zeros((CONV_ROWS, LANES), F32)
            for k in range(K):
                da = da + w_ref[pl.ds(k, 1), :] * dpad[pl.ds(r0 + (K - 1) - k, CONV_ROWS), :]
            rows = pl.ds(r0, CONV_ROWS)
            s = sg[r0:r0 + CONV_ROWS, :]
            a_v = av[r0:r0 + CONV_ROWS, :]
            dav_ref[rows, :] = (da * s).astype(BF)
            dag_ref[rows, :] = (da * a_v * s * (1.0 - s)).astype(BF)

    blk = lambda off: pl.BlockSpec((S, LANES), lambda j, b: (b, off + j))
    return _pcall(
        body, (p, p, dc, conv_w), grid=(nc, B),
        in_specs=[blk(0), blk(nc), blk(0), pl.BlockSpec((K, LANES), lambda j, b: (0, j))],
        out_specs=[blk(0), blk(0), pl.BlockSpec((K, LANES), lambda j, b: (0, j)), pl.BlockSpec((1, LANES), lambda j, b: (0, j))],
        out_shape=[_sds((B * S, D), BF), _sds((B * S, D), BF), _sds((K, D), F32), _sds((1, D), F32)],
        scratch_shapes=[pltpu.VMEM((S + CONV_PAD, LANES), F32), pltpu.VMEM((S + CONV_PAD, LANES), F32)],
        name="conv_bwd", vmem=32, comm=comm)


TM1 = 512


def _in_proj_bwd(x, dx1, dps, g1, w_in, tm=TM1, comm=None):
    T, D = x.shape
    N = w_in.shape[1]
    widths = [d.shape[1] for d in dps]

    def body(x_ref, dx1_ref, *refs):
        dp_refs, (g_ref, w_ref, dx_ref, dg_ref) = refs[:len(dps)], refs[len(dps):]
        x_t = x_ref[...]
        _, r = _rms(x_t, g_ref[...])
        dh = jnp.zeros((tm, D), F32)
        for q, dp_ref in enumerate(dp_refs):
            cols = pl.ds(sum(widths[:q]), widths[q])
            dh = dh + lax.dot_general(dp_ref[...], w_ref[:, cols], (((1,), (1,)), ((), ())), preferred_element_type=F32)
        dxa, dg = _rms_bwd(x_t, g_ref[...], r, dh)
        dx_ref[...] = dx1_ref[...] + dxa
        _acc(dg_ref, dg, pl.program_id(0) == 0)

    tile = pl.BlockSpec((tm, D), lambda i: (i, 0))
    return _pcall(
        body, (x, dx1, *dps, g1, w_in), grid=(T // tm,),
        in_specs=[tile, tile] + [pl.BlockSpec((tm, w), lambda i: (i, 0)) for w in widths] + [_const((1, D)), _const((D, N))],
        out_specs=[tile, pl.BlockSpec((1, D), lambda i: (0, 0))],
        out_shape=[_sds((T, D), F32), _sds((1, D), F32)],
        name="in_proj_bwd", vmem=48, comm=comm)


def _pick(n, cands):
    for c in cands:
        if n % c == 0:
            return c
    raise ValueError(f"no tile of {cands} divides {n}")


def _mm_tn(x, dys, name, comm=None):
    T, K = x.shape
    dys = list(dys) if isinstance(dys, (list, tuple)) else [dys]
    widths = [d.shape[1] for d in dys]
    N = sum(widths)
    tm = _pick(T, (1024, 512, 256))
    tk = _pick(K, (1024, 1408, 512))
    tn = _pick(math.gcd(*widths), (1024, 1408, 512))
    nt = T // tm
    first = [sum(widths[:q]) // tn for q in range(len(dys))]
    count = [w // tn for w in widths]

    def body(x_ref, *refs):
        dy_refs, (o_ref, ob_ref, acc) = refs[:len(dys)], refs[len(dys):]
        j, t = pl.program_id(1), pl.program_id(2)

        @pl.when(t == 0)
        def _():
            acc[...] = jnp.zeros_like(acc)

        for q, dy_ref in enumerate(dy_refs):
            @pl.when(jnp.logical_and(j >= first[q], j < first[q] + count[q]))
            def _():
                acc[...] += _dot_tn(x_ref[...], dy_ref[...])

        @pl.when(t == nt - 1)
        def _():
            o_ref[...] = acc[...]
            ob_ref[...] = acc[...].astype(BF)

    def dy_spec(q):
        def index(i, j, t):
            mine = jnp.logical_and(j >= first[q], j < first[q] + count[q])
            return jnp.where(mine, t, 0), jnp.clip(j - first[q], 0, count[q] - 1)
        return pl.BlockSpec((tm, tn), index)

    res = _pcall(
        body, (x, *dys), grid=(K // tk, N // tn, nt),
        in_specs=[pl.BlockSpec((tm, tk), lambda i, j, t: (t, i))] + [dy_spec(q) for q in range(len(dys))],
        out_specs=[pl.BlockSpec((tk, tn), lambda i, j, t: (i, j)), pl.BlockSpec((tk, tn), lambda i, j, t: (i, j))],
        out_shape=[_sds((K, N), F32), _sds((K, N), BF)],
        scratch_shapes=[pltpu.VMEM((tk, tn), F32)],
        name=name, vmem=48, comm=comm)
    return res[0] if comm is None else res


def _row_tile(R):
    return _pick(R, (128, 64, 32, 16, 8)) if R % 8 == 0 else R


def _sum_landed(full, land, kind, chip, name):
    _, R, C = land.shape
    tr = _row_tile(R)
    nb = R // tr

    def body(k_ref, o_ref, l_ref, s_ref):
        s_ref[...] = ((o_ref[...] + l_ref[0].astype(F32)) + l_ref[1].astype(F32)) + l_ref[2].astype(F32)

    own = (pl.BlockSpec((tr, C), lambda i, k: (k[0] * nb + i, 0)) if kind == "row"
           else pl.BlockSpec((tr, C), lambda i, k: (i, k[0])))
    return pl.pallas_call(
        body,
        grid_spec=pltpu.PrefetchScalarGridSpec(
            num_scalar_prefetch=1, grid=(nb,),
            in_specs=[own, pl.BlockSpec((3, tr, C), lambda i, k: (0, i, 0))],
            out_specs=pl.BlockSpec((tr, C), lambda i, k: (i, 0))),
        out_shape=_sds((R, C), F32), name=name, compiler_params=_cp(32))(chip, full, land)


def _adamw(g, w, m, v):
    m = ADAM_B1 * m + (1.0 - ADAM_B1) * g
    v = ADAM_B2 * v + (1.0 - ADAM_B2) * (g * g)
    m_hat = m / (1.0 - ADAM_B1 ** ADAM_STEP)
    v_hat = v / (1.0 - ADAM_B2 ** ADAM_STEP)
    return -ADAM_LR * (m_hat / (jnp.sqrt(v_hat) + ADAM_EPS) + ADAM_WD * w), m, v


def _update(parts, w, m, v, name):
    R, C = w.shape
    tr = _row_tile(R)
    k = len(parts)

    def body(*refs):
        g = refs[0][...]
        for r in refs[1:k]:
            g = g + r[...]
        w_ref, m_ref, v_ref, g_out, d_out, m_out, v_out = refs[k:]
        d, m_new, v_new = _adamw(g, w_ref[...], m_ref[...], v_ref[...])
        g_out[...] = g
        d_out[...] = d
        m_out[...] = m_new
        v_out[...] = v_new

    blk = pl.BlockSpec((tr, C), lambda i: (i, 0))
    return pl.pallas_call(
        body, grid=(R // tr,), in_specs=[blk] * (k + 3), out_specs=[blk] * 4, out_shape=[_sds((R, C), F32)] * 4,
        name=name, compiler_params=_cp(40))(*parts, w, m, v)


VECS = ("norm_mix", "conv_b", "conv_ln_g", "conv_ln_b", "sgu_ln_g", "sgu_ln_b", "norm_xattn", "norm_mem", "norm_ffn",
        "norm_final")
SMALL = VECS + ("b_gate", "conv_w", "sgu_w", "sgu_b")


def _update_small(land_vec, land_w, land_g1, chip, a):
    _, R, D = land_vec.shape
    Dq, G, K, nv = D // N_CHIPS, SGU_GROUPS, a["conv_w"].shape[1], len(VECS)

    def as2d(nm, arr):
        if nm in VECS:
            return arr.reshape(1, D)
        if nm == "sgu_w":
            return arr.reshape(G * LANES, LANES)
        return arr.reshape(G, LANES) if nm == "sgu_b" else arr[0]

    params = [as2d(nm, a[pre + nm]) for nm in SMALL for pre in ("", "m_", "v_")]

    def body(k_ref, lv, lvc, lw, l1, *refs):
        prm = refs[:3 * len(SMALL)]
        outs = refs[3 * len(SMALL):7 * len(SMALL)]
        tv, tvc, tw, t1 = refs[7 * len(SMALL):]
        for land, tot in ((lv, tv), (lvc, tvc), (lw, tw), (l1, t1)):
            acc = land[0]
            for dev in range(1, N_DEV):
                acc = acc + land[dev]
            tot[...] = acc
        grads = [t1[pl.ds(0, 1), :] if nm == "norm_mix" else tv[pl.ds(i, 1), :] for i, nm in enumerate(VECS)]
        grads += [tvc[pl.ds(nv, 2), :], tvc[pl.ds(nv + 2, K), :], tw[pl.ds(0, G * LANES), :], tw[pl.ds(G * LANES, G), :]]
        for i, g in enumerate(grads):
            d, m_new, v_new = _adamw(g, prm[3 * i][...], prm[3 * i + 1][...], prm[3 * i + 2][...])
            for o_ref, val in zip(outs[4 * i:4 * i + 4], (g, d, m_new, v_new)):
                o_ref[...] = val

    whole = lambda shape: pl.BlockSpec(tuple(shape), lambda i, k: (0,) * len(shape))
    res = pl.pallas_call(
        body,
        grid_spec=pltpu.PrefetchScalarGridSpec(
            num_scalar_prefetch=1, grid=(1,),
            in_specs=[whole(land_vec.shape), pl.BlockSpec((N_DEV, R, Dq), lambda i, k: (0, 0, k[0])),
                      whole(land_w.shape), whole(land_g1.shape)] + [whole(p.shape) for p in params],
            out_specs=[whole(params[3 * i].shape) for i in range(len(SMALL)) for _ in range(4)],
            scratch_shapes=[pltpu.VMEM((R, D), F32), pltpu.VMEM((R, Dq), F32), pltpu.VMEM(land_w.shape[1:], F32),
                            pltpu.VMEM(land_g1.shape[1:], F32)]),
        out_shape=[_sds(params[3 * i].shape, F32) for i in range(len(SMALL)) for _ in range(4)],
        name="upd_small", compiler_params=_cp(40))(chip, land_vec, land_vec, land_w, land_g1, *params)
    return {nm: list(res[4 * i:4 * i + 4]) for i, nm in enumerate(SMALL)}


BIG = ("w_in", "w_conv_out", "w_sgu_out", "w_mix_out", "w_q", "w_kv", "w_xo", "w_gu", "w_down")
BIG_KIND = {"w_in": "col", "w_conv_out": "row", "w_sgu_out": "row", "w_mix_out": "row", "w_q": "row",
            "w_kv": "col", "w_xo": "row", "w_gu": "col", "w_down": "row"}

def _step(a):
    x3d, mem3d, tgt3d = a["x"], a["mem"], a["loss_target"]
    B, S, D = x3d.shape
    M = mem3d.shape[1]
    T = B * S
    x = x3d.reshape(T, D)
    mem = mem3d.reshape(B * M, D)
    tgt = tgt3d.reshape(T, D)
    xi, yi = lax.axis_index("x"), lax.axis_index("y")
    chip = 2 * xi + yi

    def gather(names):
        return _Gather([a[nm][0] if nm in ("b_gate", "conv_w") else a[nm][0].astype(BF) for nm in names],
                       [BIG_KIND.get(nm, "col") for nm in names])

    first = ("w_in", "b_gate", "conv_w")
    on_in_proj = ("w_conv_out", "w_sgu_out", "w_kv", "w_mix_out", "w_q", "w_xo")
    full = dict(zip(first, _comm_call(gather(first), "gather_w_in")))
    (p, h1), got = _in_proj(x, a["norm_mix"], full["w_in"], comm=gather(on_in_proj))
    full.update(zip(on_in_proj, got))
    (c,), got = _conv_fwd(p, full["conv_w"], a["conv_b"], B, S, comm=gather(("w_down",)))
    full["w_down"] = got[0]

    sgu_b = a["sgu_b"][0]
    bz = jnp.repeat(jnp.transpose(sgu_b), LANES, axis=1)
    prm = dict(w_co=full["w_conv_out"], w_so=full["w_sgu_out"], w_mo=full["w_mix_out"], w_q=full["w_q"],
               w_xo=full["w_xo"], w_down=full["w_down"],
               la_g=a["conv_ln_g"], la_b=a["conv_ln_b"], ls_g=a["sgu_ln_g"], ls_b=a["sgu_ln_b"],
               sgu_w=a["sgu_w"][0], bz=bz, b_gate=full["b_gate"], g2=a["norm_xattn"], g3=a["norm_ffn"],
               gf=a["norm_final"].reshape(1, D))

    (merged, s_a, sg), got = _branch_fwd(c, p, prm, comm=gather(("w_gu",)))
    prm["w_gu"] = got[0]
    mem_n, kv = _kv_fwd(mem, a["norm_mem"], full["w_kv"], B, M)
    x1, x2, h2, o = _attn_fwd(x, merged, kv, prm, S, M)
    dx2, dx3, dgu, h3, f, lsum, d_g3, d_gf = _ffn_loss(x2, tgt, prm)
    loss = lax.psum(0.5 * jnp.sum(lsum) / D, ("x", "y", "c"))

    size_of = {nm: a[nm].shape[1] if BIG_KIND[nm] == "row" else a[nm].shape[2] for nm in BIG}
    landed = {}

    def scatter(names):
        return _Scatter([gw[nm][1] for nm in names], [BIG_KIND[nm] for nm in names], [size_of[nm] for nm in names])

    gw = {}
    gw["w_down"] = _mm_tn(f, dx3, "dw_down")
    gw["w_gu"] = _mm_tn(h3, dgu, "dw_gu")
    (dx1, dmerged, dq, dkv, d_g2), got = _attn_bwd(x1, kv, dx2, prm, S, M, comm=scatter(("w_gu",)))
    landed["w_gu"] = got[0]
    gw["w_xo"] = _mm_tn(o, dx2, "dw_xo")
    gw["w_q"] = _mm_tn(h2, dq, "dw_q")
    gw["w_kv"] = _mm_tn(mem_n, dkv, "dw_kv")
    d_gm = _kv_bwd(mem, a["norm_mem"], full["w_kv"], dkv, B, M)
    gw["w_mix_out"] = _mm_tn(merged, dx1, "dw_mix_out")
    group = ("w_down", "w_xo", "w_q", "w_kv", "w_mix_out")
    (dc, dpb, dya, dyb, d_wm, d_bz, d_lag, d_lab, d_lsg, d_lsb, d_bg), got = _branch_bwd(
        c, p, dmerged, prm, comm=scatter(group))
    landed.update(zip(group, got))
    gw["w_conv_out"] = _mm_tn(s_a, dya, "dw_conv_out")
    gw["w_sgu_out"] = _mm_tn(sg, dyb, "dw_sgu_out")
    group = ("w_conv_out", "w_sgu_out")
    (dav, dag, d_cw, d_cb), got = _conv_bwd(p, dc, full["conv_w"], B, S, comm=scatter(group))
    landed.update(zip(group, got))
    dp = [dav, dag, dpb]

    chip_arr = jnp.reshape(chip, (1,)).astype(jnp.int32)
    early = [nm for nm in BIG if nm != "w_in"]
    part = {nm: _sum_landed(gw[nm][0], landed[nm], BIG_KIND[nm], chip_arr, "sum_" + nm) for nm in early}
    G = SGU_GROUPS
    d_sb = jnp.transpose(d_bz.reshape(LANES, G, LANES).sum(axis=-1))
    vec_g = dict(norm_mix=jnp.zeros((1, D), F32), conv_b=d_cb, conv_ln_g=d_lag, conv_ln_b=d_lab, sgu_ln_g=d_lsg,
                 sgu_ln_b=d_lsb, norm_xattn=d_g2, norm_mem=d_gm, norm_ffn=d_g3, norm_final=d_gf)
    rows = [vec_g[nm] for nm in VECS] + [d_bg, d_cw]
    pad = (-sum(r.shape[0] for r in rows)) % 8
    g_vec = jnp.concatenate(rows + [jnp.zeros((pad, D), F32)], axis=0)
    g_w = jnp.concatenate([d_wm.reshape(G * LANES, LANES), d_sb], axis=0)
    gw["w_in"], got = _mm_tn(h1, dp, "dw_in", comm=_Both(_Swap([part[nm] for nm in early]), _Spread([g_vec, g_w])))
    other = dict(zip(early, got[:len(early)]))
    land_vec, land_w = got[len(early):]
    (grad_x, d_g1), got = _in_proj_bwd(x, dx1, dp, a["norm_mix"], full["w_in"], comm=scatter(("w_in",)))
    part["w_in"] = _sum_landed(gw["w_in"][0], got[0], BIG_KIND["w_in"], chip_arr, "sum_w_in")
    g1 = jnp.concatenate([d_g1, jnp.zeros((7, D), F32)], axis=0)
    other["w_in"], land_g1 = _comm_call(_Both(_Swap([part["w_in"]]), _Spread([g1])), "swap_w_in")

    out = {}
    for nm in BIG:
        res = _update([part[nm], other[nm]], a[nm][0], a["m_" + nm][0], a["v_" + nm][0], "upd_" + nm)
        out[nm] = [r[None] for r in res]
    for nm, res in _update_small(land_vec, land_w, land_g1, chip_arr, a).items():
        out[nm] = [r.reshape(a[nm].shape) for r in res]
    return loss, grad_x.reshape(B, S, D), out


WEIGHTS = ("norm_mix", "w_in", "b_gate", "conv_w", "conv_b", "conv_ln_g", "conv_ln_b", "w_conv_out", "sgu_ln_g",
           "sgu_ln_b", "sgu_w", "sgu_b", "w_sgu_out", "w_mix_out", "norm_xattn", "norm_mem", "w_q", "w_kv", "w_xo",
           "norm_ffn", "w_gu", "w_down", "norm_final")


def kernel(x, mem, norm_mix, w_in, b_gate, conv_w, conv_b, conv_ln_g, conv_ln_b, w_conv_out, sgu_ln_g, sgu_ln_b, sgu_w, sgu_b, w_sgu_out, w_mix_out, norm_xattn, norm_mem, w_q, w_kv, w_xo, norm_ffn, w_gu, w_down, norm_final, loss_target, m_norm_mix, m_w_in, m_b_gate, m_conv_w, m_conv_b, m_conv_ln_g, m_conv_ln_b, m_w_conv_out, m_sgu_ln_g, m_sgu_ln_b, m_sgu_w, m_sgu_b, m_w_sgu_out, m_w_mix_out, m_norm_xattn, m_norm_mem, m_w_q, m_w_kv, m_w_xo, m_norm_ffn, m_w_gu, m_w_down, m_norm_final, v_norm_mix, v_w_in, v_b_gate, v_conv_w, v_conv_b, v_conv_ln_g, v_conv_ln_b, v_w_conv_out, v_sgu_ln_g, v_sgu_ln_b, v_sgu_w, v_sgu_b, v_w_sgu_out, v_w_mix_out, v_norm_xattn, v_norm_mem, v_w_q, v_w_kv, v_w_xo, v_norm_ffn, v_w_gu, v_w_down, v_norm_final):
    a = dict(locals())
    loss, grad_x, out = _step(a)
    res = [loss, grad_x]
    for q in range(4):
        res += [out[nm][q] for nm in WEIGHTS]
    return tuple(res)
```

```python
import functools
import math

import jax
import jax.numpy as jnp
from jax import lax
from jax.experimental import pallas as pl
from jax.experimental.pallas import tpu as pltpu

BF = jnp.bfloat16
F32 = jnp.float32
MESH = pl.DeviceIdType.MESH
ANY = pl.BlockSpec(memory_space=pl.ANY)

RMS_EPS = 1e-6
LN_EPS = 1e-5
HEADS = 4
SGU_GROUPS = 8
LANES = 128
ADAM_LR = 0.001
ADAM_B1 = 0.9
ADAM_B2 = 0.999
ADAM_EPS = 1e-08
ADAM_WD = 0.01
ADAM_STEP = 10
N_CHIPS = 4
N_DEV = 8
MIB = 1024 * 1024


def _sds(shape, dtype):
    return jax.ShapeDtypeStruct(tuple(shape), dtype)


def _cp(vmem_mib):
    return pltpu.CompilerParams(vmem_limit_bytes=vmem_mib * MIB)


def _const(shape):
    nd = len(shape)
    return pl.BlockSpec(tuple(shape), lambda *_: (0,) * nd, pipeline_mode=pl.Buffered(1))


def _dot(a, b):
    return jnp.dot(a.astype(BF), b.astype(BF), preferred_element_type=F32)


def _dot_nt(a, b):
    return lax.dot_general(a.astype(BF), b.astype(BF), (((1,), (1,)), ((), ())), preferred_element_type=F32)


def _dot_tn(a, b):
    return lax.dot_general(a.astype(BF), b.astype(BF), (((0,), (0,)), ((), ())), preferred_element_type=F32)


def _sig(x):
    return 1.0 / (1.0 + jnp.exp(-x))


def _dsilu(x, s):
    return s * (1.0 + x * (1.0 - s))


_GELU_C = math.sqrt(2.0 / math.pi)


def _gelu(x):
    x2 = x * x
    t = jnp.tanh((_GELU_C * x) * (1.0 + 0.044715 * x2))
    cdf = 0.5 * (1.0 + t)
    return x * cdf, (t, x2, cdf)


def _dgelu(x, shared):
    t, x2, cdf = shared
    return cdf + ((0.5 * _GELU_C) * x) * (1.0 - t * t) * (1.0 + (3.0 * 0.044715) * x2)


def _rms(x, g):
    r = lax.rsqrt(jnp.mean(x * x, axis=-1, keepdims=True) + RMS_EPS)
    return x * r * g, r


def _rms_bwd(x, g, r, dh):
    xr = x * r
    dxh = dh * g
    dx = r * (dxh - xr * jnp.mean(dxh * xr, axis=-1, keepdims=True))
    return dx, jnp.sum(dh * xr, axis=0, keepdims=True)


def _ln(x, g, b):
    mu = jnp.mean(x, axis=-1, keepdims=True)
    xc = x - mu
    rstd = lax.rsqrt(jnp.mean(xc * xc, axis=-1, keepdims=True) + LN_EPS)
    xh = xc * rstd
    return xh * g + b, xh, rstd


def _ln_bwd(xh, rstd, g, dy):
    dxh = dy * g
    dx = rstd * (dxh - jnp.mean(dxh, axis=-1, keepdims=True) - xh * jnp.mean(dxh * xh, axis=-1, keepdims=True))
    return dx, jnp.sum(dy * xh, axis=0, keepdims=True), jnp.sum(dy, axis=0, keepdims=True)


def _acc(ref, val, first):
    @pl.when(first)
    def _():
        ref[...] = val

    @pl.when(jnp.logical_not(first))
    def _():
        ref[...] += val


def _place():
    x, y, c = lax.axis_index("x"), lax.axis_index("y"), lax.axis_index("c")
    chips = [(1 - x, y), (x, 1 - y), (1 - x, 1 - y)]
    return x, y, c, chips


def _shard_of(ref, kind, k, n):
    if kind == "row":
        return ref.at[pl.ds(k * n, n), :]
    return ref.at[:, pl.ds(k * n, n)]


class _Gather:
    def __init__(self, shards, kinds):
        n = len(shards)
        self.srcs, self.kinds = list(shards), list(kinds)
        self.sizes = [s.shape[0] if kd == "row" else s.shape[1] for s, kd in zip(shards, kinds)]
        self.halves = [s.shape[0] // 2 if s.shape[0] % 32 == 0 else None for s in shards]
        self.out_shape = [
            _sds((s.shape[0] * N_CHIPS, s.shape[1]) if kd == "row" else (s.shape[0], s.shape[1] * N_CHIPS), s.dtype)
            for s, kd in zip(shards, kinds)]
        dma = pltpu.SemaphoreType.DMA
        self.sems = [dma((3 * n,)), dma((3 * n,)), dma((n,)), dma((3 * n,)), dma((3 * n,))]

    def _part(self, ref, t, core):
        h = self.halves[t]
        return ref if h is None else ref.at[pl.ds(core * h, h), :]

    def _copies(self, ins, outs, send, recv, loc, fsend, frecv):
        x, y, c, chips = _place()
        k = 2 * x + y
        local, remote = [], []
        for t in range(len(ins)):
            block = lambda q: _shard_of(outs[t], self.kinds[t], q, self.sizes[t])
            local.append(pltpu.make_async_copy(ins[t], block(k), loc.at[t]))
            for j, (px, py) in enumerate(chips):
                sems = dict(send_sem=send.at[3 * t + j], recv_sem=recv.at[3 * t + j])
                there = dict(device_id=(px, py, c), device_id_type=MESH)
                sent = pltpu.make_async_remote_copy(
                    src_ref=self._part(ins[t], t, c), dst_ref=self._part(block(k), t, c), **sems, **there)
                got = self._part(block(2 * px + py), t, c)
                landed = pltpu.make_async_remote_copy(src_ref=self._part(ins[t], t, c), dst_ref=got, **sems, **there)
                passed = handed = None
                if self.halves[t] is not None:
                    fsems = dict(send_sem=fsend.at[3 * t + j], recv_sem=frecv.at[3 * t + j])
                    sibling = dict(device_id=(x, y, 1 - c), device_id_type=MESH)
                    passed = pltpu.make_async_remote_copy(src_ref=got, dst_ref=got, **fsems, **sibling)
                    other = self._part(block(2 * px + py), t, 1 - c)
                    handed = pltpu.make_async_remote_copy(src_ref=got, dst_ref=other, **fsems, **sibling)
                remote.append((sent, landed, passed, handed))
        return local, remote

    def start(self, ins, outs, *sems):
        local, remote = self._copies(ins, outs, *sems)
        for cp in local:
            cp.start()
        for sent, _, _, _ in remote:
            sent.start()

    def wait(self, ins, outs, *sems):
        local, remote = self._copies(ins, outs, *sems)
        for sent, landed, passed, handed in remote:
            landed.wait_recv()
            if passed is not None:
                passed.start()
        for sent, landed, passed, handed in remote:
            if passed is not None:
                handed.wait_recv()
                passed.wait_send()
            sent.wait_send()
        for cp in local:
            cp.wait()


class _Scatter:
    def __init__(self, grads, kinds, sizes):
        n = len(grads)
        self.srcs, self.kinds, self.sizes = list(grads), list(kinds), list(sizes)
        self.out_shape = [_sds((3,) + ((sz, g.shape[1]) if kd == "row" else (g.shape[0], sz)), g.dtype)
                          for g, kd, sz in zip(grads, kinds, sizes)]
        self.sems = [pltpu.SemaphoreType.DMA((3 * n,)), pltpu.SemaphoreType.DMA((3 * n,))]

    def _copies(self, ins, outs, send, recv):
        x, y, c, chips = _place()
        return [pltpu.make_async_remote_copy(
            src_ref=_shard_of(ins[t], self.kinds[t], 2 * px + py, self.sizes[t]), dst_ref=outs[t].at[j],
            send_sem=send.at[3 * t + j], recv_sem=recv.at[3 * t + j], device_id=(px, py, c), device_id_type=MESH)
            for t in range(len(ins)) for j, (px, py) in enumerate(chips)]

    def start(self, ins, outs, send, recv):
        for cp in self._copies(ins, outs, send, recv):
            cp.start()

    def wait(self, ins, outs, send, recv):
        for cp in self._copies(ins, outs, send, recv):
            cp.wait_recv()
            cp.wait_send()


class _Swap:
    def __init__(self, parts):
        n = len(parts)
        self.srcs = list(parts)
        self.out_shape = [_sds(p.shape, p.dtype) for p in parts]
        self.sems = [pltpu.SemaphoreType.DMA((n,)), pltpu.SemaphoreType.DMA((n,))]

    def _copies(self, ins, outs, send, recv):
        x, y, c, _ = _place()
        return [pltpu.make_async_remote_copy(
            src_ref=ins[t], dst_ref=outs[t], send_sem=send.at[t], recv_sem=recv.at[t],
            device_id=(x, y, 1 - c), device_id_type=MESH) for t in range(len(ins))]

    def start(self, ins, outs, send, recv):
        for cp in self._copies(ins, outs, send, recv):
            cp.start()

    def wait(self, ins, outs, send, recv):
        for cp in self._copies(ins, outs, send, recv):
            cp.wait_recv()
            cp.wait_send()


class _Spread:
    def __init__(self, packs):
        n = len(packs)
        self.srcs = list(packs)
        self.out_shape = [_sds((N_DEV,) + p.shape, p.dtype) for p in packs]
        dma = pltpu.SemaphoreType.DMA
        self.sems = [dma((N_DEV * n,)), dma((N_DEV * n,)), dma((n,))]

    def _copies(self, ins, outs, send, recv, loc):
        x, y, c, _ = _place()
        me = 4 * x + 2 * y + c
        local = [pltpu.make_async_copy(ins[t], outs[t].at[me], loc.at[t]) for t in range(len(ins))]
        remote = []
        for t in range(len(ins)):
            for mask in range(1, N_DEV):
                peer = ((1 - x) if mask & 4 else x, (1 - y) if mask & 2 else y, (1 - c) if mask & 1 else c)
                src = peer[0] * 4 + peer[1] * 2 + peer[2]
                sems = dict(send_sem=send.at[N_DEV * t + mask], recv_sem=recv.at[N_DEV * t + mask])
                sent = pltpu.make_async_remote_copy(
                    src_ref=ins[t], dst_ref=outs[t].at[me], device_id=peer, device_id_type=MESH, **sems)
                landed = pltpu.make_async_remote_copy(
                    src_ref=ins[t], dst_ref=outs[t].at[src], device_id=peer, device_id_type=MESH, **sems)
                remote.append((sent, landed))
        return local, remote

    def start(self, ins, outs, send, recv, loc):
        local, remote = self._copies(ins, outs, send, recv, loc)
        for cp in local:
            cp.start()
        for sent, _ in remote:
            sent.start()

    def wait(self, ins, outs, send, recv, loc):
        local, remote = self._copies(ins, outs, send, recv, loc)
        for sent, landed in remote:
            landed.wait_recv()
            sent.wait_send()
        for cp in local:
            cp.wait()


class _Both:
    def __init__(self, *comms):
        self.comms = comms
        self.srcs = [s for cm in comms for s in cm.srcs]
        self.out_shape = [s for cm in comms for s in cm.out_shape]
        self.sems = [s for cm in comms for s in cm.sems]

    def _each(self, ins, outs, sems):
        i = o = k = 0
        for cm in self.comms:
            ni, no, nk = len(cm.srcs), len(cm.out_shape), len(cm.sems)
            yield cm, ins[i:i + ni], outs[o:o + no], sems[k:k + nk]
            i, o, k = i + ni, o + no, k + nk

    def start(self, ins, outs, *sems):
        for cm, i, o, s in self._each(ins, outs, sems):
            cm.start(i, o, *s)

    def wait(self, ins, outs, *sems):
        for cm, i, o, s in self._each(ins, outs, sems):
            cm.wait(i, o, *s)


def _comm_call(comm, name):
    n, m = len(comm.srcs), len(comm.out_shape)

    def body(*refs):
        comm.start(refs[:n], refs[n:n + m], *refs[n + m:])
        comm.wait(refs[:n], refs[n:n + m], *refs[n + m:])

    return pl.pallas_call(body, in_specs=[ANY] * n, out_specs=[ANY] * m, out_shape=comm.out_shape,
                          scratch_shapes=comm.sems, name=name)(*comm.srcs)


def _pcall(body, args, *, grid, in_specs, out_specs, out_shape, name, vmem, scratch_shapes=(), comm=None):
    in_specs, out_specs, out_shape = list(in_specs), list(out_specs), list(out_shape)
    scratch_shapes = list(scratch_shapes)
    if comm is None:
        res = pl.pallas_call(body, grid=grid, in_specs=in_specs, out_specs=out_specs, out_shape=out_shape,
                             scratch_shapes=scratch_shapes, name=name, compiler_params=_cp(vmem))(*args)
        return list(res), []
    ni, no, ns = len(in_specs), len(out_specs), len(scratch_shapes)
    ci, co = len(comm.srcs), len(comm.out_shape)

    def carried(*refs):
        c_in = refs[ni:ni + ci]
        c_out = refs[ni + ci + no:ni + ci + no + co]
        sems = refs[ni + ci + no + co + ns:]
        ids = [pl.program_id(d) for d in range(len(grid))]
        first = functools.reduce(jnp.logical_and, [i == 0 for i in ids])
        last = functools.reduce(jnp.logical_and, [i == g - 1 for i, g in zip(ids, grid)])

        @pl.when(first)
        def _():
            comm.start(c_in, c_out, *sems)

        body(*refs[:ni], *refs[ni + ci:ni + ci + no], *refs[ni + ci + no + co:ni + ci + no + co + ns])

        @pl.when(last)
        def _():
            comm.wait(c_in, c_out, *sems)

    res = pl.pallas_call(carried, grid=grid, in_specs=in_specs + [ANY] * ci, out_specs=out_specs + [ANY] * co,
                         out_shape=out_shape + list(comm.out_shape), scratch_shapes=scratch_shapes + list(comm.sems),
                         name=name, compiler_params=_cp(vmem))(*args, *comm.srcs)
    return list(res[:no]), list(res[no:])


def _in_proj(x, g1, w_in, comm=None):
    T, D = x.shape
    N = w_in.shape[1]
    tm, tn = 512, 1024

    def body(x_ref, g_ref, w_ref, p_ref, h_ref):
        h, _ = _rms(x_ref[...], g_ref[...])
        h_ref[...] = h.astype(BF)
        for j in range(N // tn):
            cols = pl.ds(j * tn, tn)
            p_ref[:, cols] = jnp.dot(h_ref[...], w_ref[:, cols], preferred_element_type=F32)

    return _pcall(
        body, (x, g1, w_in), grid=(T // tm,),
        in_specs=[pl.BlockSpec((tm, D), lambda i: (i, 0)), _const((1, D)), _const((D, N))],
        out_specs=[pl.BlockSpec((tm, N), lambda i: (i, 0)), pl.BlockSpec((tm, D), lambda i: (i, 0))],
        out_shape=[_sds((T, N), F32), _sds((T, D), BF)],
        name="in_proj", vmem=56, comm=comm)


CONV_PAD = 32
CONV_ROWS = 256


def _conv_fwd(p, conv_w, conv_b, B, S, comm=None):
    K, D = conv_w.shape
    nc = D // LANES

    def body(av_ref, ag_ref, w_ref, b_ref, c_ref, apad):
        apad[pl.ds(0, CONV_PAD), :] = jnp.zeros((CONV_PAD, LANES), F32)
        apad[pl.ds(CONV_PAD, S), :] = av_ref[...] * _sig(ag_ref[...])
        for r0 in range(0, S, CONV_ROWS):
            acc = jnp.zeros((CONV_ROWS, LANES), F32) + b_ref[...]
            for k in range(K):
                acc = acc + w_ref[pl.ds(k, 1), :] * apad[pl.ds(r0 + k + CONV_PAD - (K - 1), CONV_ROWS), :]
            c_ref[pl.ds(r0, CONV_ROWS), :] = acc

    return _pcall(
        body, (p, p, conv_w, conv_b), grid=(B, nc),
        in_specs=[pl.BlockSpec((S, LANES), lambda b, j: (b, j)), pl.BlockSpec((S, LANES), lambda b, j: (b, nc + j)),
                  pl.BlockSpec((K, LANES), lambda b, j: (0, j)), pl.BlockSpec((1, LANES), lambda b, j: (0, j))],
        out_specs=[pl.BlockSpec((S, LANES), lambda b, j: (b, j))],
        out_shape=[_sds((B * S, D), F32)],
        scratch_shapes=[pltpu.VMEM((S + CONV_PAD, LANES), F32)],
        name="conv_fwd", vmem=32, comm=comm)


def _tril_mask():
    t = lax.broadcasted_iota(jnp.int32, (LANES, LANES), 0)
    s = lax.broadcasted_iota(jnp.int32, (LANES, LANES), 1)
    return t >= s


def _branch_a(c, g, b):
    ln_a, xh, rstd = _ln(c, g, b)
    s = _sig(ln_a)
    return ln_a * s, ln_a, s, xh, rstd


def _branch_b(bu, bv, g, b, wm_ref, bz_ref, z_scr, v_scr):
    tm, D = bu.shape
    u, tu = _gelu(bu)
    gv, tv = _gelu(bv)
    v, vh, rstd = _ln(gv, g, b)
    v_scr[...] = v.astype(BF)
    mask = _tril_mask()
    for gi in range(SGU_GROUPS):
        wm = jnp.where(mask, wm_ref[gi], 0.0).astype(BF)
        cols = pl.ds(gi * LANES, LANES)
        for n in range(tm // LANES):
            rows = pl.ds(n * LANES, LANES)
            z_scr[rows, cols] = jnp.dot(wm, v_scr[rows, cols], preferred_element_type=F32) + bz_ref[:, cols]
    z = z_scr[...]
    return u * z, u, tu, z, tv, vh, rstd


TM3 = 256
TM3_FWD = 512


def _branch_fwd(c, p, prm, tm=TM3_FWD, comm=None):
    T, D = c.shape

    def body(c_ref, bu_ref, bv_ref, ga_ref, gb_ref, wco_ref, wso_ref, lag_ref, lab_ref, lsg_ref, lsb_ref, wm_ref,
             bz_ref, bg_ref, mg_ref, sa_ref, sg_ref, z_scr, v_scr):
        s_a = _branch_a(c_ref[...], lag_ref[...], lab_ref[...])[0]
        sa_ref[...] = s_a.astype(BF)
        y_a = jnp.dot(sa_ref[...], wco_ref[...], preferred_element_type=F32)
        sg = _branch_b(bu_ref[...], bv_ref[...], lsg_ref[...], lsb_ref[...], wm_ref, bz_ref, z_scr, v_scr)[0]
        sg_ref[...] = sg.astype(BF)
        y_b = jnp.dot(sg_ref[...], wso_ref[...], preferred_element_type=F32)
        ga = _sig(ga_ref[...] + bg_ref[pl.ds(0, 1), :])
        gb = _sig(gb_ref[...] + bg_ref[pl.ds(1, 1), :])
        mg_ref[...] = (ga * y_a + gb * y_b).astype(BF)

    tile = lambda j: pl.BlockSpec((tm, D), lambda i: (i, j))
    return _pcall(
        body, (c, p, p, p, p, prm["w_co"], prm["w_so"], prm["la_g"], prm["la_b"], prm["ls_g"], prm["ls_b"],
               prm["sgu_w"], prm["bz"], prm["b_gate"]),
        grid=(T // tm,),
        in_specs=[tile(0), tile(2), tile(3), tile(4), tile(5), _const((D, D)), _const((D, D)),
                  _const((1, D)), _const((1, D)), _const((1, D)), _const((1, D)),
                  _const((SGU_GROUPS, LANES, LANES)), _const((LANES, D)), _const((2, D))],
        out_specs=[tile(0), tile(0), tile(0)],
        out_shape=[_sds((T, D), BF)] * 3,
        scratch_shapes=[pltpu.VMEM((tm, D), F32), pltpu.VMEM((tm, D), BF)],
        name="branch_fwd", vmem=48, comm=comm)


def _kv_fwd(mem, gm, w_kv, B, M):
    D = mem.shape[1]
    N = w_kv.shape[1]

    def body(m_ref, g_ref, w_ref, mn_ref, kv_ref):
        h, _ = _rms(m_ref[...], g_ref[...])
        mn_ref[...] = h.astype(BF)
        kv_ref[...] = jnp.dot(mn_ref[...], w_ref[...], preferred_element_type=F32).astype(BF)

    return pl.pallas_call(
        body, grid=(B,),
        in_specs=[pl.BlockSpec((M, D), lambda b: (b, 0)), _const((1, D)), _const((D, N))],
        out_specs=[pl.BlockSpec((M, D), lambda b: (b, 0)), pl.BlockSpec((M, N), lambda b: (b, 0))],
        out_shape=[_sds((B * M, D), BF), _sds((B * M, N), BF)],
        name="kv_fwd", compiler_params=_cp(32))(mem, gm, w_kv)


def _softmax_rows(s):
    e = jnp.exp(s - jnp.max(s, axis=-1, keepdims=True))
    return e * (1.0 / jnp.sum(e, axis=-1, keepdims=True))


TM4 = 512


def _attn_fwd(x, merged, kv, prm, S, M, tm=TM4):
    T, D = x.shape
    hd = D // HEADS
    scale = hd ** -0.5
    tpb = S // tm

    def body(x_ref, mg_ref, kv_ref, wmo_ref, wq_ref, wxo_ref, g_ref, x1_ref, x2_ref, h2_ref, o_ref):
        x1 = x_ref[...] + jnp.dot(mg_ref[...], wmo_ref[...], preferred_element_type=F32)
        x1_ref[...] = x1
        h2, _ = _rms(x1, g_ref[...])
        h2_ref[...] = h2.astype(BF)
        qb = jnp.dot(h2_ref[...], wq_ref[...], preferred_element_type=F32).astype(BF)
        for h in range(HEADS):
            cs = pl.ds(h * hd, hd)
            s = _dot_nt(qb[:, h * hd:(h + 1) * hd], kv_ref[:, cs]) * scale
            pr = _softmax_rows(s)
            o_ref[:, cs] = _dot(pr, kv_ref[:, pl.ds(D + h * hd, hd)]).astype(BF)
        x2_ref[...] = x1 + jnp.dot(o_ref[...], wxo_ref[...], preferred_element_type=F32)

    tile = pl.BlockSpec((tm, D), lambda i: (i, 0))
    return pl.pallas_call(
        body, grid=(T // tm,),
        in_specs=[tile, tile, pl.BlockSpec((M, 2 * D), lambda i: (i // tpb, 0)),
                  _const((D, D)), _const((D, D)), _const((D, D)), _const((1, D))],
        out_specs=[tile, tile, tile, tile],
        out_shape=[_sds((T, D), F32), _sds((T, D), F32), _sds((T, D), BF), _sds((T, D), BF)],
        name="attn_fwd", compiler_params=_cp(40))(x, merged, kv, prm["w_mo"], prm["w_q"], prm["w_xo"], prm["g2"])


TM5 = 256
FFN_CHUNKS = 1


def _ffn_loss(x2, tgt, prm, tm=TM5):
    T, D = x2.shape
    F = prm["w_down"].shape[0]
    FC = F // FFN_CHUNKS

    def body(x2_ref, t_ref, wgu_ref, wd_ref, g3_ref, gf_ref, dx2_ref, dx3_ref, dgu_ref, h3_ref, f_ref, ls_ref,
             dg3_ref, dgf_ref, gu_scr):
        first = pl.program_id(0) == 0
        x2 = x2_ref[...]
        h3, r3 = _rms(x2, g3_ref[...])
        h3_ref[...] = h3.astype(BF)
        x3 = x2
        for ch in range(FFN_CHUNKS):
            gc, uc = pl.ds(ch * FC, FC), pl.ds(F + ch * FC, FC)
            gt = jnp.dot(h3_ref[...], wgu_ref[:, gc], preferred_element_type=F32)
            up = jnp.dot(h3_ref[...], wgu_ref[:, uc], preferred_element_type=F32)
            gu_scr[:, gc] = gt
            gu_scr[:, uc] = up
            f_ref[:, gc] = (gt * _sig(gt) * up).astype(BF)
            x3 = x3 + jnp.dot(f_ref[:, gc], wd_ref[gc, :], preferred_element_type=F32)
        y, rf = _rms(x3, gf_ref[...])
        e = y - t_ref[...]
        _acc(ls_ref, jnp.sum(e * e, axis=0, keepdims=True), first)
        dx3, dgf = _rms_bwd(x3, gf_ref[...], rf, e * (1.0 / D))
        _acc(dgf_ref, dgf, first)
        dx3_ref[...] = dx3.astype(BF)
        dh3 = jnp.zeros((tm, D), F32)
        for ch in range(FFN_CHUNKS):
            gc, uc = pl.ds(ch * FC, FC), pl.ds(F + ch * FC, FC)
            df = lax.dot_general(dx3_ref[...], wd_ref[gc, :], (((1,), (1,)), ((), ())), preferred_element_type=F32)
            gt, up = gu_scr[:, gc], gu_scr[:, uc]
            s = _sig(gt)
            dgu_ref[:, gc] = (df * up * _dsilu(gt, s)).astype(BF)
            dgu_ref[:, uc] = (df * gt * s).astype(BF)
            dh3 = dh3 + lax.dot_general(dgu_ref[:, gc], wgu_ref[:, gc], (((1,), (1,)), ((), ())), preferred_element_type=F32)
            dh3 = dh3 + lax.dot_general(dgu_ref[:, uc], wgu_ref[:, uc], (((1,), (1,)), ((), ())), preferred_element_type=F32)
        dxa, dg3 = _rms_bwd(x2, g3_ref[...], r3, dh3)
        _acc(dg3_ref, dg3, first)
        dx2_ref[...] = dx3 + dxa

    tile = lambda n: pl.BlockSpec((tm, n), lambda i: (i, 0))
    vec = pl.BlockSpec((1, D), lambda i: (0, 0))
    return pl.pallas_call(
        body, grid=(T // tm,),
        in_specs=[tile(D), tile(D), _const((D, 2 * F)), _const((F, D)), _const((1, D)), _const((1, D))],
        out_specs=[tile(D), tile(D), tile(2 * F), tile(D), tile(F), vec, vec, vec],
        out_shape=[_sds((T, D), F32), _sds((T, D), BF), _sds((T, 2 * F), BF), _sds((T, D), BF), _sds((T, F), BF),
                   _sds((1, D), F32), _sds((1, D), F32), _sds((1, D), F32)],
        scratch_shapes=[pltpu.VMEM((tm, 2 * F), F32)],
        name="ffn_loss", compiler_params=_cp(56))(x2, tgt, prm["w_gu"], prm["w_down"], prm["g3"], prm["gf"])


def _attn_bwd(x1, kv, dx2, prm, S, M, tm=TM4, comm=None):
    T, D = x1.shape
    hd = D // HEADS
    scale = hd ** -0.5
    tpb = S // tm

    def body(x1_ref, kv_ref, dx2_ref, wmo_ref, wq_ref, wxo_ref, g_ref, dx1_ref, dmg_ref, dq_ref, dkv_ref, dg_ref,
             h2_scr, do_scr):
        i = pl.program_id(0)
        x1 = x1_ref[...]
        dx2 = dx2_ref[...]
        h2, r2 = _rms(x1, g_ref[...])
        h2_scr[...] = h2.astype(BF)
        qb = jnp.dot(h2_scr[...], wq_ref[...], preferred_element_type=F32).astype(BF)
        do_scr[...] = _dot_nt(dx2, wxo_ref[...]).astype(BF)
        for h in range(HEADS):
            cs, vs = pl.ds(h * hd, hd), pl.ds(D + h * hd, hd)
            qh = qb[:, h * hd:(h + 1) * hd]
            pr = _softmax_rows(_dot_nt(qh, kv_ref[:, cs]) * scale)
            dpr = _dot_nt(do_scr[:, cs], kv_ref[:, vs])
            dv = _dot_tn(pr, do_scr[:, cs])
            ds = (pr * (dpr - jnp.sum(dpr * pr, axis=-1, keepdims=True)) * scale).astype(BF)
            dq_ref[:, cs] = jnp.dot(ds, kv_ref[:, cs], preferred_element_type=F32).astype(BF)
            dk = _dot_tn(ds, qh)

            @pl.when(i % tpb == 0)
            def _():
                dkv_ref[:, cs] = dk
                dkv_ref[:, vs] = dv

            @pl.when(i % tpb != 0)
            def _():
                dkv_ref[:, cs] += dk
                dkv_ref[:, vs] += dv

        dh2 = _dot_nt(dq_ref[...], wq_ref[...])
        dxa, dg = _rms_bwd(x1, g_ref[...], r2, dh2)
        _acc(dg_ref, dg, i == 0)
        dx1 = dx2 + dxa
        dx1_ref[...] = dx1
        dmg_ref[...] = _dot_nt(dx1, wmo_ref[...])

    tile = pl.BlockSpec((tm, D), lambda i: (i, 0))
    kvb = pl.BlockSpec((M, 2 * D), lambda i: (i // tpb, 0))
    B = T // S
    return _pcall(
        body, (x1, kv, dx2, prm["w_mo"], prm["w_q"], prm["w_xo"], prm["g2"]), grid=(T // tm,),
        in_specs=[tile, kvb, tile, _const((D, D)), _const((D, D)), _const((D, D)), _const((1, D))],
        out_specs=[tile, tile, tile, kvb, pl.BlockSpec((1, D), lambda i: (0, 0))],
        out_shape=[_sds((T, D), F32), _sds((T, D), F32), _sds((T, D), BF), _sds((B * M, 2 * D), F32), _sds((1, D), F32)],
        scratch_shapes=[pltpu.VMEM((tm, D), BF), pltpu.VMEM((tm, D), BF)],
        name="attn_bwd", vmem=48, comm=comm)


def _kv_bwd(mem, gm, w_kv, dkv, B, M):
    D = mem.shape[1]
    N = w_kv.shape[1]

    def body(m_ref, g_ref, w_ref, dkv_ref, dg_ref):
        mem_t = m_ref[...]
        _, r = _rms(mem_t, g_ref[...])
        dmn = _dot_nt(dkv_ref[...], w_ref[...])
        _acc(dg_ref, jnp.sum(dmn * (mem_t * r), axis=0, keepdims=True), pl.program_id(0) == 0)

    return pl.pallas_call(
        body, grid=(B,),
        in_specs=[pl.BlockSpec((M, D), lambda b: (b, 0)), _const((1, D)), _const((D, N)),
                  pl.BlockSpec((M, N), lambda b: (b, 0))],
        out_specs=pl.BlockSpec((1, D), lambda b: (0, 0)),
        out_shape=_sds((1, D), F32),
        name="kv_bwd", compiler_params=_cp(32))(mem, gm, w_kv, dkv)


def _branch_bwd(c, p, dmerged, prm, tm=TM3, comm=None):
    T, D = c.shape

    def body(c_ref, bu_ref, bv_ref, ga_ref, gb_ref, dm_ref, wco_ref, wso_ref, lag_ref, lab_ref, lsg_ref, lsb_ref,
             wm_ref, bz_ref, bg_ref,
             dc_ref, dpb_ref, dya_ref, dyb_ref, dwm_ref, dbz_ref, dlag_ref, dlab_ref, dlsg_ref, dlsb_ref, dbg_ref,
             z_scr, v_scr, sa_scr, sg_scr, dv_scr):
        first = pl.program_id(0) == 0
        s_a, ln_a, sig_a, xh_a, rstd_a = _branch_a(c_ref[...], lag_ref[...], lab_ref[...])
        sa_scr[...] = s_a.astype(BF)
        y_a = jnp.dot(sa_scr[...], wco_ref[...], preferred_element_type=F32)
        bu, bv = bu_ref[...], bv_ref[...]
        sg, u, tu, z, tv, vh, rstd_v = _branch_b(bu, bv, lsg_ref[...], lsb_ref[...], wm_ref, bz_ref, z_scr, v_scr)
        sg_scr[...] = sg.astype(BF)
        y_b = jnp.dot(sg_scr[...], wso_ref[...], preferred_element_type=F32)
        ga = _sig(ga_ref[...] + bg_ref[pl.ds(0, 1), :])
        gb = _sig(gb_ref[...] + bg_ref[pl.ds(1, 1), :])
        dm = dm_ref[...]
        dga = dm * y_a * ga * (1.0 - ga)
        dgb = dm * y_b * gb * (1.0 - gb)
        dpb_ref[:, pl.ds(2 * D, D)] = dga.astype(BF)
        dpb_ref[:, pl.ds(3 * D, D)] = dgb.astype(BF)
        _acc(dbg_ref.at[pl.ds(0, 1), :], jnp.sum(dga, axis=0, keepdims=True), first)
        _acc(dbg_ref.at[pl.ds(1, 1), :], jnp.sum(dgb, axis=0, keepdims=True), first)
        dya_ref[...] = (dm * ga).astype(BF)
        dyb_ref[...] = (dm * gb).astype(BF)
        dln = _dot_nt(dya_ref[...], wco_ref[...]) * _dsilu(ln_a, sig_a)
        dc, dlag, dlab = _ln_bwd(xh_a, rstd_a, lag_ref[...], dln)
        dc_ref[...] = dc
        _acc(dlag_ref, dlag, first)
        _acc(dlab_ref, dlab, first)
        dsg = _dot_nt(dyb_ref[...], wso_ref[...])
        dpb_ref[:, pl.ds(0, D)] = (dsg * z * _dgelu(bu, tu)).astype(BF)
        dz = dsg * u
        z_scr[...] = dz
        mask = _tril_mask()

        @pl.when(first)
        def _():
            dwm_ref[...] = jnp.zeros_like(dwm_ref)
            dbz_ref[...] = jnp.zeros_like(dbz_ref)

        for gi in range(SGU_GROUPS):
            wm = jnp.where(mask, wm_ref[gi], 0.0).astype(BF)
            cols = pl.ds(gi * LANES, LANES)
            for n in range(tm // LANES):
                rows = pl.ds(n * LANES, LANES)
                dzb = z_scr[rows, cols].astype(BF)
                dv_scr[rows, cols] = lax.dot_general(wm, dzb, (((0,), (0,)), ((), ())), preferred_element_type=F32)
                dw = lax.dot_general(dzb, v_scr[rows, cols], (((1,), (1,)), ((), ())), preferred_element_type=F32)
                dwm_ref[gi] += jnp.where(mask, dw, 0.0)
                dbz_ref[:, cols] += z_scr[rows, cols]
        dgv, dlsg, dlsb = _ln_bwd(vh, rstd_v, lsg_ref[...], dv_scr[...])
        _acc(dlsg_ref, dlsg, first)
        _acc(dlsb_ref, dlsb, first)
        dpb_ref[:, pl.ds(D, D)] = (dgv * _dgelu(bv, tv)).astype(BF)

    tile = lambda j: pl.BlockSpec((tm, D), lambda i: (i, j))
    vec = pl.BlockSpec((1, D), lambda i: (0, 0))
    return _pcall(
        body, (c, p, p, p, p, dmerged, prm["w_co"], prm["w_so"], prm["la_g"], prm["la_b"], prm["ls_g"], prm["ls_b"],
               prm["sgu_w"], prm["bz"], prm["b_gate"]),
        grid=(T // tm,),
        in_specs=[tile(0), tile(2), tile(3), tile(4), tile(5), tile(0), _const((D, D)), _const((D, D)),
                  _const((1, D)), _const((1, D)), _const((1, D)), _const((1, D)),
                  _const((SGU_GROUPS, LANES, LANES)), _const((LANES, D)), _const((2, D))],
        out_specs=[tile(0), pl.BlockSpec((tm, 4 * D), lambda i: (i, 0)), tile(0), tile(0),
                   pl.BlockSpec((SGU_GROUPS, LANES, LANES), lambda i: (0, 0, 0)),
                   pl.BlockSpec((LANES, D), lambda i: (0, 0)), vec, vec, vec, vec,
                   pl.BlockSpec((2, D), lambda i: (0, 0))],
        out_shape=[_sds((T, D), F32), _sds((T, 4 * D), BF), _sds((T, D), BF), _sds((T, D), BF),
                   _sds((SGU_GROUPS, LANES, LANES), F32), _sds((LANES, D), F32),
                   _sds((1, D), F32), _sds((1, D), F32), _sds((1, D), F32), _sds((1, D), F32), _sds((2, D), F32)],
        scratch_shapes=[pltpu.VMEM((tm, D), F32), pltpu.VMEM((tm, D), BF), pltpu.VMEM((tm, D), BF),
                        pltpu.VMEM((tm, D), BF), pltpu.VMEM((tm, D), F32)],
        name="branch_bwd", vmem=56, comm=comm)


def _conv_bwd(p, dc, conv_w, B, S, comm=None):
    K, D = conv_w.shape
    nc = D // LANES

    def body(av_ref, ag_ref, dc_ref, w_ref, dav_ref, dag_ref, dw_ref, db_ref, apad, dpad):
        b = pl.program_id(1)
        av = av_ref[...]
        sg = _sig(ag_ref[...])
        apad[pl.ds(0, CONV_PAD), :] = jnp.zeros((CONV_PAD, LANES), F32)
        apad[pl.ds(CONV_PAD, S), :] = av * sg
        dpad[pl.ds(S, CONV_PAD), :] = jnp.zeros((CONV_PAD, LANES), F32)
        dpad[pl.ds(0, S), :] = dc_ref[...]

        @pl.when(b == 0)
        def _():
            dw_ref[...] = jnp.zeros_like(dw_ref)
            db_ref[...] = jnp.zeros_like(db_ref)

        db_ref[...] += jnp.sum(dc_ref[...], axis=0, keepdims=True)
        for k in range(K):
            tot = jnp.zeros((1, LANES), F32)
            for r0 in range(0, S, CONV_ROWS):
                tot = tot + jnp.sum(dpad[pl.ds(r0, CONV_ROWS), :] * apad[pl.ds(r0 + k + CONV_PAD - (K - 1), CONV_ROWS), :],
                                    axis=0, keepdims=True)
            dw_ref[pl.ds(k, 1), :] += tot
        for r0 in range(0, S, CONV_ROWS):
            da = jnp.zeros((CONV_ROWS, LANES), F32)
            for k in range(K):
                da = da + w_ref[pl.ds(k, 1), :] * dpad[pl.ds(r0 + (K - 1) - k, CONV_ROWS), :]
            rows = pl.ds(r0, CONV_ROWS)
            s = sg[r0:r0 + CONV_ROWS, :]
            a_v = av[r0:r0 + CONV_ROWS, :]
            dav_ref[rows, :] = (da * s).astype(BF)
            dag_ref[rows, :] = (da * a_v * s * (1.0 - s)).astype(BF)

    blk = lambda off: pl.BlockSpec((S, LANES), lambda j, b: (b, off + j))
    return _pcall(
        body, (p, p, dc, conv_w), grid=(nc, B),
        in_specs=[blk(0), blk(nc), blk(0), pl.BlockSpec((K, LANES), lambda j, b: (0, j))],
        out_specs=[blk(0), blk(0), pl.BlockSpec((K, LANES), lambda j, b: (0, j)), pl.BlockSpec((1, LANES), lambda j, b: (0, j))],
        out_shape=[_sds((B * S, D), BF), _sds((B * S, D), BF), _sds((K, D), F32), _sds((1, D), F32)],
        scratch_shapes=[pltpu.VMEM((S + CONV_PAD, LANES), F32), pltpu.VMEM((S + CONV_PAD, LANES), F32)],
        name="conv_bwd", vmem=32, comm=comm)


TM1 = 512


def _in_proj_bwd(x, dx1, dps, g1, w_in, tm=TM1, comm=None):
    T, D = x.shape
    N = w_in.shape[1]
    widths = [d.shape[1] for d in dps]

    def body(x_ref, dx1_ref, *refs):
        dp_refs, (g_ref, w_ref, dx_ref, dg_ref) = refs[:len(dps)], refs[len(dps):]
        x_t = x_ref[...]
        _, r = _rms(x_t, g_ref[...])
        dh = jnp.zeros((tm, D), F32)
        for q, dp_ref in enumerate(dp_refs):
            cols = pl.ds(sum(widths[:q]), widths[q])
            dh = dh + lax.dot_general(dp_ref[...], w_ref[:, cols], (((1,), (1,)), ((), ())), preferred_element_type=F32)
        dxa, dg = _rms_bwd(x_t, g_ref[...], r, dh)
        dx_ref[...] = dx1_ref[...] + dxa
        _acc(dg_ref, dg, pl.program_id(0) == 0)

    tile = pl.BlockSpec((tm, D), lambda i: (i, 0))
    return _pcall(
        body, (x, dx1, *dps, g1, w_in), grid=(T // tm,),
        in_specs=[tile, tile] + [pl.BlockSpec((tm, w), lambda i: (i, 0)) for w in widths] + [_const((1, D)), _const((D, N))],
        out_specs=[tile, pl.BlockSpec((1, D), lambda i: (0, 0))],
        out_shape=[_sds((T, D), F32), _sds((1, D), F32)],
        name="in_proj_bwd", vmem=48, comm=comm)


def _pick(n, cands):
    for c in cands:
        if n % c == 0:
            return c
    raise ValueError(f"no tile of {cands} divides {n}")


def _mm_tn(x, dys, name, comm=None):
    T, K = x.shape
    dys = list(dys) if isinstance(dys, (list, tuple)) else [dys]
    widths = [d.shape[1] for d in dys]
    N = sum(widths)
    tm = _pick(T, (1024, 512, 256))
    tk = _pick(K, (1024, 1408, 512))
    tn = _pick(math.gcd(*widths), (1024, 1408, 512))
    nt = T // tm
    first = [sum(widths[:q]) // tn for q in range(len(dys))]
    count = [w // tn for w in widths]

    def body(x_ref, *refs):
        dy_refs, (o_ref, ob_ref, acc) = refs[:len(dys)], refs[len(dys):]
        j, t = pl.program_id(1), pl.program_id(2)

        @pl.when(t == 0)
        def _():
            acc[...] = jnp.zeros_like(acc)

        for q, dy_ref in enumerate(dy_refs):
            @pl.when(jnp.logical_and(j >= first[q], j < first[q] + count[q]))
            def _():
                acc[...] += _dot_tn(x_ref[...], dy_ref[...])

        @pl.when(t == nt - 1)
        def _():
            o_ref[...] = acc[...]
            ob_ref[...] = acc[...].astype(BF)

    def dy_spec(q):
        def index(i, j, t):
            mine = jnp.logical_and(j >= first[q], j < first[q] + count[q])
            return jnp.where(mine, t, 0), jnp.clip(j - first[q], 0, count[q] - 1)
        return pl.BlockSpec((tm, tn), index)

    res = _pcall(
        body, (x, *dys), grid=(K // tk, N // tn, nt),
        in_specs=[pl.BlockSpec((tm, tk), lambda i, j, t: (t, i))] + [dy_spec(q) for q in range(len(dys))],
        out_specs=[pl.BlockSpec((tk, tn), lambda i, j, t: (i, j)), pl.BlockSpec((tk, tn), lambda i, j, t: (i, j))],
        out_shape=[_sds((K, N), F32), _sds((K, N), BF)],
        scratch_shapes=[pltpu.VMEM((tk, tn), F32)],
        name=name, vmem=48, comm=comm)
    return res[0] if comm is None else res


def _row_tile(R):
    return _pick(R, (128, 64, 32, 16, 8)) if R % 8 == 0 else R


def _sum_landed(full, land, kind, chip, name):
    _, R, C = land.shape
    tr = _row_tile(R)
    nb = R // tr

    def body(k_ref, o_ref, l_ref, s_ref):
        s_ref[...] = ((o_ref[...] + l_ref[0].astype(F32)) + l_ref[1].astype(F32)) + l_ref[2].astype(F32)

    own = (pl.BlockSpec((tr, C), lambda i, k: (k[0] * nb + i, 0)) if kind == "row"
           else pl.BlockSpec((tr, C), lambda i, k: (i, k[0])))
    return pl.pallas_call(
        body,
        grid_spec=pltpu.PrefetchScalarGridSpec(
            num_scalar_prefetch=1, grid=(nb,),
            in_specs=[own, pl.BlockSpec((3, tr, C), lambda i, k: (0, i, 0))],
            out_specs=pl.BlockSpec((tr, C), lambda i, k: (i, 0))),
        out_shape=_sds((R, C), F32), name=name, compiler_params=_cp(32))(chip, full, land)


def _adamw(g, w, m, v):
    m = ADAM_B1 * m + (1.0 - ADAM_B1) * g
    v = ADAM_B2 * v + (1.0 - ADAM_B2) * (g * g)
    m_hat = m / (1.0 - ADAM_B1 ** ADAM_STEP)
    v_hat = v / (1.0 - ADAM_B2 ** ADAM_STEP)
    return -ADAM_LR * (m_hat / (jnp.sqrt(v_hat) + ADAM_EPS) + ADAM_WD * w), m, v


def _update(parts, w, m, v, name):
    R, C = w.shape
    tr = _row_tile(R)
    k = len(parts)

    def body(*refs):
        g = refs[0][...]
        for r in refs[1:k]:
            g = g + r[...]
        w_ref, m_ref, v_ref, g_out, d_out, m_out, v_out = refs[k:]
        d, m_new, v_new = _adamw(g, w_ref[...], m_ref[...], v_ref[...])
        g_out[...] = g
        d_out[...] = d
        m_out[...] = m_new
        v_out[...] = v_new

    blk = pl.BlockSpec((tr, C), lambda i: (i, 0))
    return pl.pallas_call(
        body, grid=(R // tr,), in_specs=[blk] * (k + 3), out_specs=[blk] * 4, out_shape=[_sds((R, C), F32)] * 4,
        name=name, compiler_params=_cp(40))(*parts, w, m, v)


VECS = ("norm_mix", "conv_b", "conv_ln_g", "conv_ln_b", "sgu_ln_g", "sgu_ln_b", "norm_xattn", "norm_mem", "norm_ffn",
        "norm_final")
SMALL = VECS + ("b_gate", "conv_w", "sgu_w", "sgu_b")


def _update_small(land_vec, land_w, land_g1, chip, a):
    _, R, D = land_vec.shape
    Dq, G, K, nv = D // N_CHIPS, SGU_GROUPS, a["conv_w"].shape[1], len(VECS)

    def as2d(nm, arr):
        if nm in VECS:
            return arr.reshape(1, D)
        if nm == "sgu_w":
            return arr.reshape(G * LANES, LANES)
        return arr.reshape(G, LANES) if nm == "sgu_b" else arr[0]

    params = [as2d(nm, a[pre + nm]) for nm in SMALL for pre in ("", "m_", "v_")]

    def body(k_ref, lv, lvc, lw, l1, *refs):
        prm = refs[:3 * len(SMALL)]
        outs = refs[3 * len(SMALL):7 * len(SMALL)]
        tv, tvc, tw, t1 = refs[7 * len(SMALL):]
        for land, tot in ((lv, tv), (lvc, tvc), (lw, tw), (l1, t1)):
            acc = land[0]
            for dev in range(1, N_DEV):
                acc = acc + land[dev]
            tot[...] = acc
        grads = [t1[pl.ds(0, 1), :] if nm == "norm_mix" else tv[pl.ds(i, 1), :] for i, nm in enumerate(VECS)]
        grads += [tvc[pl.ds(nv, 2), :], tvc[pl.ds(nv + 2, K), :], tw[pl.ds(0, G * LANES), :], tw[pl.ds(G * LANES, G), :]]
        for i, g in enumerate(grads):
            d, m_new, v_new = _adamw(g, prm[3 * i][...], prm[3 * i + 1][...], prm[3 * i + 2][...])
            for o_ref, val in zip(outs[4 * i:4 * i + 4], (g, d, m_new, v_new)):
                o_ref[...] = val

    whole = lambda shape: pl.BlockSpec(tuple(shape), lambda i, k: (0,) * len(shape))
    res = pl.pallas_call(
        body,
        grid_spec=pltpu.PrefetchScalarGridSpec(
            num_scalar_prefetch=1, grid=(1,),
            in_specs=[whole(land_vec.shape), pl.BlockSpec((N_DEV, R, Dq), lambda i, k: (0, 0, k[0])),
                      whole(land_w.shape), whole(land_g1.shape)] + [whole(p.shape) for p in params],
            out_specs=[whole(params[3 * i].shape) for i in range(len(SMALL)) for _ in range(4)],
            scratch_shapes=[pltpu.VMEM((R, D), F32), pltpu.VMEM((R, Dq), F32), pltpu.VMEM(land_w.shape[1:], F32),
                            pltpu.VMEM(land_g1.shape[1:], F32)]),
        out_shape=[_sds(params[3 * i].shape, F32) for i in range(len(SMALL)) for _ in range(4)],
        name="upd_small", compiler_params=_cp(40))(chip, land_vec, land_vec, land_w, land_g1, *params)
    return {nm: list(res[4 * i:4 * i + 4]) for i, nm in enumerate(SMALL)}


BIG = ("w_in", "w_conv_out", "w_sgu_out", "w_mix_out", "w_q", "w_kv", "w_xo", "w_gu", "w_down")
BIG_KIND = {"w_in": "col", "w_conv_out": "row", "w_sgu_out": "row", "w_mix_out": "row", "w_q": "row",
            "w_kv": "col", "w_xo": "row", "w_gu": "col", "w_down": "row"}

def _step(a):
    x3d, mem3d, tgt3d = a["x"], a["mem"], a["loss_target"]
    B, S, D = x3d.shape
    M = mem3d.shape[1]
    T = B * S
    x = x3d.reshape(T, D)
    mem = mem3d.reshape(B * M, D)
    tgt = tgt3d.reshape(T, D)
    xi, yi = lax.axis_index("x"), lax.axis_index("y")
    chip = 2 * xi + yi

    def gather(names):
        return _Gather([a[nm][0] if nm in ("b_gate", "conv_w") else a[nm][0].astype(BF) for nm in names],
                       [BIG_KIND.get(nm, "col") for nm in names])

    first = ("w_in", "b_gate", "conv_w")
    on_in_proj = ("w_conv_out", "w_sgu_out", "w_kv", "w_mix_out", "w_q", "w_xo")
    full = dict(zip(first, _comm_call(gather(first), "gather_w_in")))
    (p, h1), got = _in_proj(x, a["norm_mix"], full["w_in"], comm=gather(on_in_proj))
    full.update(zip(on_in_proj, got))
    (c,), got = _conv_fwd(p, full["conv_w"], a["conv_b"], B, S, comm=gather(("w_down",)))
    full["w_down"] = got[0]

    sgu_b = a["sgu_b"][0]
    bz = jnp.repeat(jnp.transpose(sgu_b), LANES, axis=1)
    prm = dict(w_co=full["w_conv_out"], w_so=full["w_sgu_out"], w_mo=full["w_mix_out"], w_q=full["w_q"],
               w_xo=full["w_xo"], w_down=full["w_down"],
               la_g=a["conv_ln_g"], la_b=a["conv_ln_b"], ls_g=a["sgu_ln_g"], ls_b=a["sgu_ln_b"],
               sgu_w=a["sgu_w"][0], bz=bz, b_gate=full["b_gate"], g2=a["norm_xattn"], g3=a["norm_ffn"],
               gf=a["norm_final"].reshape(1, D))

    (merged, s_a, sg), got = _branch_fwd(c, p, prm, comm=gather(("w_gu",)))
    prm["w_gu"] = got[0]
    mem_n, kv = _kv_fwd(mem, a["norm_mem"], full["w_kv"], B, M)
    x1, x2, h2, o = _attn_fwd(x, merged, kv, prm, S, M)
    dx2, dx3, dgu, h3, f, lsum, d_g3, d_gf = _ffn_loss(x2, tgt, prm)
    loss = lax.psum(0.5 * jnp.sum(lsum) / D, ("x", "y", "c"))

    size_of = {nm: a[nm].shape[1] if BIG_KIND[nm] == "row" else a[nm].shape[2] for nm in BIG}
    landed = {}

    def scatter(names):
        return _Scatter([gw[nm][1] for nm in names], [BIG_KIND[nm] for nm in names], [size_of[nm] for nm in names])

    gw = {}
    gw["w_down"] = _mm_tn(f, dx3, "dw_down")
    gw["w_gu"] = _mm_tn(h3, dgu, "dw_gu")
    (dx1, dmerged, dq, dkv, d_g2), got = _attn_bwd(x1, kv, dx2, prm, S, M, comm=scatter(("w_gu",)))
    landed["w_gu"] = got[0]
    gw["w_xo"] = _mm_tn(o, dx2, "dw_xo")
    gw["w_q"] = _mm_tn(h2, dq, "dw_q")
    gw["w_kv"] = _mm_tn(mem_n, dkv, "dw_kv")
    d_gm = _kv_bwd(mem, a["norm_mem"], full["w_kv"], dkv, B, M)
    gw["w_mix_out"] = _mm_tn(merged, dx1, "dw_mix_out")
    group = ("w_down", "w_xo", "w_q", "w_kv", "w_mix_out")
    (dc, dpb, dya, dyb, d_wm, d_bz, d_lag, d_lab, d_lsg, d_lsb, d_bg), got = _branch_bwd(
        c, p, dmerged, prm, comm=scatter(group))
    landed.update(zip(group, got))
    gw["w_conv_out"] = _mm_tn(s_a, dya, "dw_conv_out")
    gw["w_sgu_out"] = _mm_tn(sg, dyb, "dw_sgu_out")
    group = ("w_conv_out", "w_sgu_out")
    (dav, dag, d_cw, d_cb), got = _conv_bwd(p, dc, full["conv_w"], B, S, comm=scatter(group))
    landed.update(zip(group, got))
    dp = [dav, dag, dpb]

    chip_arr = jnp.reshape(chip, (1,)).astype(jnp.int32)
    early = [nm for nm in BIG if nm != "w_in"]
    part = {nm: _sum_landed(gw[nm][0], landed[nm], BIG_KIND[nm], chip_arr, "sum_" + nm) for nm in early}
    G = SGU_GROUPS
    d_sb = jnp.transpose(d_bz.reshape(LANES, G, LANES).sum(axis=-1))
    vec_g = dict(norm_mix=jnp.zeros((1, D), F32), conv_b=d_cb, conv_ln_g=d_lag, conv_ln_b=d_lab, sgu_ln_g=d_lsg,
                 sgu_ln_b=d_lsb, norm_xattn=d_g2, norm_mem=d_gm, norm_ffn=d_g3, norm_final=d_gf)
    rows = [vec_g[nm] for nm in VECS] + [d_bg, d_cw]
    pad = (-sum(r.shape[0] for r in rows)) % 8
    g_vec = jnp.concatenate(rows + [jnp.zeros((pad, D), F32)], axis=0)
    g_w = jnp.concatenate([d_wm.reshape(G * LANES, LANES), d_sb], axis=0)
    gw["w_in"], got = _mm_tn(h1, dp, "dw_in", comm=_Both(_Swap([part[nm] for nm in early]), _Spread([g_vec, g_w])))
    other = dict(zip(early, got[:len(early)]))
    land_vec, land_w = got[len(early):]
    (grad_x, d_g1), got = _in_proj_bwd(x, dx1, dp, a["norm_mix"], full["w_in"], comm=scatter(("w_in",)))
    part["w_in"] = _sum_landed(gw["w_in"][0], got[0], BIG_KIND["w_in"], chip_arr, "sum_w_in")
    g1 = jnp.concatenate([d_g1, jnp.zeros((7, D), F32)], axis=0)
    other["w_in"], land_g1 = _comm_call(_Both(_Swap([part["w_in"]]), _Spread([g1])), "swap_w_in")

    out = {}
    for nm in BIG:
        res = _update([part[nm], other[nm]], a[nm][0], a["m_" + nm][0], a["v_" + nm][0], "upd_" + nm)
        out[nm] = [r[None] for r in res]
    for nm, res in _update_small(land_vec, land_w, land_g1, chip_arr, a).items():
        out[nm] = [r.reshape(a[nm].shape) for r in res]
    return loss, grad_x.reshape(B, S, D), out


WEIGHTS = ("norm_mix", "w_in", "b_gate", "conv_w", "conv_b", "conv_ln_g", "conv_ln_b", "w_conv_out", "sgu_ln_g",
           "sgu_ln_b", "sgu_w", "sgu_b", "w_sgu_out", "w_mix_out", "norm_xattn", "norm_mem", "w_q", "w_kv", "w_xo",
           "norm_ffn", "w_gu", "w_down", "norm_final")


def kernel(x, mem, norm_mix, w_in, b_gate, conv_w, conv_b, conv_ln_g, conv_ln_b, w_conv_out, sgu_ln_g, sgu_ln_b, sgu_w, sgu_b, w_sgu_out, w_mix_out, norm_xattn, norm_mem, w_q, w_kv, w_xo, norm_ffn, w_gu, w_down, norm_final, loss_target, m_norm_mix, m_w_in, m_b_gate, m_conv_w, m_conv_b, m_conv_ln_g, m_conv_ln_b, m_w_conv_out, m_sgu_ln_g, m_sgu_ln_b, m_sgu_w, m_sgu_b, m_w_sgu_out, m_w_mix_out, m_norm_xattn, m_norm_mem, m_w_q, m_w_kv, m_w_xo, m_norm_ffn, m_w_gu, m_w_down, m_norm_final, v_norm_mix, v_w_in, v_b_gate, v_conv_w, v_conv_b, v_conv_ln_g, v_conv_ln_b, v_w_conv_out, v_sgu_ln_g, v_sgu_ln_b, v_sgu_w, v_sgu_b, v_w_sgu_out, v_w_mix_out, v_norm_xattn, v_norm_mem, v_w_q, v_w_kv, v_w_xo, v_norm_ffn, v_w_gu, v_w_down, v_norm_final):
    a = dict(locals())
    loss, grad_x, out = _step(a)
    res = [loss, grad_x]
    for q in range(4):
        res += [out[nm][q] for nm in WEIGHTS]
    return tuple(res)
```

```python
import functools
import math

import jax
import jax.numpy as jnp
from jax import lax
from jax.experimental import pallas as pl
from jax.experimental.pallas import tpu as pltpu

BF = jnp.bfloat16
F32 = jnp.float32
MESH = pl.DeviceIdType.MESH
ANY = pl.BlockSpec(memory_space=pl.ANY)

RMS_EPS = 1e-6
LN_EPS = 1e-5
HEADS = 4
SGU_GROUPS = 8
LANES = 128
ADAM_LR = 0.001
ADAM_B1 = 0.9
ADAM_B2 = 0.999
ADAM_EPS = 1e-08
ADAM_WD = 0.01
ADAM_STEP = 10
N_CHIPS = 4
N_DEV = 8
MIB = 1024 * 1024


def _sds(shape, dtype):
    return jax.ShapeDtypeStruct(tuple(shape), dtype)


def _cp(vmem_mib):
    return pltpu.CompilerParams(vmem_limit_bytes=vmem_mib * MIB)


def _const(shape):
    nd = len(shape)
    return pl.BlockSpec(tuple(shape), lambda *_: (0,) * nd, pipeline_mode=pl.Buffered(1))


def _dot(a, b):
    return jnp.dot(a.astype(BF), b.astype(BF), preferred_element_type=F32)


def _dot_nt(a, b):
    return lax.dot_general(a.astype(BF), b.astype(BF), (((1,), (1,)), ((), ())), preferred_element_type=F32)


def _dot_tn(a, b):
    return lax.dot_general(a.astype(BF), b.astype(BF), (((0,), (0,)), ((), ())), preferred_element_type=F32)


def _sig(x):
    return 1.0 / (1.0 + jnp.exp(-x))


def _dsilu(x, s):
    return s * (1.0 + x * (1.0 - s))


_GELU_C = math.sqrt(2.0 / math.pi)


def _gelu(x):
    x2 = x * x
    t = jnp.tanh((_GELU_C * x) * (1.0 + 0.044715 * x2))
    cdf = 0.5 * (1.0 + t)
    return x * cdf, (t, x2, cdf)


def _dgelu(x, shared):
    t, x2, cdf = shared
    return cdf + ((0.5 * _GELU_C) * x) * (1.0 - t * t) * (1.0 + (3.0 * 0.044715) * x2)


def _rms(x, g):
    r = lax.rsqrt(jnp.mean(x * x, axis=-1, keepdims=True) + RMS_EPS)
    return x * r * g, r


def _rms_bwd(x, g, r, dh):
    xr = x * r
    dxh = dh * g
    dx = r * (dxh - xr * jnp.mean(dxh * xr, axis=-1, keepdims=True))
    return dx, jnp.sum(dh * xr, axis=0, keepdims=True)


def _ln(x, g, b):
    mu = jnp.mean(x, axis=-1, keepdims=True)
    xc = x - mu
    rstd = lax.rsqrt(jnp.mean(xc * xc, axis=-1, keepdims=True) + LN_EPS)
    xh = xc * rstd
    return xh * g + b, xh, rstd


def _ln_bwd(xh, rstd, g, dy):
    dxh = dy * g
    dx = rstd * (dxh - jnp.mean(dxh, axis=-1, keepdims=True) - xh * jnp.mean(dxh * xh, axis=-1, keepdims=True))
    return dx, jnp.sum(dy * xh, axis=0, keepdims=True), jnp.sum(dy, axis=0, keepdims=True)


def _acc(ref, val, first):
    @pl.when(first)
    def _():
        ref[...] = val

    @pl.when(jnp.logical_not(first))
    def _():
        ref[...] += val


def _place():
    x, y, c = lax.axis_index("x"), lax.axis_index("y"), lax.axis_index("c")
    chips = [(1 - x, y), (x, 1 - y), (1 - x, 1 - y)]
    return x, y, c, chips


def _shard_of(ref, kind, k, n):
    if kind == "row":
        return ref.at[pl.ds(k * n, n), :]
    return ref.at[:, pl.ds(k * n, n)]


class _Gather:
    def __init__(self, shards, kinds):
        n = len(shards)
        self.srcs, self.kinds = list(shards), list(kinds)
        self.sizes = [s.shape[0] if kd == "row" else s.shape[1] for s, kd in zip(shards, kinds)]
        self.halves = [s.shape[0] // 2 if s.shape[0] % 32 == 0 else None for s in shards]
        self.out_shape = [
            _sds((s.shape[0] * N_CHIPS, s.shape[1]) if kd == "row" else (s.shape[0], s.shape[1] * N_CHIPS), s.dtype)
            for s, kd in zip(shards, kinds)]
        dma = pltpu.SemaphoreType.DMA
        self.sems = [dma((3 * n,)), dma((3 * n,)), dma((n,)), dma((3 * n,)), dma((3 * n,))]

    def _part(self, ref, t, core):
        h = self.halves[t]
        return ref if h is None else ref.at[pl.ds(core * h, h), :]

    def _copies(self, ins, outs, send, recv, loc, fsend, frecv):
        x, y, c, chips = _place()
        k = 2 * x + y
        local, remote = [], []
        for t in range(len(ins)):
            block = lambda q: _shard_of(outs[t], self.kinds[t], q, self.sizes[t])
            local.append(pltpu.make_async_copy(ins[t], block(k), loc.at[t]))
            for j, (px, py) in enumerate(chips):
                sems = dict(send_sem=send.at[3 * t + j], recv_sem=recv.at[3 * t + j])
                there = dict(device_id=(px, py, c), device_id_type=MESH)
                sent = pltpu.make_async_remote_copy(
                    src_ref=self._part(ins[t], t, c), dst_ref=self._part(block(k), t, c), **sems, **there)
                got = self._part(block(2 * px + py), t, c)
                landed = pltpu.make_async_remote_copy(src_ref=self._part(ins[t], t, c), dst_ref=got, **sems, **there)
                passed = handed = None
                if self.halves[t] is not None:
                    fsems = dict(send_sem=fsend.at[3 * t + j], recv_sem=frecv.at[3 * t + j])
                    sibling = dict(device_id=(x, y, 1 - c), device_id_type=MESH)
                    passed = pltpu.make_async_remote_copy(src_ref=got, dst_ref=got, **fsems, **sibling)
                    other = self._part(block(2 * px + py), t, 1 - c)
                    handed = pltpu.make_async_remote_copy(src_ref=got, dst_ref=other, **fsems, **sibling)
                remote.append((sent, landed, passed, handed))
        return local, remote

    def start(self, ins, outs, *sems):
        local, remote = self._copies(ins, outs, *sems)
        for cp in local:
            cp.start()
        for sent, _, _, _ in remote:
            sent.start()

    def wait(self, ins, outs, *sems):
        local, remote = self._copies(ins, outs, *sems)
        for sent, landed, passed, handed in remote:
            landed.wait_recv()
            if passed is not None:
                passed.start()
        for sent, landed, passed, handed in remote:
            if passed is not None:
                handed.wait_recv()
                passed.wait_send()
            sent.wait_send()
        for cp in local:
            cp.wait()


class _Scatter:
    def __init__(self, grads, kinds, sizes):
        n = len(grads)
        self.srcs, self.kinds, self.sizes = list(grads), list(kinds), list(sizes)
        self.out_shape = [_sds((3,) + ((sz, g.shape[1]) if kd == "row" else (g.shape[0], sz)), g.dtype)
                          for g, kd, sz in zip(grads, kinds, sizes)]
        self.sems = [pltpu.SemaphoreType.DMA((3 * n,)), pltpu.SemaphoreType.DMA((3 * n,))]

    def _copies(self, ins, outs, send, recv):
        x, y, c, chips = _place()
        return [pltpu.make_async_remote_copy(
            src_ref=_shard_of(ins[t], self.kinds[t], 2 * px + py, self.sizes[t]), dst_ref=outs[t].at[j],
            send_sem=send.at[3 * t + j], recv_sem=recv.at[3 * t + j], device_id=(px, py, c), device_id_type=MESH)
            for t in range(len(ins)) for j, (px, py) in enumerate(chips)]

    def start(self, ins, outs, send, recv):
        for cp in self._copies(ins, outs, send, recv):
            cp.start()

    def wait(self, ins, outs, send, recv):
        for cp in self._copies(ins, outs, send, recv):
            cp.wait_recv()
            cp.wait_send()


class _Swap:
    def __init__(self, parts):
        n = len(parts)
        self.srcs = list(parts)
        self.out_shape = [_sds(p.shape, p.dtype) for p in parts]
        self.sems = [pltpu.SemaphoreType.DMA((n,)), pltpu.SemaphoreType.DMA((n,))]

    def _copies(self, ins, outs, send, recv):
        x, y, c, _ = _place()
        return [pltpu.make_async_remote_copy(
            src_ref=ins[t], dst_ref=outs[t], send_sem=send.at[t], recv_sem=recv.at[t],
            device_id=(x, y, 1 - c), device_id_type=MESH) for t in range(len(ins))]

    def start(self, ins, outs, send, recv):
        for cp in self._copies(ins, outs, send, recv):
            cp.start()

    def wait(self, ins, outs, send, recv):
        for cp in self._copies(ins, outs, send, recv):
            cp.wait_recv()
            cp.wait_send()


class _Spread:
    def __init__(self, packs):
        n = len(packs)
        self.srcs = list(packs)
        self.out_shape = [_sds((N_DEV,) + p.shape, p.dtype) for p in packs]
        dma = pltpu.SemaphoreType.DMA
        self.sems = [dma((N_DEV * n,)), dma((N_DEV * n,)), dma((n,))]

    def _copies(self, ins, outs, send, recv, loc):
        x, y, c, _ = _place()
        me = 4 * x + 2 * y + c
        local = [pltpu.make_async_copy(ins[t], outs[t].at[me], loc.at[t]) for t in range(len(ins))]
        remote = []
        for t in range(len(ins)):
            for mask in range(1, N_DEV):
                peer = ((1 - x) if mask & 4 else x, (1 - y) if mask & 2 else y, (1 - c) if mask & 1 else c)
                src = peer[0] * 4 + peer[1] * 2 + peer[2]
                sems = dict(send_sem=send.at[N_DEV * t + mask], recv_sem=recv.at[N_DEV * t + mask])
                sent = pltpu.make_async_remote_copy(
                    src_ref=ins[t], dst_ref=outs[t].at[me], device_id=peer, device_id_type=MESH, **sems)
                landed = pltpu.make_async_remote_copy(
                    src_ref=ins[t], dst_ref=outs[t].at[src], device_id=peer, device_id_type=MESH, **sems)
                remote.append((sent, landed))
        return local, remote

    def start(self, ins, outs, send, recv, loc):
        local, remote = self._copies(ins, outs, send, recv, loc)
        for cp in local:
            cp.start()
        for sent, _ in remote:
            sent.start()

    def wait(self, ins, outs, send, recv, loc):
        local, remote = self._copies(ins, outs, send, recv, loc)
        for sent, landed in remote:
            landed.wait_recv()
            sent.wait_send()
        for cp in local:
            cp.wait()


class _Both:
    def __init__(self, *comms):
        self.comms = comms
        self.srcs = [s for cm in comms for s in cm.srcs]
        self.out_shape = [s for cm in comms for s in cm.out_shape]
        self.sems = [s for cm in comms for s in cm.sems]

    def _each(self, ins, outs, sems):
        i = o = k = 0
        for cm in self.comms:
            ni, no, nk = len(cm.srcs), len(cm.out_shape), len(cm.sems)
            yield cm, ins[i:i + ni], outs[o:o + no], sems[k:k + nk]
            i, o, k = i + ni, o + no, k + nk

    def start(self, ins, outs, *sems):
        for cm, i, o, s in self._each(ins, outs, sems):
            cm.start(i, o, *s)

    def wait(self, ins, outs, *sems):
        for cm, i, o, s in self._each(ins, outs, sems):
            cm.wait(i, o, *s)


HBM = pl.BlockSpec(memory_space=pltpu.HBM)
SEM = pl.BlockSpec(memory_space=pltpu.SEMAPHORE)
EFFECT = pltpu.SideEffectType.DATAFLOW_SIDE_EFFECTING


def _scatter_copies(g_ref, land_ref, send, recv, kind, size):
    x, y, c, chips = _place()
    return [pltpu.make_async_remote_copy(
        src_ref=_shard_of(g_ref, kind, 2 * px + py, size), dst_ref=land_ref.at[j],
        send_sem=send.at[j], recv_sem=recv.at[j], device_id=(px, py, c), device_id_type=MESH)
        for j, (px, py) in enumerate(chips)]


def _scatter_start(grad, kind, size, name):
    land = (3,) + ((size, grad.shape[1]) if kind == "row" else (grad.shape[0], size))

    def body(g_ref, land_ref, send, recv, g_thru, land_thru, token):
        for cp in _scatter_copies(g_ref, land_ref, send, recv, kind, size):
            cp.start()
        token[...] = jnp.zeros_like(token)

    return pl.pallas_call(
        body, name=name,
        out_shape=(pltpu.SemaphoreType.DMA((3,)), pltpu.SemaphoreType.DMA((3,)), pltpu.HBM(grad.shape, grad.dtype),
                   pltpu.HBM(land, grad.dtype), _sds((8, LANES), F32)),
        in_specs=(HBM, HBM), out_specs=(SEM, SEM, HBM, HBM, pl.BlockSpec(memory_space=pltpu.VMEM)),
        input_output_aliases={0: 2, 1: 3},
        compiler_params=pltpu.CompilerParams(has_side_effects=EFFECT))(
            pltpu.with_memory_space_constraint(grad, pltpu.HBM),
            pltpu.with_memory_space_constraint(lax.empty(land, grad.dtype), pltpu.HBM))


def _scatter_wait(send, recv, g_thru, land_thru, after, kind, size, name):
    def body(g_ref, land_ref, send, recv, *rest):
        for cp in _scatter_copies(g_ref, land_ref, send, recv, kind, size):
            cp.wait_send()
            cp.wait_recv()

    return pl.pallas_call(
        body, name=name,
        out_shape=(pltpu.HBM(g_thru.shape, g_thru.dtype), pltpu.HBM(land_thru.shape, land_thru.dtype)),
        in_specs=(HBM, HBM, SEM, SEM) + (ANY,) * len(after), out_specs=(HBM, HBM),
        input_output_aliases={0: 0, 1: 1},
        compiler_params=pltpu.CompilerParams(has_side_effects=EFFECT))(g_thru, land_thru, send, recv, *after)[1]


def _comm_call(comm, name):
    n, m = len(comm.srcs), len(comm.out_shape)

    def body(*refs):
        comm.start(refs[:n], refs[n:n + m], *refs[n + m:])
        comm.wait(refs[:n], refs[n:n + m], *refs[n + m:])

    return pl.pallas_call(body, in_specs=[ANY] * n, out_specs=[ANY] * m, out_shape=comm.out_shape,
                          scratch_shapes=comm.sems, name=name)(*comm.srcs)


def _pcall(body, args, *, grid, in_specs, out_specs, out_shape, name, vmem, scratch_shapes=(), comm=None):
    in_specs, out_specs, out_shape = list(in_specs), list(out_specs), list(out_shape)
    scratch_shapes = list(scratch_shapes)
    if comm is None:
        res = pl.pallas_call(body, grid=grid, in_specs=in_specs, out_specs=out_specs, out_shape=out_shape,
                             scratch_shapes=scratch_shapes, name=name, compiler_params=_cp(vmem))(*args)
        return list(res), []
    ni, no, ns = len(in_specs), len(out_specs), len(scratch_shapes)
    ci, co = len(comm.srcs), len(comm.out_shape)

    def carried(*refs):
        c_in = refs[ni:ni + ci]
        c_out = refs[ni + ci + no:ni + ci + no + co]
        sems = refs[ni + ci + no + co + ns:]
        ids = [pl.program_id(d) for d in range(len(grid))]
        first = functools.reduce(jnp.logical_and, [i == 0 for i in ids])
        last = functools.reduce(jnp.logical_and, [i == g - 1 for i, g in zip(ids, grid)])

        @pl.when(first)
        def _():
            comm.start(c_in, c_out, *sems)

        body(*refs[:ni], *refs[ni + ci:ni + ci + no], *refs[ni + ci + no + co:ni + ci + no + co + ns])

        @pl.when(last)
        def _():
            comm.wait(c_in, c_out, *sems)

    res = pl.pallas_call(carried, grid=grid, in_specs=in_specs + [ANY] * ci, out_specs=out_specs + [ANY] * co,
                         out_shape=out_shape + list(comm.out_shape), scratch_shapes=scratch_shapes + list(comm.sems),
                         name=name, compiler_params=_cp(vmem))(*args, *comm.srcs)
    return list(res[:no]), list(res[no:])


def _in_proj(x, g1, w_in, comm=None):
    T, D = x.shape
    N = w_in.shape[1]
    tm, tn = 512, 1024

    def body(x_ref, g_ref, w_ref, p_ref, h_ref):
        h, _ = _rms(x_ref[...], g_ref[...])
        h_ref[...] = h.astype(BF)
        for j in range(N // tn):
            cols = pl.ds(j * tn, tn)
            p_ref[:, cols] = jnp.dot(h_ref[...], w_ref[:, cols], preferred_element_type=F32)

    return _pcall(
        body, (x, g1, w_in), grid=(T // tm,),
        in_specs=[pl.BlockSpec((tm, D), lambda i: (i, 0)), _const((1, D)), _const((D, N))],
        out_specs=[pl.BlockSpec((tm, N), lambda i: (i, 0)), pl.BlockSpec((tm, D), lambda i: (i, 0))],
        out_shape=[_sds((T, N), F32), _sds((T, D), BF)],
        name="in_proj", vmem=56, comm=comm)


CONV_PAD = 32
CONV_ROWS = 256


def _conv_fwd(p, conv_w, conv_b, B, S, comm=None):
    K, D = conv_w.shape
    nc = D // LANES

    def body(av_ref, ag_ref, w_ref, b_ref, c_ref, apad):
        apad[pl.ds(0, CONV_PAD), :] = jnp.zeros((CONV_PAD, LANES), F32)
        apad[pl.ds(CONV_PAD, S), :] = av_ref[...] * _sig(ag_ref[...])
        for r0 in range(0, S, CONV_ROWS):
            acc = jnp.zeros((CONV_ROWS, LANES), F32) + b_ref[...]
            for k in range(K):
                acc = acc + w_ref[pl.ds(k, 1), :] * apad[pl.ds(r0 + k + CONV_PAD - (K - 1), CONV_ROWS), :]
            c_ref[pl.ds(r0, CONV_ROWS), :] = acc

    return _pcall(
        body, (p, p, conv_w, conv_b), grid=(B, nc),
        in_specs=[pl.BlockSpec((S, LANES), lambda b, j: (b, j)), pl.BlockSpec((S, LANES), lambda b, j: (b, nc + j)),
                  pl.BlockSpec((K, LANES), lambda b, j: (0, j)), pl.BlockSpec((1, LANES), lambda b, j: (0, j))],
        out_specs=[pl.BlockSpec((S, LANES), lambda b, j: (b, j))],
        out_shape=[_sds((B * S, D), F32)],
        scratch_shapes=[pltpu.VMEM((S + CONV_PAD, LANES), F32)],
        name="conv_fwd", vmem=32, comm=comm)


def _tril_mask():
    t = lax.broadcasted_iota(jnp.int32, (LANES, LANES), 0)
    s = lax.broadcasted_iota(jnp.int32, (LANES, LANES), 1)
    return t >= s


def _branch_a(c, g, b):
    ln_a, xh, rstd = _ln(c, g, b)
    s = _sig(ln_a)
    return ln_a * s, ln_a, s, xh, rstd


def _branch_b(bu, bv, g, b, wm_ref, bz_ref, z_scr, v_scr):
    tm, D = bu.shape
    u, tu = _gelu(bu)
    gv, tv = _gelu(bv)
    v, vh, rstd = _ln(gv, g, b)
    v_scr[...] = v.astype(BF)
    mask = _tril_mask()
    for gi in range(SGU_GROUPS):
        wm = jnp.where(mask, wm_ref[gi], 0.0).astype(BF)
        cols = pl.ds(gi * LANES, LANES)
        for n in range(tm // LANES):
            rows = pl.ds(n * LANES, LANES)
            z_scr[rows, cols] = jnp.dot(wm, v_scr[rows, cols], preferred_element_type=F32) + bz_ref[:, cols]
    z = z_scr[...]
    return u * z, u, tu, z, tv, vh, rstd


TM3 = 256
TM3_FWD = 512


def _branch_fwd(c, p, prm, tm=TM3_FWD, comm=None):
    T, D = c.shape

    def body(c_ref, bu_ref, bv_ref, ga_ref, gb_ref, wco_ref, wso_ref, lag_ref, lab_ref, lsg_ref, lsb_ref, wm_ref,
             bz_ref, bg_ref, mg_ref, sa_ref, sg_ref, z_scr, v_scr):
        s_a = _branch_a(c_ref[...], lag_ref[...], lab_ref[...])[0]
        sa_ref[...] = s_a.astype(BF)
        y_a = jnp.dot(sa_ref[...], wco_ref[...], preferred_element_type=F32)
        sg = _branch_b(bu_ref[...], bv_ref[...], lsg_ref[...], lsb_ref[...], wm_ref, bz_ref, z_scr, v_scr)[0]
        sg_ref[...] = sg.astype(BF)
        y_b = jnp.dot(sg_ref[...], wso_ref[...], preferred_element_type=F32)
        ga = _sig(ga_ref[...] + bg_ref[pl.ds(0, 1), :])
        gb = _sig(gb_ref[...] + bg_ref[pl.ds(1, 1), :])
        mg_ref[...] = (ga * y_a + gb * y_b).astype(BF)

    tile = lambda j: pl.BlockSpec((tm, D), lambda i: (i, j))
    return _pcall(
        body, (c, p, p, p, p, prm["w_co"], prm["w_so"], prm["la_g"], prm["la_b"], prm["ls_g"], prm["ls_b"],
               prm["sgu_w"], prm["bz"], prm["b_gate"]),
        grid=(T // tm,),
        in_specs=[tile(0), tile(2), tile(3), tile(4), tile(5), _const((D, D)), _const((D, D)),
                  _const((1, D)), _const((1, D)), _const((1, D)), _const((1, D)),
                  _const((SGU_GROUPS, LANES, LANES)), _const((LANES, D)), _const((2, D))],
        out_specs=[tile(0), tile(0), tile(0)],
        out_shape=[_sds((T, D), BF)] * 3,
        scratch_shapes=[pltpu.VMEM((tm, D), F32), pltpu.VMEM((tm, D), BF)],
        name="branch_fwd", vmem=48, comm=comm)


def _kv_fwd(mem, gm, w_kv, B, M):
    D = mem.shape[1]
    N = w_kv.shape[1]

    def body(m_ref, g_ref, w_ref, mn_ref, kv_ref):
        h, _ = _rms(m_ref[...], g_ref[...])
        mn_ref[...] = h.astype(BF)
        kv_ref[...] = jnp.dot(mn_ref[...], w_ref[...], preferred_element_type=F32).astype(BF)

    return pl.pallas_call(
        body, grid=(B,),
        in_specs=[pl.BlockSpec((M, D), lambda b: (b, 0)), _const((1, D)), _const((D, N))],
        out_specs=[pl.BlockSpec((M, D), lambda b: (b, 0)), pl.BlockSpec((M, N), lambda b: (b, 0))],
        out_shape=[_sds((B * M, D), BF), _sds((B * M, N), BF)],
        name="kv_fwd", compiler_params=_cp(32))(mem, gm, w_kv)


def _softmax_rows(s):
    e = jnp.exp(s - jnp.max(s, axis=-1, keepdims=True))
    return e * (1.0 / jnp.sum(e, axis=-1, keepdims=True))


TM4 = 512


def _attn_fwd(x, merged, kv, prm, S, M, tm=TM4):
    T, D = x.shape
    hd = D // HEADS
    scale = hd ** -0.5
    tpb = S // tm

    def body(x_ref, mg_ref, kv_ref, wmo_ref, wq_ref, wxo_ref, g_ref, x1_ref, x2_ref, h2_ref, o_ref):
        x1 = x_ref[...] + jnp.dot(mg_ref[...], wmo_ref[...], preferred_element_type=F32)
        x1_ref[...] = x1
        h2, _ = _rms(x1, g_ref[...])
        h2_ref[...] = h2.astype(BF)
        qb = jnp.dot(h2_ref[...], wq_ref[...], preferred_element_type=F32).astype(BF)
        for h in range(HEADS):
            cs = pl.ds(h * hd, hd)
            s = _dot_nt(qb[:, h * hd:(h + 1) * hd], kv_ref[:, cs]) * scale
            pr = _softmax_rows(s)
            o_ref[:, cs] = _dot(pr, kv_ref[:, pl.ds(D + h * hd, hd)]).astype(BF)
        x2_ref[...] = x1 + jnp.dot(o_ref[...], wxo_ref[...], preferred_element_type=F32)

    tile = pl.BlockSpec((tm, D), lambda i: (i, 0))
    return pl.pallas_call(
        body, grid=(T // tm,),
        in_specs=[tile, tile, pl.BlockSpec((M, 2 * D), lambda i: (i // tpb, 0)),
                  _const((D, D)), _const((D, D)), _const((D, D)), _const((1, D))],
        out_specs=[tile, tile, tile, tile],
        out_shape=[_sds((T, D), F32), _sds((T, D), F32), _sds((T, D), BF), _sds((T, D), BF)],
        name="attn_fwd", compiler_params=_cp(40))(x, merged, kv, prm["w_mo"], prm["w_q"], prm["w_xo"], prm["g2"])


TM5 = 256
FFN_CHUNKS = 1


def _ffn_loss(x2, tgt, prm, tm=TM5):
    T, D = x2.shape
    F = prm["w_down"].shape[0]
    FC = F // FFN_CHUNKS

    def body(x2_ref, t_ref, wgu_ref, wd_ref, g3_ref, gf_ref, dx2_ref, dx3_ref, dgu_ref, h3_ref, f_ref, ls_ref,
             dg3_ref, dgf_ref, gu_scr):
        first = pl.program_id(0) == 0
        x2 = x2_ref[...]
        h3, r3 = _rms(x2, g3_ref[...])
        h3_ref[...] = h3.astype(BF)
        x3 = x2
        for ch in range(FFN_CHUNKS):
            gc, uc = pl.ds(ch * FC, FC), pl.ds(F + ch * FC, FC)
            gt = jnp.dot(h3_ref[...], wgu_ref[:, gc], preferred_element_type=F32)
            up = jnp.dot(h3_ref[...], wgu_ref[:, uc], preferred_element_type=F32)
            gu_scr[:, gc] = gt
            gu_scr[:, uc] = up
            f_ref[:, gc] = (gt * _sig(gt) * up).astype(BF)
            x3 = x3 + jnp.dot(f_ref[:, gc], wd_ref[gc, :], preferred_element_type=F32)
        y, rf = _rms(x3, gf_ref[...])
        e = y - t_ref[...]
        _acc(ls_ref, jnp.sum(e * e, axis=0, keepdims=True), first)
        dx3, dgf = _rms_bwd(x3, gf_ref[...], rf, e * (1.0 / D))
        _acc(dgf_ref, dgf, first)
        dx3_ref[...] = dx3.astype(BF)
        dh3 = jnp.zeros((tm, D), F32)
        for ch in range(FFN_CHUNKS):
            gc, uc = pl.ds(ch * FC, FC), pl.ds(F + ch * FC, FC)
            df = lax.dot_general(dx3_ref[...], wd_ref[gc, :], (((1,), (1,)), ((), ())), preferred_element_type=F32)
            gt, up = gu_scr[:, gc], gu_scr[:, uc]
            s = _sig(gt)
            dgu_ref[:, gc] = (df * up * _dsilu(gt, s)).astype(BF)
            dgu_ref[:, uc] = (df * gt * s).astype(BF)
            dh3 = dh3 + lax.dot_general(dgu_ref[:, gc], wgu_ref[:, gc], (((1,), (1,)), ((), ())), preferred_element_type=F32)
            dh3 = dh3 + lax.dot_general(dgu_ref[:, uc], wgu_ref[:, uc], (((1,), (1,)), ((), ())), preferred_element_type=F32)
        dxa, dg3 = _rms_bwd(x2, g3_ref[...], r3, dh3)
        _acc(dg3_ref, dg3, first)
        dx2_ref[...] = dx3 + dxa

    tile = lambda n: pl.BlockSpec((tm, n), lambda i: (i, 0))
    vec = pl.BlockSpec((1, D), lambda i: (0, 0))
    return pl.pallas_call(
        body, grid=(T // tm,),
        in_specs=[tile(D), tile(D), _const((D, 2 * F)), _const((F, D)), _const((1, D)), _const((1, D))],
        out_specs=[tile(D), tile(D), tile(2 * F), tile(D), tile(F), vec, vec, vec],
        out_shape=[_sds((T, D), F32), _sds((T, D), BF), _sds((T, 2 * F), BF), _sds((T, D), BF), _sds((T, F), BF),
                   _sds((1, D), F32), _sds((1, D), F32), _sds((1, D), F32)],
        scratch_shapes=[pltpu.VMEM((tm, 2 * F), F32)],
        name="ffn_loss", compiler_params=_cp(56))(x2, tgt, prm["w_gu"], prm["w_down"], prm["g3"], prm["gf"])


def _attn_bwd(x1, kv, dx2, prm, S, M, tm=TM4, comm=None):
    T, D = x1.shape
    hd = D // HEADS
    scale = hd ** -0.5
    tpb = S // tm

    def body(x1_ref, kv_ref, dx2_ref, wmo_ref, wq_ref, wxo_ref, g_ref, dx1_ref, dmg_ref, dq_ref, dkv_ref, dg_ref,
             h2_scr, do_scr):
        i = pl.program_id(0)
        x1 = x1_ref[...]
        dx2 = dx2_ref[...]
        h2, r2 = _rms(x1, g_ref[...])
        h2_scr[...] = h2.astype(BF)
        qb = jnp.dot(h2_scr[...], wq_ref[...], preferred_element_type=F32).astype(BF)
        do_scr[...] = _dot_nt(dx2, wxo_ref[...]).astype(BF)
        for h in range(HEADS):
            cs, vs = pl.ds(h * hd, hd), pl.ds(D + h * hd, hd)
            qh = qb[:, h * hd:(h + 1) * hd]
            pr = _softmax_rows(_dot_nt(qh, kv_ref[:, cs]) * scale)
            dpr = _dot_nt(do_scr[:, cs], kv_ref[:, vs])
            dv = _dot_tn(pr, do_scr[:, cs])
            ds = (pr * (dpr - jnp.sum(dpr * pr, axis=-1, keepdims=True)) * scale).astype(BF)
            dq_ref[:, cs] = jnp.dot(ds, kv_ref[:, cs], preferred_element_type=F32).astype(BF)
            dk = _dot_tn(ds, qh)

            @pl.when(i % tpb == 0)
            def _():
                dkv_ref[:, cs] = dk
                dkv_ref[:, vs] = dv

            @pl.when(i % tpb != 0)
            def _():
                dkv_ref[:, cs] += dk
                dkv_ref[:, vs] += dv

        dh2 = _dot_nt(dq_ref[...], wq_ref[...])
        dxa, dg = _rms_bwd(x1, g_ref[...], r2, dh2)
        _acc(dg_ref, dg, i == 0)
        dx1 = dx2 + dxa
        dx1_ref[...] = dx1
        dmg_ref[...] = _dot_nt(dx1, wmo_ref[...])

    tile = pl.BlockSpec((tm, D), lambda i: (i, 0))
    kvb = pl.BlockSpec((M, 2 * D), lambda i: (i // tpb, 0))
    B = T // S
    return _pcall(
        body, (x1, kv, dx2, prm["w_mo"], prm["w_q"], prm["w_xo"], prm["g2"]), grid=(T // tm,),
        in_specs=[tile, kvb, tile, _const((D, D)), _const((D, D)), _const((D, D)), _const((1, D))],
        out_specs=[tile, tile, tile, kvb, pl.BlockSpec((1, D), lambda i: (0, 0))],
        out_shape=[_sds((T, D), F32), _sds((T, D), F32), _sds((T, D), BF), _sds((B * M, 2 * D), F32), _sds((1, D), F32)],
        scratch_shapes=[pltpu.VMEM((tm, D), BF), pltpu.VMEM((tm, D), BF)],
        name="attn_bwd", vmem=48, comm=comm)


def _kv_bwd(mem, gm, w_kv, dkv, B, M):
    D = mem.shape[1]
    N = w_kv.shape[1]

    def body(m_ref, g_ref, w_ref, dkv_ref, dg_ref):
        mem_t = m_ref[...]
        _, r = _rms(mem_t, g_ref[...])
        dmn = _dot_nt(dkv_ref[...], w_ref[...])
        _acc(dg_ref, jnp.sum(dmn * (mem_t * r), axis=0, keepdims=True), pl.program_id(0) == 0)

    return pl.pallas_call(
        body, grid=(B,),
        in_specs=[pl.BlockSpec((M, D), lambda b: (b, 0)), _const((1, D)), _const((D, N)),
                  pl.BlockSpec((M, N), lambda b: (b, 0))],
        out_specs=pl.BlockSpec((1, D), lambda b: (0, 0)),
        out_shape=_sds((1, D), F32),
        name="kv_bwd", compiler_params=_cp(32))(mem, gm, w_kv, dkv)


def _branch_bwd(c, p, dmerged, prm, tm=TM3, comm=None):
    T, D = c.shape

    def body(c_ref, bu_ref, bv_ref, ga_ref, gb_ref, dm_ref, wco_ref, wso_ref, lag_ref, lab_ref, lsg_ref, lsb_ref,
             wm_ref, bz_ref, bg_ref,
             dc_ref, dpb_ref, dya_ref, dyb_ref, dwm_ref, dbz_ref, dlag_ref, dlab_ref, dlsg_ref, dlsb_ref, dbg_ref,
             z_scr, v_scr, sa_scr, sg_scr, dv_scr):
        first = pl.program_id(0) == 0
        s_a, ln_a, sig_a, xh_a, rstd_a = _branch_a(c_ref[...], lag_ref[...], lab_ref[...])
        sa_scr[...] = s_a.astype(BF)
        y_a = jnp.dot(sa_scr[...], wco_ref[...], preferred_element_type=F32)
        bu, bv = bu_ref[...], bv_ref[...]
        sg, u, tu, z, tv, vh, rstd_v = _branch_b(bu, bv, lsg_ref[...], lsb_ref[...], wm_ref, bz_ref, z_scr, v_scr)
        sg_scr[...] = sg.astype(BF)
        y_b = jnp.dot(sg_scr[...], wso_ref[...], preferred_element_type=F32)
        ga = _sig(ga_ref[...] + bg_ref[pl.ds(0, 1), :])
        gb = _sig(gb_ref[...] + bg_ref[pl.ds(1, 1), :])
        dm = dm_ref[...]
        dga = dm * y_a * ga * (1.0 - ga)
        dgb = dm * y_b * gb * (1.0 - gb)
        dpb_ref[:, pl.ds(2 * D, D)] = dga.astype(BF)
        dpb_ref[:, pl.ds(3 * D, D)] = dgb.astype(BF)
        _acc(dbg_ref.at[pl.ds(0, 1), :], jnp.sum(dga, axis=0, keepdims=True), first)
        _acc(dbg_ref.at[pl.ds(1, 1), :], jnp.sum(dgb, axis=0, keepdims=True), first)
        dya_ref[...] = (dm * ga).astype(BF)
        dyb_ref[...] = (dm * gb).astype(BF)
        dln = _dot_nt(dya_ref[...], wco_ref[...]) * _dsilu(ln_a, sig_a)
        dc, dlag, dlab = _ln_bwd(xh_a, rstd_a, lag_ref[...], dln)
        dc_ref[...] = dc
        _acc(dlag_ref, dlag, first)
        _acc(dlab_ref, dlab, first)
        dsg = _dot_nt(dyb_ref[...], wso_ref[...])
        dpb_ref[:, pl.ds(0, D)] = (dsg * z * _dgelu(bu, tu)).astype(BF)
        dz = dsg * u
        z_scr[...] = dz
        mask = _tril_mask()

        @pl.when(first)
        def _():
            dwm_ref[...] = jnp.zeros_like(dwm_ref)
            dbz_ref[...] = jnp.zeros_like(dbz_ref)

        for gi in range(SGU_GROUPS):
            wm = jnp.where(mask, wm_ref[gi], 0.0).astype(BF)
            cols = pl.ds(gi * LANES, LANES)
            for n in range(tm // LANES):
                rows = pl.ds(n * LANES, LANES)
                dzb = z_scr[rows, cols].astype(BF)
                dv_scr[rows, cols] = lax.dot_general(wm, dzb, (((0,), (0,)), ((), ())), preferred_element_type=F32)
                dw = lax.dot_general(dzb, v_scr[rows, cols], (((1,), (1,)), ((), ())), preferred_element_type=F32)
                dwm_ref[gi] += jnp.where(mask, dw, 0.0)
                dbz_ref[:, cols] += z_scr[rows, cols]
        dgv, dlsg, dlsb = _ln_bwd(vh, rstd_v, lsg_ref[...], dv_scr[...])
        _acc(dlsg_ref, dlsg, first)
        _acc(dlsb_ref, dlsb, first)
        dpb_ref[:, pl.ds(D, D)] = (dgv * _dgelu(bv, tv)).astype(BF)

    tile = lambda j: pl.BlockSpec((tm, D), lambda i: (i, j))
    vec = pl.BlockSpec((1, D), lambda i: (0, 0))
    return _pcall(
        body, (c, p, p, p, p, dmerged, prm["w_co"], prm["w_so"], prm["la_g"], prm["la_b"], prm["ls_g"], prm["ls_b"],
               prm["sgu_w"], prm["bz"], prm["b_gate"]),
        grid=(T // tm,),
        in_specs=[tile(0), tile(2), tile(3), tile(4), tile(5), tile(0), _const((D, D)), _const((D, D)),
                  _const((1, D)), _const((1, D)), _const((1, D)), _const((1, D)),
                  _const((SGU_GROUPS, LANES, LANES)), _const((LANES, D)), _const((2, D))],
        out_specs=[tile(0), pl.BlockSpec((tm, 4 * D), lambda i: (i, 0)), tile(0), tile(0),
                   pl.BlockSpec((SGU_GROUPS, LANES, LANES), lambda i: (0, 0, 0)),
                   pl.BlockSpec((LANES, D), lambda i: (0, 0)), vec, vec, vec, vec,
                   pl.BlockSpec((2, D), lambda i: (0, 0))],
        out_shape=[_sds((T, D), F32), _sds((T, 4 * D), BF), _sds((T, D), BF), _sds((T, D), BF),
                   _sds((SGU_GROUPS, LANES, LANES), F32), _sds((LANES, D), F32),
                   _sds((1, D), F32), _sds((1, D), F32), _sds((1, D), F32), _sds((1, D), F32), _sds((2, D), F32)],
        scratch_shapes=[pltpu.VMEM((tm, D), F32), pltpu.VMEM((tm, D), BF), pltpu.VMEM((tm, D), BF),
                        pltpu.VMEM((tm, D), BF), pltpu.VMEM((tm, D), F32)],
        name="branch_bwd", vmem=56, comm=comm)


def _conv_bwd(p, dc, conv_w, B, S, comm=None):
    K, D = conv_w.shape
    nc = D // LANES

    def body(av_ref, ag_ref, dc_ref, w_ref, dav_ref, dag_ref, dw_ref, db_ref, apad, dpad):
        b = pl.program_id(1)
        av = av_ref[...]
        sg = _sig(ag_ref[...])
        apad[pl.ds(0, CONV_PAD), :] = jnp.zeros((CONV_PAD, LANES), F32)
        apad[pl.ds(CONV_PAD, S), :] = av * sg
        dpad[pl.ds(S, CONV_PAD), :] = jnp.zeros((CONV_PAD, LANES), F32)
        dpad[pl.ds(0, S), :] = dc_ref[...]

        @pl.when(b == 0)
        def _():
            dw_ref[...] = jnp.zeros_like(dw_ref)
            db_ref[...] = jnp.zeros_like(db_ref)

        db_ref[...] += jnp.sum(dc_ref[...], axis=0, keepdims=True)
        for k in range(K):
            tot = jnp.zeros((1, LANES), F32)
            for r0 in range(0, S, CONV_ROWS):
                tot = tot + jnp.sum(dpad[pl.ds(r0, CONV_ROWS), :] * apad[pl.ds(r0 + k + CONV_PAD - (K - 1), CONV_ROWS), :],
                                    axis=0, keepdims=True)
            dw_ref[pl.ds(k, 1), :] += tot
        for r0 in range(0, S, CONV_ROWS):
            da = jnp.zeros((CONV_ROWS, LANES), F32)
            for k in range(K):
                da = da + w_ref[pl.ds(k, 1), :] * dpad[pl.ds(r0 + (K - 1) - k, CONV_ROWS), :]
            rows = pl.ds(r0, CONV_ROWS)
            s = sg[r0:r0 + CONV_ROWS, :]
            a_v = av[r0:r0 + CONV_ROWS, :]
            dav_ref[rows, :] = (da * s).astype(BF)
            dag_ref[rows, :] = (da * a_v * s * (1.0 - s)).astype(BF)

    blk = lambda off: pl.BlockSpec((S, LANES), lambda j, b: (b, off + j))
    return _pcall(
        body, (p, p, dc, conv_w), grid=(nc, B),
        in_specs=[blk(0), blk(nc), blk(0), pl.BlockSpec((K, LANES), lambda j, b: (0, j))],
        out_specs=[blk(0), blk(0), pl.BlockSpec((K, LANES), lambda j, b: (0, j)), pl.BlockSpec((1, LANES), lambda j, b: (0, j))],
        out_shape=[_sds((B * S, D), BF), _sds((B * S, D), BF), _sds((K, D), F32), _sds((1, D), F32)],
        scratch_shapes=[pltpu.VMEM((S + CONV_PAD, LANES), F32), pltpu.VMEM((S + CONV_PAD, LANES), F32)],
        name="conv_bwd", vmem=32, comm=comm)


TM1 = 512


def _in_proj_bwd(x, dx1, dps, g1, w_in, tm=TM1, comm=None):
    T, D = x.shape
    N = w_in.shape[1]
    widths = [d.shape[1] for d in dps]

    def body(x_ref, dx1_ref, *refs):
        dp_refs, (g_ref, w_ref, dx_ref, dg_ref) = refs[:len(dps)], refs[len(dps):]
        x_t = x_ref[...]
        _, r = _rms(x_t, g_ref[...])
        dh = jnp.zeros((tm, D), F32)
        for q, dp_ref in enumerate(dp_refs):
            cols = pl.ds(sum(widths[:q]), widths[q])
            dh = dh + lax.dot_general(dp_ref[...], w_ref[:, cols], (((1,), (1,)), ((), ())), preferred_element_type=F32)
        dxa, dg = _rms_bwd(x_t, g_ref[...], r, dh)
        dx_ref[...] = dx1_ref[...] + dxa
        _acc(dg_ref, dg, pl.program_id(0) == 0)

    tile = pl.BlockSpec((tm, D), lambda i: (i, 0))
    return _pcall(
        body, (x, dx1, *dps, g1, w_in), grid=(T // tm,),
        in_specs=[tile, tile] + [pl.BlockSpec((tm, w), lambda i: (i, 0)) for w in widths] + [_const((1, D)), _const((D, N))],
        out_specs=[tile, pl.BlockSpec((1, D), lambda i: (0, 0))],
        out_shape=[_sds((T, D), F32), _sds((1, D), F32)],
        name="in_proj_bwd", vmem=48, comm=comm)


def _pick(n, cands):
    for c in cands:
        if n % c == 0:
            return c
    raise ValueError(f"no tile of {cands} divides {n}")


def _mm_tn(x, dys, name, comm=None):
    T, K = x.shape
    dys = list(dys) if isinstance(dys, (list, tuple)) else [dys]
    widths = [d.shape[1] for d in dys]
    N = sum(widths)
    tm = _pick(T, (1024, 512, 256))
    tk = _pick(K, (1024, 1408, 512))
    tn = _pick(math.gcd(*widths), (1024, 1408, 512))
    nt = T // tm
    first = [sum(widths[:q]) // tn for q in range(len(dys))]
    count = [w // tn for w in widths]

    def body(x_ref, *refs):
        dy_refs, (o_ref, ob_ref, acc) = refs[:len(dys)], refs[len(dys):]
        j, t = pl.program_id(1), pl.program_id(2)

        @pl.when(t == 0)
        def _():
            acc[...] = jnp.zeros_like(acc)

        for q, dy_ref in enumerate(dy_refs):
            @pl.when(jnp.logical_and(j >= first[q], j < first[q] + count[q]))
            def _():
                acc[...] += _dot_tn(x_ref[...], dy_ref[...])

        @pl.when(t == nt - 1)
        def _():
            o_ref[...] = acc[...]
            ob_ref[...] = acc[...].astype(BF)

    def dy_spec(q):
        def index(i, j, t):
            mine = jnp.logical_and(j >= first[q], j < first[q] + count[q])
            return jnp.where(mine, t, 0), jnp.clip(j - first[q], 0, count[q] - 1)
        return pl.BlockSpec((tm, tn), index)

    res = _pcall(
        body, (x, *dys), grid=(K // tk, N // tn, nt),
        in_specs=[pl.BlockSpec((tm, tk), lambda i, j, t: (t, i))] + [dy_spec(q) for q in range(len(dys))],
        out_specs=[pl.BlockSpec((tk, tn), lambda i, j, t: (i, j)), pl.BlockSpec((tk, tn), lambda i, j, t: (i, j))],
        out_shape=[_sds((K, N), F32), _sds((K, N), BF)],
        scratch_shapes=[pltpu.VMEM((tk, tn), F32)],
        name=name, vmem=48, comm=comm)
    return res[0] if comm is None else res


def _row_tile(R):
    return _pick(R, (128, 64, 32, 16, 8)) if R % 8 == 0 else R


def _sum_landed(full, land, kind, chip, name):
    _, R, C = land.shape
    tr = _row_tile(R)
    nb = R // tr

    def body(k_ref, o_ref, l_ref, s_ref):
        s_ref[...] = ((o_ref[...] + l_ref[0].astype(F32)) + l_ref[1].astype(F32)) + l_ref[2].astype(F32)

    own = (pl.BlockSpec((tr, C), lambda i, k: (k[0] * nb + i, 0)) if kind == "row"
           else pl.BlockSpec((tr, C), lambda i, k: (i, k[0])))
    return pl.pallas_call(
        body,
        grid_spec=pltpu.PrefetchScalarGridSpec(
            num_scalar_prefetch=1, grid=(nb,),
            in_specs=[own, pl.BlockSpec((3, tr, C), lambda i, k: (0, i, 0))],
            out_specs=pl.BlockSpec((tr, C), lambda i, k: (i, 0))),
        out_shape=_sds((R, C), F32), name=name, compiler_params=_cp(32))(chip, full, land)


def _adamw(g, w, m, v):
    m = ADAM_B1 * m + (1.0 - ADAM_B1) * g
    v = ADAM_B2 * v + (1.0 - ADAM_B2) * (g * g)
    m_hat = m / (1.0 - ADAM_B1 ** ADAM_STEP)
    v_hat = v / (1.0 - ADAM_B2 ** ADAM_STEP)
    return -ADAM_LR * (m_hat / (jnp.sqrt(v_hat) + ADAM_EPS) + ADAM_WD * w), m, v


def _update(parts, w, m, v, name):
    R, C = w.shape
    tr = _row_tile(R)
    k = len(parts)

    def body(*refs):
        g = refs[0][...]
        for r in refs[1:k]:
            g = g + r[...]
        w_ref, m_ref, v_ref, g_out, d_out, m_out, v_out = refs[k:]
        d, m_new, v_new = _adamw(g, w_ref[...], m_ref[...], v_ref[...])
        g_out[...] = g
        d_out[...] = d
        m_out[...] = m_new
        v_out[...] = v_new

    blk = pl.BlockSpec((tr, C), lambda i: (i, 0))
    return pl.pallas_call(
        body, grid=(R // tr,), in_specs=[blk] * (k + 3), out_specs=[blk] * 4, out_shape=[_sds((R, C), F32)] * 4,
        name=name, compiler_params=_cp(40))(*parts, w, m, v)


VECS = ("norm_mix", "conv_b", "conv_ln_g", "conv_ln_b", "sgu_ln_g", "sgu_ln_b", "norm_xattn", "norm_mem", "norm_ffn",
        "norm_final")
SMALL = VECS + ("b_gate", "conv_w", "sgu_w", "sgu_b")


def _update_small(land_vec, land_w, land_g1, chip, a):
    _, R, D = land_vec.shape
    Dq, G, K, nv = D // N_CHIPS, SGU_GROUPS, a["conv_w"].shape[1], len(VECS)

    def as2d(nm, arr):
        if nm in VECS:
            return arr.reshape(1, D)
        if nm == "sgu_w":
            return arr.reshape(G * LANES, LANES)
        return arr.reshape(G, LANES) if nm == "sgu_b" else arr[0]

    params = [as2d(nm, a[pre + nm]) for nm in SMALL for pre in ("", "m_", "v_")]

    def body(k_ref, lv, lvc, lw, l1, *refs):
        prm = refs[:3 * len(SMALL)]
        outs = refs[3 * len(SMALL):7 * len(SMALL)]
        tv, tvc, tw, t1 = refs[7 * len(SMALL):]
        for land, tot in ((lv, tv), (lvc, tvc), (lw, tw), (l1, t1)):
            acc = land[0]
            for dev in range(1, N_DEV):
                acc = acc + land[dev]
            tot[...] = acc
        grads = [t1[pl.ds(0, 1), :] if nm == "norm_mix" else tv[pl.ds(i, 1), :] for i, nm in enumerate(VECS)]
        grads += [tvc[pl.ds(nv, 2), :], tvc[pl.ds(nv + 2, K), :], tw[pl.ds(0, G * LANES), :], tw[pl.ds(G * LANES, G), :]]
        for i, g in enumerate(grads):
            d, m_new, v_new = _adamw(g, prm[3 * i][...], prm[3 * i + 1][...], prm[3 * i + 2][...])
            for o_ref, val in zip(outs[4 * i:4 * i + 4], (g, d, m_new, v_new)):
                o_ref[...] = val

    whole = lambda shape: pl.BlockSpec(tuple(shape), lambda i, k: (0,) * len(shape))
    res = pl.pallas_call(
        body,
        grid_spec=pltpu.PrefetchScalarGridSpec(
            num_scalar_prefetch=1, grid=(1,),
            in_specs=[whole(land_vec.shape), pl.BlockSpec((N_DEV, R, Dq), lambda i, k: (0, 0, k[0])),
                      whole(land_w.shape), whole(land_g1.shape)] + [whole(p.shape) for p in params],
            out_specs=[whole(params[3 * i].shape) for i in range(len(SMALL)) for _ in range(4)],
            scratch_shapes=[pltpu.VMEM((R, D), F32), pltpu.VMEM((R, Dq), F32), pltpu.VMEM(land_w.shape[1:], F32),
                            pltpu.VMEM(land_g1.shape[1:], F32)]),
        out_shape=[_sds(params[3 * i].shape, F32) for i in range(len(SMALL)) for _ in range(4)],
        name="upd_small", compiler_params=_cp(40))(chip, land_vec, land_vec, land_w, land_g1, *params)
    return {nm: list(res[4 * i:4 * i + 4]) for i, nm in enumerate(SMALL)}


BIG = ("w_in", "w_conv_out", "w_sgu_out", "w_mix_out", "w_q", "w_kv", "w_xo", "w_gu", "w_down")
BIG_KIND = {"w_in": "col", "w_conv_out": "row", "w_sgu_out": "row", "w_mix_out": "row", "w_q": "row",
            "w_kv": "col", "w_xo": "row", "w_gu": "col", "w_down": "row"}

def _step(a):
    x3d, mem3d, tgt3d = a["x"], a["mem"], a["loss_target"]
    B, S, D = x3d.shape
    M = mem3d.shape[1]
    T = B * S
    x = x3d.reshape(T, D)
    mem = mem3d.reshape(B * M, D)
    tgt = tgt3d.reshape(T, D)
    xi, yi = lax.axis_index("x"), lax.axis_index("y")
    chip = 2 * xi + yi

    def gather(names):
        return _Gather([a[nm][0] if nm in ("b_gate", "conv_w") else a[nm][0].astype(BF) for nm in names],
                       [BIG_KIND.get(nm, "col") for nm in names])

    first = ("w_in", "b_gate", "conv_w")
    on_in_proj = ("w_conv_out", "w_sgu_out", "w_kv", "w_mix_out", "w_q", "w_xo")
    full = dict(zip(first, _comm_call(gather(first), "gather_w_in")))
    (p, h1), got = _in_proj(x, a["norm_mix"], full["w_in"], comm=gather(on_in_proj))
    full.update(zip(on_in_proj, got))
    (c,), got = _conv_fwd(p, full["conv_w"], a["conv_b"], B, S, comm=gather(("w_down",)))
    full["w_down"] = got[0]

    sgu_b = a["sgu_b"][0]
    bz = jnp.repeat(jnp.transpose(sgu_b), LANES, axis=1)
    prm = dict(w_co=full["w_conv_out"], w_so=full["w_sgu_out"], w_mo=full["w_mix_out"], w_q=full["w_q"],
               w_xo=full["w_xo"], w_down=full["w_down"],
               la_g=a["conv_ln_g"], la_b=a["conv_ln_b"], ls_g=a["sgu_ln_g"], ls_b=a["sgu_ln_b"],
               sgu_w=a["sgu_w"][0], bz=bz, b_gate=full["b_gate"], g2=a["norm_xattn"], g3=a["norm_ffn"],
               gf=a["norm_final"].reshape(1, D))

    (merged, s_a, sg), got = _branch_fwd(c, p, prm, comm=gather(("w_gu",)))
    prm["w_gu"] = got[0]
    mem_n, kv = _kv_fwd(mem, a["norm_mem"], full["w_kv"], B, M)
    x1, x2, h2, o = _attn_fwd(x, merged, kv, prm, S, M)
    dx2, dx3, dgu, h3, f, lsum, d_g3, d_gf = _ffn_loss(x2, tgt, prm)
    loss = lax.psum(0.5 * jnp.sum(lsum) / D, ("x", "y", "c"))

    size_of = {nm: a[nm].shape[1] if BIG_KIND[nm] == "row" else a[nm].shape[2] for nm in BIG}
    landed = {}

    def scatter(names):
        return _Scatter([gw[nm][1] for nm in names], [BIG_KIND[nm] for nm in names], [size_of[nm] for nm in names])

    gw = {}
    gw["w_down"] = _mm_tn(f, dx3, "dw_down")
    gw["w_gu"] = _mm_tn(h3, dgu, "dw_gu")
    (dx1, dmerged, dq, dkv, d_g2), got = _attn_bwd(x1, kv, dx2, prm, S, M, comm=scatter(("w_gu",)))
    landed["w_gu"] = got[0]
    gw["w_xo"] = _mm_tn(o, dx2, "dw_xo")
    gw["w_q"] = _mm_tn(h2, dq, "dw_q")
    gw["w_kv"] = _mm_tn(mem_n, dkv, "dw_kv")
    d_gm = _kv_bwd(mem, a["norm_mem"], full["w_kv"], dkv, B, M)
    gw["w_mix_out"] = _mm_tn(merged, dx1, "dw_mix_out")
    group = ("w_down", "w_xo", "w_q", "w_kv", "w_mix_out")
    (dc, dpb, dya, dyb, d_wm, d_bz, d_lag, d_lab, d_lsg, d_lsb, d_bg), got = _branch_bwd(
        c, p, dmerged, prm, comm=scatter(group))
    landed.update(zip(group, got))
    gw["w_conv_out"] = _mm_tn(s_a, dya, "dw_conv_out")
    gw["w_sgu_out"] = _mm_tn(sg, dyb, "dw_sgu_out")
    group = ("w_conv_out", "w_sgu_out")
    (dav, dag, d_cw, d_cb), got = _conv_bwd(p, dc, full["conv_w"], B, S, comm=scatter(group))
    landed.update(zip(group, got))
    dp = [dav, dag, dpb]

    chip_arr = jnp.reshape(chip, (1,)).astype(jnp.int32)
    early = [nm for nm in BIG if nm != "w_in"]
    part = {nm: _sum_landed(gw[nm][0], landed[nm], BIG_KIND[nm], chip_arr, "sum_" + nm) for nm in early}
    G = SGU_GROUPS
    d_sb = jnp.transpose(d_bz.reshape(LANES, G, LANES).sum(axis=-1))
    vec_g = dict(norm_mix=jnp.zeros((1, D), F32), conv_b=d_cb, conv_ln_g=d_lag, conv_ln_b=d_lab, sgu_ln_g=d_lsg,
                 sgu_ln_b=d_lsb, norm_xattn=d_g2, norm_mem=d_gm, norm_ffn=d_g3, norm_final=d_gf)
    rows = [vec_g[nm] for nm in VECS] + [d_bg, d_cw]
    pad = (-sum(r.shape[0] for r in rows)) % 8
    g_vec = jnp.concatenate(rows + [jnp.zeros((pad, D), F32)], axis=0)
    g_w = jnp.concatenate([d_wm.reshape(G * LANES, LANES), d_sb], axis=0)
    gw["w_in"], got = _mm_tn(h1, dp, "dw_in", comm=_Both(_Swap([part[nm] for nm in early]), _Spread([g_vec, g_w])))
    other = dict(zip(early, got[:len(early)]))
    land_vec, land_w = got[len(early):]
    kind_in, size_in = BIG_KIND["w_in"], size_of["w_in"]
    send, recv, g_thru, land_thru, token = _scatter_start(gw["w_in"][1], kind_in, size_in, "scatter_w_in_start")
    (grad_x, d_g1), _ = _in_proj_bwd(x, dx1, dp, a["norm_mix"] + token[:1, :1], full["w_in"])

    out = {}

    def update(nm):
        res = _update([part[nm], other[nm]], a[nm][0], a["m_" + nm][0], a["v_" + nm][0], "upd_" + nm)
        out[nm] = [r[None] for r in res]

    for nm in early:
        update(nm)
    after = [grad_x] + [out[nm][1] for nm in early]
    land_in = _scatter_wait(send, recv, g_thru, land_thru, after, kind_in, size_in, "scatter_w_in_wait")
    part["w_in"] = _sum_landed(gw["w_in"][0], land_in, kind_in, chip_arr, "sum_w_in")
    g1 = jnp.concatenate([d_g1, jnp.zeros((7, D), F32)], axis=0)
    other["w_in"], land_g1 = _comm_call(_Both(_Swap([part["w_in"]]), _Spread([g1])), "swap_w_in")
    update("w_in")
    for nm, res in _update_small(land_vec, land_w, land_g1, chip_arr, a).items():
        out[nm] = [r.reshape(a[nm].shape) for r in res]
    return loss, grad_x.reshape(B, S, D), out


WEIGHTS = ("norm_mix", "w_in", "b_gate", "conv_w", "conv_b", "conv_ln_g", "conv_ln_b", "w_conv_out", "sgu_ln_g",
           "sgu_ln_b", "sgu_w", "sgu_b", "w_sgu_out", "w_mix_out", "norm_xattn", "norm_mem", "w_q", "w_kv", "w_xo",
           "norm_ffn", "w_gu", "w_down", "norm_final")


def kernel(x, mem, norm_mix, w_in, b_gate, conv_w, conv_b, conv_ln_g, conv_ln_b, w_conv_out, sgu_ln_g, sgu_ln_b, sgu_w, sgu_b, w_sgu_out, w_mix_out, norm_xattn, norm_mem, w_q, w_kv, w_xo, norm_ffn, w_gu, w_down, norm_final, loss_target, m_norm_mix, m_w_in, m_b_gate, m_conv_w, m_conv_b, m_conv_ln_g, m_conv_ln_b, m_w_conv_out, m_sgu_ln_g, m_sgu_ln_b, m_sgu_w, m_sgu_b, m_w_sgu_out, m_w_mix_out, m_norm_xattn, m_norm_mem, m_w_q, m_w_kv, m_w_xo, m_norm_ffn, m_w_gu, m_w_down, m_norm_final, v_norm_mix, v_w_in, v_b_gate, v_conv_w, v_conv_b, v_conv_ln_g, v_conv_ln_b, v_w_conv_out, v_sgu_ln_g, v_sgu_ln_b, v_sgu_w, v_sgu_b, v_w_sgu_out, v_w_mix_out, v_norm_xattn, v_norm_mem, v_w_q, v_w_kv, v_w_xo, v_norm_ffn, v_w_gu, v_w_down, v_norm_final):
    a = dict(locals())
    loss, grad_x, out = _step(a)
    res = [loss, grad_x]
    for q in range(4):
        res += [out[nm][q] for nm in WEIGHTS]
    return tuple(res)
```

```python
import functools
import math

import jax
import jax.numpy as jnp
from jax import lax
from jax.experimental import pallas as pl
from jax.experimental.pallas import tpu as pltpu

BF = jnp.bfloat16
F32 = jnp.float32
MESH = pl.DeviceIdType.MESH
ANY = pl.BlockSpec(memory_space=pl.ANY)

RMS_EPS = 1e-6
LN_EPS = 1e-5
HEADS = 4
SGU_GROUPS = 8
LANES = 128
ADAM_LR = 0.001
ADAM_B1 = 0.9
ADAM_B2 = 0.999
ADAM_EPS = 1e-08
ADAM_WD = 0.01
ADAM_STEP = 10
N_CHIPS = 4
N_DEV = 8
MIB = 1024 * 1024


def _sds(shape, dtype):
    return jax.ShapeDtypeStruct(tuple(shape), dtype)


def _cp(vmem_mib):
    return pltpu.CompilerParams(vmem_limit_bytes=vmem_mib * MIB)


def _const(shape):
    nd = len(shape)
    return pl.BlockSpec(tuple(shape), lambda *_: (0,) * nd, pipeline_mode=pl.Buffered(1))


def _dot(a, b):
    return jnp.dot(a.astype(BF), b.astype(BF), preferred_element_type=F32)


def _dot_nt(a, b):
    return lax.dot_general(a.astype(BF), b.astype(BF), (((1,), (1,)), ((), ())), preferred_element_type=F32)


def _dot_tn(a, b):
    return lax.dot_general(a.astype(BF), b.astype(BF), (((0,), (0,)), ((), ())), preferred_element_type=F32)


def _sig(x):
    return 1.0 / (1.0 + jnp.exp(-x))


def _dsilu(x, s):
    return s * (1.0 + x * (1.0 - s))


_GELU_C = math.sqrt(2.0 / math.pi)


def _gelu(x):
    x2 = x * x
    t = jnp.tanh((_GELU_C * x) * (1.0 + 0.044715 * x2))
    cdf = 0.5 * (1.0 + t)
    return x * cdf, (t, x2, cdf)


def _dgelu(x, shared):
    t, x2, cdf = shared
    return cdf + ((0.5 * _GELU_C) * x) * (1.0 - t * t) * (1.0 + (3.0 * 0.044715) * x2)


def _rms(x, g):
    r = lax.rsqrt(jnp.mean(x * x, axis=-1, keepdims=True) + RMS_EPS)
    return x * r * g, r


def _rms_bwd(x, g, r, dh):
    xr = x * r
    dxh = dh * g
    dx = r * (dxh - xr * jnp.mean(dxh * xr, axis=-1, keepdims=True))
    return dx, jnp.sum(dh * xr, axis=0, keepdims=True)


def _ln(x, g, b):
    mu = jnp.mean(x, axis=-1, keepdims=True)
    xc = x - mu
    rstd = lax.rsqrt(jnp.mean(xc * xc, axis=-1, keepdims=True) + LN_EPS)
    xh = xc * rstd
    return xh * g + b, xh, rstd


def _ln_bwd(xh, rstd, g, dy):
    dxh = dy * g
    dx = rstd * (dxh - jnp.mean(dxh, axis=-1, keepdims=True) - xh * jnp.mean(dxh * xh, axis=-1, keepdims=True))
    return dx, jnp.sum(dy * xh, axis=0, keepdims=True), jnp.sum(dy, axis=0, keepdims=True)


def _acc(ref, val, first):
    @pl.when(first)
    def _():
        ref[...] = val

    @pl.when(jnp.logical_not(first))
    def _():
        ref[...] += val


def _place():
    x, y, c = lax.axis_index("x"), lax.axis_index("y"), lax.axis_index("c")
    chips = [(1 - x, y), (x, 1 - y), (1 - x, 1 - y)]
    return x, y, c, chips


def _shard_of(ref, kind, k, n):
    if kind == "row":
        return ref.at[pl.ds(k * n, n), :]
    return ref.at[:, pl.ds(k * n, n)]


class _Gather:
    def __init__(self, shards, kinds):
        n = len(shards)
        self.srcs, self.kinds = list(shards), list(kinds)
        self.sizes = [s.shape[0] if kd == "row" else s.shape[1] for s, kd in zip(shards, kinds)]
        self.halves = [s.shape[0] // 2 if s.shape[0] % 32 == 0 else None for s in shards]
        self.out_shape = [
            _sds((s.shape[0] * N_CHIPS, s.shape[1]) if kd == "row" else (s.shape[0], s.shape[1] * N_CHIPS), s.dtype)
            for s, kd in zip(shards, kinds)]
        dma = pltpu.SemaphoreType.DMA
        self.sems = [dma((3 * n,)), dma((3 * n,)), dma((n,)), dma((3 * n,)), dma((3 * n,))]

    def _part(self, ref, t, core):
        h = self.halves[t]
        return ref if h is None else ref.at[pl.ds(core * h, h), :]

    def _copies(self, ins, outs, send, recv, loc, fsend, frecv):
        x, y, c, chips = _place()
        k = 2 * x + y
        local, remote = [], []
        for t in range(len(ins)):
            block = lambda q: _shard_of(outs[t], self.kinds[t], q, self.sizes[t])
            local.append(pltpu.make_async_copy(ins[t], block(k), loc.at[t]))
            for j, (px, py) in enumerate(chips):
                sems = dict(send_sem=send.at[3 * t + j], recv_sem=recv.at[3 * t + j])
                there = dict(device_id=(px, py, c), device_id_type=MESH)
                sent = pltpu.make_async_remote_copy(
                    src_ref=self._part(ins[t], t, c), dst_ref=self._part(block(k), t, c), **sems, **there)
                got = self._part(block(2 * px + py), t, c)
                landed = pltpu.make_async_remote_copy(src_ref=self._part(ins[t], t, c), dst_ref=got, **sems, **there)
                passed = handed = None
                if self.halves[t] is not None:
                    fsems = dict(send_sem=fsend.at[3 * t + j], recv_sem=frecv.at[3 * t + j])
                    sibling = dict(device_id=(x, y, 1 - c), device_id_type=MESH)
                    passed = pltpu.make_async_remote_copy(src_ref=got, dst_ref=got, **fsems, **sibling)
                    other = self._part(block(2 * px + py), t, 1 - c)
                    handed = pltpu.make_async_remote_copy(src_ref=got, dst_ref=other, **fsems, **sibling)
                remote.append((sent, landed, passed, handed))
        return local, remote

    def start(self, ins, outs, *sems):
        local, remote = self._copies(ins, outs, *sems)
        for cp in local:
            cp.start()
        for sent, _, _, _ in remote:
            sent.start()

    def wait(self, ins, outs, *sems):
        local, remote = self._copies(ins, outs, *sems)
        for sent, landed, passed, handed in remote:
            landed.wait_recv()
            if passed is not None:
                passed.start()
        for sent, landed, passed, handed in remote:
            if passed is not None:
                handed.wait_recv()
                passed.wait_send()
            sent.wait_send()
        for cp in local:
            cp.wait()


class _Scatter:
    def __init__(self, grads, kinds, sizes):
        n = len(grads)
        self.srcs, self.kinds, self.sizes = list(grads), list(kinds), list(sizes)
        self.out_shape = [_sds((3,) + ((sz, g.shape[1]) if kd == "row" else (g.shape[0], sz)), g.dtype)
                          for g, kd, sz in zip(grads, kinds, sizes)]
        self.sems = [pltpu.SemaphoreType.DMA((3 * n,)), pltpu.SemaphoreType.DMA((3 * n,))]

    def _copies(self, ins, outs, send, recv):
        x, y, c, chips = _place()
        return [pltpu.make_async_remote_copy(
            src_ref=_shard_of(ins[t], self.kinds[t], 2 * px + py, self.sizes[t]), dst_ref=outs[t].at[j],
            send_sem=send.at[3 * t + j], recv_sem=recv.at[3 * t + j], device_id=(px, py, c), device_id_type=MESH)
            for t in range(len(ins)) for j, (px, py) in enumerate(chips)]

    def start(self, ins, outs, send, recv):
        for cp in self._copies(ins, outs, send, recv):
            cp.start()

    def wait(self, ins, outs, send, recv):
        for cp in self._copies(ins, outs, send, recv):
            cp.wait_recv()
            cp.wait_send()


class _Swap:
    def __init__(self, parts):
        n = len(parts)
        self.srcs = list(parts)
        self.out_shape = [_sds(p.shape, p.dtype) for p in parts]
        self.sems = [pltpu.SemaphoreType.DMA((n,)), pltpu.SemaphoreType.DMA((n,))]

    def _copies(self, ins, outs, send, recv):
        x, y, c, _ = _place()
        return [pltpu.make_async_remote_copy(
            src_ref=ins[t], dst_ref=outs[t], send_sem=send.at[t], recv_sem=recv.at[t],
            device_id=(x, y, 1 - c), device_id_type=MESH) for t in range(len(ins))]

    def start(self, ins, outs, send, recv):
        for cp in self._copies(ins, outs, send, recv):
            cp.start()

    def wait(self, ins, outs, send, recv):
        for cp in self._copies(ins, outs, send, recv):
            cp.wait_recv()
            cp.wait_send()


class _Spread:
    def __init__(self, packs):
        n = len(packs)
        self.srcs = list(packs)
        self.out_shape = [_sds((N_DEV,) + p.shape, p.dtype) for p in packs]
        dma = pltpu.SemaphoreType.DMA
        self.sems = [dma((N_DEV * n,)), dma((N_DEV * n,)), dma((n,))]

    def _copies(self, ins, outs, send, recv, loc):
        x, y, c, _ = _place()
        me = 4 * x + 2 * y + c
        local = [pltpu.make_async_copy(ins[t], outs[t].at[me], loc.at[t]) for t in range(len(ins))]
        remote = []
        for t in range(len(ins)):
            for mask in range(1, N_DEV):
                peer = ((1 - x) if mask & 4 else x, (1 - y) if mask & 2 else y, (1 - c) if mask & 1 else c)
                src = peer[0] * 4 + peer[1] * 2 + peer[2]
                sems = dict(send_sem=send.at[N_DEV * t + mask], recv_sem=recv.at[N_DEV * t + mask])
                sent = pltpu.make_async_remote_copy(
                    src_ref=ins[t], dst_ref=outs[t].at[me], device_id=peer, device_id_type=MESH, **sems)
                landed = pltpu.make_async_remote_copy(
                    src_ref=ins[t], dst_ref=outs[t].at[src], device_id=peer, device_id_type=MESH, **sems)
                remote.append((sent, landed))
        return local, remote

    def start(self, ins, outs, send, recv, loc):
        local, remote = self._copies(ins, outs, send, recv, loc)
        for cp in local:
            cp.start()
        for sent, _ in remote:
            sent.start()

    def wait(self, ins, outs, send, recv, loc):
        local, remote = self._copies(ins, outs, send, recv, loc)
        for sent, landed in remote:
            landed.wait_recv()
            sent.wait_send()
        for cp in local:
            cp.wait()


class _Both:
    def __init__(self, *comms):
        self.comms = comms
        self.srcs = [s for cm in comms for s in cm.srcs]
        self.out_shape = [s for cm in comms for s in cm.out_shape]
        self.sems = [s for cm in comms for s in cm.sems]

    def _each(self, ins, outs, sems):
        i = o = k = 0
        for cm in self.comms:
            ni, no, nk = len(cm.srcs), len(cm.out_shape), len(cm.sems)
            yield cm, ins[i:i + ni], outs[o:o + no], sems[k:k + nk]
            i, o, k = i + ni, o + no, k + nk

    def start(self, ins, outs, *sems):
        for cm, i, o, s in self._each(ins, outs, sems):
            cm.start(i, o, *s)

    def wait(self, ins, outs, *sems):
        for cm, i, o, s in self._each(ins, outs, sems):
            cm.wait(i, o, *s)


HBM = pl.BlockSpec(memory_space=pltpu.HBM)
SEM = pl.BlockSpec(memory_space=pltpu.SEMAPHORE)
EFFECT = pltpu.SideEffectType.DATAFLOW_SIDE_EFFECTING


def _scatter_copies(g_ref, land_ref, send, recv, kind, size):
    x, y, c, chips = _place()
    return [pltpu.make_async_remote_copy(
        src_ref=_shard_of(g_ref, kind, 2 * px + py, size), dst_ref=land_ref.at[j],
        send_sem=send.at[j], recv_sem=recv.at[j], device_id=(px, py, c), device_id_type=MESH)
        for j, (px, py) in enumerate(chips)]


def _scatter_start(grad, kind, size, name):
    land = (3,) + ((size, grad.shape[1]) if kind == "row" else (grad.shape[0], size))

    def body(g_ref, land_ref, send, recv, g_thru, land_thru, token):
        for cp in _scatter_copies(g_ref, land_ref, send, recv, kind, size):
            cp.start()
        token[...] = jnp.zeros_like(token)

    return pl.pallas_call(
        body, name=name,
        out_shape=(pltpu.SemaphoreType.DMA((3,)), pltpu.SemaphoreType.DMA((3,)), pltpu.HBM(grad.shape, grad.dtype),
                   pltpu.HBM(land, grad.dtype), _sds((8, LANES), F32)),
        in_specs=(HBM, HBM), out_specs=(SEM, SEM, HBM, HBM, pl.BlockSpec(memory_space=pltpu.VMEM)),
        input_output_aliases={0: 2, 1: 3},
        compiler_params=pltpu.CompilerParams(has_side_effects=EFFECT))(
            pltpu.with_memory_space_constraint(grad, pltpu.HBM),
            pltpu.with_memory_space_constraint(lax.empty(land, grad.dtype), pltpu.HBM))


def _scatter_wait(send, recv, g_thru, land_thru, after, kind, size, name):
    def body(g_ref, land_ref, send, recv, *rest):
        for cp in _scatter_copies(g_ref, land_ref, send, recv, kind, size):
            cp.wait_send()
            cp.wait_recv()

    return pl.pallas_call(
        body, name=name,
        out_shape=(pltpu.HBM(g_thru.shape, g_thru.dtype), pltpu.HBM(land_thru.shape, land_thru.dtype)),
        in_specs=(HBM, HBM, SEM, SEM) + (ANY,) * len(after), out_specs=(HBM, HBM),
        input_output_aliases={0: 0, 1: 1},
        compiler_params=pltpu.CompilerParams(has_side_effects=EFFECT))(g_thru, land_thru, send, recv, *after)[1]


def _comm_call(comm, name):
    n, m = len(comm.srcs), len(comm.out_shape)

    def body(*refs):
        comm.start(refs[:n], refs[n:n + m], *refs[n + m:])
        comm.wait(refs[:n], refs[n:n + m], *refs[n + m:])

    return pl.pallas_call(body, in_specs=[ANY] * n, out_specs=[ANY] * m, out_shape=comm.out_shape,
                          scratch_shapes=comm.sems, name=name)(*comm.srcs)


def _pcall(body, args, *, grid, in_specs, out_specs, out_shape, name, vmem, scratch_shapes=(), comm=None):
    in_specs, out_specs, out_shape = list(in_specs), list(out_specs), list(out_shape)
    scratch_shapes = list(scratch_shapes)
    if comm is None:
        res = pl.pallas_call(body, grid=grid, in_specs=in_specs, out_specs=out_specs, out_shape=out_shape,
                             scratch_shapes=scratch_shapes, name=name, compiler_params=_cp(vmem))(*args)
        return list(res), []
    ni, no, ns = len(in_specs), len(out_specs), len(scratch_shapes)
    ci, co = len(comm.srcs), len(comm.out_shape)

    def carried(*refs):
        c_in = refs[ni:ni + ci]
        c_out = refs[ni + ci + no:ni + ci + no + co]
        sems = refs[ni + ci + no + co + ns:]
        ids = [pl.program_id(d) for d in range(len(grid))]
        first = functools.reduce(jnp.logical_and, [i == 0 for i in ids])
        last = functools.reduce(jnp.logical_and, [i == g - 1 for i, g in zip(ids, grid)])

        @pl.when(first)
        def _():
            comm.start(c_in, c_out, *sems)

        body(*refs[:ni], *refs[ni + ci:ni + ci + no], *refs[ni + ci + no + co:ni + ci + no + co + ns])

        @pl.when(last)
        def _():
            comm.wait(c_in, c_out, *sems)

    res = pl.pallas_call(carried, grid=grid, in_specs=in_specs + [ANY] * ci, out_specs=out_specs + [ANY] * co,
                         out_shape=out_shape + list(comm.out_shape), scratch_shapes=scratch_shapes + list(comm.sems),
                         name=name, compiler_params=_cp(vmem))(*args, *comm.srcs)
    return list(res[:no]), list(res[no:])


def _in_proj(x, g1, w_in, comm=None):
    T, D = x.shape
    N = w_in.shape[1]
    tm, tn = 512, 1024

    def body(x_ref, g_ref, w_ref, p_ref, h_ref):
        h, _ = _rms(x_ref[...], g_ref[...])
        h_ref[...] = h.astype(BF)
        for j in range(N // tn):
            cols = pl.ds(j * tn, tn)
            p_ref[:, cols] = jnp.dot(h_ref[...], w_ref[:, cols], preferred_element_type=F32)

    return _pcall(
        body, (x, g1, w_in), grid=(T // tm,),
        in_specs=[pl.BlockSpec((tm, D), lambda i: (i, 0)), _const((1, D)), _const((D, N))],
        out_specs=[pl.BlockSpec((tm, N), lambda i: (i, 0)), pl.BlockSpec((tm, D), lambda i: (i, 0))],
        out_shape=[_sds((T, N), F32), _sds((T, D), BF)],
        name="in_proj", vmem=56, comm=comm)


CONV_PAD = 32
CONV_ROWS = 256


def _conv_fwd(p, conv_w, conv_b, B, S, comm=None):
    K, D = conv_w.shape
    nc = D // LANES

    def body(av_ref, ag_ref, w_ref, b_ref, c_ref, apad):
        apad[pl.ds(0, CONV_PAD), :] = jnp.zeros((CONV_PAD, LANES), F32)
        apad[pl.ds(CONV_PAD, S), :] = av_ref[...] * _sig(ag_ref[...])
        for r0 in range(0, S, CONV_ROWS):
            acc = jnp.zeros((CONV_ROWS, LANES), F32) + b_ref[...]
            for k in range(K):
                acc = acc + w_ref[pl.ds(k, 1), :] * apad[pl.ds(r0 + k + CONV_PAD - (K - 1), CONV_ROWS), :]
            c_ref[pl.ds(r0, CONV_ROWS), :] = acc

    return _pcall(
        body, (p, p, conv_w, conv_b), grid=(B, nc),
        in_specs=[pl.BlockSpec((S, LANES), lambda b, j: (b, j)), pl.BlockSpec((S, LANES), lambda b, j: (b, nc + j)),
                  pl.BlockSpec((K, LANES), lambda b, j: (0, j)), pl.BlockSpec((1, LANES), lambda b, j: (0, j))],
        out_specs=[pl.BlockSpec((S, LANES), lambda b, j: (b, j))],
        out_shape=[_sds((B * S, D), F32)],
        scratch_shapes=[pltpu.VMEM((S + CONV_PAD, LANES), F32)],
        name="conv_fwd", vmem=32, comm=comm)


def _tril_mask():
    t = lax.broadcasted_iota(jnp.int32, (LANES, LANES), 0)
    s = lax.broadcasted_iota(jnp.int32, (LANES, LANES), 1)
    return t >= s


def _branch_a(c, g, b):
    ln_a, xh, rstd = _ln(c, g, b)
    s = _sig(ln_a)
    return ln_a * s, ln_a, s, xh, rstd


def _branch_b(bu, bv, g, b, wm_ref, bz_ref, z_scr, v_scr):
    tm, D = bu.shape
    u, tu = _gelu(bu)
    gv, tv = _gelu(bv)
    v, vh, rstd = _ln(gv, g, b)
    v_scr[...] = v.astype(BF)
    mask = _tril_mask()
    for gi in range(SGU_GROUPS):
        wm = jnp.where(mask, wm_ref[gi], 0.0).astype(BF)
        cols = pl.ds(gi * LANES, LANES)
        for n in range(tm // LANES):
            rows = pl.ds(n * LANES, LANES)
            z_scr[rows, cols] = jnp.dot(wm, v_scr[rows, cols], preferred_element_type=F32) + bz_ref[:, cols]
    z = z_scr[...]
    return u * z, u, tu, z, tv, vh, rstd


TM3 = 256
TM3_FWD = 512


def _branch_fwd(c, p, prm, tm=TM3_FWD, comm=None):
    T, D = c.shape

    def body(c_ref, bu_ref, bv_ref, ga_ref, gb_ref, wco_ref, wso_ref, lag_ref, lab_ref, lsg_ref, lsb_ref, wm_ref,
             bz_ref, bg_ref, mg_ref, sa_ref, sg_ref, z_scr, v_scr):
        s_a = _branch_a(c_ref[...], lag_ref[...], lab_ref[...])[0]
        sa_ref[...] = s_a.astype(BF)
        y_a = jnp.dot(sa_ref[...], wco_ref[...], preferred_element_type=F32)
        sg = _branch_b(bu_ref[...], bv_ref[...], lsg_ref[...], lsb_ref[...], wm_ref, bz_ref, z_scr, v_scr)[0]
        sg_ref[...] = sg.astype(BF)
        y_b = jnp.dot(sg_ref[...], wso_ref[...], preferred_element_type=F32)
        ga = _sig(ga_ref[...] + bg_ref[pl.ds(0, 1), :])
        gb = _sig(gb_ref[...] + bg_ref[pl.ds(1, 1), :])
        mg_ref[...] = (ga * y_a + gb * y_b).astype(BF)

    tile = lambda j: pl.BlockSpec((tm, D), lambda i: (i, j))
    return _pcall(
        body, (c, p, p, p, p, prm["w_co"], prm["w_so"], prm["la_g"], prm["la_b"], prm["ls_g"], prm["ls_b"],
               prm["sgu_w"], prm["bz"], prm["b_gate"]),
        grid=(T // tm,),
        in_specs=[tile(0), tile(2), tile(3), tile(4), tile(5), _const((D, D)), _const((D, D)),
                  _const((1, D)), _const((1, D)), _const((1, D)), _const((1, D)),
                  _const((SGU_GROUPS, LANES, LANES)), _const((LANES, D)), _const((2, D))],
        out_specs=[tile(0), tile(0), tile(0)],
        out_shape=[_sds((T, D), BF)] * 3,
        scratch_shapes=[pltpu.VMEM((tm, D), F32), pltpu.VMEM((tm, D), BF)],
        name="branch_fwd", vmem=48, comm=comm)


def _kv_fwd(mem, gm, w_kv, B, M):
    D = mem.shape[1]
    N = w_kv.shape[1]

    def body(m_ref, g_ref, w_ref, mn_ref, kv_ref):
        h, _ = _rms(m_ref[...], g_ref[...])
        mn_ref[...] = h.astype(BF)
        kv_ref[...] = jnp.dot(mn_ref[...], w_ref[...], preferred_element_type=F32).astype(BF)

    return pl.pallas_call(
        body, grid=(B,),
        in_specs=[pl.BlockSpec((M, D), lambda b: (b, 0)), _const((1, D)), _const((D, N))],
        out_specs=[pl.BlockSpec((M, D), lambda b: (b, 0)), pl.BlockSpec((M, N), lambda b: (b, 0))],
        out_shape=[_sds((B * M, D), BF), _sds((B * M, N), BF)],
        name="kv_fwd", compiler_params=_cp(32))(mem, gm, w_kv)


def _softmax_rows(s):
    e = jnp.exp(s - jnp.max(s, axis=-1, keepdims=True))
    return e * (1.0 / jnp.sum(e, axis=-1, keepdims=True))


TM4 = 512


def _attn_fwd(x, merged, kv, prm, S, M, tm=TM4):
    T, D = x.shape
    hd = D // HEADS
    scale = hd ** -0.5
    tpb = S // tm

    def body(x_ref, mg_ref, kv_ref, wmo_ref, wq_ref, wxo_ref, g_ref, x1_ref, x2_ref, h2_ref, o_ref):
        x1 = x_ref[...] + jnp.dot(mg_ref[...], wmo_ref[...], preferred_element_type=F32)
        x1_ref[...] = x1
        h2, _ = _rms(x1, g_ref[...])
        h2_ref[...] = h2.astype(BF)
        qb = jnp.dot(h2_ref[...], wq_ref[...], preferred_element_type=F32).astype(BF)
        for h in range(HEADS):
            cs = pl.ds(h * hd, hd)
            s = _dot_nt(qb[:, h * hd:(h + 1) * hd], kv_ref[:, cs]) * scale
            pr = _softmax_rows(s)
            o_ref[:, cs] = _dot(pr, kv_ref[:, pl.ds(D + h * hd, hd)]).astype(BF)
        x2_ref[...] = x1 + jnp.dot(o_ref[...], wxo_ref[...], preferred_element_type=F32)

    tile = pl.BlockSpec((tm, D), lambda i: (i, 0))
    return pl.pallas_call(
        body, grid=(T // tm,),
        in_specs=[tile, tile, pl.BlockSpec((M, 2 * D), lambda i: (i // tpb, 0)),
                  _const((D, D)), _const((D, D)), _const((D, D)), _const((1, D))],
        out_specs=[tile, tile, tile, tile],
        out_shape=[_sds((T, D), F32), _sds((T, D), F32), _sds((T, D), BF), _sds((T, D), BF)],
        name="attn_fwd", compiler_params=_cp(40))(x, merged, kv, prm["w_mo"], prm["w_q"], prm["w_xo"], prm["g2"])


TM5 = 256
FFN_CHUNKS = 1


def _ffn_loss(x2, tgt, prm, tm=TM5):
    T, D = x2.shape
    F = prm["w_down"].shape[0]
    FC = F // FFN_CHUNKS

    def body(x2_ref, t_ref, wgu_ref, wd_ref, g3_ref, gf_ref, dx2_ref, dx3_ref, dgu_ref, h3_ref, f_ref, ls_ref,
             dg3_ref, dgf_ref, gu_scr):
        first = pl.program_id(0) == 0
        x2 = x2_ref[...]
        h3, r3 = _rms(x2, g3_ref[...])
        h3_ref[...] = h3.astype(BF)
        x3 = x2
        for ch in range(FFN_CHUNKS):
            gc, uc = pl.ds(ch * FC, FC), pl.ds(F + ch * FC, FC)
            gt = jnp.dot(h3_ref[...], wgu_ref[:, gc], preferred_element_type=F32)
            up = jnp.dot(h3_ref[...], wgu_ref[:, uc], preferred_element_type=F32)
            gu_scr[:, gc] = gt
            gu_scr[:, uc] = up
            f_ref[:, gc] = (gt * _sig(gt) * up).astype(BF)
            x3 = x3 + jnp.dot(f_ref[:, gc], wd_ref[gc, :], preferred_element_type=F32)
        y, rf = _rms(x3, gf_ref[...])
        e = y - t_ref[...]
        _acc(ls_ref, jnp.sum(e * e, axis=0, keepdims=True), first)
        dx3, dgf = _rms_bwd(x3, gf_ref[...], rf, e * (1.0 / D))
        _acc(dgf_ref, dgf, first)
        dx3_ref[...] = dx3.astype(BF)
        dh3 = jnp.zeros((tm, D), F32)
        for ch in range(FFN_CHUNKS):
            gc, uc = pl.ds(ch * FC, FC), pl.ds(F + ch * FC, FC)
            df = lax.dot_general(dx3_ref[...], wd_ref[gc, :], (((1,), (1,)), ((), ())), preferred_element_type=F32)
            gt, up = gu_scr[:, gc], gu_scr[:, uc]
            s = _sig(gt)
            dgu_ref[:, gc] = (df * up * _dsilu(gt, s)).astype(BF)
            dgu_ref[:, uc] = (df * gt * s).astype(BF)
            dh3 = dh3 + lax.dot_general(dgu_ref[:, gc], wgu_ref[:, gc], (((1,), (1,)), ((), ())), preferred_element_type=F32)
            dh3 = dh3 + lax.dot_general(dgu_ref[:, uc], wgu_ref[:, uc], (((1,), (1,)), ((), ())), preferred_element_type=F32)
        dxa, dg3 = _rms_bwd(x2, g3_ref[...], r3, dh3)
        _acc(dg3_ref, dg3, first)
        dx2_ref[...] = dx3 + dxa

    tile = lambda n: pl.BlockSpec((tm, n), lambda i: (i, 0))
    vec = pl.BlockSpec((1, D), lambda i: (0, 0))
    return pl.pallas_call(
        body, grid=(T // tm,),
        in_specs=[tile(D), tile(D), _const((D, 2 * F)), _const((F, D)), _const((1, D)), _const((1, D))],
        out_specs=[tile(D), tile(D), tile(2 * F), tile(D), tile(F), vec, vec, vec],
        out_shape=[_sds((T, D), F32), _sds((T, D), BF), _sds((T, 2 * F), BF), _sds((T, D), BF), _sds((T, F), BF),
                   _sds((1, D), F32), _sds((1, D), F32), _sds((1, D), F32)],
        scratch_shapes=[pltpu.VMEM((tm, 2 * F), F32)],
        name="ffn_loss", compiler_params=_cp(56))(x2, tgt, prm["w_gu"], prm["w_down"], prm["g3"], prm["gf"])


def _attn_bwd(x1, kv, dx2, prm, S, M, tm=TM4, comm=None):
    T, D = x1.shape
    hd = D // HEADS
    scale = hd ** -0.5
    tpb = S // tm

    def body(x1_ref, kv_ref, dx2_ref, wmo_ref, wq_ref, wxo_ref, g_ref, dx1_ref, dmg_ref, dq_ref, dkv_ref, dg_ref,
             h2_scr, do_scr):
        i = pl.program_id(0)
        x1 = x1_ref[...]
        dx2 = dx2_ref[...]
        h2, r2 = _rms(x1, g_ref[...])
        h2_scr[...] = h2.astype(BF)
        qb = jnp.dot(h2_scr[...], wq_ref[...], preferred_element_type=F32).astype(BF)
        do_scr[...] = _dot_nt(dx2, wxo_ref[...]).astype(BF)
        for h in range(HEADS):
            cs, vs = pl.ds(h * hd, hd), pl.ds(D + h * hd, hd)
            qh = qb[:, h * hd:(h + 1) * hd]
            pr = _softmax_rows(_dot_nt(qh, kv_ref[:, cs]) * scale)
            dpr = _dot_nt(do_scr[:, cs], kv_ref[:, vs])
            dv = _dot_tn(pr, do_scr[:, cs])
            ds = (pr * (dpr - jnp.sum(dpr * pr, axis=-1, keepdims=True)) * scale).astype(BF)
            dq_ref[:, cs] = jnp.dot(ds, kv_ref[:, cs], preferred_element_type=F32).astype(BF)
            dk = _dot_tn(ds, qh)

            @pl.when(i % tpb == 0)
            def _():
                dkv_ref[:, cs] = dk
                dkv_ref[:, vs] = dv

            @pl.when(i % tpb != 0)
            def _():
                dkv_ref[:, cs] += dk
                dkv_ref[:, vs] += dv

        dh2 = _dot_nt(dq_ref[...], wq_ref[...])
        dxa, dg = _rms_bwd(x1, g_ref[...], r2, dh2)
        _acc(dg_ref, dg, i == 0)
        dx1 = dx2 + dxa
        dx1_ref[...] = dx1
        dmg_ref[...] = _dot_nt(dx1, wmo_ref[...])

    tile = pl.BlockSpec((tm, D), lambda i: (i, 0))
    kvb = pl.BlockSpec((M, 2 * D), lambda i: (i // tpb, 0))
    B = T // S
    return _pcall(
        body, (x1, kv, dx2, prm["w_mo"], prm["w_q"], prm["w_xo"], prm["g2"]), grid=(T // tm,),
        in_specs=[tile, kvb, tile, _const((D, D)), _const((D, D)), _const((D, D)), _const((1, D))],
        out_specs=[tile, tile, tile, kvb, pl.BlockSpec((1, D), lambda i: (0, 0))],
        out_shape=[_sds((T, D), F32), _sds((T, D), F32), _sds((T, D), BF), _sds((B * M, 2 * D), F32), _sds((1, D), F32)],
        scratch_shapes=[pltpu.VMEM((tm, D), BF), pltpu.VMEM((tm, D), BF)],
        name="attn_bwd", vmem=48, comm=comm)


def _kv_bwd(mem, gm, w_kv, dkv, B, M):
    D = mem.shape[1]
    N = w_kv.shape[1]

    def body(m_ref, g_ref, w_ref, dkv_ref, dg_ref):
        mem_t = m_ref[...]
        _, r = _rms(mem_t, g_ref[...])
        dmn = _dot_nt(dkv_ref[...], w_ref[...])
        _acc(dg_ref, jnp.sum(dmn * (mem_t * r), axis=0, keepdims=True), pl.program_id(0) == 0)

    return pl.pallas_call(
        body, grid=(B,),
        in_specs=[pl.BlockSpec((M, D), lambda b: (b, 0)), _const((1, D)), _const((D, N)),
                  pl.BlockSpec((M, N), lambda b: (b, 0))],
        out_specs=pl.BlockSpec((1, D), lambda b: (0, 0)),
        out_shape=_sds((1, D), F32),
        name="kv_bwd", compiler_params=_cp(32))(mem, gm, w_kv, dkv)


def _branch_bwd(c, p, dmerged, prm, tm=TM3, comm=None):
    T, D = c.shape

    def body(c_ref, bu_ref, bv_ref, ga_ref, gb_ref, dm_ref, wco_ref, wso_ref, lag_ref, lab_ref, lsg_ref, lsb_ref,
             wm_ref, bz_ref, bg_ref,
             dc_ref, dpb_ref, dya_ref, dyb_ref, dwm_ref, dbz_ref, dlag_ref, dlab_ref, dlsg_ref, dlsb_ref, dbg_ref,
             z_scr, v_scr, sa_scr, sg_scr, dv_scr):
        first = pl.program_id(0) == 0
        s_a, ln_a, sig_a, xh_a, rstd_a = _branch_a(c_ref[...], lag_ref[...], lab_ref[...])
        sa_scr[...] = s_a.astype(BF)
        y_a = jnp.dot(sa_scr[...], wco_ref[...], preferred_element_type=F32)
        bu, bv = bu_ref[...], bv_ref[...]
        sg, u, tu, z, tv, vh, rstd_v = _branch_b(bu, bv, lsg_ref[...], lsb_ref[...], wm_ref, bz_ref, z_scr, v_scr)
        sg_scr[...] = sg.astype(BF)
        y_b = jnp.dot(sg_scr[...], wso_ref[...], preferred_element_type=F32)
        ga = _sig(ga_ref[...] + bg_ref[pl.ds(0, 1), :])
        gb = _sig(gb_ref[...] + bg_ref[pl.ds(1, 1), :])
        dm = dm_ref[...]
        dga = dm * y_a * ga * (1.0 - ga)
        dgb = dm * y_b * gb * (1.0 - gb)
        dpb_ref[:, pl.ds(2 * D, D)] = dga.astype(BF)
        dpb_ref[:, pl.ds(3 * D, D)] = dgb.astype(BF)
        _acc(dbg_ref.at[pl.ds(0, 1), :], jnp.sum(dga, axis=0, keepdims=True), first)
        _acc(dbg_ref.at[pl.ds(1, 1), :], jnp.sum(dgb, axis=0, keepdims=True), first)
        dya_ref[...] = (dm * ga).astype(BF)
        dyb_ref[...] = (dm * gb).astype(BF)
        dln = _dot_nt(dya_ref[...], wco_ref[...]) * _dsilu(ln_a, sig_a)
        dc, dlag, dlab = _ln_bwd(xh_a, rstd_a, lag_ref[...], dln)
        dc_ref[...] = dc
        _acc(dlag_ref, dlag, first)
        _acc(dlab_ref, dlab, first)
        dsg = _dot_nt(dyb_ref[...], wso_ref[...])
        dpb_ref[:, pl.ds(0, D)] = (dsg * z * _dgelu(bu, tu)).astype(BF)
        dz = dsg * u
        z_scr[...] = dz
        mask = _tril_mask()

        @pl.when(first)
        def _():
            dwm_ref[...] = jnp.zeros_like(dwm_ref)
            dbz_ref[...] = jnp.zeros_like(dbz_ref)

        for gi in range(SGU_GROUPS):
            wm = jnp.where(mask, wm_ref[gi], 0.0).astype(BF)
            cols = pl.ds(gi * LANES, LANES)
            for n in range(tm // LANES):
                rows = pl.ds(n * LANES, LANES)
                dzb = z_scr[rows, cols].astype(BF)
                dv_scr[rows, cols] = lax.dot_general(wm, dzb, (((0,), (0,)), ((), ())), preferred_element_type=F32)
                dw = lax.dot_general(dzb, v_scr[rows, cols], (((1,), (1,)), ((), ())), preferred_element_type=F32)
                dwm_ref[gi] += jnp.where(mask, dw, 0.0)
                dbz_ref[:, cols] += z_scr[rows, cols]
        dgv, dlsg, dlsb = _ln_bwd(vh, rstd_v, lsg_ref[...], dv_scr[...])
        _acc(dlsg_ref, dlsg, first)
        _acc(dlsb_ref, dlsb, first)
        dpb_ref[:, pl.ds(D, D)] = (dgv * _dgelu(bv, tv)).astype(BF)

    tile = lambda j: pl.BlockSpec((tm, D), lambda i: (i, j))
    vec = pl.BlockSpec((1, D), lambda i: (0, 0))
    return _pcall(
        body, (c, p, p, p, p, dmerged, prm["w_co"], prm["w_so"], prm["la_g"], prm["la_b"], prm["ls_g"], prm["ls_b"],
               prm["sgu_w"], prm["bz"], prm["b_gate"]),
        grid=(T // tm,),
        in_specs=[tile(0), tile(2), tile(3), tile(4), tile(5), tile(0), _const((D, D)), _const((D, D)),
                  _const((1, D)), _const((1, D)), _const((1, D)), _const((1, D)),
                  _const((SGU_GROUPS, LANES, LANES)), _const((LANES, D)), _const((2, D))],
        out_specs=[tile(0), pl.BlockSpec((tm, 4 * D), lambda i: (i, 0)), tile(0), tile(0),
                   pl.BlockSpec((SGU_GROUPS, LANES, LANES), lambda i: (0, 0, 0)),
                   pl.BlockSpec((LANES, D), lambda i: (0, 0)), vec, vec, vec, vec,
                   pl.BlockSpec((2, D), lambda i: (0, 0))],
        out_shape=[_sds((T, D), F32), _sds((T, 4 * D), BF), _sds((T, D), BF), _sds((T, D), BF),
                   _sds((SGU_GROUPS, LANES, LANES), F32), _sds((LANES, D), F32),
                   _sds((1, D), F32), _sds((1, D), F32), _sds((1, D), F32), _sds((1, D), F32), _sds((2, D), F32)],
        scratch_shapes=[pltpu.VMEM((tm, D), F32), pltpu.VMEM((tm, D), BF), pltpu.VMEM((tm, D), BF),
                        pltpu.VMEM((tm, D), BF), pltpu.VMEM((tm, D), F32)],
        name="branch_bwd", vmem=56, comm=comm)


def _conv_bwd(p, dc, conv_w, B, S, comm=None):
    K, D = conv_w.shape
    nc = D // LANES

    def body(av_ref, ag_ref, dc_ref, w_ref, dav_ref, dag_ref, dw_ref, db_ref, apad, dpad):
        b = pl.program_id(1)
        av = av_ref[...]
        sg = _sig(ag_ref[...])
        apad[pl.ds(0, CONV_PAD), :] = jnp.zeros((CONV_PAD, LANES), F32)
        apad[pl.ds(CONV_PAD, S), :] = av * sg
        dpad[pl.ds(S, CONV_PAD), :] = jnp.zeros((CONV_PAD, LANES), F32)
        dpad[pl.ds(0, S), :] = dc_ref[...]

        @pl.when(b == 0)
        def _():
            dw_ref[...] = jnp.zeros_like(dw_ref)
            db_ref[...] = jnp.zeros_like(db_ref)

        db_ref[...] += jnp.sum(dc_ref[...], axis=0, keepdims=True)
        for k in range(K):
            tot = jnp.zeros((1, LANES), F32)
            for r0 in range(0, S, CONV_ROWS):
                tot = tot + jnp.sum(dpad[pl.ds(r0, CONV_ROWS), :] * apad[pl.ds(r0 + k + CONV_PAD - (K - 1), CONV_ROWS), :],
                                    axis=0, keepdims=True)
            dw_ref[pl.ds(k, 1), :] += tot
        for r0 in range(0, S, CONV_ROWS):
            da = jnp.zeros((CONV_ROWS, LANES), F32)
            for k in range(K):
                da = da + w_ref[pl.ds(k, 1), :] * dpad[pl.ds(r0 + (K - 1) - k, CONV_ROWS), :]
            rows = pl.ds(r0, CONV_ROWS)
            s = sg[r0:r0 + CONV_ROWS, :]
            a_v = av[r0:r0 + CONV_ROWS, :]
            dav_ref[rows, :] = (da * s).astype(BF)
            dag_ref[rows, :] = (da * a_v * s * (1.0 - s)).astype(BF)

    blk = lambda off: pl.BlockSpec((S, LANES), lambda j, b: (b, off + j))
    return _pcall(
        body, (p, p, dc, conv_w), grid=(nc, B),
        in_specs=[blk(0), blk(nc), blk(0), pl.BlockSpec((K, LANES), lambda j, b: (0, j))],
        out_specs=[blk(0), blk(0), pl.BlockSpec((K, LANES), lambda j, b: (0, j)), pl.BlockSpec((1, LANES), lambda j, b: (0, j))],
        out_shape=[_sds((B * S, D), BF), _sds((B * S, D), BF), _sds((K, D), F32), _sds((1, D), F32)],
        scratch_shapes=[pltpu.VMEM((S + CONV_PAD, LANES), F32), pltpu.VMEM((S + CONV_PAD, LANES), F32)],
        name="conv_bwd", vmem=32, comm=comm)


TM1 = 512


def _in_proj_bwd(x, dx1, dps, g1, w_in, tm=TM1, comm=None):
    T, D = x.shape
    N = w_in.shape[1]
    widths = [d.shape[1] for d in dps]

    def body(x_ref, dx1_ref, *refs):
        dp_refs, (g_ref, w_ref, dx_ref, dg_ref) = refs[:len(dps)], refs[len(dps):]
        x_t = x_ref[...]
        _, r = _rms(x_t, g_ref[...])
        dh = jnp.zeros((tm, D), F32)
        for q, dp_ref in enumerate(dp_refs):
            cols = pl.ds(sum(widths[:q]), widths[q])
            dh = dh + lax.dot_general(dp_ref[...], w_ref[:, cols], (((1,), (1,)), ((), ())), preferred_element_type=F32)
        dxa, dg = _rms_bwd(x_t, g_ref[...], r, dh)
        dx_ref[...] = dx1_ref[...] + dxa
        _acc(dg_ref, dg, pl.program_id(0) == 0)

    tile = pl.BlockSpec((tm, D), lambda i: (i, 0))
    return _pcall(
        body, (x, dx1, *dps, g1, w_in), grid=(T // tm,),
        in_specs=[tile, tile] + [pl.BlockSpec((tm, w), lambda i: (i, 0)) for w in widths] + [_const((1, D)), _const((D, N))],
        out_specs=[tile, pl.BlockSpec((1, D), lambda i: (0, 0))],
        out_shape=[_sds((T, D), F32), _sds((1, D), F32)],
        name="in_proj_bwd", vmem=48, comm=comm)


def _pick(n, cands):
    for c in cands:
        if n % c == 0:
            return c
    raise ValueError(f"no tile of {cands} divides {n}")


def _mm_tn(x, dys, name, comm=None):
    T, K = x.shape
    dys = list(dys) if isinstance(dys, (list, tuple)) else [dys]
    widths = [d.shape[1] for d in dys]
    N = sum(widths)
    tm = _pick(T, (1024, 512, 256))
    tk = _pick(K, (1024, 1408, 512))
    tn = _pick(math.gcd(*widths), (1024, 1408, 512))
    nt = T // tm
    first = [sum(widths[:q]) // tn for q in range(len(dys))]
    count = [w // tn for w in widths]

    def body(x_ref, *refs):
        dy_refs, (o_ref, ob_ref, acc) = refs[:len(dys)], refs[len(dys):]
        j, t = pl.program_id(1), pl.program_id(2)

        @pl.when(t == 0)
        def _():
            acc[...] = jnp.zeros_like(acc)

        for q, dy_ref in enumerate(dy_refs):
            @pl.when(jnp.logical_and(j >= first[q], j < first[q] + count[q]))
            def _():
                acc[...] += _dot_tn(x_ref[...], dy_ref[...])

        @pl.when(t == nt - 1)
        def _():
            o_ref[...] = acc[...]
            ob_ref[...] = acc[...].astype(BF)

    def dy_spec(q):
        def index(i, j, t):
            mine = jnp.logical_and(j >= first[q], j < first[q] + count[q])
            return jnp.where(mine, t, 0), jnp.clip(j - first[q], 0, count[q] - 1)
        return pl.BlockSpec((tm, tn), index)

    res = _pcall(
        body, (x, *dys), grid=(K // tk, N // tn, nt),
        in_specs=[pl.BlockSpec((tm, tk), lambda i, j, t: (t, i))] + [dy_spec(q) for q in range(len(dys))],
        out_specs=[pl.BlockSpec((tk, tn), lambda i, j, t: (i, j)), pl.BlockSpec((tk, tn), lambda i, j, t: (i, j))],
        out_shape=[_sds((K, N), F32), _sds((K, N), BF)],
        scratch_shapes=[pltpu.VMEM((tk, tn), F32)],
        name=name, vmem=48, comm=comm)
    return res[0] if comm is None else res


def _row_tile(R):
    return _pick(R, (128, 64, 32, 16, 8)) if R % 8 == 0 else R


def _sum_landed(full, land, kind, chip, name):
    _, R, C = land.shape
    tr = _row_tile(R)
    nb = R // tr

    def body(k_ref, o_ref, l_ref, s_ref):
        s_ref[...] = ((o_ref[...] + l_ref[0].astype(F32)) + l_ref[1].astype(F32)) + l_ref[2].astype(F32)

    own = (pl.BlockSpec((tr, C), lambda i, k: (k[0] * nb + i, 0)) if kind == "row"
           else pl.BlockSpec((tr, C), lambda i, k: (i, k[0])))
    return pl.pallas_call(
        body,
        grid_spec=pltpu.PrefetchScalarGridSpec(
            num_scalar_prefetch=1, grid=(nb,),
            in_specs=[own, pl.BlockSpec((3, tr, C), lambda i, k: (0, i, 0))],
            out_specs=pl.BlockSpec((tr, C), lambda i, k: (i, 0))),
        out_shape=_sds((R, C), F32), name=name, compiler_params=_cp(32))(chip, full, land)


def _adamw(g, w, m, v):
    m = ADAM_B1 * m + (1.0 - ADAM_B1) * g
    v = ADAM_B2 * v + (1.0 - ADAM_B2) * (g * g)
    m_hat = m / (1.0 - ADAM_B1 ** ADAM_STEP)
    v_hat = v / (1.0 - ADAM_B2 ** ADAM_STEP)
    return -ADAM_LR * (m_hat / (jnp.sqrt(v_hat) + ADAM_EPS) + ADAM_WD * w), m, v


def _update(parts, w, m, v, name):
    R, C = w.shape
    tr = _row_tile(R)
    k = len(parts)

    def body(*refs):
        g = refs[0][...]
        for r in refs[1:k]:
            g = g + r[...]
        w_ref, m_ref, v_ref, g_out, d_out, m_out, v_out = refs[k:]
        d, m_new, v_new = _adamw(g, w_ref[...], m_ref[...], v_ref[...])
        g_out[...] = g
        d_out[...] = d
        m_out[...] = m_new
        v_out[...] = v_new

    blk = pl.BlockSpec((tr, C), lambda i: (i, 0))
    return pl.pallas_call(
        body, grid=(R // tr,), in_specs=[blk] * (k + 3), out_specs=[blk] * 4, out_shape=[_sds((R, C), F32)] * 4,
        name=name, compiler_params=_cp(40))(*parts, w, m, v)


VECS = ("norm_mix", "conv_b", "conv_ln_g", "conv_ln_b", "sgu_ln_g", "sgu_ln_b", "norm_xattn", "norm_mem", "norm_ffn",
        "norm_final")
SMALL = VECS + ("b_gate", "conv_w", "sgu_w", "sgu_b")


def _update_small(land_vec, land_w, land_g1, chip, a):
    _, R, D = land_vec.shape
    Dq, G, K, nv = D // N_CHIPS, SGU_GROUPS, a["conv_w"].shape[1], len(VECS)

    def as2d(nm, arr):
        if nm in VECS:
            return arr.reshape(1, D)
        if nm == "sgu_w":
            return arr.reshape(G * LANES, LANES)
        return arr.reshape(G, LANES) if nm == "sgu_b" else arr[0]

    params = [as2d(nm, a[pre + nm]) for nm in SMALL for pre in ("", "m_", "v_")]

    def body(k_ref, lv, lvc, lw, l1, *refs):
        prm = refs[:3 * len(SMALL)]
        outs = refs[3 * len(SMALL):7 * len(SMALL)]
        tv, tvc, tw, t1 = refs[7 * len(SMALL):]
        for land, tot in ((lv, tv), (lvc, tvc), (lw, tw), (l1, t1)):
            acc = land[0]
            for dev in range(1, N_DEV):
                acc = acc + land[dev]
            tot[...] = acc
        grads = [t1[pl.ds(0, 1), :] if nm == "norm_mix" else tv[pl.ds(i, 1), :] for i, nm in enumerate(VECS)]
        grads += [tvc[pl.ds(nv, 2), :], tvc[pl.ds(nv + 2, K), :], tw[pl.ds(0, G * LANES), :], tw[pl.ds(G * LANES, G), :]]
        for i, g in enumerate(grads):
            d, m_new, v_new = _adamw(g, prm[3 * i][...], prm[3 * i + 1][...], prm[3 * i + 2][...])
            for o_ref, val in zip(outs[4 * i:4 * i + 4], (g, d, m_new, v_new)):
                o_ref[...] = val

    whole = lambda shape: pl.BlockSpec(tuple(shape), lambda i, k: (0,) * len(shape))
    res = pl.pallas_call(
        body,
        grid_spec=pltpu.PrefetchScalarGridSpec(
            num_scalar_prefetch=1, grid=(1,),
            in_specs=[whole(land_vec.shape), pl.BlockSpec((N_DEV, R, Dq), lambda i, k: (0, 0, k[0])),
                      whole(land_w.shape), whole(land_g1.shape)] + [whole(p.shape) for p in params],
            out_specs=[whole(params[3 * i].shape) for i in range(len(SMALL)) for _ in range(4)],
            scratch_shapes=[pltpu.VMEM((R, D), F32), pltpu.VMEM((R, Dq), F32), pltpu.VMEM(land_w.shape[1:], F32),
                            pltpu.VMEM(land_g1.shape[1:], F32)]),
        out_shape=[_sds(params[3 * i].shape, F32) for i in range(len(SMALL)) for _ in range(4)],
        name="upd_small", compiler_params=_cp(40))(chip, land_vec, land_vec, land_w, land_g1, *params)
    return {nm: list(res[4 * i:4 * i + 4]) for i, nm in enumerate(SMALL)}


BIG = ("w_in", "w_conv_out", "w_sgu_out", "w_mix_out", "w_q", "w_kv", "w_xo", "w_gu", "w_down")
BIG_KIND = {"w_in": "col", "w_conv_out": "row", "w_sgu_out": "row", "w_mix_out": "row", "w_q": "row",
            "w_kv": "col", "w_xo": "row", "w_gu": "col", "w_down": "row"}

def _step(a):
    x3d, mem3d, tgt3d = a["x"], a["mem"], a["loss_target"]
    B, S, D = x3d.shape
    M = mem3d.shape[1]
    T = B * S
    x = x3d.reshape(T, D)
    mem = mem3d.reshape(B * M, D)
    tgt = tgt3d.reshape(T, D)
    xi, yi = lax.axis_index("x"), lax.axis_index("y")
    chip = 2 * xi + yi

    def gather(names):
        return _Gather([a[nm][0] if nm in ("b_gate", "conv_w") else a[nm][0].astype(BF) for nm in names],
                       [BIG_KIND.get(nm, "col") for nm in names])

    first = ("w_in", "b_gate", "conv_w")
    on_in_proj = ("w_conv_out", "w_sgu_out", "w_kv", "w_mix_out", "w_q", "w_xo")
    full = dict(zip(first, _comm_call(gather(first), "gather_w_in")))
    (p, h1), got = _in_proj(x, a["norm_mix"], full["w_in"], comm=gather(on_in_proj))
    full.update(zip(on_in_proj, got))
    (c,), got = _conv_fwd(p, full["conv_w"], a["conv_b"], B, S, comm=gather(("w_down",)))
    full["w_down"] = got[0]

    sgu_b = a["sgu_b"][0]
    bz = jnp.repeat(jnp.transpose(sgu_b), LANES, axis=1)
    prm = dict(w_co=full["w_conv_out"], w_so=full["w_sgu_out"], w_mo=full["w_mix_out"], w_q=full["w_q"],
               w_xo=full["w_xo"], w_down=full["w_down"],
               la_g=a["conv_ln_g"], la_b=a["conv_ln_b"], ls_g=a["sgu_ln_g"], ls_b=a["sgu_ln_b"],
               sgu_w=a["sgu_w"][0], bz=bz, b_gate=full["b_gate"], g2=a["norm_xattn"], g3=a["norm_ffn"],
               gf=a["norm_final"].reshape(1, D))

    (merged, s_a, sg), got = _branch_fwd(c, p, prm, comm=gather(("w_gu",)))
    prm["w_gu"] = got[0]
    mem_n, kv = _kv_fwd(mem, a["norm_mem"], full["w_kv"], B, M)
    x1, x2, h2, o = _attn_fwd(x, merged, kv, prm, S, M)
    dx2, dx3, dgu, h3, f, lsum, d_g3, d_gf = _ffn_loss(x2, tgt, prm)
    loss = lax.psum(0.5 * jnp.sum(lsum) / D, ("x", "y", "c"))

    size_of = {nm: a[nm].shape[1] if BIG_KIND[nm] == "row" else a[nm].shape[2] for nm in BIG}
    landed = {}

    def scatter(names):
        return _Scatter([gw[nm][1] for nm in names], [BIG_KIND[nm] for nm in names], [size_of[nm] for nm in names])

    gw = {}
    gw["w_down"] = _mm_tn(f, dx3, "dw_down")
    gw["w_gu"] = _mm_tn(h3, dgu, "dw_gu")
    gu_send, gu_recv, gu_thru, gu_land, token = _scatter_start(
        gw["w_gu"][1], BIG_KIND["w_gu"], size_of["w_gu"], "scatter_w_gu_start")
    (dx1, dmerged, dq, dkv, d_g2), _ = _attn_bwd(x1, kv, dx2, dict(prm, g2=prm["g2"] + token[:1, :1]), S, M)
    gw["w_xo"] = _mm_tn(o, dx2, "dw_xo")
    gw["w_q"] = _mm_tn(h2, dq, "dw_q")
    gw["w_kv"] = _mm_tn(mem_n, dkv, "dw_kv")
    d_gm = _kv_bwd(mem, a["norm_mem"], full["w_kv"], dkv, B, M)
    gw["w_mix_out"] = _mm_tn(merged, dx1, "dw_mix_out")
    group = ("w_down", "w_xo", "w_q", "w_kv", "w_mix_out")
    (dc, dpb, dya, dyb, d_wm, d_bz, d_lag, d_lab, d_lsg, d_lsb, d_bg), got = _branch_bwd(
        c, p, dmerged, prm, comm=scatter(group))
    landed.update(zip(group, got))
    gw["w_conv_out"] = _mm_tn(s_a, dya, "dw_conv_out")
    gw["w_sgu_out"] = _mm_tn(sg, dyb, "dw_sgu_out")
    group = ("w_conv_out", "w_sgu_out")
    (dav, dag, d_cw, d_cb), got = _conv_bwd(p, dc, full["conv_w"], B, S, comm=scatter(group))
    landed.update(zip(group, got))
    dp = [dav, dag, dpb]

    chip_arr = jnp.reshape(chip, (1,)).astype(jnp.int32)
    early = [nm for nm in BIG if nm != "w_in"]
    landed["w_gu"] = _scatter_wait(gu_send, gu_recv, gu_thru, gu_land, [d_cb], BIG_KIND["w_gu"], size_of["w_gu"],
                                   "scatter_w_gu_wait")
    part = {nm: _sum_landed(gw[nm][0], landed[nm], BIG_KIND[nm], chip_arr, "sum_" + nm) for nm in early}
    G = SGU_GROUPS
    d_sb = jnp.transpose(d_bz.reshape(LANES, G, LANES).sum(axis=-1))
    vec_g = dict(norm_mix=jnp.zeros((1, D), F32), conv_b=d_cb, conv_ln_g=d_lag, conv_ln_b=d_lab, sgu_ln_g=d_lsg,
                 sgu_ln_b=d_lsb, norm_xattn=d_g2, norm_mem=d_gm, norm_ffn=d_g3, norm_final=d_gf)
    rows = [vec_g[nm] for nm in VECS] + [d_bg, d_cw]
    pad = (-sum(r.shape[0] for r in rows)) % 8
    g_vec = jnp.concatenate(rows + [jnp.zeros((pad, D), F32)], axis=0)
    g_w = jnp.concatenate([d_wm.reshape(G * LANES, LANES), d_sb], axis=0)
    gw["w_in"], got = _mm_tn(h1, dp, "dw_in", comm=_Both(_Swap([part[nm] for nm in early]), _Spread([g_vec, g_w])))
    other = dict(zip(early, got[:len(early)]))
    land_vec, land_w = got[len(early):]
    kind_in, size_in = BIG_KIND["w_in"], size_of["w_in"]
    send, recv, g_thru, land_thru, token = _scatter_start(gw["w_in"][1], kind_in, size_in, "scatter_w_in_start")
    (grad_x, d_g1), _ = _in_proj_bwd(x, dx1, dp, a["norm_mix"] + token[:1, :1], full["w_in"])

    out = {}

    def update(nm):
        res = _update([part[nm], other[nm]], a[nm][0], a["m_" + nm][0], a["v_" + nm][0], "upd_" + nm)
        out[nm] = [r[None] for r in res]

    for nm in early:
        update(nm)
    after = [grad_x] + [out[nm][1] for nm in early]
    land_in = _scatter_wait(send, recv, g_thru, land_thru, after, kind_in, size_in, "scatter_w_in_wait")
    part["w_in"] = _sum_landed(gw["w_in"][0], land_in, kind_in, chip_arr, "sum_w_in")
    g1 = jnp.concatenate([d_g1, jnp.zeros((7, D), F32)], axis=0)
    other["w_in"], land_g1 = _comm_call(_Both(_Swap([part["w_in"]]), _Spread([g1])), "swap_w_in")
    update("w_in")
    for nm, res in _update_small(land_vec, land_w, land_g1, chip_arr, a).items():
        out[nm] = [r.reshape(a[nm].shape) for r in res]
    return loss, grad_x.reshape(B, S, D), out


WEIGHTS = ("norm_mix", "w_in", "b_gate", "conv_w", "conv_b", "conv_ln_g", "conv_ln_b", "w_conv_out", "sgu_ln_g",
           "sgu_ln_b", "sgu_w", "sgu_b", "w_sgu_out", "w_mix_out", "norm_xattn", "norm_mem", "w_q", "w_kv", "w_xo",
           "norm_ffn", "w_gu", "w_down", "norm_final")


def kernel(x, mem, norm_mix, w_in, b_gate, conv_w, conv_b, conv_ln_g, conv_ln_b, w_conv_out, sgu_ln_g, sgu_ln_b, sgu_w, sgu_b, w_sgu_out, w_mix_out, norm_xattn, norm_mem, w_q, w_kv, w_xo, norm_ffn, w_gu, w_down, norm_final, loss_target, m_norm_mix, m_w_in, m_b_gate, m_conv_w, m_conv_b, m_conv_ln_g, m_conv_ln_b, m_w_conv_out, m_sgu_ln_g, m_sgu_ln_b, m_sgu_w, m_sgu_b, m_w_sgu_out, m_w_mix_out, m_norm_xattn, m_norm_mem, m_w_q, m_w_kv, m_w_xo, m_norm_ffn, m_w_gu, m_w_down, m_norm_final, v_norm_mix, v_w_in, v_b_gate, v_conv_w, v_conv_b, v_conv_ln_g, v_conv_ln_b, v_w_conv_out, v_sgu_ln_g, v_sgu_ln_b, v_sgu_w, v_sgu_b, v_w_sgu_out, v_w_mix_out, v_norm_xattn, v_norm_mem, v_w_q, v_w_kv, v_w_xo, v_norm_ffn, v_w_gu, v_w_down, v_norm_final):
    a = dict(locals())
    loss, grad_x, out = _step(a)
    res = [loss, grad_x]
    for q in range(4):
        res += [out[nm][q] for nm in WEIGHTS]
    return tuple(res)
```

```python
import functools
import math

import jax
import jax.numpy as jnp
from jax import lax
from jax.experimental import pallas as pl
from jax.experimental.pallas import tpu as pltpu

BF = jnp.bfloat16
F32 = jnp.float32
MESH = pl.DeviceIdType.MESH
ANY = pl.BlockSpec(memory_space=pl.ANY)

RMS_EPS = 1e-6
LN_EPS = 1e-5
HEADS = 4
SGU_GROUPS = 8
LANES = 128
ADAM_LR = 0.001
ADAM_B1 = 0.9
ADAM_B2 = 0.999
ADAM_EPS = 1e-08
ADAM_WD = 0.01
ADAM_STEP = 10
N_CHIPS = 4
N_DEV = 8
MIB = 1024 * 1024


def _sds(shape, dtype):
    return jax.ShapeDtypeStruct(tuple(shape), dtype)


def _cp(vmem_mib):
    return pltpu.CompilerParams(vmem_limit_bytes=vmem_mib * MIB)


def _const(shape):
    nd = len(shape)
    return pl.BlockSpec(tuple(shape), lambda *_: (0,) * nd, pipeline_mode=pl.Buffered(1))


def _dot(a, b):
    return jnp.dot(a.astype(BF), b.astype(BF), preferred_element_type=F32)


def _dot_nt(a, b):
    return lax.dot_general(a.astype(BF), b.astype(BF), (((1,), (1,)), ((), ())), preferred_element_type=F32)


def _dot_tn(a, b):
    return lax.dot_general(a.astype(BF), b.astype(BF), (((0,), (0,)), ((), ())), preferred_element_type=F32)


def _sig(x):
    return 1.0 / (1.0 + jnp.exp(-x))


def _dsilu(x, s):
    return s * (1.0 + x * (1.0 - s))


_GELU_C = math.sqrt(2.0 / math.pi)


def _gelu(x):
    x2 = x * x
    t = jnp.tanh((_GELU_C * x) * (1.0 + 0.044715 * x2))
    cdf = 0.5 * (1.0 + t)
    return x * cdf, (t, x2, cdf)


def _dgelu(x, shared):
    t, x2, cdf = shared
    return cdf + ((0.5 * _GELU_C) * x) * (1.0 - t * t) * (1.0 + (3.0 * 0.044715) * x2)


def _rms(x, g):
    r = lax.rsqrt(jnp.mean(x * x, axis=-1, keepdims=True) + RMS_EPS)
    return x * r * g, r


def _rms_bwd(x, g, r, dh):
    xr = x * r
    dxh = dh * g
    dx = r * (dxh - xr * jnp.mean(dxh * xr, axis=-1, keepdims=True))
    return dx, jnp.sum(dh * xr, axis=0, keepdims=True)


def _ln(x, g, b):
    mu = jnp.mean(x, axis=-1, keepdims=True)
    xc = x - mu
    rstd = lax.rsqrt(jnp.mean(xc * xc, axis=-1, keepdims=True) + LN_EPS)
    xh = xc * rstd
    return xh * g + b, xh, rstd


def _ln_bwd(xh, rstd, g, dy):
    dxh = dy * g
    dx = rstd * (dxh - jnp.mean(dxh, axis=-1, keepdims=True) - xh * jnp.mean(dxh * xh, axis=-1, keepdims=True))
    return dx, jnp.sum(dy * xh, axis=0, keepdims=True), jnp.sum(dy, axis=0, keepdims=True)


def _acc(ref, val, first):
    @pl.when(first)
    def _():
        ref[...] = val

    @pl.when(jnp.logical_not(first))
    def _():
        ref[...] += val


def _place():
    x, y, c = lax.axis_index("x"), lax.axis_index("y"), lax.axis_index("c")
    chips = [(1 - x, y), (x, 1 - y), (1 - x, 1 - y)]
    return x, y, c, chips


def _shard_of(ref, kind, k, n):
    if kind == "row":
        return ref.at[pl.ds(k * n, n), :]
    return ref.at[:, pl.ds(k * n, n)]


class _Gather:
    def __init__(self, shards, kinds):
        n = len(shards)
        self.srcs, self.kinds = list(shards), list(kinds)
        self.sizes = [s.shape[0] if kd == "row" else s.shape[1] for s, kd in zip(shards, kinds)]
        self.halves = [s.shape[0] // 2 if s.shape[0] % 32 == 0 else None for s in shards]
        self.out_shape = [
            _sds((s.shape[0] * N_CHIPS, s.shape[1]) if kd == "row" else (s.shape[0], s.shape[1] * N_CHIPS), s.dtype)
            for s, kd in zip(shards, kinds)]
        dma = pltpu.SemaphoreType.DMA
        self.sems = [dma((3 * n,)), dma((3 * n,)), dma((n,)), dma((3 * n,)), dma((3 * n,))]

    def _part(self, ref, t, core):
        h = self.halves[t]
        return ref if h is None else ref.at[pl.ds(core * h, h), :]

    def _copies(self, ins, outs, send, recv, loc, fsend, frecv):
        x, y, c, chips = _place()
        k = 2 * x + y
        local, remote = [], []
        for t in range(len(ins)):
            block = lambda q: _shard_of(outs[t], self.kinds[t], q, self.sizes[t])
            local.append(pltpu.make_async_copy(ins[t], block(k), loc.at[t]))
            for j, (px, py) in enumerate(chips):
                sems = dict(send_sem=send.at[3 * t + j], recv_sem=recv.at[3 * t + j])
                there = dict(device_id=(px, py, c), device_id_type=MESH)
                sent = pltpu.make_async_remote_copy(
                    src_ref=self._part(ins[t], t, c), dst_ref=self._part(block(k), t, c), **sems, **there)
                got = self._part(block(2 * px + py), t, c)
                landed = pltpu.make_async_remote_copy(src_ref=self._part(ins[t], t, c), dst_ref=got, **sems, **there)
                passed = handed = None
                if self.halves[t] is not None:
                    fsems = dict(send_sem=fsend.at[3 * t + j], recv_sem=frecv.at[3 * t + j])
                    sibling = dict(device_id=(x, y, 1 - c), device_id_type=MESH)
                    passed = pltpu.make_async_remote_copy(src_ref=got, dst_ref=got, **fsems, **sibling)
                    other = self._part(block(2 * px + py), t, 1 - c)
                    handed = pltpu.make_async_remote_copy(src_ref=got, dst_ref=other, **fsems, **sibling)
                remote.append((sent, landed, passed, handed))
        return local, remote

    def start(self, ins, outs, *sems):
        local, remote = self._copies(ins, outs, *sems)
        for cp in local:
            cp.start()
        for sent, _, _, _ in remote:
            sent.start()

    def wait(self, ins, outs, *sems):
        local, remote = self._copies(ins, outs, *sems)
        for sent, landed, passed, handed in remote:
            landed.wait_recv()
            if passed is not None:
                passed.start()
        for sent, landed, passed, handed in remote:
            if passed is not None:
                handed.wait_recv()
                passed.wait_send()
            sent.wait_send()
        for cp in local:
            cp.wait()


class _Scatter:
    def __init__(self, grads, kinds, sizes):
        n = len(grads)
        self.srcs, self.kinds, self.sizes = list(grads), list(kinds), list(sizes)
        self.out_shape = [_sds((3,) + ((sz, g.shape[1]) if kd == "row" else (g.shape[0], sz)), g.dtype)
                          for g, kd, sz in zip(grads, kinds, sizes)]
        self.sems = [pltpu.SemaphoreType.DMA((3 * n,)), pltpu.SemaphoreType.DMA((3 * n,))]

    def _copies(self, ins, outs, send, recv):
        x, y, c, chips = _place()
        return [pltpu.make_async_remote_copy(
            src_ref=_shard_of(ins[t], self.kinds[t], 2 * px + py, self.sizes[t]), dst_ref=outs[t].at[j],
            send_sem=send.at[3 * t + j], recv_sem=recv.at[3 * t + j], device_id=(px, py, c), device_id_type=MESH)
            for t in range(len(ins)) for j, (px, py) in enumerate(chips)]

    def start(self, ins, outs, send, recv):
        for cp in self._copies(ins, outs, send, recv):
            cp.start()

    def wait(self, ins, outs, send, recv):
        for cp in self._copies(ins, outs, send, recv):
            cp.wait_recv()
            cp.wait_send()


class _Swap:
    def __init__(self, parts):
        n = len(parts)
        self.srcs = list(parts)
        self.out_shape = [_sds(p.shape, p.dtype) for p in parts]
        self.sems = [pltpu.SemaphoreType.DMA((n,)), pltpu.SemaphoreType.DMA((n,))]

    def _copies(self, ins, outs, send, recv):
        x, y, c, _ = _place()
        return [pltpu.make_async_remote_copy(
            src_ref=ins[t], dst_ref=outs[t], send_sem=send.at[t], recv_sem=recv.at[t],
            device_id=(x, y, 1 - c), device_id_type=MESH) for t in range(len(ins))]

    def start(self, ins, outs, send, recv):
        for cp in self._copies(ins, outs, send, recv):
            cp.start()

    def wait(self, ins, outs, send, recv):
        for cp in self._copies(ins, outs, send, recv):
            cp.wait_recv()
            cp.wait_send()


class _Spread:
    def __init__(self, packs):
        n = len(packs)
        self.srcs = list(packs)
        self.out_shape = [_sds((N_DEV,) + p.shape, p.dtype) for p in packs]
        dma = pltpu.SemaphoreType.DMA
        self.sems = [dma((N_DEV * n,)), dma((N_DEV * n,)), dma((n,))]

    def _copies(self, ins, outs, send, recv, loc):
        x, y, c, _ = _place()
        me = 4 * x + 2 * y + c
        local = [pltpu.make_async_copy(ins[t], outs[t].at[me], loc.at[t]) for t in range(len(ins))]
        remote = []
        for t in range(len(ins)):
            for mask in range(1, N_DEV):
                peer = ((1 - x) if mask & 4 else x, (1 - y) if mask & 2 else y, (1 - c) if mask & 1 else c)
                src = peer[0] * 4 + peer[1] * 2 + peer[2]
                sems = dict(send_sem=send.at[N_DEV * t + mask], recv_sem=recv.at[N_DEV * t + mask])
                sent = pltpu.make_async_remote_copy(
                    src_ref=ins[t], dst_ref=outs[t].at[me], device_id=peer, device_id_type=MESH, **sems)
                landed = pltpu.make_async_remote_copy(
                    src_ref=ins[t], dst_ref=outs[t].at[src], device_id=peer, device_id_type=MESH, **sems)
                remote.append((sent, landed))
        return local, remote

    def start(self, ins, outs, send, recv, loc):
        local, remote = self._copies(ins, outs, send, recv, loc)
        for cp in local:
            cp.start()
        for sent, _ in remote:
            sent.start()

    def wait(self, ins, outs, send, recv, loc):
        local, remote = self._copies(ins, outs, send, recv, loc)
        for sent, landed in remote:
            landed.wait_recv()
            sent.wait_send()
        for cp in local:
            cp.wait()


class _Both:
    def __init__(self, *comms):
        self.comms = comms
        self.srcs = [s for cm in comms for s in cm.srcs]
        self.out_shape = [s for cm in comms for s in cm.out_shape]
        self.sems = [s for cm in comms for s in cm.sems]

    def _each(self, ins, outs, sems):
        i = o = k = 0
        for cm in self.comms:
            ni, no, nk = len(cm.srcs), len(cm.out_shape), len(cm.sems)
            yield cm, ins[i:i + ni], outs[o:o + no], sems[k:k + nk]
            i, o, k = i + ni, o + no, k + nk

    def start(self, ins, outs, *sems):
        for cm, i, o, s in self._each(ins, outs, sems):
            cm.start(i, o, *s)

    def wait(self, ins, outs, *sems):
        for cm, i, o, s in self._each(ins, outs, sems):
            cm.wait(i, o, *s)


HBM = pl.BlockSpec(memory_space=pltpu.HBM)
SEM = pl.BlockSpec(memory_space=pltpu.SEMAPHORE)
EFFECT = pltpu.SideEffectType.DATAFLOW_SIDE_EFFECTING


def _scatter_copies(g_ref, land_ref, send, recv, kind, size):
    x, y, c, chips = _place()
    return [pltpu.make_async_remote_copy(
        src_ref=_shard_of(g_ref, kind, 2 * px + py, size), dst_ref=land_ref.at[j],
        send_sem=send.at[j], recv_sem=recv.at[j], device_id=(px, py, c), device_id_type=MESH)
        for j, (px, py) in enumerate(chips)]


def _scatter_start(grad, kind, size, name):
    land = (3,) + ((size, grad.shape[1]) if kind == "row" else (grad.shape[0], size))

    def body(g_ref, land_ref, send, recv, g_thru, land_thru, token):
        for cp in _scatter_copies(g_ref, land_ref, send, recv, kind, size):
            cp.start()
        token[...] = jnp.zeros_like(token)

    return pl.pallas_call(
        body, name=name,
        out_shape=(pltpu.SemaphoreType.DMA((3,)), pltpu.SemaphoreType.DMA((3,)), pltpu.HBM(grad.shape, grad.dtype),
                   pltpu.HBM(land, grad.dtype), _sds((8, LANES), F32)),
        in_specs=(HBM, HBM), out_specs=(SEM, SEM, HBM, HBM, pl.BlockSpec(memory_space=pltpu.VMEM)),
        input_output_aliases={0: 2, 1: 3},
        compiler_params=pltpu.CompilerParams(has_side_effects=EFFECT))(
            pltpu.with_memory_space_constraint(grad, pltpu.HBM),
            pltpu.with_memory_space_constraint(lax.empty(land, grad.dtype), pltpu.HBM))


def _scatter_wait(send, recv, g_thru, land_thru, after, kind, size, name):
    def body(g_ref, land_ref, send, recv, *rest):
        for cp in _scatter_copies(g_ref, land_ref, send, recv, kind, size):
            cp.wait_send()
            cp.wait_recv()

    return pl.pallas_call(
        body, name=name,
        out_shape=(pltpu.HBM(g_thru.shape, g_thru.dtype), pltpu.HBM(land_thru.shape, land_thru.dtype)),
        in_specs=(HBM, HBM, SEM, SEM) + (ANY,) * len(after), out_specs=(HBM, HBM),
        input_output_aliases={0: 0, 1: 1},
        compiler_params=pltpu.CompilerParams(has_side_effects=EFFECT))(g_thru, land_thru, send, recv, *after)[1]


def _comm_call(comm, name):
    n, m = len(comm.srcs), len(comm.out_shape)

    def body(*refs):
        comm.start(refs[:n], refs[n:n + m], *refs[n + m:])
        comm.wait(refs[:n], refs[n:n + m], *refs[n + m:])

    return pl.pallas_call(body, in_specs=[ANY] * n, out_specs=[ANY] * m, out_shape=comm.out_shape,
                          scratch_shapes=comm.sems, name=name)(*comm.srcs)


def _pcall(body, args, *, grid, in_specs, out_specs, out_shape, name, vmem, scratch_shapes=(), comm=None):
    in_specs, out_specs, out_shape = list(in_specs), list(out_specs), list(out_shape)
    scratch_shapes = list(scratch_shapes)
    if comm is None:
        res = pl.pallas_call(body, grid=grid, in_specs=in_specs, out_specs=out_specs, out_shape=out_shape,
                             scratch_shapes=scratch_shapes, name=name, compiler_params=_cp(vmem))(*args)
        return list(res), []
    ni, no, ns = len(in_specs), len(out_specs), len(scratch_shapes)
    ci, co = len(comm.srcs), len(comm.out_shape)

    def carried(*refs):
        c_in = refs[ni:ni + ci]
        c_out = refs[ni + ci + no:ni + ci + no + co]
        sems = refs[ni + ci + no + co + ns:]
        ids = [pl.program_id(d) for d in range(len(grid))]
        first = functools.reduce(jnp.logical_and, [i == 0 for i in ids])
        last = functools.reduce(jnp.logical_and, [i == g - 1 for i, g in zip(ids, grid)])

        @pl.when(first)
        def _():
            comm.start(c_in, c_out, *sems)

        body(*refs[:ni], *refs[ni + ci:ni + ci + no], *refs[ni + ci + no + co:ni + ci + no + co + ns])

        @pl.when(last)
        def _():
            comm.wait(c_in, c_out, *sems)

    res = pl.pallas_call(carried, grid=grid, in_specs=in_specs + [ANY] * ci, out_specs=out_specs + [ANY] * co,
                         out_shape=out_shape + list(comm.out_shape), scratch_shapes=scratch_shapes + list(comm.sems),
                         name=name, compiler_params=_cp(vmem))(*args, *comm.srcs)
    return list(res[:no]), list(res[no:])


def _in_proj(x, g1, w_in, comm=None):
    T, D = x.shape
    N = w_in.shape[1]
    tm, tn = 512, 1024

    def body(x_ref, g_ref, w_ref, p_ref, h_ref):
        h, _ = _rms(x_ref[...], g_ref[...])
        h_ref[...] = h.astype(BF)
        for j in range(N // tn):
            cols = pl.ds(j * tn, tn)
            p_ref[:, cols] = jnp.dot(h_ref[...], w_ref[:, cols], preferred_element_type=F32)

    return _pcall(
        body, (x, g1, w_in), grid=(T // tm,),
        in_specs=[pl.BlockSpec((tm, D), lambda i: (i, 0)), _const((1, D)), _const((D, N))],
        out_specs=[pl.BlockSpec((tm, N), lambda i: (i, 0)), pl.BlockSpec((tm, D), lambda i: (i, 0))],
        out_shape=[_sds((T, N), F32), _sds((T, D), BF)],
        name="in_proj", vmem=56, comm=comm)


CONV_PAD = 32
CONV_ROWS = 128
CONV_ROWS_BWD = 64


def _conv_fwd(p, conv_w, conv_b, B, S, comm=None):
    K, D = conv_w.shape
    nc = D // LANES

    def body(av_ref, ag_ref, w_ref, b_ref, c_ref, apad):
        apad[pl.ds(0, CONV_PAD), :] = jnp.zeros((CONV_PAD, LANES), F32)
        apad[pl.ds(CONV_PAD, S), :] = av_ref[...] * _sig(ag_ref[...])
        for r0 in range(0, S, CONV_ROWS):
            acc = jnp.zeros((CONV_ROWS, LANES), F32) + b_ref[...]
            for k in range(K):
                acc = acc + w_ref[pl.ds(k, 1), :] * apad[pl.ds(r0 + k + CONV_PAD - (K - 1), CONV_ROWS), :]
            c_ref[pl.ds(r0, CONV_ROWS), :] = acc

    return _pcall(
        body, (p, p, conv_w, conv_b), grid=(B, nc),
        in_specs=[pl.BlockSpec((S, LANES), lambda b, j: (b, j)), pl.BlockSpec((S, LANES), lambda b, j: (b, nc + j)),
                  pl.BlockSpec((K, LANES), lambda b, j: (0, j)), pl.BlockSpec((1, LANES), lambda b, j: (0, j))],
        out_specs=[pl.BlockSpec((S, LANES), lambda b, j: (b, j))],
        out_shape=[_sds((B * S, D), F32)],
        scratch_shapes=[pltpu.VMEM((S + CONV_PAD, LANES), F32)],
        name="conv_fwd", vmem=32, comm=comm)


def _tril_mask():
    t = lax.broadcasted_iota(jnp.int32, (LANES, LANES), 0)
    s = lax.broadcasted_iota(jnp.int32, (LANES, LANES), 1)
    return t >= s


def _branch_a(c, g, b):
    ln_a, xh, rstd = _ln(c, g, b)
    s = _sig(ln_a)
    return ln_a * s, ln_a, s, xh, rstd


def _branch_b(bu, bv, g, b, wm_ref, bz_ref, z_scr, v_scr):
    tm, D = bu.shape
    u, tu = _gelu(bu)
    gv, tv = _gelu(bv)
    v, vh, rstd = _ln(gv, g, b)
    v_scr[...] = v.astype(BF)
    mask = _tril_mask()
    for gi in range(SGU_GROUPS):
        wm = jnp.where(mask, wm_ref[gi], 0.0).astype(BF)
        cols = pl.ds(gi * LANES, LANES)
        for n in range(tm // LANES):
            rows = pl.ds(n * LANES, LANES)
            z_scr[rows, cols] = jnp.dot(wm, v_scr[rows, cols], preferred_element_type=F32) + bz_ref[:, cols]
    z = z_scr[...]
    return u * z, u, tu, z, tv, vh, rstd


TM3 = 256
TM3_FWD = 512


def _branch_fwd(c, p, prm, tm=TM3_FWD, comm=None):
    T, D = c.shape

    def body(c_ref, bu_ref, bv_ref, ga_ref, gb_ref, wco_ref, wso_ref, lag_ref, lab_ref, lsg_ref, lsb_ref, wm_ref,
             bz_ref, bg_ref, mg_ref, sa_ref, sg_ref, z_scr, v_scr):
        s_a = _branch_a(c_ref[...], lag_ref[...], lab_ref[...])[0]
        sa_ref[...] = s_a.astype(BF)
        y_a = jnp.dot(sa_ref[...], wco_ref[...], preferred_element_type=F32)
        sg = _branch_b(bu_ref[...], bv_ref[...], lsg_ref[...], lsb_ref[...], wm_ref, bz_ref, z_scr, v_scr)[0]
        sg_ref[...] = sg.astype(BF)
        y_b = jnp.dot(sg_ref[...], wso_ref[...], preferred_element_type=F32)
        ga = _sig(ga_ref[...] + bg_ref[pl.ds(0, 1), :])
        gb = _sig(gb_ref[...] + bg_ref[pl.ds(1, 1), :])
        mg_ref[...] = (ga * y_a + gb * y_b).astype(BF)

    tile = lambda j: pl.BlockSpec((tm, D), lambda i: (i, j))
    return _pcall(
        body, (c, p, p, p, p, prm["w_co"], prm["w_so"], prm["la_g"], prm["la_b"], prm["ls_g"], prm["ls_b"],
               prm["sgu_w"], prm["bz"], prm["b_gate"]),
        grid=(T // tm,),
        in_specs=[tile(0), tile(2), tile(3), tile(4), tile(5), _const((D, D)), _const((D, D)),
                  _const((1, D)), _const((1, D)), _const((1, D)), _const((1, D)),
                  _const((SGU_GROUPS, LANES, LANES)), _const((LANES, D)), _const((2, D))],
        out_specs=[tile(0), tile(0), tile(0)],
        out_shape=[_sds((T, D), BF)] * 3,
        scratch_shapes=[pltpu.VMEM((tm, D), F32), pltpu.VMEM((tm, D), BF)],
        name="branch_fwd", vmem=48, comm=comm)


def _kv_fwd(mem, gm, w_kv, B, M):
    D = mem.shape[1]
    N = w_kv.shape[1]

    def body(m_ref, g_ref, w_ref, mn_ref, kv_ref):
        h, _ = _rms(m_ref[...], g_ref[...])
        mn_ref[...] = h.astype(BF)
        kv_ref[...] = jnp.dot(mn_ref[...], w_ref[...], preferred_element_type=F32).astype(BF)

    return pl.pallas_call(
        body, grid=(B,),
        in_specs=[pl.BlockSpec((M, D), lambda b: (b, 0)), _const((1, D)), _const((D, N))],
        out_specs=[pl.BlockSpec((M, D), lambda b: (b, 0)), pl.BlockSpec((M, N), lambda b: (b, 0))],
        out_shape=[_sds((B * M, D), BF), _sds((B * M, N), BF)],
        name="kv_fwd", compiler_params=_cp(32))(mem, gm, w_kv)


def _softmax_rows(s):
    e = jnp.exp(s - jnp.max(s, axis=-1, keepdims=True))
    return e * (1.0 / jnp.sum(e, axis=-1, keepdims=True))


TM4 = 512


def _attn_fwd(x, merged, kv, prm, S, M, tm=TM4):
    T, D = x.shape
    hd = D // HEADS
    scale = hd ** -0.5
    tpb = S // tm

    def body(x_ref, mg_ref, kv_ref, wmo_ref, wq_ref, wxo_ref, g_ref, x1_ref, x2_ref, h2_ref, o_ref):
        x1 = x_ref[...] + jnp.dot(mg_ref[...], wmo_ref[...], preferred_element_type=F32)
        x1_ref[...] = x1
        h2, _ = _rms(x1, g_ref[...])
        h2_ref[...] = h2.astype(BF)
        qb = jnp.dot(h2_ref[...], wq_ref[...], preferred_element_type=F32).astype(BF)
        for h in range(HEADS):
            cs = pl.ds(h * hd, hd)
            s = _dot_nt(qb[:, h * hd:(h + 1) * hd], kv_ref[:, cs]) * scale
            pr = _softmax_rows(s)
            o_ref[:, cs] = _dot(pr, kv_ref[:, pl.ds(D + h * hd, hd)]).astype(BF)
        x2_ref[...] = x1 + jnp.dot(o_ref[...], wxo_ref[...], preferred_element_type=F32)

    tile = pl.BlockSpec((tm, D), lambda i: (i, 0))
    return pl.pallas_call(
        body, grid=(T // tm,),
        in_specs=[tile, tile, pl.BlockSpec((M, 2 * D), lambda i: (i // tpb, 0)),
                  _const((D, D)), _const((D, D)), _const((D, D)), _const((1, D))],
        out_specs=[tile, tile, tile, tile],
        out_shape=[_sds((T, D), F32), _sds((T, D), F32), _sds((T, D), BF), _sds((T, D), BF)],
        name="attn_fwd", compiler_params=_cp(40))(x, merged, kv, prm["w_mo"], prm["w_q"], prm["w_xo"], prm["g2"])


TM5 = 256
FFN_CHUNKS = 1


def _ffn_loss(x2, tgt, prm, tm=TM5):
    T, D = x2.shape
    F = prm["w_down"].shape[0]
    FC = F // FFN_CHUNKS

    def body(x2_ref, t_ref, wgu_ref, wd_ref, g3_ref, gf_ref, dx2_ref, dx3_ref, dgu_ref, h3_ref, f_ref, ls_ref,
             dg3_ref, dgf_ref, gu_scr):
        first = pl.program_id(0) == 0
        x2 = x2_ref[...]
        h3, r3 = _rms(x2, g3_ref[...])
        h3_ref[...] = h3.astype(BF)
        x3 = x2
        for ch in range(FFN_CHUNKS):
            gc, uc = pl.ds(ch * FC, FC), pl.ds(F + ch * FC, FC)
            gt = jnp.dot(h3_ref[...], wgu_ref[:, gc], preferred_element_type=F32)
            up = jnp.dot(h3_ref[...], wgu_ref[:, uc], preferred_element_type=F32)
            gu_scr[:, gc] = gt
            gu_scr[:, uc] = up
            f_ref[:, gc] = (gt * _sig(gt) * up).astype(BF)
            x3 = x3 + jnp.dot(f_ref[:, gc], wd_ref[gc, :], preferred_element_type=F32)
        y, rf = _rms(x3, gf_ref[...])
        e = y - t_ref[...]
        _acc(ls_ref, jnp.sum(e * e, axis=0, keepdims=True), first)
        dx3, dgf = _rms_bwd(x3, gf_ref[...], rf, e * (1.0 / D))
        _acc(dgf_ref, dgf, first)
        dx3_ref[...] = dx3.astype(BF)
        dh3 = jnp.zeros((tm, D), F32)
        for ch in range(FFN_CHUNKS):
            gc, uc = pl.ds(ch * FC, FC), pl.ds(F + ch * FC, FC)
            df = lax.dot_general(dx3_ref[...], wd_ref[gc, :], (((1,), (1,)), ((), ())), preferred_element_type=F32)
            gt, up = gu_scr[:, gc], gu_scr[:, uc]
            s = _sig(gt)
            dgu_ref[:, gc] = (df * up * _dsilu(gt, s)).astype(BF)
            dgu_ref[:, uc] = (df * gt * s).astype(BF)
            dh3 = dh3 + lax.dot_general(dgu_ref[:, gc], wgu_ref[:, gc], (((1,), (1,)), ((), ())), preferred_element_type=F32)
            dh3 = dh3 + lax.dot_general(dgu_ref[:, uc], wgu_ref[:, uc], (((1,), (1,)), ((), ())), preferred_element_type=F32)
        dxa, dg3 = _rms_bwd(x2, g3_ref[...], r3, dh3)
        _acc(dg3_ref, dg3, first)
        dx2_ref[...] = dx3 + dxa

    tile = lambda n: pl.BlockSpec((tm, n), lambda i: (i, 0))
    vec = pl.BlockSpec((1, D), lambda i: (0, 0))
    return pl.pallas_call(
        body, grid=(T // tm,),
        in_specs=[tile(D), tile(D), _const((D, 2 * F)), _const((F, D)), _const((1, D)), _const((1, D))],
        out_specs=[tile(D), tile(D), tile(2 * F), tile(D), tile(F), vec, vec, vec],
        out_shape=[_sds((T, D), F32), _sds((T, D), BF), _sds((T, 2 * F), BF), _sds((T, D), BF), _sds((T, F), BF),
                   _sds((1, D), F32), _sds((1, D), F32), _sds((1, D), F32)],
        scratch_shapes=[pltpu.VMEM((tm, 2 * F), F32)],
        name="ffn_loss", compiler_params=_cp(56))(x2, tgt, prm["w_gu"], prm["w_down"], prm["g3"], prm["gf"])


def _attn_bwd(x1, kv, dx2, prm, S, M, tm=TM4, comm=None):
    T, D = x1.shape
    hd = D // HEADS
    scale = hd ** -0.5
    tpb = S // tm

    def body(x1_ref, kv_ref, dx2_ref, wmo_ref, wq_ref, wxo_ref, g_ref, dx1_ref, dmg_ref, dq_ref, dkv_ref, dg_ref,
             h2_scr, do_scr):
        i = pl.program_id(0)
        x1 = x1_ref[...]
        dx2 = dx2_ref[...]
        h2, r2 = _rms(x1, g_ref[...])
        h2_scr[...] = h2.astype(BF)
        qb = jnp.dot(h2_scr[...], wq_ref[...], preferred_element_type=F32).astype(BF)
        do_scr[...] = _dot_nt(dx2, wxo_ref[...]).astype(BF)
        for h in range(HEADS):
            cs, vs = pl.ds(h * hd, hd), pl.ds(D + h * hd, hd)
            qh = qb[:, h * hd:(h + 1) * hd]
            pr = _softmax_rows(_dot_nt(qh, kv_ref[:, cs]) * scale)
            dpr = _dot_nt(do_scr[:, cs], kv_ref[:, vs])
            dv = _dot_tn(pr, do_scr[:, cs])
            ds = (pr * (dpr - jnp.sum(dpr * pr, axis=-1, keepdims=True)) * scale).astype(BF)
            dq_ref[:, cs] = jnp.dot(ds, kv_ref[:, cs], preferred_element_type=F32).astype(BF)
            dk = _dot_tn(ds, qh)

            @pl.when(i % tpb == 0)
            def _():
                dkv_ref[:, cs] = dk
                dkv_ref[:, vs] = dv

            @pl.when(i % tpb != 0)
            def _():
                dkv_ref[:, cs] += dk
                dkv_ref[:, vs] += dv

        dh2 = _dot_nt(dq_ref[...], wq_ref[...])
        dxa, dg = _rms_bwd(x1, g_ref[...], r2, dh2)
        _acc(dg_ref, dg, i == 0)
        dx1 = dx2 + dxa
        dx1_ref[...] = dx1
        dmg_ref[...] = _dot_nt(dx1, wmo_ref[...])

    tile = pl.BlockSpec((tm, D), lambda i: (i, 0))
    kvb = pl.BlockSpec((M, 2 * D), lambda i: (i // tpb, 0))
    B = T // S
    return _pcall(
        body, (x1, kv, dx2, prm["w_mo"], prm["w_q"], prm["w_xo"], prm["g2"]), grid=(T // tm,),
        in_specs=[tile, kvb, tile, _const((D, D)), _const((D, D)), _const((D, D)), _const((1, D))],
        out_specs=[tile, tile, tile, kvb, pl.BlockSpec((1, D), lambda i: (0, 0))],
        out_shape=[_sds((T, D), F32), _sds((T, D), F32), _sds((T, D), BF), _sds((B * M, 2 * D), F32), _sds((1, D), F32)],
        scratch_shapes=[pltpu.VMEM((tm, D), BF), pltpu.VMEM((tm, D), BF)],
        name="attn_bwd", vmem=48, comm=comm)


def _kv_bwd(mem, gm, w_kv, dkv, B, M):
    D = mem.shape[1]
    N = w_kv.shape[1]

    def body(m_ref, g_ref, w_ref, dkv_ref, dg_ref):
        mem_t = m_ref[...]
        _, r = _rms(mem_t, g_ref[...])
        dmn = _dot_nt(dkv_ref[...], w_ref[...])
        _acc(dg_ref, jnp.sum(dmn * (mem_t * r), axis=0, keepdims=True), pl.program_id(0) == 0)

    return pl.pallas_call(
        body, grid=(B,),
        in_specs=[pl.BlockSpec((M, D), lambda b: (b, 0)), _const((1, D)), _const((D, N)),
                  pl.BlockSpec((M, N), lambda b: (b, 0))],
        out_specs=pl.BlockSpec((1, D), lambda b: (0, 0)),
        out_shape=_sds((1, D), F32),
        name="kv_bwd", compiler_params=_cp(32))(mem, gm, w_kv, dkv)


def _branch_bwd(c, p, dmerged, prm, tm=TM3, comm=None):
    T, D = c.shape

    def body(c_ref, bu_ref, bv_ref, ga_ref, gb_ref, dm_ref, wco_ref, wso_ref, lag_ref, lab_ref, lsg_ref, lsb_ref,
             wm_ref, bz_ref, bg_ref,
             dc_ref, dpb_ref, dya_ref, dyb_ref, dwm_ref, dbz_ref, dlag_ref, dlab_ref, dlsg_ref, dlsb_ref, dbg_ref,
             z_scr, v_scr, sa_scr, sg_scr, dv_scr):
        first = pl.program_id(0) == 0
        s_a, ln_a, sig_a, xh_a, rstd_a = _branch_a(c_ref[...], lag_ref[...], lab_ref[...])
        sa_scr[...] = s_a.astype(BF)
        y_a = jnp.dot(sa_scr[...], wco_ref[...], preferred_element_type=F32)
        bu, bv = bu_ref[...], bv_ref[...]
        sg, u, tu, z, tv, vh, rstd_v = _branch_b(bu, bv, lsg_ref[...], lsb_ref[...], wm_ref, bz_ref, z_scr, v_scr)
        sg_scr[...] = sg.astype(BF)
        y_b = jnp.dot(sg_scr[...], wso_ref[...], preferred_element_type=F32)
        ga = _sig(ga_ref[...] + bg_ref[pl.ds(0, 1), :])
        gb = _sig(gb_ref[...] + bg_ref[pl.ds(1, 1), :])
        dm = dm_ref[...]
        dga = dm * y_a * ga * (1.0 - ga)
        dgb = dm * y_b * gb * (1.0 - gb)
        dpb_ref[:, pl.ds(2 * D, D)] = dga.astype(BF)
        dpb_ref[:, pl.ds(3 * D, D)] = dgb.astype(BF)
        _acc(dbg_ref.at[pl.ds(0, 1), :], jnp.sum(dga, axis=0, keepdims=True), first)
        _acc(dbg_ref.at[pl.ds(1, 1), :], jnp.sum(dgb, axis=0, keepdims=True), first)
        dya_ref[...] = (dm * ga).astype(BF)
        dyb_ref[...] = (dm * gb).astype(BF)
        dln = _dot_nt(dya_ref[...], wco_ref[...]) * _dsilu(ln_a, sig_a)
        dc, dlag, dlab = _ln_bwd(xh_a, rstd_a, lag_ref[...], dln)
        dc_ref[...] = dc
        _acc(dlag_ref, dlag, first)
        _acc(dlab_ref, dlab, first)
        dsg = _dot_nt(dyb_ref[...], wso_ref[...])
        dpb_ref[:, pl.ds(0, D)] = (dsg * z * _dgelu(bu, tu)).astype(BF)
        dz = dsg * u
        z_scr[...] = dz
        mask = _tril_mask()

        @pl.when(first)
        def _():
            dwm_ref[...] = jnp.zeros_like(dwm_ref)
            dbz_ref[...] = jnp.zeros_like(dbz_ref)

        for gi in range(SGU_GROUPS):
            wm = jnp.where(mask, wm_ref[gi], 0.0).astype(BF)
            cols = pl.ds(gi * LANES, LANES)
            for n in range(tm // LANES):
                rows = pl.ds(n * LANES, LANES)
                dzb = z_scr[rows, cols].astype(BF)
                dv_scr[rows, cols] = lax.dot_general(wm, dzb, (((0,), (0,)), ((), ())), preferred_element_type=F32)
                dw = lax.dot_general(dzb, v_scr[rows, cols], (((1,), (1,)), ((), ())), preferred_element_type=F32)
                dwm_ref[gi] += jnp.where(mask, dw, 0.0)
                dbz_ref[:, cols] += z_scr[rows, cols]
        dgv, dlsg, dlsb = _ln_bwd(vh, rstd_v, lsg_ref[...], dv_scr[...])
        _acc(dlsg_ref, dlsg, first)
        _acc(dlsb_ref, dlsb, first)
        dpb_ref[:, pl.ds(D, D)] = (dgv * _dgelu(bv, tv)).astype(BF)

    tile = lambda j: pl.BlockSpec((tm, D), lambda i: (i, j))
    vec = pl.BlockSpec((1, D), lambda i: (0, 0))
    return _pcall(
        body, (c, p, p, p, p, dmerged, prm["w_co"], prm["w_so"], prm["la_g"], prm["la_b"], prm["ls_g"], prm["ls_b"],
               prm["sgu_w"], prm["bz"], prm["b_gate"]),
        grid=(T // tm,),
        in_specs=[tile(0), tile(2), tile(3), tile(4), tile(5), tile(0), _const((D, D)), _const((D, D)),
                  _const((1, D)), _const((1, D)), _const((1, D)), _const((1, D)),
                  _const((SGU_GROUPS, LANES, LANES)), _const((LANES, D)), _const((2, D))],
        out_specs=[tile(0), pl.BlockSpec((tm, 4 * D), lambda i: (i, 0)), tile(0), tile(0),
                   pl.BlockSpec((SGU_GROUPS, LANES, LANES), lambda i: (0, 0, 0)),
                   pl.BlockSpec((LANES, D), lambda i: (0, 0)), vec, vec, vec, vec,
                   pl.BlockSpec((2, D), lambda i: (0, 0))],
        out_shape=[_sds((T, D), F32), _sds((T, 4 * D), BF), _sds((T, D), BF), _sds((T, D), BF),
                   _sds((SGU_GROUPS, LANES, LANES), F32), _sds((LANES, D), F32),
                   _sds((1, D), F32), _sds((1, D), F32), _sds((1, D), F32), _sds((1, D), F32), _sds((2, D), F32)],
        scratch_shapes=[pltpu.VMEM((tm, D), F32), pltpu.VMEM((tm, D), BF), pltpu.VMEM((tm, D), BF),
                        pltpu.VMEM((tm, D), BF), pltpu.VMEM((tm, D), F32)],
        name="branch_bwd", vmem=56, comm=comm)


def _conv_bwd(p, dc, conv_w, B, S, comm=None):
    K, D = conv_w.shape
    nc = D // LANES

    R = CONV_ROWS_BWD

    def body(av_ref, ag_ref, dc_ref, w_ref, dav_ref, dag_ref, dw_ref, db_ref, apad, dpad):
        b = pl.program_id(1)
        av = av_ref[...]
        sg = _sig(ag_ref[...])
        apad[pl.ds(0, CONV_PAD), :] = jnp.zeros((CONV_PAD, LANES), F32)
        apad[pl.ds(CONV_PAD, S), :] = av * sg
        dpad[pl.ds(S, CONV_PAD), :] = jnp.zeros((CONV_PAD, LANES), F32)
        dpad[pl.ds(0, S), :] = dc_ref[...]

        @pl.when(b == 0)
        def _():
            dw_ref[...] = jnp.zeros_like(dw_ref)
            db_ref[...] = jnp.zeros_like(db_ref)

        db_ref[...] += jnp.sum(dc_ref[...], axis=0, keepdims=True)
        for k in range(K):
            tot = jnp.zeros((1, LANES), F32)
            for r0 in range(0, S, R):
                tot = tot + jnp.sum(dpad[pl.ds(r0, R), :] * apad[pl.ds(r0 + k + CONV_PAD - (K - 1), R), :],
                                    axis=0, keepdims=True)
            dw_ref[pl.ds(k, 1), :] += tot
        for r0 in range(0, S, R):
            da = jnp.zeros((R, LANES), F32)
            for k in range(K):
                da = da + w_ref[pl.ds(k, 1), :] * dpad[pl.ds(r0 + (K - 1) - k, R), :]
            rows = pl.ds(r0, R)
            s = sg[r0:r0 + R, :]
            a_v = av[r0:r0 + R, :]
            dav_ref[rows, :] = (da * s).astype(BF)
            dag_ref[rows, :] = (da * a_v * s * (1.0 - s)).astype(BF)

    blk = lambda off: pl.BlockSpec((S, LANES), lambda j, b: (b, off + j))
    return _pcall(
        body, (p, p, dc, conv_w), grid=(nc, B),
        in_specs=[blk(0), blk(nc), blk(0), pl.BlockSpec((K, LANES), lambda j, b: (0, j))],
        out_specs=[blk(0), blk(0), pl.BlockSpec((K, LANES), lambda j, b: (0, j)), pl.BlockSpec((1, LANES), lambda j, b: (0, j))],
        out_shape=[_sds((B * S, D), BF), _sds((B * S, D), BF), _sds((K, D), F32), _sds((1, D), F32)],
        scratch_shapes=[pltpu.VMEM((S + CONV_PAD, LANES), F32), pltpu.VMEM((S + CONV_PAD, LANES), F32)],
        name="conv_bwd", vmem=32, comm=comm)


TM1 = 512


def _in_proj_bwd(x, dx1, dps, g1, w_in, tm=TM1, comm=None):
    T, D = x.shape
    N = w_in.shape[1]
    widths = [d.shape[1] for d in dps]

    def body(x_ref, dx1_ref, *refs):
        dp_refs, (g_ref, w_ref, dx_ref, dg_ref) = refs[:len(dps)], refs[len(dps):]
        x_t = x_ref[...]
        _, r = _rms(x_t, g_ref[...])
        dh = jnp.zeros((tm, D), F32)
        for q, dp_ref in enumerate(dp_refs):
            cols = pl.ds(sum(widths[:q]), widths[q])
            dh = dh + lax.dot_general(dp_ref[...], w_ref[:, cols], (((1,), (1,)), ((), ())), preferred_element_type=F32)
        dxa, dg = _rms_bwd(x_t, g_ref[...], r, dh)
        dx_ref[...] = dx1_ref[...] + dxa
        _acc(dg_ref, dg, pl.program_id(0) == 0)

    tile = pl.BlockSpec((tm, D), lambda i: (i, 0))
    return _pcall(
        body, (x, dx1, *dps, g1, w_in), grid=(T // tm,),
        in_specs=[tile, tile] + [pl.BlockSpec((tm, w), lambda i: (i, 0)) for w in widths] + [_const((1, D)), _const((D, N))],
        out_specs=[tile, pl.BlockSpec((1, D), lambda i: (0, 0))],
        out_shape=[_sds((T, D), F32), _sds((1, D), F32)],
        name="in_proj_bwd", vmem=48, comm=comm)


def _pick(n, cands):
    for c in cands:
        if n % c == 0:
            return c
    raise ValueError(f"no tile of {cands} divides {n}")


def _mm_tn(x, dys, name, comm=None):
    T, K = x.shape
    dys = list(dys) if isinstance(dys, (list, tuple)) else [dys]
    widths = [d.shape[1] for d in dys]
    N = sum(widths)
    tm = _pick(T, (2048, 1024, 512, 256))
    tk = _pick(K, (1024, 1408, 512))
    tn = _pick(math.gcd(*widths), (1024, 1408, 512))
    nt = T // tm
    first = [sum(widths[:q]) // tn for q in range(len(dys))]
    count = [w // tn for w in widths]

    def body(x_ref, *refs):
        dy_refs, (o_ref, ob_ref, acc) = refs[:len(dys)], refs[len(dys):]
        j, t = pl.program_id(1), pl.program_id(2)

        @pl.when(t == 0)
        def _():
            acc[...] = jnp.zeros_like(acc)

        for q, dy_ref in enumerate(dy_refs):
            @pl.when(jnp.logical_and(j >= first[q], j < first[q] + count[q]))
            def _():
                acc[...] += _dot_tn(x_ref[...], dy_ref[...])

        @pl.when(t == nt - 1)
        def _():
            o_ref[...] = acc[...]
            ob_ref[...] = acc[...].astype(BF)

    def dy_spec(q):
        def index(i, j, t):
            mine = jnp.logical_and(j >= first[q], j < first[q] + count[q])
            return jnp.where(mine, t, 0), jnp.clip(j - first[q], 0, count[q] - 1)
        return pl.BlockSpec((tm, tn), index)

    res = _pcall(
        body, (x, *dys), grid=(K // tk, N // tn, nt),
        in_specs=[pl.BlockSpec((tm, tk), lambda i, j, t: (t, i))] + [dy_spec(q) for q in range(len(dys))],
        out_specs=[pl.BlockSpec((tk, tn), lambda i, j, t: (i, j)), pl.BlockSpec((tk, tn), lambda i, j, t: (i, j))],
        out_shape=[_sds((K, N), F32), _sds((K, N), BF)],
        scratch_shapes=[pltpu.VMEM((tk, tn), F32)],
        name=name, vmem=60, comm=comm)
    return res[0] if comm is None else res


def _row_tile(R):
    return _pick(R, (128, 64, 32, 16, 8)) if R % 8 == 0 else R


def _sum_landed(full, land, kind, chip, name):
    _, R, C = land.shape
    tr = _row_tile(R)
    nb = R // tr

    def body(k_ref, o_ref, l_ref, s_ref):
        s_ref[...] = ((o_ref[...] + l_ref[0].astype(F32)) + l_ref[1].astype(F32)) + l_ref[2].astype(F32)

    own = (pl.BlockSpec((tr, C), lambda i, k: (k[0] * nb + i, 0)) if kind == "row"
           else pl.BlockSpec((tr, C), lambda i, k: (i, k[0])))
    return pl.pallas_call(
        body,
        grid_spec=pltpu.PrefetchScalarGridSpec(
            num_scalar_prefetch=1, grid=(nb,),
            in_specs=[own, pl.BlockSpec((3, tr, C), lambda i, k: (0, i, 0))],
            out_specs=pl.BlockSpec((tr, C), lambda i, k: (i, 0))),
        out_shape=_sds((R, C), F32), name=name, compiler_params=_cp(32))(chip, full, land)


def _adamw(g, w, m, v):
    m = ADAM_B1 * m + (1.0 - ADAM_B1) * g
    v = ADAM_B2 * v + (1.0 - ADAM_B2) * (g * g)
    m_hat = m / (1.0 - ADAM_B1 ** ADAM_STEP)
    v_hat = v / (1.0 - ADAM_B2 ** ADAM_STEP)
    return -ADAM_LR * (m_hat / (jnp.sqrt(v_hat) + ADAM_EPS) + ADAM_WD * w), m, v


def _update(parts, w, m, v, name):
    R, C = w.shape
    tr = _row_tile(R)
    k = len(parts)

    def body(*refs):
        g = refs[0][...]
        for r in refs[1:k]:
            g = g + r[...]
        w_ref, m_ref, v_ref, g_out, d_out, m_out, v_out = refs[k:]
        d, m_new, v_new = _adamw(g, w_ref[...], m_ref[...], v_ref[...])
        g_out[...] = g
        d_out[...] = d
        m_out[...] = m_new
        v_out[...] = v_new

    blk = pl.BlockSpec((tr, C), lambda i: (i, 0))
    return pl.pallas_call(
        body, grid=(R // tr,), in_specs=[blk] * (k + 3), out_specs=[blk] * 4, out_shape=[_sds((R, C), F32)] * 4,
        name=name, compiler_params=_cp(40))(*parts, w, m, v)


VECS = ("norm_mix", "conv_b", "conv_ln_g", "conv_ln_b", "sgu_ln_g", "sgu_ln_b", "norm_xattn", "norm_mem", "norm_ffn",
        "norm_final")
SMALL = VECS + ("b_gate", "conv_w", "sgu_w", "sgu_b")
SUBLANES = 8


def _pad_rows(arr):
    return jnp.pad(arr, ((0, (-arr.shape[0]) % SUBLANES), (0, 0)))


def _update_small(land_vec, land_w, land_g1, chip, a):
    _, R, D = land_vec.shape
    Dq, G, K, nv = D // N_CHIPS, SGU_GROUPS, a["conv_w"].shape[1], len(VECS)

    def as2d(nm, arr):
        if nm in VECS:
            return arr.reshape(1, D)
        if nm == "sgu_w":
            return arr.reshape(G * LANES, LANES)
        return arr.reshape(G, LANES) if nm == "sgu_b" else arr[0]

    params = [as2d(nm, a[pre + nm]) for nm in SMALL for pre in ("", "m_", "v_")]

    def body(k_ref, lv, lvc, lw, l1, *refs):
        prm = refs[:3 * len(SMALL)]
        outs = refs[3 * len(SMALL):7 * len(SMALL)]
        tv, tvc, tw, t1 = refs[7 * len(SMALL):]
        for land, tot in ((lv, tv), (lvc, tvc), (lw, tw), (l1, t1)):
            acc = land[0]
            for dev in range(1, N_DEV):
                acc = acc + land[dev]
            tot[...] = acc
        at = SUBLANES * nv
        grads = [t1[pl.ds(0, 1), :] if nm == "norm_mix" else tv[pl.ds(SUBLANES * i, 1), :]
                 for i, nm in enumerate(VECS)]
        grads += [tvc[pl.ds(at, 2), :], tvc[pl.ds(at + SUBLANES, K), :],
                  tw[pl.ds(0, G * LANES), :], tw[pl.ds(G * LANES, G), :]]
        for i, g in enumerate(grads):
            d, m_new, v_new = _adamw(g, prm[3 * i][...], prm[3 * i + 1][...], prm[3 * i + 2][...])
            for o_ref, val in zip(outs[4 * i:4 * i + 4], (g, d, m_new, v_new)):
                o_ref[...] = val

    whole = lambda shape: pl.BlockSpec(tuple(shape), lambda i, k: (0,) * len(shape))
    res = pl.pallas_call(
        body,
        grid_spec=pltpu.PrefetchScalarGridSpec(
            num_scalar_prefetch=1, grid=(1,),
            in_specs=[whole(land_vec.shape), pl.BlockSpec((N_DEV, R, Dq), lambda i, k: (0, 0, k[0])),
                      whole(land_w.shape), whole(land_g1.shape)] + [whole(p.shape) for p in params],
            out_specs=[whole(params[3 * i].shape) for i in range(len(SMALL)) for _ in range(4)],
            scratch_shapes=[pltpu.VMEM((R, D), F32), pltpu.VMEM((R, Dq), F32), pltpu.VMEM(land_w.shape[1:], F32),
                            pltpu.VMEM(land_g1.shape[1:], F32)]),
        out_shape=[_sds(params[3 * i].shape, F32) for i in range(len(SMALL)) for _ in range(4)],
        name="upd_small", compiler_params=_cp(40))(chip, land_vec, land_vec, land_w, land_g1, *params)
    return {nm: list(res[4 * i:4 * i + 4]) for i, nm in enumerate(SMALL)}


BIG = ("w_in", "w_conv_out", "w_sgu_out", "w_mix_out", "w_q", "w_kv", "w_xo", "w_gu", "w_down")
BIG_KIND = {"w_in": "col", "w_conv_out": "row", "w_sgu_out": "row", "w_mix_out": "row", "w_q": "row",
            "w_kv": "col", "w_xo": "row", "w_gu": "col", "w_down": "row"}

def _step(a):
    x3d, mem3d, tgt3d = a["x"], a["mem"], a["loss_target"]
    B, S, D = x3d.shape
    M = mem3d.shape[1]
    T = B * S
    x = x3d.reshape(T, D)
    mem = mem3d.reshape(B * M, D)
    tgt = tgt3d.reshape(T, D)
    xi, yi = lax.axis_index("x"), lax.axis_index("y")
    chip = 2 * xi + yi

    def gather(names):
        return _Gather([a[nm][0] if nm in ("b_gate", "conv_w") else a[nm][0].astype(BF) for nm in names],
                       [BIG_KIND.get(nm, "col") for nm in names])

    first = ("w_in", "b_gate", "conv_w")
    on_in_proj = ("w_conv_out", "w_sgu_out", "w_kv", "w_mix_out", "w_q", "w_xo")
    full = dict(zip(first, _comm_call(gather(first), "gather_w_in")))
    (p, h1), got = _in_proj(x, a["norm_mix"], full["w_in"], comm=gather(on_in_proj))
    full.update(zip(on_in_proj, got))
    (c,), got = _conv_fwd(p, full["conv_w"], a["conv_b"], B, S, comm=gather(("w_down",)))
    full["w_down"] = got[0]

    sgu_b = a["sgu_b"][0]
    bz = jnp.repeat(jnp.transpose(sgu_b), LANES, axis=1)
    prm = dict(w_co=full["w_conv_out"], w_so=full["w_sgu_out"], w_mo=full["w_mix_out"], w_q=full["w_q"],
               w_xo=full["w_xo"], w_down=full["w_down"],
               la_g=a["conv_ln_g"], la_b=a["conv_ln_b"], ls_g=a["sgu_ln_g"], ls_b=a["sgu_ln_b"],
               sgu_w=a["sgu_w"][0], bz=bz, b_gate=full["b_gate"], g2=a["norm_xattn"], g3=a["norm_ffn"],
               gf=a["norm_final"].reshape(1, D))

    (merged, s_a, sg), got = _branch_fwd(c, p, prm, comm=gather(("w_gu",)))
    prm["w_gu"] = got[0]
    mem_n, kv = _kv_fwd(mem, a["norm_mem"], full["w_kv"], B, M)
    x1, x2, h2, o = _attn_fwd(x, merged, kv, prm, S, M)
    dx2, dx3, dgu, h3, f, lsum, d_g3, d_gf = _ffn_loss(x2, tgt, prm)
    loss = lax.psum(0.5 * jnp.sum(lsum) / D, ("x", "y", "c"))

    size_of = {nm: a[nm].shape[1] if BIG_KIND[nm] == "row" else a[nm].shape[2] for nm in BIG}
    landed = {}

    def scatter(names):
        return _Scatter([gw[nm][1] for nm in names], [BIG_KIND[nm] for nm in names], [size_of[nm] for nm in names])

    gw = {}
    gw["w_down"] = _mm_tn(f, dx3, "dw_down")
    gw["w_gu"] = _mm_tn(h3, dgu, "dw_gu")
    (dx1, dmerged, dq, dkv, d_g2), got = _attn_bwd(x1, kv, dx2, prm, S, M, comm=scatter(("w_gu",)))
    landed["w_gu"] = got[0]
    gw["w_xo"] = _mm_tn(o, dx2, "dw_xo")
    gw["w_q"] = _mm_tn(h2, dq, "dw_q")
    gw["w_kv"] = _mm_tn(mem_n, dkv, "dw_kv")
    d_gm = _kv_bwd(mem, a["norm_mem"], full["w_kv"], dkv, B, M)
    gw["w_mix_out"] = _mm_tn(merged, dx1, "dw_mix_out")
    group = ("w_down", "w_xo", "w_q", "w_kv", "w_mix_out")
    (dc, dpb, dya, dyb, d_wm, d_bz, d_lag, d_lab, d_lsg, d_lsb, d_bg), got = _branch_bwd(
        c, p, dmerged, prm, comm=scatter(group))
    landed.update(zip(group, got))
    gw["w_conv_out"] = _mm_tn(s_a, dya, "dw_conv_out")
    gw["w_sgu_out"] = _mm_tn(sg, dyb, "dw_sgu_out")
    group = ("w_conv_out", "w_sgu_out")
    (dav, dag, d_cw, d_cb), got = _conv_bwd(p, dc, full["conv_w"], B, S, comm=scatter(group))
    landed.update(zip(group, got))
    dp = [dav, dag, dpb]

    chip_arr = jnp.reshape(chip, (1,)).astype(jnp.int32)
    early = [nm for nm in BIG if nm != "w_in"]
    part = {nm: _sum_landed(gw[nm][0], landed[nm], BIG_KIND[nm], chip_arr, "sum_" + nm) for nm in early}
    G = SGU_GROUPS
    d_sb = jnp.transpose(d_bz.reshape(LANES, G, LANES).sum(axis=-1))
    vec_g = dict(norm_mix=jnp.zeros((1, D), F32), conv_b=d_cb, conv_ln_g=d_lag, conv_ln_b=d_lab, sgu_ln_g=d_lsg,
                 sgu_ln_b=d_lsb, norm_xattn=d_g2, norm_mem=d_gm, norm_ffn=d_g3, norm_final=d_gf)
    g_vec = jnp.concatenate([_pad_rows(vec_g[nm]) for nm in VECS] + [_pad_rows(d_bg), _pad_rows(d_cw)], axis=0)
    g_w = jnp.concatenate([d_wm.reshape(G * LANES, LANES), d_sb], axis=0)
    gw["w_in"], got = _mm_tn(h1, dp, "dw_in", comm=_Both(_Swap([part[nm] for nm in early]), _Spread([g_vec, g_w])))
    other = dict(zip(early, got[:len(early)]))
    land_vec, land_w = got[len(early):]
    kind_in, size_in = BIG_KIND["w_in"], size_of["w_in"]
    send, recv, g_thru, land_thru, token = _scatter_start(gw["w_in"][1], kind_in, size_in, "scatter_w_in_start")
    (grad_x, d_g1), _ = _in_proj_bwd(x, dx1, dp, a["norm_mix"] + token[:1, :1], full["w_in"])

    out = {}

    def update(nm):
        res = _update([part[nm], other[nm]], a[nm][0], a["m_" + nm][0], a["v_" + nm][0], "upd_" + nm)
        out[nm] = [r[None] for r in res]

    for nm in early:
        update(nm)
    after = [grad_x] + [out[nm][1] for nm in early]
    land_in = _scatter_wait(send, recv, g_thru, land_thru, after, kind_in, size_in, "scatter_w_in_wait")
    part["w_in"] = _sum_landed(gw["w_in"][0], land_in, kind_in, chip_arr, "sum_w_in")
    g1 = _pad_rows(d_g1)
    other["w_in"], land_g1 = _comm_call(_Both(_Swap([part["w_in"]]), _Spread([g1])), "swap_w_in")
    update("w_in")
    for nm, res in _update_small(land_vec, land_w, land_g1, chip_arr, a).items():
        out[nm] = [r.reshape(a[nm].shape) for r in res]
    return loss, grad_x.reshape(B, S, D), out


WEIGHTS = ("norm_mix", "w_in", "b_gate", "conv_w", "conv_b", "conv_ln_g", "conv_ln_b", "w_conv_out", "sgu_ln_g",
           "sgu_ln_b", "sgu_w", "sgu_b", "w_sgu_out", "w_mix_out", "norm_xattn", "norm_mem", "w_q", "w_kv", "w_xo",
           "norm_ffn", "w_gu", "w_down", "norm_final")


def kernel(x, mem, norm_mix, w_in, b_gate, conv_w, conv_b, conv_ln_g, conv_ln_b, w_conv_out, sgu_ln_g, sgu_ln_b, sgu_w, sgu_b, w_sgu_out, w_mix_out, norm_xattn, norm_mem, w_q, w_kv, w_xo, norm_ffn, w_gu, w_down, norm_final, loss_target, m_norm_mix, m_w_in, m_b_gate, m_conv_w, m_conv_b, m_conv_ln_g, m_conv_ln_b, m_w_conv_out, m_sgu_ln_g, m_sgu_ln_b, m_sgu_w, m_sgu_b, m_w_sgu_out, m_w_mix_out, m_norm_xattn, m_norm_mem, m_w_q, m_w_kv, m_w_xo, m_norm_ffn, m_w_gu, m_w_down, m_norm_final, v_norm_mix, v_w_in, v_b_gate, v_conv_w, v_conv_b, v_conv_ln_g, v_conv_ln_b, v_w_conv_out, v_sgu_ln_g, v_sgu_ln_b, v_sgu_w, v_sgu_b, v_w_sgu_out, v_w_mix_out, v_norm_xattn, v_norm_mem, v_w_q, v_w_kv, v_w_xo, v_norm_ffn, v_w_gu, v_w_down, v_norm_final):
    a = dict(locals())
    loss, grad_x, out = _step(a)
    res = [loss, grad_x]
    for q in range(4):
        res += [out[nm][q] for nm in WEIGHTS]
    return tuple(res)
```

```python
import functools
import math

import jax
import jax.numpy as jnp
from jax import lax
from jax.experimental import pallas as pl
from jax.experimental.pallas import tpu as pltpu

BF = jnp.bfloat16
F32 = jnp.float32
MESH = pl.DeviceIdType.MESH
ANY = pl.BlockSpec(memory_space=pl.ANY)

RMS_EPS = 1e-6
LN_EPS = 1e-5
HEADS = 4
SGU_GROUPS = 8
LANES = 128
ADAM_LR = 0.001
ADAM_B1 = 0.9
ADAM_B2 = 0.999
ADAM_EPS = 1e-08
ADAM_WD = 0.01
ADAM_STEP = 10
N_CHIPS = 4
N_DEV = 8
MIB = 1024 * 1024


def _sds(shape, dtype):
    return jax.ShapeDtypeStruct(tuple(shape), dtype)


def _cp(vmem_mib):
    return pltpu.CompilerParams(vmem_limit_bytes=vmem_mib * MIB)


def _const(shape):
    nd = len(shape)
    return pl.BlockSpec(tuple(shape), lambda *_: (0,) * nd, pipeline_mode=pl.Buffered(1))


def _dot(a, b):
    return jnp.dot(a.astype(BF), b.astype(BF), preferred_element_type=F32)


def _dot_nt(a, b):
    return lax.dot_general(a.astype(BF), b.astype(BF), (((1,), (1,)), ((), ())), preferred_element_type=F32)


def _dot_tn(a, b):
    return lax.dot_general(a.astype(BF), b.astype(BF), (((0,), (0,)), ((), ())), preferred_element_type=F32)


def _sig(x):
    return 1.0 / (1.0 + jnp.exp(-x))


def _dsilu(x, s):
    return s * (1.0 + x * (1.0 - s))


_GELU_C = math.sqrt(2.0 / math.pi)


def _gelu(x):
    x2 = x * x
    t = jnp.tanh((_GELU_C * x) * (1.0 + 0.044715 * x2))
    cdf = 0.5 * (1.0 + t)
    return x * cdf, (t, x2, cdf)


def _dgelu(x, shared):
    t, x2, cdf = shared
    return cdf + ((0.5 * _GELU_C) * x) * (1.0 - t * t) * (1.0 + (3.0 * 0.044715) * x2)


def _rms(x, g):
    r = lax.rsqrt(jnp.mean(x * x, axis=-1, keepdims=True) + RMS_EPS)
    return x * r * g, r


def _rms_bwd(x, g, r, dh):
    xr = x * r
    dxh = dh * g
    dx = r * (dxh - xr * jnp.mean(dxh * xr, axis=-1, keepdims=True))
    return dx, jnp.sum(dh * xr, axis=0, keepdims=True)


def _ln(x, g, b):
    mu = jnp.mean(x, axis=-1, keepdims=True)
    xc = x - mu
    rstd = lax.rsqrt(jnp.mean(xc * xc, axis=-1, keepdims=True) + LN_EPS)
    xh = xc * rstd
    return xh * g + b, xh, rstd


def _ln_bwd(xh, rstd, g, dy):
    dxh = dy * g
    dx = rstd * (dxh - jnp.mean(dxh, axis=-1, keepdims=True) - xh * jnp.mean(dxh * xh, axis=-1, keepdims=True))
    return dx, jnp.sum(dy * xh, axis=0, keepdims=True), jnp.sum(dy, axis=0, keepdims=True)


def _acc(ref, val, first):
    @pl.when(first)
    def _():
        ref[...] = val

    @pl.when(jnp.logical_not(first))
    def _():
        ref[...] += val


def _place():
    x, y, c = lax.axis_index("x"), lax.axis_index("y"), lax.axis_index("c")
    chips = [(1 - x, y), (x, 1 - y), (1 - x, 1 - y)]
    return x, y, c, chips


def _shard_of(ref, kind, k, n):
    if kind == "row":
        return ref.at[pl.ds(k * n, n), :]
    return ref.at[:, pl.ds(k * n, n)]


class _Gather:
    def __init__(self, shards, kinds):
        n = len(shards)
        self.srcs, self.kinds = list(shards), list(kinds)
        self.sizes = [s.shape[0] if kd == "row" else s.shape[1] for s, kd in zip(shards, kinds)]
        self.halves = [s.shape[0] // 2 if s.shape[0] % 32 == 0 else None for s in shards]
        self.out_shape = [
            _sds((s.shape[0] * N_CHIPS, s.shape[1]) if kd == "row" else (s.shape[0], s.shape[1] * N_CHIPS), s.dtype)
            for s, kd in zip(shards, kinds)]
        dma = pltpu.SemaphoreType.DMA
        self.sems = [dma((3 * n,)), dma((3 * n,)), dma((n,)), dma((3 * n,)), dma((3 * n,))]

    def _part(self, ref, t, core):
        h = self.halves[t]
        return ref if h is None else ref.at[pl.ds(core * h, h), :]

    def _copies(self, ins, outs, send, recv, loc, fsend, frecv):
        x, y, c, chips = _place()
        k = 2 * x + y
        local, remote = [], []
        for t in range(len(ins)):
            block = lambda q: _shard_of(outs[t], self.kinds[t], q, self.sizes[t])
            local.append(pltpu.make_async_copy(ins[t], block(k), loc.at[t]))
            for j, (px, py) in enumerate(chips):
                sems = dict(send_sem=send.at[3 * t + j], recv_sem=recv.at[3 * t + j])
                there = dict(device_id=(px, py, c), device_id_type=MESH)
                sent = pltpu.make_async_remote_copy(
                    src_ref=self._part(ins[t], t, c), dst_ref=self._part(block(k), t, c), **sems, **there)
                got = self._part(block(2 * px + py), t, c)
                landed = pltpu.make_async_remote_copy(src_ref=self._part(ins[t], t, c), dst_ref=got, **sems, **there)
                passed = handed = None
                if self.halves[t] is not None:
                    fsems = dict(send_sem=fsend.at[3 * t + j], recv_sem=frecv.at[3 * t + j])
                    sibling = dict(device_id=(x, y, 1 - c), device_id_type=MESH)
                    passed = pltpu.make_async_remote_copy(src_ref=got, dst_ref=got, **fsems, **sibling)
                    other = self._part(block(2 * px + py), t, 1 - c)
                    handed = pltpu.make_async_remote_copy(src_ref=got, dst_ref=other, **fsems, **sibling)
                remote.append((sent, landed, passed, handed))
        return local, remote

    def start(self, ins, outs, *sems):
        local, remote = self._copies(ins, outs, *sems)
        for cp in local:
            cp.start()
        for sent, _, _, _ in remote:
            sent.start()

    def wait(self, ins, outs, *sems):
        local, remote = self._copies(ins, outs, *sems)
        for sent, landed, passed, handed in remote:
            landed.wait_recv()
            if passed is not None:
                passed.start()
        for sent, landed, passed, handed in remote:
            if passed is not None:
                handed.wait_recv()
                passed.wait_send()
            sent.wait_send()
        for cp in local:
            cp.wait()


class _Scatter:
    def __init__(self, grads, kinds, sizes):
        n = len(grads)
        self.srcs, self.kinds, self.sizes = list(grads), list(kinds), list(sizes)
        self.out_shape = [_sds((3,) + ((sz, g.shape[1]) if kd == "row" else (g.shape[0], sz)), g.dtype)
                          for g, kd, sz in zip(grads, kinds, sizes)]
        self.sems = [pltpu.SemaphoreType.DMA((3 * n,)), pltpu.SemaphoreType.DMA((3 * n,))]

    def _copies(self, ins, outs, send, recv):
        x, y, c, chips = _place()
        return [pltpu.make_async_remote_copy(
            src_ref=_shard_of(ins[t], self.kinds[t], 2 * px + py, self.sizes[t]), dst_ref=outs[t].at[j],
            send_sem=send.at[3 * t + j], recv_sem=recv.at[3 * t + j], device_id=(px, py, c), device_id_type=MESH)
            for t in range(len(ins)) for j, (px, py) in enumerate(chips)]

    def start(self, ins, outs, send, recv):
        for cp in self._copies(ins, outs, send, recv):
            cp.start()

    def wait(self, ins, outs, send, recv):
        for cp in self._copies(ins, outs, send, recv):
            cp.wait_recv()
            cp.wait_send()


class _Swap:
    def __init__(self, parts):
        n = len(parts)
        self.srcs = list(parts)
        self.out_shape = [_sds(p.shape, p.dtype) for p in parts]
        self.sems = [pltpu.SemaphoreType.DMA((n,)), pltpu.SemaphoreType.DMA((n,))]

    def _copies(self, ins, outs, send, recv):
        x, y, c, _ = _place()
        return [pltpu.make_async_remote_copy(
            src_ref=ins[t], dst_ref=outs[t], send_sem=send.at[t], recv_sem=recv.at[t],
            device_id=(x, y, 1 - c), device_id_type=MESH) for t in range(len(ins))]

    def start(self, ins, outs, send, recv):
        for cp in self._copies(ins, outs, send, recv):
            cp.start()

    def wait(self, ins, outs, send, recv):
        for cp in self._copies(ins, outs, send, recv):
            cp.wait_recv()
            cp.wait_send()


class _Spread:
    def __init__(self, packs):
        n = len(packs)
        self.srcs = list(packs)
        self.out_shape = [_sds((N_DEV,) + p.shape, p.dtype) for p in packs]
        dma = pltpu.SemaphoreType.DMA
        self.sems = [dma((N_DEV * n,)), dma((N_DEV * n,)), dma((n,))]

    def _copies(self, ins, outs, send, recv, loc):
        x, y, c, _ = _place()
        me = 4 * x + 2 * y + c
        local = [pltpu.make_async_copy(ins[t], outs[t].at[me], loc.at[t]) for t in range(len(ins))]
        remote = []
        for t in range(len(ins)):
            for mask in range(1, N_DEV):
                peer = ((1 - x) if mask & 4 else x, (1 - y) if mask & 2 else y, (1 - c) if mask & 1 else c)
                src = peer[0] * 4 + peer[1] * 2 + peer[2]
                sems = dict(send_sem=send.at[N_DEV * t + mask], recv_sem=recv.at[N_DEV * t + mask])
                sent = pltpu.make_async_remote_copy(
                    src_ref=ins[t], dst_ref=outs[t].at[me], device_id=peer, device_id_type=MESH, **sems)
                landed = pltpu.make_async_remote_copy(
                    src_ref=ins[t], dst_ref=outs[t].at[src], device_id=peer, device_id_type=MESH, **sems)
                remote.append((sent, landed))
        return local, remote

    def start(self, ins, outs, send, recv, loc):
        local, remote = self._copies(ins, outs, send, recv, loc)
        for cp in local:
            cp.start()
        for sent, _ in remote:
            sent.start()

    def wait(self, ins, outs, send, recv, loc):
        local, remote = self._copies(ins, outs, send, recv, loc)
        for sent, landed in remote:
            landed.wait_recv()
            sent.wait_send()
        for cp in local:
            cp.wait()


class _Both:
    def __init__(self, *comms):
        self.comms = comms
        self.srcs = [s for cm in comms for s in cm.srcs]
        self.out_shape = [s for cm in comms for s in cm.out_shape]
        self.sems = [s for cm in comms for s in cm.sems]

    def _each(self, ins, outs, sems):
        i = o = k = 0
        for cm in self.comms:
            ni, no, nk = len(cm.srcs), len(cm.out_shape), len(cm.sems)
            yield cm, ins[i:i + ni], outs[o:o + no], sems[k:k + nk]
            i, o, k = i + ni, o + no, k + nk

    def start(self, ins, outs, *sems):
        for cm, i, o, s in self._each(ins, outs, sems):
            cm.start(i, o, *s)

    def wait(self, ins, outs, *sems):
        for cm, i, o, s in self._each(ins, outs, sems):
            cm.wait(i, o, *s)


HBM = pl.BlockSpec(memory_space=pltpu.HBM)
SEM = pl.BlockSpec(memory_space=pltpu.SEMAPHORE)
EFFECT = pltpu.SideEffectType.DATAFLOW_SIDE_EFFECTING


def _scatter_copies(g_ref, land_ref, send, recv, kind, size):
    x, y, c, chips = _place()
    return [pltpu.make_async_remote_copy(
        src_ref=_shard_of(g_ref, kind, 2 * px + py, size), dst_ref=land_ref.at[j],
        send_sem=send.at[j], recv_sem=recv.at[j], device_id=(px, py, c), device_id_type=MESH)
        for j, (px, py) in enumerate(chips)]


def _scatter_start(grad, kind, size, name):
    land = (3,) + ((size, grad.shape[1]) if kind == "row" else (grad.shape[0], size))

    def body(g_ref, land_ref, send, recv, g_thru, land_thru, token):
        for cp in _scatter_copies(g_ref, land_ref, send, recv, kind, size):
            cp.start()
        token[...] = jnp.zeros_like(token)

    return pl.pallas_call(
        body, name=name,
        out_shape=(pltpu.SemaphoreType.DMA((3,)), pltpu.SemaphoreType.DMA((3,)), pltpu.HBM(grad.shape, grad.dtype),
                   pltpu.HBM(land, grad.dtype), _sds((8, LANES), F32)),
        in_specs=(HBM, HBM), out_specs=(SEM, SEM, HBM, HBM, pl.BlockSpec(memory_space=pltpu.VMEM)),
        input_output_aliases={0: 2, 1: 3},
        compiler_params=pltpu.CompilerParams(has_side_effects=EFFECT))(
            pltpu.with_memory_space_constraint(grad, pltpu.HBM),
            pltpu.with_memory_space_constraint(lax.empty(land, grad.dtype), pltpu.HBM))


def _scatter_wait(send, recv, g_thru, land_thru, after, kind, size, name):
    def body(g_ref, land_ref, send, recv, *rest):
        for cp in _scatter_copies(g_ref, land_ref, send, recv, kind, size):
            cp.wait_send()
            cp.wait_recv()

    return pl.pallas_call(
        body, name=name,
        out_shape=(pltpu.HBM(g_thru.shape, g_thru.dtype), pltpu.HBM(land_thru.shape, land_thru.dtype)),
        in_specs=(HBM, HBM, SEM, SEM) + (ANY,) * len(after), out_specs=(HBM, HBM),
        input_output_aliases={0: 0, 1: 1},
        compiler_params=pltpu.CompilerParams(has_side_effects=EFFECT))(g_thru, land_thru, send, recv, *after)[1]


def _comm_call(comm, name):
    n, m = len(comm.srcs), len(comm.out_shape)

    def body(*refs):
        comm.start(refs[:n], refs[n:n + m], *refs[n + m:])
        comm.wait(refs[:n], refs[n:n + m], *refs[n + m:])

    return pl.pallas_call(body, in_specs=[ANY] * n, out_specs=[ANY] * m, out_shape=comm.out_shape,
                          scratch_shapes=comm.sems, name=name)(*comm.srcs)


def _pcall(body, args, *, grid, in_specs, out_specs, out_shape, name, vmem, scratch_shapes=(), comm=None):
    in_specs, out_specs, out_shape = list(in_specs), list(out_specs), list(out_shape)
    scratch_shapes = list(scratch_shapes)
    if comm is None:
        res = pl.pallas_call(body, grid=grid, in_specs=in_specs, out_specs=out_specs, out_shape=out_shape,
                             scratch_shapes=scratch_shapes, name=name, compiler_params=_cp(vmem))(*args)
        return list(res), []
    ni, no, ns = len(in_specs), len(out_specs), len(scratch_shapes)
    ci, co = len(comm.srcs), len(comm.out_shape)

    def carried(*refs):
        c_in = refs[ni:ni + ci]
        c_out = refs[ni + ci + no:ni + ci + no + co]
        sems = refs[ni + ci + no + co + ns:]
        ids = [pl.program_id(d) for d in range(len(grid))]
        first = functools.reduce(jnp.logical_and, [i == 0 for i in ids])
        last = functools.reduce(jnp.logical_and, [i == g - 1 for i, g in zip(ids, grid)])

        @pl.when(first)
        def _():
            comm.start(c_in, c_out, *sems)

        body(*refs[:ni], *refs[ni + ci:ni + ci + no], *refs[ni + ci + no + co:ni + ci + no + co + ns])

        @pl.when(last)
        def _():
            comm.wait(c_in, c_out, *sems)

    res = pl.pallas_call(carried, grid=grid, in_specs=in_specs + [ANY] * ci, out_specs=out_specs + [ANY] * co,
                         out_shape=out_shape + list(comm.out_shape), scratch_shapes=scratch_shapes + list(comm.sems),
                         name=name, compiler_params=_cp(vmem))(*args, *comm.srcs)
    return list(res[:no]), list(res[no:])


def _in_proj(x, g1, w_in, comm=None):
    T, D = x.shape
    N = w_in.shape[1]
    tm, tn = 512, 1024

    def body(x_ref, g_ref, w_ref, p_ref, h_ref):
        h, _ = _rms(x_ref[...], g_ref[...])
        h_ref[...] = h.astype(BF)
        for j in range(N // tn):
            cols = pl.ds(j * tn, tn)
            p_ref[:, cols] = jnp.dot(h_ref[...], w_ref[:, cols], preferred_element_type=F32)

    return _pcall(
        body, (x, g1, w_in), grid=(T // tm,),
        in_specs=[pl.BlockSpec((tm, D), lambda i: (i, 0)), _const((1, D)), _const((D, N))],
        out_specs=[pl.BlockSpec((tm, N), lambda i: (i, 0)), pl.BlockSpec((tm, D), lambda i: (i, 0))],
        out_shape=[_sds((T, N), F32), _sds((T, D), BF)],
        name="in_proj", vmem=56, comm=comm)


CONV_PAD = 32
CONV_ROWS = 128
CONV_ROWS_BWD = 64


def _conv_fwd(p, conv_w, conv_b, B, S, comm=None):
    K, D = conv_w.shape
    nc = D // LANES

    def body(av_ref, ag_ref, w_ref, b_ref, c_ref, apad):
        apad[pl.ds(0, CONV_PAD), :] = jnp.zeros((CONV_PAD, LANES), F32)
        apad[pl.ds(CONV_PAD, S), :] = av_ref[...] * _sig(ag_ref[...])
        for r0 in range(0, S, CONV_ROWS):
            acc = jnp.zeros((CONV_ROWS, LANES), F32) + b_ref[...]
            for k in range(K):
                acc = acc + w_ref[pl.ds(k, 1), :] * apad[pl.ds(r0 + k + CONV_PAD - (K - 1), CONV_ROWS), :]
            c_ref[pl.ds(r0, CONV_ROWS), :] = acc

    return _pcall(
        body, (p, p, conv_w, conv_b), grid=(B, nc),
        in_specs=[pl.BlockSpec((S, LANES), lambda b, j: (b, j)), pl.BlockSpec((S, LANES), lambda b, j: (b, nc + j)),
                  pl.BlockSpec((K, LANES), lambda b, j: (0, j)), pl.BlockSpec((1, LANES), lambda b, j: (0, j))],
        out_specs=[pl.BlockSpec((S, LANES), lambda b, j: (b, j))],
        out_shape=[_sds((B * S, D), F32)],
        scratch_shapes=[pltpu.VMEM((S + CONV_PAD, LANES), F32)],
        name="conv_fwd", vmem=32, comm=comm)


def _tril_mask():
    t = lax.broadcasted_iota(jnp.int32, (LANES, LANES), 0)
    s = lax.broadcasted_iota(jnp.int32, (LANES, LANES), 1)
    return t >= s


def _branch_a(c, g, b):
    ln_a, xh, rstd = _ln(c, g, b)
    s = _sig(ln_a)
    return ln_a * s, ln_a, s, xh, rstd


def _branch_b(bu, bv, g, b, wm_ref, bz_ref, z_scr, v_scr):
    tm, D = bu.shape
    u, tu = _gelu(bu)
    gv, tv = _gelu(bv)
    v, vh, rstd = _ln(gv, g, b)
    v_scr[...] = v.astype(BF)
    mask = _tril_mask()
    for gi in range(SGU_GROUPS):
        wm = jnp.where(mask, wm_ref[gi], 0.0).astype(BF)
        cols = pl.ds(gi * LANES, LANES)
        for n in range(tm // LANES):
            rows = pl.ds(n * LANES, LANES)
            z_scr[rows, cols] = jnp.dot(wm, v_scr[rows, cols], preferred_element_type=F32) + bz_ref[:, cols]
    z = z_scr[...]
    return u * z, u, tu, z, tv, vh, rstd


TM3 = 256
TM3_FWD = 512


def _branch_fwd(c, p, prm, tm=TM3_FWD, comm=None):
    T, D = c.shape

    def body(c_ref, bu_ref, bv_ref, ga_ref, gb_ref, wco_ref, wso_ref, lag_ref, lab_ref, lsg_ref, lsb_ref, wm_ref,
             bz_ref, bg_ref, mg_ref, sa_ref, sg_ref, z_scr, v_scr):
        s_a = _branch_a(c_ref[...], lag_ref[...], lab_ref[...])[0]
        sa_ref[...] = s_a.astype(BF)
        y_a = jnp.dot(sa_ref[...], wco_ref[...], preferred_element_type=F32)
        sg = _branch_b(bu_ref[...], bv_ref[...], lsg_ref[...], lsb_ref[...], wm_ref, bz_ref, z_scr, v_scr)[0]
        sg_ref[...] = sg.astype(BF)
        y_b = jnp.dot(sg_ref[...], wso_ref[...], preferred_element_type=F32)
        ga = _sig(ga_ref[...] + bg_ref[pl.ds(0, 1), :])
        gb = _sig(gb_ref[...] + bg_ref[pl.ds(1, 1), :])
        mg_ref[...] = (ga * y_a + gb * y_b).astype(BF)

    tile = lambda j: pl.BlockSpec((tm, D), lambda i: (i, j))
    return _pcall(
        body, (c, p, p, p, p, prm["w_co"], prm["w_so"], prm["la_g"], prm["la_b"], prm["ls_g"], prm["ls_b"],
               prm["sgu_w"], prm["bz"], prm["b_gate"]),
        grid=(T // tm,),
        in_specs=[tile(0), tile(2), tile(3), tile(4), tile(5), _const((D, D)), _const((D, D)),
                  _const((1, D)), _const((1, D)), _const((1, D)), _const((1, D)),
                  _const((SGU_GROUPS, LANES, LANES)), _const((LANES, D)), _const((2, D))],
        out_specs=[tile(0), tile(0), tile(0)],
        out_shape=[_sds((T, D), BF)] * 3,
        scratch_shapes=[pltpu.VMEM((tm, D), F32), pltpu.VMEM((tm, D), BF)],
        name="branch_fwd", vmem=48, comm=comm)


def _kv_fwd(mem, gm, w_kv, B, M):
    D = mem.shape[1]
    N = w_kv.shape[1]

    def body(m_ref, g_ref, w_ref, mn_ref, kv_ref):
        h, _ = _rms(m_ref[...], g_ref[...])
        mn_ref[...] = h.astype(BF)
        kv_ref[...] = jnp.dot(mn_ref[...], w_ref[...], preferred_element_type=F32).astype(BF)

    return pl.pallas_call(
        body, grid=(B,),
        in_specs=[pl.BlockSpec((M, D), lambda b: (b, 0)), _const((1, D)), _const((D, N))],
        out_specs=[pl.BlockSpec((M, D), lambda b: (b, 0)), pl.BlockSpec((M, N), lambda b: (b, 0))],
        out_shape=[_sds((B * M, D), BF), _sds((B * M, N), BF)],
        name="kv_fwd", compiler_params=_cp(32))(mem, gm, w_kv)


def _softmax_rows(s):
    e = jnp.exp(s - jnp.max(s, axis=-1, keepdims=True))
    return e * (1.0 / jnp.sum(e, axis=-1, keepdims=True))


TM4 = 512


def _attn_fwd(x, merged, kv, prm, S, M, tm=TM4):
    T, D = x.shape
    hd = D // HEADS
    scale = hd ** -0.5
    tpb = S // tm

    def body(x_ref, mg_ref, kv_ref, wmo_ref, wq_ref, wxo_ref, g_ref, x1_ref, x2_ref, h2_ref, o_ref):
        x1 = x_ref[...] + jnp.dot(mg_ref[...], wmo_ref[...], preferred_element_type=F32)
        x1_ref[...] = x1
        h2, _ = _rms(x1, g_ref[...])
        h2_ref[...] = h2.astype(BF)
        qb = jnp.dot(h2_ref[...], wq_ref[...], preferred_element_type=F32).astype(BF)
        for h in range(HEADS):
            cs = pl.ds(h * hd, hd)
            s = _dot_nt(qb[:, h * hd:(h + 1) * hd], kv_ref[:, cs]) * scale
            pr = _softmax_rows(s)
            o_ref[:, cs] = _dot(pr, kv_ref[:, pl.ds(D + h * hd, hd)]).astype(BF)
        x2_ref[...] = x1 + jnp.dot(o_ref[...], wxo_ref[...], preferred_element_type=F32)

    tile = pl.BlockSpec((tm, D), lambda i: (i, 0))
    return pl.pallas_call(
        body, grid=(T // tm,),
        in_specs=[tile, tile, pl.BlockSpec((M, 2 * D), lambda i: (i // tpb, 0)),
                  _const((D, D)), _const((D, D)), _const((D, D)), _const((1, D))],
        out_specs=[tile, tile, tile, tile],
        out_shape=[_sds((T, D), F32), _sds((T, D), F32), _sds((T, D), BF), _sds((T, D), BF)],
        name="attn_fwd", compiler_params=_cp(40))(x, merged, kv, prm["w_mo"], prm["w_q"], prm["w_xo"], prm["g2"])


TM5 = 256
FFN_CHUNKS = 1


def _ffn_loss(x2, tgt, prm, tm=TM5):
    T, D = x2.shape
    F = prm["w_down"].shape[0]
    FC = F // FFN_CHUNKS

    def body(x2_ref, t_ref, wgu_ref, wd_ref, g3_ref, gf_ref, dx2_ref, dx3_ref, dgu_ref, h3_ref, f_ref, ls_ref,
             dg3_ref, dgf_ref, gu_scr):
        first = pl.program_id(0) == 0
        x2 = x2_ref[...]
        h3, r3 = _rms(x2, g3_ref[...])
        h3_ref[...] = h3.astype(BF)
        x3 = x2
        for ch in range(FFN_CHUNKS):
            gc, uc = pl.ds(ch * FC, FC), pl.ds(F + ch * FC, FC)
            gt = jnp.dot(h3_ref[...], wgu_ref[:, gc], preferred_element_type=F32)
            up = jnp.dot(h3_ref[...], wgu_ref[:, uc], preferred_element_type=F32)
            gu_scr[:, gc] = gt
            gu_scr[:, uc] = up
            f_ref[:, gc] = (gt * _sig(gt) * up).astype(BF)
            x3 = x3 + jnp.dot(f_ref[:, gc], wd_ref[gc, :], preferred_element_type=F32)
        y, rf = _rms(x3, gf_ref[...])
        e = y - t_ref[...]
        _acc(ls_ref, jnp.sum(e * e, axis=0, keepdims=True), first)
        dx3, dgf = _rms_bwd(x3, gf_ref[...], rf, e * (1.0 / D))
        _acc(dgf_ref, dgf, first)
        dx3_ref[...] = dx3.astype(BF)
        dh3 = jnp.zeros((tm, D), F32)
        for ch in range(FFN_CHUNKS):
            gc, uc = pl.ds(ch * FC, FC), pl.ds(F + ch * FC, FC)
            df = lax.dot_general(dx3_ref[...], wd_ref[gc, :], (((1,), (1,)), ((), ())), preferred_element_type=F32)
            gt, up = gu_scr[:, gc], gu_scr[:, uc]
            s = _sig(gt)
            dgu_ref[:, gc] = (df * up * _dsilu(gt, s)).astype(BF)
            dgu_ref[:, uc] = (df * gt * s).astype(BF)
            dh3 = dh3 + lax.dot_general(dgu_ref[:, gc], wgu_ref[:, gc], (((1,), (1,)), ((), ())), preferred_element_type=F32)
            dh3 = dh3 + lax.dot_general(dgu_ref[:, uc], wgu_ref[:, uc], (((1,), (1,)), ((), ())), preferred_element_type=F32)
        dxa, dg3 = _rms_bwd(x2, g3_ref[...], r3, dh3)
        _acc(dg3_ref, dg3, first)
        dx2_ref[...] = dx3 + dxa

    tile = lambda n: pl.BlockSpec((tm, n), lambda i: (i, 0))
    vec = pl.BlockSpec((1, D), lambda i: (0, 0))
    return pl.pallas_call(
        body, grid=(T // tm,),
        in_specs=[tile(D), tile(D), _const((D, 2 * F)), _const((F, D)), _const((1, D)), _const((1, D))],
        out_specs=[tile(D), tile(D), tile(2 * F), tile(D), tile(F), vec, vec, vec],
        out_shape=[_sds((T, D), F32), _sds((T, D), BF), _sds((T, 2 * F), BF), _sds((T, D), BF), _sds((T, F), BF),
                   _sds((1, D), F32), _sds((1, D), F32), _sds((1, D), F32)],
        scratch_shapes=[pltpu.VMEM((tm, 2 * F), F32)],
        name="ffn_loss", compiler_params=_cp(56))(x2, tgt, prm["w_gu"], prm["w_down"], prm["g3"], prm["gf"])


def _attn_bwd(x1, kv, dx2, prm, S, M, tm=TM4, comm=None):
    T, D = x1.shape
    hd = D // HEADS
    scale = hd ** -0.5
    tpb = S // tm

    def body(x1_ref, kv_ref, dx2_ref, wmo_ref, wq_ref, wxo_ref, g_ref, dx1_ref, dmg_ref, dq_ref, dkv_ref, dg_ref,
             h2_scr, do_scr):
        i = pl.program_id(0)
        x1 = x1_ref[...]
        dx2 = dx2_ref[...]
        h2, r2 = _rms(x1, g_ref[...])
        h2_scr[...] = h2.astype(BF)
        qb = jnp.dot(h2_scr[...], wq_ref[...], preferred_element_type=F32).astype(BF)
        do_scr[...] = _dot_nt(dx2, wxo_ref[...]).astype(BF)
        for h in range(HEADS):
            cs, vs = pl.ds(h * hd, hd), pl.ds(D + h * hd, hd)
            qh = qb[:, h * hd:(h + 1) * hd]
            pr = _softmax_rows(_dot_nt(qh, kv_ref[:, cs]) * scale)
            dpr = _dot_nt(do_scr[:, cs], kv_ref[:, vs])
            dv = _dot_tn(pr, do_scr[:, cs])
            ds = (pr * (dpr - jnp.sum(dpr * pr, axis=-1, keepdims=True)) * scale).astype(BF)
            dq_ref[:, cs] = jnp.dot(ds, kv_ref[:, cs], preferred_element_type=F32).astype(BF)
            dk = _dot_tn(ds, qh)

            @pl.when(i % tpb == 0)
            def _():
                dkv_ref[:, cs] = dk
                dkv_ref[:, vs] = dv

            @pl.when(i % tpb != 0)
            def _():
                dkv_ref[:, cs] += dk
                dkv_ref[:, vs] += dv

        dh2 = _dot_nt(dq_ref[...], wq_ref[...])
        dxa, dg = _rms_bwd(x1, g_ref[...], r2, dh2)
        _acc(dg_ref, dg, i == 0)
        dx1 = dx2 + dxa
        dx1_ref[...] = dx1
        dmg_ref[...] = _dot_nt(dx1, wmo_ref[...])

    tile = pl.BlockSpec((tm, D), lambda i: (i, 0))
    kvb = pl.BlockSpec((M, 2 * D), lambda i: (i // tpb, 0))
    B = T // S
    return _pcall(
        body, (x1, kv, dx2, prm["w_mo"], prm["w_q"], prm["w_xo"], prm["g2"]), grid=(T // tm,),
        in_specs=[tile, kvb, tile, _const((D, D)), _const((D, D)), _const((D, D)), _const((1, D))],
        out_specs=[tile, tile, tile, kvb, pl.BlockSpec((1, D), lambda i: (0, 0))],
        out_shape=[_sds((T, D), F32), _sds((T, D), F32), _sds((T, D), BF), _sds((B * M, 2 * D), F32), _sds((1, D), F32)],
        scratch_shapes=[pltpu.VMEM((tm, D), BF), pltpu.VMEM((tm, D), BF)],
        name="attn_bwd", vmem=48, comm=comm)


def _kv_bwd(mem, gm, w_kv, dkv, B, M):
    D = mem.shape[1]
    N = w_kv.shape[1]

    def body(m_ref, g_ref, w_ref, dkv_ref, dg_ref):
        mem_t = m_ref[...]
        _, r = _rms(mem_t, g_ref[...])
        dmn = _dot_nt(dkv_ref[...], w_ref[...])
        _acc(dg_ref, jnp.sum(dmn * (mem_t * r), axis=0, keepdims=True), pl.program_id(0) == 0)

    return pl.pallas_call(
        body, grid=(B,),
        in_specs=[pl.BlockSpec((M, D), lambda b: (b, 0)), _const((1, D)), _const((D, N)),
                  pl.BlockSpec((M, N), lambda b: (b, 0))],
        out_specs=pl.BlockSpec((1, D), lambda b: (0, 0)),
        out_shape=_sds((1, D), F32),
        name="kv_bwd", compiler_params=_cp(32))(mem, gm, w_kv, dkv)


def _branch_bwd(c, p, dmerged, prm, tm=TM3, comm=None):
    T, D = c.shape

    def body(c_ref, bu_ref, bv_ref, ga_ref, gb_ref, dm_ref, wco_ref, wso_ref, lag_ref, lab_ref, lsg_ref, lsb_ref,
             wm_ref, bz_ref, bg_ref,
             dc_ref, dpb_ref, dya_ref, dyb_ref, dwm_ref, dbz_ref, dlag_ref, dlab_ref, dlsg_ref, dlsb_ref, dbg_ref,
             z_scr, v_scr, sa_scr, sg_scr, dv_scr):
        first = pl.program_id(0) == 0
        s_a, ln_a, sig_a, xh_a, rstd_a = _branch_a(c_ref[...], lag_ref[...], lab_ref[...])
        sa_scr[...] = s_a.astype(BF)
        y_a = jnp.dot(sa_scr[...], wco_ref[...], preferred_element_type=F32)
        bu, bv = bu_ref[...], bv_ref[...]
        sg, u, tu, z, tv, vh, rstd_v = _branch_b(bu, bv, lsg_ref[...], lsb_ref[...], wm_ref, bz_ref, z_scr, v_scr)
        sg_scr[...] = sg.astype(BF)
        y_b = jnp.dot(sg_scr[...], wso_ref[...], preferred_element_type=F32)
        ga = _sig(ga_ref[...] + bg_ref[pl.ds(0, 1), :])
        gb = _sig(gb_ref[...] + bg_ref[pl.ds(1, 1), :])
        dm = dm_ref[...]
        dga = dm * y_a * ga * (1.0 - ga)
        dgb = dm * y_b * gb * (1.0 - gb)
        dpb_ref[:, pl.ds(2 * D, D)] = dga.astype(BF)
        dpb_ref[:, pl.ds(3 * D, D)] = dgb.astype(BF)
        _acc(dbg_ref.at[pl.ds(0, 1), :], jnp.sum(dga, axis=0, keepdims=True), first)
        _acc(dbg_ref.at[pl.ds(1, 1), :], jnp.sum(dgb, axis=0, keepdims=True), first)
        dya_ref[...] = (dm * ga).astype(BF)
        dyb_ref[...] = (dm * gb).astype(BF)
        dln = _dot_nt(dya_ref[...], wco_ref[...]) * _dsilu(ln_a, sig_a)
        dc, dlag, dlab = _ln_bwd(xh_a, rstd_a, lag_ref[...], dln)
        dc_ref[...] = dc
        _acc(dlag_ref, dlag, first)
        _acc(dlab_ref, dlab, first)
        dsg = _dot_nt(dyb_ref[...], wso_ref[...])
        dpb_ref[:, pl.ds(0, D)] = (dsg * z * _dgelu(bu, tu)).astype(BF)
        dz = dsg * u
        z_scr[...] = dz
        mask = _tril_mask()

        @pl.when(first)
        def _():
            dwm_ref[...] = jnp.zeros_like(dwm_ref)
            dbz_ref[...] = jnp.zeros_like(dbz_ref)

        for gi in range(SGU_GROUPS):
            wm = jnp.where(mask, wm_ref[gi], 0.0).astype(BF)
            cols = pl.ds(gi * LANES, LANES)
            for n in range(tm // LANES):
                rows = pl.ds(n * LANES, LANES)
                dzb = z_scr[rows, cols].astype(BF)
                dv_scr[rows, cols] = lax.dot_general(wm, dzb, (((0,), (0,)), ((), ())), preferred_element_type=F32)
                dw = lax.dot_general(dzb, v_scr[rows, cols], (((1,), (1,)), ((), ())), preferred_element_type=F32)
                dwm_ref[gi] += jnp.where(mask, dw, 0.0)
                dbz_ref[:, cols] += z_scr[rows, cols]
        dgv, dlsg, dlsb = _ln_bwd(vh, rstd_v, lsg_ref[...], dv_scr[...])
        _acc(dlsg_ref, dlsg, first)
        _acc(dlsb_ref, dlsb, first)
        dpb_ref[:, pl.ds(D, D)] = (dgv * _dgelu(bv, tv)).astype(BF)

    tile = lambda j: pl.BlockSpec((tm, D), lambda i: (i, j))
    vec = pl.BlockSpec((1, D), lambda i: (0, 0))
    return _pcall(
        body, (c, p, p, p, p, dmerged, prm["w_co"], prm["w_so"], prm["la_g"], prm["la_b"], prm["ls_g"], prm["ls_b"],
               prm["sgu_w"], prm["bz"], prm["b_gate"]),
        grid=(T // tm,),
        in_specs=[tile(0), tile(2), tile(3), tile(4), tile(5), tile(0), _const((D, D)), _const((D, D)),
                  _const((1, D)), _const((1, D)), _const((1, D)), _const((1, D)),
                  _const((SGU_GROUPS, LANES, LANES)), _const((LANES, D)), _const((2, D))],
        out_specs=[tile(0), pl.BlockSpec((tm, 4 * D), lambda i: (i, 0)), tile(0), tile(0),
                   pl.BlockSpec((SGU_GROUPS, LANES, LANES), lambda i: (0, 0, 0)),
                   pl.BlockSpec((LANES, D), lambda i: (0, 0)), vec, vec, vec, vec,
                   pl.BlockSpec((2, D), lambda i: (0, 0))],
        out_shape=[_sds((T, D), F32), _sds((T, 4 * D), BF), _sds((T, D), BF), _sds((T, D), BF),
                   _sds((SGU_GROUPS, LANES, LANES), F32), _sds((LANES, D), F32),
                   _sds((1, D), F32), _sds((1, D), F32), _sds((1, D), F32), _sds((1, D), F32), _sds((2, D), F32)],
        scratch_shapes=[pltpu.VMEM((tm, D), F32), pltpu.VMEM((tm, D), BF), pltpu.VMEM((tm, D), BF),
                        pltpu.VMEM((tm, D), BF), pltpu.VMEM((tm, D), F32)],
        name="branch_bwd", vmem=56, comm=comm)


def _conv_bwd(p, dc, conv_w, B, S, comm=None):
    K, D = conv_w.shape
    nc = D // LANES

    R = CONV_ROWS_BWD

    def body(av_ref, ag_ref, dc_ref, w_ref, dav_ref, dag_ref, dw_ref, db_ref, apad, dpad):
        b = pl.program_id(1)
        av = av_ref[...]
        sg = _sig(ag_ref[...])
        apad[pl.ds(0, CONV_PAD), :] = jnp.zeros((CONV_PAD, LANES), F32)
        apad[pl.ds(CONV_PAD, S), :] = av * sg
        dpad[pl.ds(S, CONV_PAD), :] = jnp.zeros((CONV_PAD, LANES), F32)
        dpad[pl.ds(0, S), :] = dc_ref[...]

        @pl.when(b == 0)
        def _():
            dw_ref[...] = jnp.zeros_like(dw_ref)
            db_ref[...] = jnp.zeros_like(db_ref)

        db_ref[...] += jnp.sum(dc_ref[...], axis=0, keepdims=True)
        for k in range(K):
            tot = jnp.zeros((1, LANES), F32)
            for r0 in range(0, S, R):
                tot = tot + jnp.sum(dpad[pl.ds(r0, R), :] * apad[pl.ds(r0 + k + CONV_PAD - (K - 1), R), :],
                                    axis=0, keepdims=True)
            dw_ref[pl.ds(k, 1), :] += tot
        for r0 in range(0, S, R):
            da = jnp.zeros((R, LANES), F32)
            for k in range(K):
                da = da + w_ref[pl.ds(k, 1), :] * dpad[pl.ds(r0 + (K - 1) - k, R), :]
            rows = pl.ds(r0, R)
            s = sg[r0:r0 + R, :]
            a_v = av[r0:r0 + R, :]
            dav_ref[rows, :] = (da * s).astype(BF)
            dag_ref[rows, :] = (da * a_v * s * (1.0 - s)).astype(BF)

    blk = lambda off: pl.BlockSpec((S, LANES), lambda j, b: (b, off + j))
    return _pcall(
        body, (p, p, dc, conv_w), grid=(nc, B),
        in_specs=[blk(0), blk(nc), blk(0), pl.BlockSpec((K, LANES), lambda j, b: (0, j))],
        out_specs=[blk(0), blk(0), pl.BlockSpec((K, LANES), lambda j, b: (0, j)), pl.BlockSpec((1, LANES), lambda j, b: (0, j))],
        out_shape=[_sds((B * S, D), BF), _sds((B * S, D), BF), _sds((K, D), F32), _sds((1, D), F32)],
        scratch_shapes=[pltpu.VMEM((S + CONV_PAD, LANES), F32), pltpu.VMEM((S + CONV_PAD, LANES), F32)],
        name="conv_bwd", vmem=32, comm=comm)


TM1 = 512


def _in_proj_bwd(x, dx1, dps, g1, w_in, tm=TM1, comm=None):
    T, D = x.shape
    N = w_in.shape[1]
    widths = [d.shape[1] for d in dps]

    def body(x_ref, dx1_ref, *refs):
        dp_refs, (g_ref, w_ref, dx_ref, dg_ref) = refs[:len(dps)], refs[len(dps):]
        x_t = x_ref[...]
        _, r = _rms(x_t, g_ref[...])
        dh = jnp.zeros((tm, D), F32)
        for q, dp_ref in enumerate(dp_refs):
            cols = pl.ds(sum(widths[:q]), widths[q])
            dh = dh + lax.dot_general(dp_ref[...], w_ref[:, cols], (((1,), (1,)), ((), ())), preferred_element_type=F32)
        dxa, dg = _rms_bwd(x_t, g_ref[...], r, dh)
        dx_ref[...] = dx1_ref[...] + dxa
        _acc(dg_ref, dg, pl.program_id(0) == 0)

    tile = pl.BlockSpec((tm, D), lambda i: (i, 0))
    return _pcall(
        body, (x, dx1, *dps, g1, w_in), grid=(T // tm,),
        in_specs=[tile, tile] + [pl.BlockSpec((tm, w), lambda i: (i, 0)) for w in widths] + [_const((1, D)), _const((D, N))],
        out_specs=[tile, pl.BlockSpec((1, D), lambda i: (0, 0))],
        out_shape=[_sds((T, D), F32), _sds((1, D), F32)],
        name="in_proj_bwd", vmem=48, comm=comm)


def _pick(n, cands):
    for c in cands:
        if n % c == 0:
            return c
    raise ValueError(f"no tile of {cands} divides {n}")


def _mm_tn(x, dys, name, comm=None):
    T, K = x.shape
    dys = list(dys) if isinstance(dys, (list, tuple)) else [dys]
    widths = [d.shape[1] for d in dys]
    N = sum(widths)
    tm = _pick(T, (2048, 1024, 512, 256))
    tk = _pick(K, (1024, 1408, 512))
    tn = _pick(math.gcd(*widths), (1024, 1408, 512))
    nt = T // tm
    first = [sum(widths[:q]) // tn for q in range(len(dys))]
    count = [w // tn for w in widths]

    def body(x_ref, *refs):
        dy_refs, (o_ref, ob_ref, acc) = refs[:len(dys)], refs[len(dys):]
        j, t = pl.program_id(1), pl.program_id(2)

        @pl.when(t == 0)
        def _():
            acc[...] = jnp.zeros_like(acc)

        for q, dy_ref in enumerate(dy_refs):
            @pl.when(jnp.logical_and(j >= first[q], j < first[q] + count[q]))
            def _():
                acc[...] += _dot_tn(x_ref[...], dy_ref[...])

        @pl.when(t == nt - 1)
        def _():
            o_ref[...] = acc[...]
            ob_ref[...] = acc[...].astype(BF)

    def dy_spec(q):
        def index(i, j, t):
            mine = jnp.logical_and(j >= first[q], j < first[q] + count[q])
            return jnp.where(mine, t, 0), jnp.clip(j - first[q], 0, count[q] - 1)
        return pl.BlockSpec((tm, tn), index)

    res = _pcall(
        body, (x, *dys), grid=(K // tk, N // tn, nt),
        in_specs=[pl.BlockSpec((tm, tk), lambda i, j, t: (t, i))] + [dy_spec(q) for q in range(len(dys))],
        out_specs=[pl.BlockSpec((tk, tn), lambda i, j, t: (i, j)), pl.BlockSpec((tk, tn), lambda i, j, t: (i, j))],
        out_shape=[_sds((K, N), F32), _sds((K, N), BF)],
        scratch_shapes=[pltpu.VMEM((tk, tn), F32)],
        name=name, vmem=60, comm=comm)
    return res[0] if comm is None else res


def _row_tile(R):
    return _pick(R, (128, 64, 32, 16, 8)) if R % 8 == 0 else R


def _sum_landed(full, land, kind, chip, name):
    _, R, C = land.shape
    tr = _row_tile(R)
    nb = R // tr

    def body(k_ref, o_ref, l_ref, s_ref):
        s_ref[...] = ((o_ref[...] + l_ref[0].astype(F32)) + l_ref[1].astype(F32)) + l_ref[2].astype(F32)

    own = (pl.BlockSpec((tr, C), lambda i, k: (k[0] * nb + i, 0)) if kind == "row"
           else pl.BlockSpec((tr, C), lambda i, k: (i, k[0])))
    return pl.pallas_call(
        body,
        grid_spec=pltpu.PrefetchScalarGridSpec(
            num_scalar_prefetch=1, grid=(nb,),
            in_specs=[own, pl.BlockSpec((3, tr, C), lambda i, k: (0, i, 0))],
            out_specs=pl.BlockSpec((tr, C), lambda i, k: (i, 0))),
        out_shape=_sds((R, C), F32), name=name, compiler_params=_cp(32))(chip, full, land)


def _adamw(g, w, m, v):
    m = ADAM_B1 * m + (1.0 - ADAM_B1) * g
    v = ADAM_B2 * v + (1.0 - ADAM_B2) * (g * g)
    m_hat = m / (1.0 - ADAM_B1 ** ADAM_STEP)
    v_hat = v / (1.0 - ADAM_B2 ** ADAM_STEP)
    return -ADAM_LR * (m_hat / (jnp.sqrt(v_hat) + ADAM_EPS) + ADAM_WD * w), m, v


def _update(parts, w, m, v, name):
    R, C = w.shape
    tr = _row_tile(R)
    k = len(parts)

    def body(*refs):
        g = refs[0][...]
        for r in refs[1:k]:
            g = g + r[...]
        w_ref, m_ref, v_ref, g_out, d_out, m_out, v_out = refs[k:]
        d, m_new, v_new = _adamw(g, w_ref[...], m_ref[...], v_ref[...])
        g_out[...] = g
        d_out[...] = d
        m_out[...] = m_new
        v_out[...] = v_new

    blk = pl.BlockSpec((tr, C), lambda i: (i, 0))
    return pl.pallas_call(
        body, grid=(R // tr,), in_specs=[blk] * (k + 3), out_specs=[blk] * 4, out_shape=[_sds((R, C), F32)] * 4,
        name=name, compiler_params=_cp(40))(*parts, w, m, v)


VECS = ("norm_mix", "conv_b", "conv_ln_g", "conv_ln_b", "sgu_ln_g", "sgu_ln_b", "norm_xattn", "norm_mem", "norm_ffn",
        "norm_final")
SMALL = VECS + ("b_gate", "conv_w", "sgu_w", "sgu_b")
SUBLANES = 8


def _pad_rows(arr):
    return jnp.pad(arr, ((0, (-arr.shape[0]) % SUBLANES), (0, 0)))


def _update_small(land_vec, land_w, land_g1, chip, a):
    _, R, D = land_vec.shape
    Dq, G, K, nv = D // N_CHIPS, SGU_GROUPS, a["conv_w"].shape[1], len(VECS)

    def as2d(nm, arr):
        if nm in VECS:
            return arr.reshape(1, D)
        if nm == "sgu_w":
            return arr.reshape(G * LANES, LANES)
        return arr.reshape(G, LANES) if nm == "sgu_b" else arr[0]

    params = [as2d(nm, a[pre + nm]) for nm in SMALL for pre in ("", "m_", "v_")]

    def body(k_ref, lv, lvc, lw, l1, *refs):
        prm = refs[:3 * len(SMALL)]
        outs = refs[3 * len(SMALL):7 * len(SMALL)]
        tv, tvc, tw, t1 = refs[7 * len(SMALL):]
        for land, tot in ((lv, tv), (lvc, tvc), (lw, tw), (l1, t1)):
            acc = land[0]
            for dev in range(1, N_DEV):
                acc = acc + land[dev]
            tot[...] = acc
        at = SUBLANES * nv
        grads = [t1[pl.ds(0, 1), :] if nm == "norm_mix" else tv[pl.ds(SUBLANES * i, 1), :]
                 for i, nm in enumerate(VECS)]
        grads += [tvc[pl.ds(at, 2), :], tvc[pl.ds(at + SUBLANES, K), :],
                  tw[pl.ds(0, G * LANES), :], tw[pl.ds(G * LANES, G), :]]
        for i, g in enumerate(grads):
            d, m_new, v_new = _adamw(g, prm[3 * i][...], prm[3 * i + 1][...], prm[3 * i + 2][...])
            for o_ref, val in zip(outs[4 * i:4 * i + 4], (g, d, m_new, v_new)):
                o_ref[...] = val

    whole = lambda shape: pl.BlockSpec(tuple(shape), lambda i, k: (0,) * len(shape))
    res = pl.pallas_call(
        body,
        grid_spec=pltpu.PrefetchScalarGridSpec(
            num_scalar_prefetch=1, grid=(1,),
            in_specs=[whole(land_vec.shape), pl.BlockSpec((N_DEV, R, Dq), lambda i, k: (0, 0, k[0])),
                      whole(land_w.shape), whole(land_g1.shape)] + [whole(p.shape) for p in params],
            out_specs=[whole(params[3 * i].shape) for i in range(len(SMALL)) for _ in range(4)],
            scratch_shapes=[pltpu.VMEM((R, D), F32), pltpu.VMEM((R, Dq), F32), pltpu.VMEM(land_w.shape[1:], F32),
                            pltpu.VMEM(land_g1.shape[1:], F32)]),
        out_shape=[_sds(params[3 * i].shape, F32) for i in range(len(SMALL)) for _ in range(4)],
        name="upd_small", compiler_params=_cp(40))(chip, land_vec, land_vec, land_w, land_g1, *params)
    return {nm: list(res[4 * i:4 * i + 4]) for i, nm in enumerate(SMALL)}


BIG = ("w_in", "w_conv_out", "w_sgu_out", "w_mix_out", "w_q", "w_kv", "w_xo", "w_gu", "w_down")
BIG_KIND = {"w_in": "col", "w_conv_out": "row", "w_sgu_out": "row", "w_mix_out": "row", "w_q": "row",
            "w_kv": "col", "w_xo": "row", "w_gu": "col", "w_down": "row"}

def _step(a):
    x3d, mem3d, tgt3d = a["x"], a["mem"], a["loss_target"]
    B, S, D = x3d.shape
    M = mem3d.shape[1]
    T = B * S
    x = x3d.reshape(T, D)
    mem = mem3d.reshape(B * M, D)
    tgt = tgt3d.reshape(T, D)
    xi, yi = lax.axis_index("x"), lax.axis_index("y")
    chip = 2 * xi + yi

    def gather(names):
        return _Gather([a[nm][0] if nm in ("b_gate", "conv_w") else a[nm][0].astype(BF) for nm in names],
                       [BIG_KIND.get(nm, "col") for nm in names])

    first = ("w_in", "b_gate", "conv_w")
    on_in_proj = ("w_conv_out", "w_sgu_out", "w_kv")
    on_conv = ("w_mix_out", "w_q", "w_xo", "w_down")
    full = dict(zip(first, _comm_call(gather(first), "gather_w_in")))
    (p, h1), got = _in_proj(x, a["norm_mix"], full["w_in"], comm=gather(on_in_proj))
    full.update(zip(on_in_proj, got))
    (c,), got = _conv_fwd(p, full["conv_w"], a["conv_b"], B, S, comm=gather(on_conv))
    full.update(zip(on_conv, got))

    sgu_b = a["sgu_b"][0]
    bz = jnp.repeat(jnp.transpose(sgu_b), LANES, axis=1)
    prm = dict(w_co=full["w_conv_out"], w_so=full["w_sgu_out"], w_mo=full["w_mix_out"], w_q=full["w_q"],
               w_xo=full["w_xo"], w_down=full["w_down"],
               la_g=a["conv_ln_g"], la_b=a["conv_ln_b"], ls_g=a["sgu_ln_g"], ls_b=a["sgu_ln_b"],
               sgu_w=a["sgu_w"][0], bz=bz, b_gate=full["b_gate"], g2=a["norm_xattn"], g3=a["norm_ffn"],
               gf=a["norm_final"].reshape(1, D))

    (merged, s_a, sg), got = _branch_fwd(c, p, prm, comm=gather(("w_gu",)))
    prm["w_gu"] = got[0]
    mem_n, kv = _kv_fwd(mem, a["norm_mem"], full["w_kv"], B, M)
    x1, x2, h2, o = _attn_fwd(x, merged, kv, prm, S, M)
    dx2, dx3, dgu, h3, f, lsum, d_g3, d_gf = _ffn_loss(x2, tgt, prm)
    loss = lax.psum(0.5 * jnp.sum(lsum) / D, ("x", "y", "c"))

    size_of = {nm: a[nm].shape[1] if BIG_KIND[nm] == "row" else a[nm].shape[2] for nm in BIG}
    landed = {}

    def scatter(names):
        return _Scatter([gw[nm][1] for nm in names], [BIG_KIND[nm] for nm in names], [size_of[nm] for nm in names])

    gw = {}
    gw["w_down"] = _mm_tn(f, dx3, "dw_down")
    gw["w_gu"] = _mm_tn(h3, dgu, "dw_gu")
    (dx1, dmerged, dq, dkv, d_g2), got = _attn_bwd(x1, kv, dx2, prm, S, M, comm=scatter(("w_gu",)))
    landed["w_gu"] = got[0]
    gw["w_xo"] = _mm_tn(o, dx2, "dw_xo")
    gw["w_q"] = _mm_tn(h2, dq, "dw_q")
    gw["w_kv"] = _mm_tn(mem_n, dkv, "dw_kv")
    d_gm = _kv_bwd(mem, a["norm_mem"], full["w_kv"], dkv, B, M)
    gw["w_mix_out"] = _mm_tn(merged, dx1, "dw_mix_out")
    group = ("w_down", "w_xo", "w_q", "w_kv", "w_mix_out")
    (dc, dpb, dya, dyb, d_wm, d_bz, d_lag, d_lab, d_lsg, d_lsb, d_bg), got = _branch_bwd(
        c, p, dmerged, prm, comm=scatter(group))
    landed.update(zip(group, got))
    gw["w_conv_out"] = _mm_tn(s_a, dya, "dw_conv_out")
    gw["w_sgu_out"] = _mm_tn(sg, dyb, "dw_sgu_out")
    group = ("w_conv_out", "w_sgu_out")
    (dav, dag, d_cw, d_cb), got = _conv_bwd(p, dc, full["conv_w"], B, S, comm=scatter(group))
    landed.update(zip(group, got))
    dp = [dav, dag, dpb]

    chip_arr = jnp.reshape(chip, (1,)).astype(jnp.int32)
    early = [nm for nm in BIG if nm != "w_in"]
    part = {nm: _sum_landed(gw[nm][0], landed[nm], BIG_KIND[nm], chip_arr, "sum_" + nm) for nm in early}
    G = SGU_GROUPS
    d_sb = jnp.transpose(d_bz.reshape(LANES, G, LANES).sum(axis=-1))
    vec_g = dict(norm_mix=jnp.zeros((1, D), F32), conv_b=d_cb, conv_ln_g=d_lag, conv_ln_b=d_lab, sgu_ln_g=d_lsg,
                 sgu_ln_b=d_lsb, norm_xattn=d_g2, norm_mem=d_gm, norm_ffn=d_g3, norm_final=d_gf)
    g_vec = jnp.concatenate([_pad_rows(vec_g[nm]) for nm in VECS] + [_pad_rows(d_bg), _pad_rows(d_cw)], axis=0)
    g_w = jnp.concatenate([d_wm.reshape(G * LANES, LANES), d_sb], axis=0)
    gw["w_in"], got = _mm_tn(h1, dp, "dw_in", comm=_Both(_Swap([part[nm] for nm in early]), _Spread([g_vec, g_w])))
    other = dict(zip(early, got[:len(early)]))
    land_vec, land_w = got[len(early):]
    kind_in, size_in = BIG_KIND["w_in"], size_of["w_in"]
    send, recv, g_thru, land_thru, token = _scatter_start(gw["w_in"][1], kind_in, size_in, "scatter_w_in_start")
    (grad_x, d_g1), _ = _in_proj_bwd(x, dx1, dp, a["norm_mix"] + token[:1, :1], full["w_in"])

    out = {}

    def update(nm):
        res = _update([part[nm], other[nm]], a[nm][0], a["m_" + nm][0], a["v_" + nm][0], "upd_" + nm)
        out[nm] = [r[None] for r in res]

    for nm in early:
        update(nm)
    after = [grad_x] + [out[nm][1] for nm in early]
    land_in = _scatter_wait(send, recv, g_thru, land_thru, after, kind_in, size_in, "scatter_w_in_wait")
    part["w_in"] = _sum_landed(gw["w_in"][0], land_in, kind_in, chip_arr, "sum_w_in")
    g1 = _pad_rows(d_g1)
    other["w_in"], land_g1 = _comm_call(_Both(_Swap([part["w_in"]]), _Spread([g1])), "swap_w_in")
    update("w_in")
    for nm, res in _update_small(land_vec, land_w, land_g1, chip_arr, a).items():
        out[nm] = [r.reshape(a[nm].shape) for r in res]
    return loss, grad_x.reshape(B, S, D), out


WEIGHTS = ("norm_mix", "w_in", "b_gate", "conv_w", "conv_b", "conv_ln_g", "conv_ln_b", "w_conv_out", "sgu_ln_g",
           "sgu_ln_b", "sgu_w", "sgu_b", "w_sgu_out", "w_mix_out", "norm_xattn", "norm_mem", "w_q", "w_kv", "w_xo",
           "norm_ffn", "w_gu", "w_down", "norm_final")


def kernel(x, mem, norm_mix, w_in, b_gate, conv_w, conv_b, conv_ln_g, conv_ln_b, w_conv_out, sgu_ln_g, sgu_ln_b, sgu_w, sgu_b, w_sgu_out, w_mix_out, norm_xattn, norm_mem, w_q, w_kv, w_xo, norm_ffn, w_gu, w_down, norm_final, loss_target, m_norm_mix, m_w_in, m_b_gate, m_conv_w, m_conv_b, m_conv_ln_g, m_conv_ln_b, m_w_conv_out, m_sgu_ln_g, m_sgu_ln_b, m_sgu_w, m_sgu_b, m_w_sgu_out, m_w_mix_out, m_norm_xattn, m_norm_mem, m_w_q, m_w_kv, m_w_xo, m_norm_ffn, m_w_gu, m_w_down, m_norm_final, v_norm_mix, v_w_in, v_b_gate, v_conv_w, v_conv_b, v_conv_ln_g, v_conv_ln_b, v_w_conv_out, v_sgu_ln_g, v_sgu_ln_b, v_sgu_w, v_sgu_b, v_w_sgu_out, v_w_mix_out, v_norm_xattn, v_norm_mem, v_w_q, v_w_kv, v_w_xo, v_norm_ffn, v_w_gu, v_w_down, v_norm_final):
    a = dict(locals())
    loss, grad_x, out = _step(a)
    res = [loss, grad_x]
    for q in range(4):
        res += [out[nm][q] for nm in WEIGHTS]
    return tuple(res)
```

```python
import functools
import math

import jax
import jax.numpy as jnp
from jax import lax
from jax.experimental import pallas as pl
from jax.experimental.pallas import tpu as pltpu

BF = jnp.bfloat16
F32 = jnp.float32
MESH = pl.DeviceIdType.MESH
ANY = pl.BlockSpec(memory_space=pl.ANY)

RMS_EPS = 1e-6
LN_EPS = 1e-5
HEADS = 4
SGU_GROUPS = 8
LANES = 128
ADAM_LR = 0.001
ADAM_B1 = 0.9
ADAM_B2 = 0.999
ADAM_EPS = 1e-08
ADAM_WD = 0.01
ADAM_STEP = 10
N_CHIPS = 4
N_DEV = 8
MIB = 1024 * 1024


def _sds(shape, dtype):
    return jax.ShapeDtypeStruct(tuple(shape), dtype)


def _cp(vmem_mib):
    return pltpu.CompilerParams(vmem_limit_bytes=vmem_mib * MIB)


def _const(shape):
    nd = len(shape)
    return pl.BlockSpec(tuple(shape), lambda *_: (0,) * nd, pipeline_mode=pl.Buffered(1))


def _dot(a, b):
    return jnp.dot(a.astype(BF), b.astype(BF), preferred_element_type=F32)


def _dot_nt(a, b):
    return lax.dot_general(a.astype(BF), b.astype(BF), (((1,), (1,)), ((), ())), preferred_element_type=F32)


def _dot_tn(a, b):
    return lax.dot_general(a.astype(BF), b.astype(BF), (((0,), (0,)), ((), ())), preferred_element_type=F32)


def _sig(x):
    return 1.0 / (1.0 + jnp.exp(-x))


def _dsilu(x, s):
    return s * (1.0 + x * (1.0 - s))


_GELU_C = math.sqrt(2.0 / math.pi)


def _gelu(x):
    x2 = x * x
    t = jnp.tanh((_GELU_C * x) * (1.0 + 0.044715 * x2))
    cdf = 0.5 * (1.0 + t)
    return x * cdf, (t, x2, cdf)


def _dgelu(x, shared):
    t, x2, cdf = shared
    return cdf + ((0.5 * _GELU_C) * x) * (1.0 - t * t) * (1.0 + (3.0 * 0.044715) * x2)


def _rms(x, g):
    r = lax.rsqrt(jnp.mean(x * x, axis=-1, keepdims=True) + RMS_EPS)
    return x * r * g, r


def _rms_bwd(x, g, r, dh):
    xr = x * r
    dxh = dh * g
    dx = r * (dxh - xr * jnp.mean(dxh * xr, axis=-1, keepdims=True))
    return dx, jnp.sum(dh * xr, axis=0, keepdims=True)


def _ln(x, g, b):
    mu = jnp.mean(x, axis=-1, keepdims=True)
    xc = x - mu
    rstd = lax.rsqrt(jnp.mean(xc * xc, axis=-1, keepdims=True) + LN_EPS)
    xh = xc * rstd
    return xh * g + b, xh, rstd


def _ln_bwd(xh, rstd, g, dy):
    dxh = dy * g
    dx = rstd * (dxh - jnp.mean(dxh, axis=-1, keepdims=True) - xh * jnp.mean(dxh * xh, axis=-1, keepdims=True))
    return dx, jnp.sum(dy * xh, axis=0, keepdims=True), jnp.sum(dy, axis=0, keepdims=True)


def _acc(ref, val, first):
    @pl.when(first)
    def _():
        ref[...] = val

    @pl.when(jnp.logical_not(first))
    def _():
        ref[...] += val


def _place():
    x, y, c = lax.axis_index("x"), lax.axis_index("y"), lax.axis_index("c")
    chips = [(1 - x, y), (x, 1 - y), (1 - x, 1 - y)]
    return x, y, c, chips


def _shard_of(ref, kind, k, n):
    if kind == "row":
        return ref.at[pl.ds(k * n, n), :]
    return ref.at[:, pl.ds(k * n, n)]


class _Gather:
    def __init__(self, shards, kinds):
        n = len(shards)
        self.srcs, self.kinds = list(shards), list(kinds)
        self.sizes = [s.shape[0] if kd == "row" else s.shape[1] for s, kd in zip(shards, kinds)]
        self.halves = [s.shape[0] // 2 if s.shape[0] % 32 == 0 else None for s in shards]
        self.out_shape = [
            _sds((s.shape[0] * N_CHIPS, s.shape[1]) if kd == "row" else (s.shape[0], s.shape[1] * N_CHIPS), s.dtype)
            for s, kd in zip(shards, kinds)]
        dma = pltpu.SemaphoreType.DMA
        self.sems = [dma((3 * n,)), dma((3 * n,)), dma((n,)), dma((3 * n,)), dma((3 * n,))]

    def _part(self, ref, t, core):
        h = self.halves[t]
        return ref if h is None else ref.at[pl.ds(core * h, h), :]

    def _copies(self, ins, outs, send, recv, loc, fsend, frecv):
        x, y, c, chips = _place()
        k = 2 * x + y
        local, remote = [], []
        for t in range(len(ins)):
            block = lambda q: _shard_of(outs[t], self.kinds[t], q, self.sizes[t])
            local.append(pltpu.make_async_copy(ins[t], block(k), loc.at[t]))
            for j, (px, py) in enumerate(chips):
                sems = dict(send_sem=send.at[3 * t + j], recv_sem=recv.at[3 * t + j])
                there = dict(device_id=(px, py, c), device_id_type=MESH)
                sent = pltpu.make_async_remote_copy(
                    src_ref=self._part(ins[t], t, c), dst_ref=self._part(block(k), t, c), **sems, **there)
                got = self._part(block(2 * px + py), t, c)
                landed = pltpu.make_async_remote_copy(src_ref=self._part(ins[t], t, c), dst_ref=got, **sems, **there)
                passed = handed = None
                if self.halves[t] is not None:
                    fsems = dict(send_sem=fsend.at[3 * t + j], recv_sem=frecv.at[3 * t + j])
                    sibling = dict(device_id=(x, y, 1 - c), device_id_type=MESH)
                    passed = pltpu.make_async_remote_copy(src_ref=got, dst_ref=got, **fsems, **sibling)
                    other = self._part(block(2 * px + py), t, 1 - c)
                    handed = pltpu.make_async_remote_copy(src_ref=got, dst_ref=other, **fsems, **sibling)
                remote.append((sent, landed, passed, handed))
        return local, remote

    def start(self, ins, outs, *sems):
        local, remote = self._copies(ins, outs, *sems)
        for cp in local:
            cp.start()
        for sent, _, _, _ in remote:
            sent.start()

    def wait(self, ins, outs, *sems):
        local, remote = self._copies(ins, outs, *sems)
        for sent, landed, passed, handed in remote:
            landed.wait_recv()
            if passed is not None:
                passed.start()
        for sent, landed, passed, handed in remote:
            if passed is not None:
                handed.wait_recv()
                passed.wait_send()
            sent.wait_send()
        for cp in local:
            cp.wait()


class _Scatter:
    def __init__(self, grads, kinds, sizes):
        n = len(grads)
        self.srcs, self.kinds, self.sizes = list(grads), list(kinds), list(sizes)
        self.out_shape = [_sds((3,) + ((sz, g.shape[1]) if kd == "row" else (g.shape[0], sz)), g.dtype)
                          for g, kd, sz in zip(grads, kinds, sizes)]
        self.sems = [pltpu.SemaphoreType.DMA((3 * n,)), pltpu.SemaphoreType.DMA((3 * n,))]

    def _copies(self, ins, outs, send, recv):
        x, y, c, chips = _place()
        return [pltpu.make_async_remote_copy(
            src_ref=_shard_of(ins[t], self.kinds[t], 2 * px + py, self.sizes[t]), dst_ref=outs[t].at[j],
            send_sem=send.at[3 * t + j], recv_sem=recv.at[3 * t + j], device_id=(px, py, c), device_id_type=MESH)
            for t in range(len(ins)) for j, (px, py) in enumerate(chips)]

    def start(self, ins, outs, send, recv):
        for cp in self._copies(ins, outs, send, recv):
            cp.start()

    def wait(self, ins, outs, send, recv):
        for cp in self._copies(ins, outs, send, recv):
            cp.wait_recv()
            cp.wait_send()


class _Swap:
    def __init__(self, parts):
        n = len(parts)
        self.srcs = list(parts)
        self.out_shape = [_sds(p.shape, p.dtype) for p in parts]
        self.sems = [pltpu.SemaphoreType.DMA((n,)), pltpu.SemaphoreType.DMA((n,))]

    def _copies(self, ins, outs, send, recv):
        x, y, c, _ = _place()
        return [pltpu.make_async_remote_copy(
            src_ref=ins[t], dst_ref=outs[t], send_sem=send.at[t], recv_sem=recv.at[t],
            device_id=(x, y, 1 - c), device_id_type=MESH) for t in range(len(ins))]

    def start(self, ins, outs, send, recv):
        for cp in self._copies(ins, outs, send, recv):
            cp.start()

    def wait(self, ins, outs, send, recv):
        for cp in self._copies(ins, outs, send, recv):
            cp.wait_recv()
            cp.wait_send()


class _Spread:
    def __init__(self, packs):
        n = len(packs)
        self.srcs = list(packs)
        self.out_shape = [_sds((N_DEV,) + p.shape, p.dtype) for p in packs]
        dma = pltpu.SemaphoreType.DMA
        self.sems = [dma((N_DEV * n,)), dma((N_DEV * n,)), dma((n,))]

    def _copies(self, ins, outs, send, recv, loc):
        x, y, c, _ = _place()
        me = 4 * x + 2 * y + c
        local = [pltpu.make_async_copy(ins[t], outs[t].at[me], loc.at[t]) for t in range(len(ins))]
        remote = []
        for t in range(len(ins)):
            for mask in range(1, N_DEV):
                peer = ((1 - x) if mask & 4 else x, (1 - y) if mask & 2 else y, (1 - c) if mask & 1 else c)
                src = peer[0] * 4 + peer[1] * 2 + peer[2]
                sems = dict(send_sem=send.at[N_DEV * t + mask], recv_sem=recv.at[N_DEV * t + mask])
                sent = pltpu.make_async_remote_copy(
                    src_ref=ins[t], dst_ref=outs[t].at[me], device_id=peer, device_id_type=MESH, **sems)
                landed = pltpu.make_async_remote_copy(
                    src_ref=ins[t], dst_ref=outs[t].at[src], device_id=peer, device_id_type=MESH, **sems)
                remote.append((sent, landed))
        return local, remote

    def start(self, ins, outs, send, recv, loc):
        local, remote = self._copies(ins, outs, send, recv, loc)
        for cp in local:
            cp.start()
        for sent, _ in remote:
            sent.start()

    def wait(self, ins, outs, send, recv, loc):
        local, remote = self._copies(ins, outs, send, recv, loc)
        for sent, landed in remote:
            landed.wait_recv()
            sent.wait_send()
        for cp in local:
            cp.wait()


class _Both:
    def __init__(self, *comms):
        self.comms = comms
        self.srcs = [s for cm in comms for s in cm.srcs]
        self.out_shape = [s for cm in comms for s in cm.out_shape]
        self.sems = [s for cm in comms for s in cm.sems]

    def _each(self, ins, outs, sems):
        i = o = k = 0
        for cm in self.comms:
            ni, no, nk = len(cm.srcs), len(cm.out_shape), len(cm.sems)
            yield cm, ins[i:i + ni], outs[o:o + no], sems[k:k + nk]
            i, o, k = i + ni, o + no, k + nk

    def start(self, ins, outs, *sems):
        for cm, i, o, s in self._each(ins, outs, sems):
            cm.start(i, o, *s)

    def wait(self, ins, outs, *sems):
        for cm, i, o, s in self._each(ins, outs, sems):
            cm.wait(i, o, *s)


HBM = pl.BlockSpec(memory_space=pltpu.HBM)
SEM = pl.BlockSpec(memory_space=pltpu.SEMAPHORE)
EFFECT = pltpu.SideEffectType.DATAFLOW_SIDE_EFFECTING


def _scatter_copies(g_ref, land_ref, send, recv, kind, size):
    x, y, c, chips = _place()
    return [pltpu.make_async_remote_copy(
        src_ref=_shard_of(g_ref, kind, 2 * px + py, size), dst_ref=land_ref.at[j],
        send_sem=send.at[j], recv_sem=recv.at[j], device_id=(px, py, c), device_id_type=MESH)
        for j, (px, py) in enumerate(chips)]


def _scatter_start(grad, kind, size, name):
    land = (3,) + ((size, grad.shape[1]) if kind == "row" else (grad.shape[0], size))

    def body(g_ref, land_ref, send, recv, g_thru, land_thru, token):
        for cp in _scatter_copies(g_ref, land_ref, send, recv, kind, size):
            cp.start()
        token[...] = jnp.zeros_like(token)

    return pl.pallas_call(
        body, name=name,
        out_shape=(pltpu.SemaphoreType.DMA((3,)), pltpu.SemaphoreType.DMA((3,)), pltpu.HBM(grad.shape, grad.dtype),
                   pltpu.HBM(land, grad.dtype), _sds((8, LANES), F32)),
        in_specs=(HBM, HBM), out_specs=(SEM, SEM, HBM, HBM, pl.BlockSpec(memory_space=pltpu.VMEM)),
        input_output_aliases={0: 2, 1: 3},
        compiler_params=pltpu.CompilerParams(has_side_effects=EFFECT))(
            pltpu.with_memory_space_constraint(grad, pltpu.HBM),
            pltpu.with_memory_space_constraint(lax.empty(land, grad.dtype), pltpu.HBM))


def _scatter_wait(send, recv, g_thru, land_thru, after, kind, size, name):
    def body(g_ref, land_ref, send, recv, *rest):
        for cp in _scatter_copies(g_ref, land_ref, send, recv, kind, size):
            cp.wait_send()
            cp.wait_recv()

    return pl.pallas_call(
        body, name=name,
        out_shape=(pltpu.HBM(g_thru.shape, g_thru.dtype), pltpu.HBM(land_thru.shape, land_thru.dtype)),
        in_specs=(HBM, HBM, SEM, SEM) + (ANY,) * len(after), out_specs=(HBM, HBM),
        input_output_aliases={0: 0, 1: 1},
        compiler_params=pltpu.CompilerParams(has_side_effects=EFFECT))(g_thru, land_thru, send, recv, *after)[1]


def _cast_bf16(arrs):
    n = len(arrs)

    def body(*refs):
        for src, dst in zip(refs[:n], refs[n:]):
            dst[...] = src[...].astype(BF)

    vm = pl.BlockSpec(memory_space=pltpu.VMEM)
    return pl.pallas_call(body, in_specs=[vm] * n, out_specs=[vm] * n, out_shape=[_sds(x.shape, BF) for x in arrs],
                          name="cast_shards", compiler_params=_cp(48))(*arrs)


def _comm_call(comm, name):
    n, m = len(comm.srcs), len(comm.out_shape)

    def body(*refs):
        comm.start(refs[:n], refs[n:n + m], *refs[n + m:])
        comm.wait(refs[:n], refs[n:n + m], *refs[n + m:])

    return pl.pallas_call(body, in_specs=[ANY] * n, out_specs=[ANY] * m, out_shape=comm.out_shape,
                          scratch_shapes=comm.sems, name=name)(*comm.srcs)


def _pcall(body, args, *, grid, in_specs, out_specs, out_shape, name, vmem, scratch_shapes=(), comm=None):
    in_specs, out_specs, out_shape = list(in_specs), list(out_specs), list(out_shape)
    scratch_shapes = list(scratch_shapes)
    if comm is None:
        res = pl.pallas_call(body, grid=grid, in_specs=in_specs, out_specs=out_specs, out_shape=out_shape,
                             scratch_shapes=scratch_shapes, name=name, compiler_params=_cp(vmem))(*args)
        return list(res), []
    ni, no, ns = len(in_specs), len(out_specs), len(scratch_shapes)
    ci, co = len(comm.srcs), len(comm.out_shape)

    def carried(*refs):
        c_in = refs[ni:ni + ci]
        c_out = refs[ni + ci + no:ni + ci + no + co]
        sems = refs[ni + ci + no + co + ns:]
        ids = [pl.program_id(d) for d in range(len(grid))]
        first = functools.reduce(jnp.logical_and, [i == 0 for i in ids])
        last = functools.reduce(jnp.logical_and, [i == g - 1 for i, g in zip(ids, grid)])

        @pl.when(first)
        def _():
            comm.start(c_in, c_out, *sems)

        body(*refs[:ni], *refs[ni + ci:ni + ci + no], *refs[ni + ci + no + co:ni + ci + no + co + ns])

        @pl.when(last)
        def _():
            comm.wait(c_in, c_out, *sems)

    res = pl.pallas_call(carried, grid=grid, in_specs=in_specs + [ANY] * ci, out_specs=out_specs + [ANY] * co,
                         out_shape=out_shape + list(comm.out_shape), scratch_shapes=scratch_shapes + list(comm.sems),
                         name=name, compiler_params=_cp(vmem))(*args, *comm.srcs)
    return list(res[:no]), list(res[no:])


def _in_proj(x, g1, w_in, comm=None):
    T, D = x.shape
    N = w_in.shape[1]
    tm, tn = 512, 1024

    def body(x_ref, g_ref, w_ref, p_ref, h_ref):
        h, _ = _rms(x_ref[...], g_ref[...])
        h_ref[...] = h.astype(BF)
        for j in range(N // tn):
            cols = pl.ds(j * tn, tn)
            p_ref[:, cols] = jnp.dot(h_ref[...], w_ref[:, cols], preferred_element_type=F32)

    return _pcall(
        body, (x, g1, w_in), grid=(T // tm,),
        in_specs=[pl.BlockSpec((tm, D), lambda i: (i, 0)), _const((1, D)), _const((D, N))],
        out_specs=[pl.BlockSpec((tm, N), lambda i: (i, 0)), pl.BlockSpec((tm, D), lambda i: (i, 0))],
        out_shape=[_sds((T, N), F32), _sds((T, D), BF)],
        name="in_proj", vmem=56, comm=comm)


CONV_PAD = 32
CONV_ROWS = 128
CONV_ROWS_BWD = 64


def _conv_fwd(p, conv_w, conv_b, B, S, comm=None):
    K, D = conv_w.shape
    nc = D // LANES

    def body(av_ref, ag_ref, w_ref, b_ref, c_ref, apad):
        apad[pl.ds(0, CONV_PAD), :] = jnp.zeros((CONV_PAD, LANES), F32)
        apad[pl.ds(CONV_PAD, S), :] = av_ref[...] * _sig(ag_ref[...])
        for r0 in range(0, S, CONV_ROWS):
            acc = jnp.zeros((CONV_ROWS, LANES), F32) + b_ref[...]
            for k in range(K):
                acc = acc + w_ref[pl.ds(k, 1), :] * apad[pl.ds(r0 + k + CONV_PAD - (K - 1), CONV_ROWS), :]
            c_ref[pl.ds(r0, CONV_ROWS), :] = acc

    return _pcall(
        body, (p, p, conv_w, conv_b), grid=(B, nc),
        in_specs=[pl.BlockSpec((S, LANES), lambda b, j: (b, j)), pl.BlockSpec((S, LANES), lambda b, j: (b, nc + j)),
                  pl.BlockSpec((K, LANES), lambda b, j: (0, j)), pl.BlockSpec((1, LANES), lambda b, j: (0, j))],
        out_specs=[pl.BlockSpec((S, LANES), lambda b, j: (b, j))],
        out_shape=[_sds((B * S, D), F32)],
        scratch_shapes=[pltpu.VMEM((S + CONV_PAD, LANES), F32)],
        name="conv_fwd", vmem=32, comm=comm)


def _tril_mask():
    t = lax.broadcasted_iota(jnp.int32, (LANES, LANES), 0)
    s = lax.broadcasted_iota(jnp.int32, (LANES, LANES), 1)
    return t >= s


def _branch_a(c, g, b):
    ln_a, xh, rstd = _ln(c, g, b)
    s = _sig(ln_a)
    return ln_a * s, ln_a, s, xh, rstd


def _branch_b(bu, bv, g, b, wm_ref, bz_ref, z_scr, v_scr):
    tm, D = bu.shape
    u, tu = _gelu(bu)
    gv, tv = _gelu(bv)
    v, vh, rstd = _ln(gv, g, b)
    v_scr[...] = v.astype(BF)
    mask = _tril_mask()
    for gi in range(SGU_GROUPS):
        wm = jnp.where(mask, wm_ref[gi], 0.0).astype(BF)
        cols = pl.ds(gi * LANES, LANES)
        for n in range(tm // LANES):
            rows = pl.ds(n * LANES, LANES)
            z_scr[rows, cols] = jnp.dot(wm, v_scr[rows, cols], preferred_element_type=F32) + bz_ref[:, cols]
    z = z_scr[...]
    return u * z, u, tu, z, tv, vh, rstd


TM3 = 256
TM3_FWD = 512


def _branch_fwd(c, p, prm, tm=TM3_FWD, comm=None):
    T, D = c.shape

    def body(c_ref, bu_ref, bv_ref, ga_ref, gb_ref, wco_ref, wso_ref, lag_ref, lab_ref, lsg_ref, lsb_ref, wm_ref,
             bz_ref, bg_ref, mg_ref, sa_ref, sg_ref, z_scr, v_scr):
        s_a = _branch_a(c_ref[...], lag_ref[...], lab_ref[...])[0]
        sa_ref[...] = s_a.astype(BF)
        y_a = jnp.dot(sa_ref[...], wco_ref[...], preferred_element_type=F32)
        sg = _branch_b(bu_ref[...], bv_ref[...], lsg_ref[...], lsb_ref[...], wm_ref, bz_ref, z_scr, v_scr)[0]
        sg_ref[...] = sg.astype(BF)
        y_b = jnp.dot(sg_ref[...], wso_ref[...], preferred_element_type=F32)
        ga = _sig(ga_ref[...] + bg_ref[pl.ds(0, 1), :])
        gb = _sig(gb_ref[...] + bg_ref[pl.ds(1, 1), :])
        mg_ref[...] = (ga * y_a + gb * y_b).astype(BF)

    tile = lambda j: pl.BlockSpec((tm, D), lambda i: (i, j))
    return _pcall(
        body, (c, p, p, p, p, prm["w_co"], prm["w_so"], prm["la_g"], prm["la_b"], prm["ls_g"], prm["ls_b"],
               prm["sgu_w"], prm["bz"], prm["b_gate"]),
        grid=(T // tm,),
        in_specs=[tile(0), tile(2), tile(3), tile(4), tile(5), _const((D, D)), _const((D, D)),
                  _const((1, D)), _const((1, D)), _const((1, D)), _const((1, D)),
                  _const((SGU_GROUPS, LANES, LANES)), _const((LANES, D)), _const((2, D))],
        out_specs=[tile(0), tile(0), tile(0)],
        out_shape=[_sds((T, D), BF)] * 3,
        scratch_shapes=[pltpu.VMEM((tm, D), F32), pltpu.VMEM((tm, D), BF)],
        name="branch_fwd", vmem=48, comm=comm)


def _kv_fwd(mem, gm, w_kv, B, M):
    D = mem.shape[1]
    N = w_kv.shape[1]

    def body(m_ref, g_ref, w_ref, mn_ref, kv_ref):
        h, _ = _rms(m_ref[...], g_ref[...])
        mn_ref[...] = h.astype(BF)
        kv_ref[...] = jnp.dot(mn_ref[...], w_ref[...], preferred_element_type=F32).astype(BF)

    return pl.pallas_call(
        body, grid=(B,),
        in_specs=[pl.BlockSpec((M, D), lambda b: (b, 0)), _const((1, D)), _const((D, N))],
        out_specs=[pl.BlockSpec((M, D), lambda b: (b, 0)), pl.BlockSpec((M, N), lambda b: (b, 0))],
        out_shape=[_sds((B * M, D), BF), _sds((B * M, N), BF)],
        name="kv_fwd", compiler_params=_cp(32))(mem, gm, w_kv)


def _softmax_rows(s):
    e = jnp.exp(s - jnp.max(s, axis=-1, keepdims=True))
    return e * (1.0 / jnp.sum(e, axis=-1, keepdims=True))


TM4 = 512


def _attn_fwd(x, merged, kv, prm, S, M, tm=TM4):
    T, D = x.shape
    hd = D // HEADS
    scale = hd ** -0.5
    tpb = S // tm

    def body(x_ref, mg_ref, kv_ref, wmo_ref, wq_ref, wxo_ref, g_ref, x1_ref, x2_ref, h2_ref, o_ref):
        x1 = x_ref[...] + jnp.dot(mg_ref[...], wmo_ref[...], preferred_element_type=F32)
        x1_ref[...] = x1
        h2, _ = _rms(x1, g_ref[...])
        h2_ref[...] = h2.astype(BF)
        qb = jnp.dot(h2_ref[...], wq_ref[...], preferred_element_type=F32).astype(BF)
        for h in range(HEADS):
            cs = pl.ds(h * hd, hd)
            s = _dot_nt(qb[:, h * hd:(h + 1) * hd], kv_ref[:, cs]) * scale
            pr = _softmax_rows(s)
            o_ref[:, cs] = _dot(pr, kv_ref[:, pl.ds(D + h * hd, hd)]).astype(BF)
        x2_ref[...] = x1 + jnp.dot(o_ref[...], wxo_ref[...], preferred_element_type=F32)

    tile = pl.BlockSpec((tm, D), lambda i: (i, 0))
    return pl.pallas_call(
        body, grid=(T // tm,),
        in_specs=[tile, tile, pl.BlockSpec((M, 2 * D), lambda i: (i // tpb, 0)),
                  _const((D, D)), _const((D, D)), _const((D, D)), _const((1, D))],
        out_specs=[tile, tile, tile, tile],
        out_shape=[_sds((T, D), F32), _sds((T, D), F32), _sds((T, D), BF), _sds((T, D), BF)],
        name="attn_fwd", compiler_params=_cp(40))(x, merged, kv, prm["w_mo"], prm["w_q"], prm["w_xo"], prm["g2"])


TM5 = 256
FFN_CHUNKS = 1


def _ffn_loss(x2, tgt, prm, tm=TM5):
    T, D = x2.shape
    F = prm["w_down"].shape[0]
    FC = F // FFN_CHUNKS

    def body(x2_ref, t_ref, wgu_ref, wd_ref, g3_ref, gf_ref, dx2_ref, dx3_ref, dgu_ref, h3_ref, f_ref, ls_ref,
             dg3_ref, dgf_ref, gu_scr):
        first = pl.program_id(0) == 0
        x2 = x2_ref[...]
        h3, r3 = _rms(x2, g3_ref[...])
        h3_ref[...] = h3.astype(BF)
        x3 = x2
        for ch in range(FFN_CHUNKS):
            gc, uc = pl.ds(ch * FC, FC), pl.ds(F + ch * FC, FC)
            gt = jnp.dot(h3_ref[...], wgu_ref[:, gc], preferred_element_type=F32)
            up = jnp.dot(h3_ref[...], wgu_ref[:, uc], preferred_element_type=F32)
            gu_scr[:, gc] = gt
            gu_scr[:, uc] = up
            f_ref[:, gc] = (gt * _sig(gt) * up).astype(BF)
            x3 = x3 + jnp.dot(f_ref[:, gc], wd_ref[gc, :], preferred_element_type=F32)
        y, rf = _rms(x3, gf_ref[...])
        e = y - t_ref[...]
        _acc(ls_ref, jnp.sum(e * e, axis=0, keepdims=True), first)
        dx3, dgf = _rms_bwd(x3, gf_ref[...], rf, e * (1.0 / D))
        _acc(dgf_ref, dgf, first)
        dx3_ref[...] = dx3.astype(BF)
        dh3 = jnp.zeros((tm, D), F32)
        for ch in range(FFN_CHUNKS):
            gc, uc = pl.ds(ch * FC, FC), pl.ds(F + ch * FC, FC)
            df = lax.dot_general(dx3_ref[...], wd_ref[gc, :], (((1,), (1,)), ((), ())), preferred_element_type=F32)
            gt, up = gu_scr[:, gc], gu_scr[:, uc]
            s = _sig(gt)
            dgu_ref[:, gc] = (df * up * _dsilu(gt, s)).astype(BF)
            dgu_ref[:, uc] = (df * gt * s).astype(BF)
            dh3 = dh3 + lax.dot_general(dgu_ref[:, gc], wgu_ref[:, gc], (((1,), (1,)), ((), ())), preferred_element_type=F32)
            dh3 = dh3 + lax.dot_general(dgu_ref[:, uc], wgu_ref[:, uc], (((1,), (1,)), ((), ())), preferred_element_type=F32)
        dxa, dg3 = _rms_bwd(x2, g3_ref[...], r3, dh3)
        _acc(dg3_ref, dg3, first)
        dx2_ref[...] = dx3 + dxa

    tile = lambda n: pl.BlockSpec((tm, n), lambda i: (i, 0))
    vec = pl.BlockSpec((1, D), lambda i: (0, 0))
    return pl.pallas_call(
        body, grid=(T // tm,),
        in_specs=[tile(D), tile(D), _const((D, 2 * F)), _const((F, D)), _const((1, D)), _const((1, D))],
        out_specs=[tile(D), tile(D), tile(2 * F), tile(D), tile(F), vec, vec, vec],
        out_shape=[_sds((T, D), F32), _sds((T, D), BF), _sds((T, 2 * F), BF), _sds((T, D), BF), _sds((T, F), BF),
                   _sds((1, D), F32), _sds((1, D), F32), _sds((1, D), F32)],
        scratch_shapes=[pltpu.VMEM((tm, 2 * F), F32)],
        name="ffn_loss", compiler_params=_cp(56))(x2, tgt, prm["w_gu"], prm["w_down"], prm["g3"], prm["gf"])


def _attn_bwd(x1, kv, dx2, prm, S, M, tm=TM4, comm=None):
    T, D = x1.shape
    hd = D // HEADS
    scale = hd ** -0.5
    tpb = S // tm

    def body(x1_ref, kv_ref, dx2_ref, wmo_ref, wq_ref, wxo_ref, g_ref, dx1_ref, dmg_ref, dq_ref, dkv_ref, dg_ref,
             h2_scr, do_scr):
        i = pl.program_id(0)
        x1 = x1_ref[...]
        dx2 = dx2_ref[...]
        h2, r2 = _rms(x1, g_ref[...])
        h2_scr[...] = h2.astype(BF)
        qb = jnp.dot(h2_scr[...], wq_ref[...], preferred_element_type=F32).astype(BF)
        do_scr[...] = _dot_nt(dx2, wxo_ref[...]).astype(BF)
        for h in range(HEADS):
            cs, vs = pl.ds(h * hd, hd), pl.ds(D + h * hd, hd)
            qh = qb[:, h * hd:(h + 1) * hd]
            pr = _softmax_rows(_dot_nt(qh, kv_ref[:, cs]) * scale)
            dpr = _dot_nt(do_scr[:, cs], kv_ref[:, vs])
            dv = _dot_tn(pr, do_scr[:, cs])
            ds = (pr * (dpr - jnp.sum(dpr * pr, axis=-1, keepdims=True)) * scale).astype(BF)
            dq_ref[:, cs] = jnp.dot(ds, kv_ref[:, cs], preferred_element_type=F32).astype(BF)
            dk = _dot_tn(ds, qh)

            @pl.when(i % tpb == 0)
            def _():
                dkv_ref[:, cs] = dk
                dkv_ref[:, vs] = dv

            @pl.when(i % tpb != 0)
            def _():
                dkv_ref[:, cs] += dk
                dkv_ref[:, vs] += dv

        dh2 = _dot_nt(dq_ref[...], wq_ref[...])
        dxa, dg = _rms_bwd(x1, g_ref[...], r2, dh2)
        _acc(dg_ref, dg, i == 0)
        dx1 = dx2 + dxa
        dx1_ref[...] = dx1
        dmg_ref[...] = _dot_nt(dx1, wmo_ref[...])

    tile = pl.BlockSpec((tm, D), lambda i: (i, 0))
    kvb = pl.BlockSpec((M, 2 * D), lambda i: (i // tpb, 0))
    B = T // S
    return _pcall(
        body, (x1, kv, dx2, prm["w_mo"], prm["w_q"], prm["w_xo"], prm["g2"]), grid=(T // tm,),
        in_specs=[tile, kvb, tile, _const((D, D)), _const((D, D)), _const((D, D)), _const((1, D))],
        out_specs=[tile, tile, tile, kvb, pl.BlockSpec((1, D), lambda i: (0, 0))],
        out_shape=[_sds((T, D), F32), _sds((T, D), F32), _sds((T, D), BF), _sds((B * M, 2 * D), F32), _sds((1, D), F32)],
        scratch_shapes=[pltpu.VMEM((tm, D), BF), pltpu.VMEM((tm, D), BF)],
        name="attn_bwd", vmem=48, comm=comm)


def _kv_bwd(mem, gm, w_kv, dkv, B, M):
    D = mem.shape[1]
    N = w_kv.shape[1]

    def body(m_ref, g_ref, w_ref, dkv_ref, dg_ref):
        mem_t = m_ref[...]
        _, r = _rms(mem_t, g_ref[...])
        dmn = _dot_nt(dkv_ref[...], w_ref[...])
        _acc(dg_ref, jnp.sum(dmn * (mem_t * r), axis=0, keepdims=True), pl.program_id(0) == 0)

    return pl.pallas_call(
        body, grid=(B,),
        in_specs=[pl.BlockSpec((M, D), lambda b: (b, 0)), _const((1, D)), _const((D, N)),
                  pl.BlockSpec((M, N), lambda b: (b, 0))],
        out_specs=pl.BlockSpec((1, D), lambda b: (0, 0)),
        out_shape=_sds((1, D), F32),
        name="kv_bwd", compiler_params=_cp(32))(mem, gm, w_kv, dkv)


def _branch_bwd(c, p, dmerged, prm, tm=TM3, comm=None):
    T, D = c.shape

    def body(c_ref, bu_ref, bv_ref, ga_ref, gb_ref, dm_ref, wco_ref, wso_ref, lag_ref, lab_ref, lsg_ref, lsb_ref,
             wm_ref, bz_ref, bg_ref,
             dc_ref, dpb_ref, dya_ref, dyb_ref, dwm_ref, dbz_ref, dlag_ref, dlab_ref, dlsg_ref, dlsb_ref, dbg_ref,
             z_scr, v_scr, sa_scr, sg_scr, dv_scr):
        first = pl.program_id(0) == 0
        s_a, ln_a, sig_a, xh_a, rstd_a = _branch_a(c_ref[...], lag_ref[...], lab_ref[...])
        sa_scr[...] = s_a.astype(BF)
        y_a = jnp.dot(sa_scr[...], wco_ref[...], preferred_element_type=F32)
        bu, bv = bu_ref[...], bv_ref[...]
        sg, u, tu, z, tv, vh, rstd_v = _branch_b(bu, bv, lsg_ref[...], lsb_ref[...], wm_ref, bz_ref, z_scr, v_scr)
        sg_scr[...] = sg.astype(BF)
        y_b = jnp.dot(sg_scr[...], wso_ref[...], preferred_element_type=F32)
        ga = _sig(ga_ref[...] + bg_ref[pl.ds(0, 1), :])
        gb = _sig(gb_ref[...] + bg_ref[pl.ds(1, 1), :])
        dm = dm_ref[...]
        dga = dm * y_a * ga * (1.0 - ga)
        dgb = dm * y_b * gb * (1.0 - gb)
        dpb_ref[:, pl.ds(2 * D, D)] = dga.astype(BF)
        dpb_ref[:, pl.ds(3 * D, D)] = dgb.astype(BF)
        _acc(dbg_ref.at[pl.ds(0, 1), :], jnp.sum(dga, axis=0, keepdims=True), first)
        _acc(dbg_ref.at[pl.ds(1, 1), :], jnp.sum(dgb, axis=0, keepdims=True), first)
        dya_ref[...] = (dm * ga).astype(BF)
        dyb_ref[...] = (dm * gb).astype(BF)
        dln = _dot_nt(dya_ref[...], wco_ref[...]) * _dsilu(ln_a, sig_a)
        dc, dlag, dlab = _ln_bwd(xh_a, rstd_a, lag_ref[...], dln)
        dc_ref[...] = dc
        _acc(dlag_ref, dlag, first)
        _acc(dlab_ref, dlab, first)
        dsg = _dot_nt(dyb_ref[...], wso_ref[...])
        dpb_ref[:, pl.ds(0, D)] = (dsg * z * _dgelu(bu, tu)).astype(BF)
        dz = dsg * u
        z_scr[...] = dz
        mask = _tril_mask()

        @pl.when(first)
        def _():
            dwm_ref[...] = jnp.zeros_like(dwm_ref)
            dbz_ref[...] = jnp.zeros_like(dbz_ref)

        for gi in range(SGU_GROUPS):
            wm = jnp.where(mask, wm_ref[gi], 0.0).astype(BF)
            cols = pl.ds(gi * LANES, LANES)
            for n in range(tm // LANES):
                rows = pl.ds(n * LANES, LANES)
                dzb = z_scr[rows, cols].astype(BF)
                dv_scr[rows, cols] = lax.dot_general(wm, dzb, (((0,), (0,)), ((), ())), preferred_element_type=F32)
                dw = lax.dot_general(dzb, v_scr[rows, cols], (((1,), (1,)), ((), ())), preferred_element_type=F32)
                dwm_ref[gi] += jnp.where(mask, dw, 0.0)
                dbz_ref[:, cols] += z_scr[rows, cols]
        dgv, dlsg, dlsb = _ln_bwd(vh, rstd_v, lsg_ref[...], dv_scr[...])
        _acc(dlsg_ref, dlsg, first)
        _acc(dlsb_ref, dlsb, first)
        dpb_ref[:, pl.ds(D, D)] = (dgv * _dgelu(bv, tv)).astype(BF)

    tile = lambda j: pl.BlockSpec((tm, D), lambda i: (i, j))
    vec = pl.BlockSpec((1, D), lambda i: (0, 0))
    return _pcall(
        body, (c, p, p, p, p, dmerged, prm["w_co"], prm["w_so"], prm["la_g"], prm["la_b"], prm["ls_g"], prm["ls_b"],
               prm["sgu_w"], prm["bz"], prm["b_gate"]),
        grid=(T // tm,),
        in_specs=[tile(0), tile(2), tile(3), tile(4), tile(5), tile(0), _const((D, D)), _const((D, D)),
                  _const((1, D)), _const((1, D)), _const((1, D)), _const((1, D)),
                  _const((SGU_GROUPS, LANES, LANES)), _const((LANES, D)), _const((2, D))],
        out_specs=[tile(0), pl.BlockSpec((tm, 4 * D), lambda i: (i, 0)), tile(0), tile(0),
                   pl.BlockSpec((SGU_GROUPS, LANES, LANES), lambda i: (0, 0, 0)),
                   pl.BlockSpec((LANES, D), lambda i: (0, 0)), vec, vec, vec, vec,
                   pl.BlockSpec((2, D), lambda i: (0, 0))],
        out_shape=[_sds((T, D), F32), _sds((T, 4 * D), BF), _sds((T, D), BF), _sds((T, D), BF),
                   _sds((SGU_GROUPS, LANES, LANES), F32), _sds((LANES, D), F32),
                   _sds((1, D), F32), _sds((1, D), F32), _sds((1, D), F32), _sds((1, D), F32), _sds((2, D), F32)],
        scratch_shapes=[pltpu.VMEM((tm, D), F32), pltpu.VMEM((tm, D), BF), pltpu.VMEM((tm, D), BF),
                        pltpu.VMEM((tm, D), BF), pltpu.VMEM((tm, D), F32)],
        name="branch_bwd", vmem=56, comm=comm)


def _conv_bwd(p, dc, conv_w, B, S, comm=None):
    K, D = conv_w.shape
    nc = D // LANES

    R = CONV_ROWS_BWD

    def body(av_ref, ag_ref, dc_ref, w_ref, dav_ref, dag_ref, dw_ref, db_ref, apad, dpad):
        b = pl.program_id(1)
        av = av_ref[...]
        sg = _sig(ag_ref[...])
        apad[pl.ds(0, CONV_PAD), :] = jnp.zeros((CONV_PAD, LANES), F32)
        apad[pl.ds(CONV_PAD, S), :] = av * sg
        dpad[pl.ds(S, CONV_PAD), :] = jnp.zeros((CONV_PAD, LANES), F32)
        dpad[pl.ds(0, S), :] = dc_ref[...]

        @pl.when(b == 0)
        def _():
            dw_ref[...] = jnp.zeros_like(dw_ref)
            db_ref[...] = jnp.zeros_like(db_ref)

        db_ref[...] += jnp.sum(dc_ref[...], axis=0, keepdims=True)
        for k in range(K):
            tot = jnp.zeros((1, LANES), F32)
            for r0 in range(0, S, R):
                tot = tot + jnp.sum(dpad[pl.ds(r0, R), :] * apad[pl.ds(r0 + k + CONV_PAD - (K - 1), R), :],
                                    axis=0, keepdims=True)
            dw_ref[pl.ds(k, 1), :] += tot
        for r0 in range(0, S, R):
            da = jnp.zeros((R, LANES), F32)
            for k in range(K):
                da = da + w_ref[pl.ds(k, 1), :] * dpad[pl.ds(r0 + (K - 1) - k, R), :]
            rows = pl.ds(r0, R)
            s = sg[r0:r0 + R, :]
            a_v = av[r0:r0 + R, :]
            dav_ref[rows, :] = (da * s).astype(BF)
            dag_ref[rows, :] = (da * a_v * s * (1.0 - s)).astype(BF)

    blk = lambda off: pl.BlockSpec((S, LANES), lambda j, b: (b, off + j))
    return _pcall(
        body, (p, p, dc, conv_w), grid=(nc, B),
        in_specs=[blk(0), blk(nc), blk(0), pl.BlockSpec((K, LANES), lambda j, b: (0, j))],
        out_specs=[blk(0), blk(0), pl.BlockSpec((K, LANES), lambda j, b: (0, j)), pl.BlockSpec((1, LANES), lambda j, b: (0, j))],
        out_shape=[_sds((B * S, D), BF), _sds((B * S, D), BF), _sds((K, D), F32), _sds((1, D), F32)],
        scratch_shapes=[pltpu.VMEM((S + CONV_PAD, LANES), F32), pltpu.VMEM((S + CONV_PAD, LANES), F32)],
        name="conv_bwd", vmem=32, comm=comm)


TM1 = 512


def _in_proj_bwd(x, dx1, dps, g1, w_in, tm=TM1, comm=None):
    T, D = x.shape
    N = w_in.shape[1]
    widths = [d.shape[1] for d in dps]

    def body(x_ref, dx1_ref, *refs):
        dp_refs, (g_ref, w_ref, dx_ref, dg_ref) = refs[:len(dps)], refs[len(dps):]
        x_t = x_ref[...]
        _, r = _rms(x_t, g_ref[...])
        dh = jnp.zeros((tm, D), F32)
        for q, dp_ref in enumerate(dp_refs):
            cols = pl.ds(sum(widths[:q]), widths[q])
            dh = dh + lax.dot_general(dp_ref[...], w_ref[:, cols], (((1,), (1,)), ((), ())), preferred_element_type=F32)
        dxa, dg = _rms_bwd(x_t, g_ref[...], r, dh)
        dx_ref[...] = dx1_ref[...] + dxa
        _acc(dg_ref, dg, pl.program_id(0) == 0)

    tile = pl.BlockSpec((tm, D), lambda i: (i, 0))
    return _pcall(
        body, (x, dx1, *dps, g1, w_in), grid=(T // tm,),
        in_specs=[tile, tile] + [pl.BlockSpec((tm, w), lambda i: (i, 0)) for w in widths] + [_const((1, D)), _const((D, N))],
        out_specs=[tile, pl.BlockSpec((1, D), lambda i: (0, 0))],
        out_shape=[_sds((T, D), F32), _sds((1, D), F32)],
        name="in_proj_bwd", vmem=48, comm=comm)


def _pick(n, cands):
    for c in cands:
        if n % c == 0:
            return c
    raise ValueError(f"no tile of {cands} divides {n}")


def _mm_tn(x, dys, name, comm=None):
    T, K = x.shape
    dys = list(dys) if isinstance(dys, (list, tuple)) else [dys]
    widths = [d.shape[1] for d in dys]
    N = sum(widths)
    tm = _pick(T, (2048, 1024, 512, 256))
    tk = _pick(K, (1024, 1408, 512))
    tn = _pick(math.gcd(*widths), (1024, 1408, 512))
    nt = T // tm
    first = [sum(widths[:q]) // tn for q in range(len(dys))]
    count = [w // tn for w in widths]

    def body(x_ref, *refs):
        dy_refs, (o_ref, ob_ref, acc) = refs[:len(dys)], refs[len(dys):]
        j, t = pl.program_id(1), pl.program_id(2)

        @pl.when(t == 0)
        def _():
            acc[...] = jnp.zeros_like(acc)

        for q, dy_ref in enumerate(dy_refs):
            @pl.when(jnp.logical_and(j >= first[q], j < first[q] + count[q]))
            def _():
                acc[...] += _dot_tn(x_ref[...], dy_ref[...])

        @pl.when(t == nt - 1)
        def _():
            o_ref[...] = acc[...]
            ob_ref[...] = acc[...].astype(BF)

    def dy_spec(q):
        def index(i, j, t):
            mine = jnp.logical_and(j >= first[q], j < first[q] + count[q])
            return jnp.where(mine, t, 0), jnp.clip(j - first[q], 0, count[q] - 1)
        return pl.BlockSpec((tm, tn), index)

    res = _pcall(
        body, (x, *dys), grid=(K // tk, N // tn, nt),
        in_specs=[pl.BlockSpec((tm, tk), lambda i, j, t: (t, i))] + [dy_spec(q) for q in range(len(dys))],
        out_specs=[pl.BlockSpec((tk, tn), lambda i, j, t: (i, j)), pl.BlockSpec((tk, tn), lambda i, j, t: (i, j))],
        out_shape=[_sds((K, N), F32), _sds((K, N), BF)],
        scratch_shapes=[pltpu.VMEM((tk, tn), F32)],
        name=name, vmem=60, comm=comm)
    return res[0] if comm is None else res


def _row_tile(R):
    return _pick(R, (128, 64, 32, 16, 8)) if R % 8 == 0 else R


def _sum_landed(full, land, kind, chip, name):
    _, R, C = land.shape
    tr = _row_tile(R)
    nb = R // tr

    def body(k_ref, o_ref, l_ref, s_ref):
        s_ref[...] = ((o_ref[...] + l_ref[0].astype(F32)) + l_ref[1].astype(F32)) + l_ref[2].astype(F32)

    own = (pl.BlockSpec((tr, C), lambda i, k: (k[0] * nb + i, 0)) if kind == "row"
           else pl.BlockSpec((tr, C), lambda i, k: (i, k[0])))
    return pl.pallas_call(
        body,
        grid_spec=pltpu.PrefetchScalarGridSpec(
            num_scalar_prefetch=1, grid=(nb,),
            in_specs=[own, pl.BlockSpec((3, tr, C), lambda i, k: (0, i, 0))],
            out_specs=pl.BlockSpec((tr, C), lambda i, k: (i, 0))),
        out_shape=_sds((R, C), F32), name=name, compiler_params=_cp(32))(chip, full, land)


def _adamw(g, w, m, v):
    m = ADAM_B1 * m + (1.0 - ADAM_B1) * g
    v = ADAM_B2 * v + (1.0 - ADAM_B2) * (g * g)
    m_hat = m / (1.0 - ADAM_B1 ** ADAM_STEP)
    v_hat = v / (1.0 - ADAM_B2 ** ADAM_STEP)
    return -ADAM_LR * (m_hat / (jnp.sqrt(v_hat) + ADAM_EPS) + ADAM_WD * w), m, v


def _update(parts, w, m, v, name):
    R, C = w.shape
    tr = _row_tile(R)
    k = len(parts)

    def body(*refs):
        g = refs[0][...]
        for r in refs[1:k]:
            g = g + r[...]
        w_ref, m_ref, v_ref, g_out, d_out, m_out, v_out = refs[k:]
        d, m_new, v_new = _adamw(g, w_ref[...], m_ref[...], v_ref[...])
        g_out[...] = g
        d_out[...] = d
        m_out[...] = m_new
        v_out[...] = v_new

    blk = pl.BlockSpec((tr, C), lambda i: (i, 0))
    return pl.pallas_call(
        body, grid=(R // tr,), in_specs=[blk] * (k + 3), out_specs=[blk] * 4, out_shape=[_sds((R, C), F32)] * 4,
        name=name, compiler_params=_cp(40))(*parts, w, m, v)


VECS = ("norm_mix", "conv_b", "conv_ln_g", "conv_ln_b", "sgu_ln_g", "sgu_ln_b", "norm_xattn", "norm_mem", "norm_ffn",
        "norm_final")
SMALL = VECS + ("b_gate", "conv_w", "sgu_w", "sgu_b")
SUBLANES = 8


def _pad_rows(arr):
    return jnp.pad(arr, ((0, (-arr.shape[0]) % SUBLANES), (0, 0)))


def _update_small(land_vec, land_w, land_g1, chip, a):
    _, R, D = land_vec.shape
    Dq, G, K, nv = D // N_CHIPS, SGU_GROUPS, a["conv_w"].shape[1], len(VECS)

    def as2d(nm, arr):
        if nm in VECS:
            return arr.reshape(1, D)
        if nm == "sgu_w":
            return arr.reshape(G * LANES, LANES)
        return arr.reshape(G, LANES) if nm == "sgu_b" else arr[0]

    params = [as2d(nm, a[pre + nm]) for nm in SMALL for pre in ("", "m_", "v_")]

    def body(k_ref, lv, lvc, lw, l1, *refs):
        prm = refs[:3 * len(SMALL)]
        outs = refs[3 * len(SMALL):7 * len(SMALL)]
        tv, tvc, tw, t1 = refs[7 * len(SMALL):]
        for land, tot in ((lv, tv), (lvc, tvc), (lw, tw), (l1, t1)):
            acc = land[0]
            for dev in range(1, N_DEV):
                acc = acc + land[dev]
            tot[...] = acc
        at = SUBLANES * nv
        grads = [t1[pl.ds(0, 1), :] if nm == "norm_mix" else tv[pl.ds(SUBLANES * i, 1), :]
                 for i, nm in enumerate(VECS)]
        grads += [tvc[pl.ds(at, 2), :], tvc[pl.ds(at + SUBLANES, K), :],
                  tw[pl.ds(0, G * LANES), :], tw[pl.ds(G * LANES, G), :]]
        for i, g in enumerate(grads):
            d, m_new, v_new = _adamw(g, prm[3 * i][...], prm[3 * i + 1][...], prm[3 * i + 2][...])
            for o_ref, val in zip(outs[4 * i:4 * i + 4], (g, d, m_new, v_new)):
                o_ref[...] = val

    whole = lambda shape: pl.BlockSpec(tuple(shape), lambda i, k: (0,) * len(shape))
    res = pl.pallas_call(
        body,
        grid_spec=pltpu.PrefetchScalarGridSpec(
            num_scalar_prefetch=1, grid=(1,),
            in_specs=[whole(land_vec.shape), pl.BlockSpec((N_DEV, R, Dq), lambda i, k: (0, 0, k[0])),
                      whole(land_w.shape), whole(land_g1.shape)] + [whole(p.shape) for p in params],
            out_specs=[whole(params[3 * i].shape) for i in range(len(SMALL)) for _ in range(4)],
            scratch_shapes=[pltpu.VMEM((R, D), F32), pltpu.VMEM((R, Dq), F32), pltpu.VMEM(land_w.shape[1:], F32),
                            pltpu.VMEM(land_g1.shape[1:], F32)]),
        out_shape=[_sds(params[3 * i].shape, F32) for i in range(len(SMALL)) for _ in range(4)],
        name="upd_small", compiler_params=_cp(40))(chip, land_vec, land_vec, land_w, land_g1, *params)
    return {nm: list(res[4 * i:4 * i + 4]) for i, nm in enumerate(SMALL)}


BIG = ("w_in", "w_conv_out", "w_sgu_out", "w_mix_out", "w_q", "w_kv", "w_xo", "w_gu", "w_down")
BIG_KIND = {"w_in": "col", "w_conv_out": "row", "w_sgu_out": "row", "w_mix_out": "row", "w_q": "row",
            "w_kv": "col", "w_xo": "row", "w_gu": "col", "w_down": "row"}

def _step(a):
    x3d, mem3d, tgt3d = a["x"], a["mem"], a["loss_target"]
    B, S, D = x3d.shape
    M = mem3d.shape[1]
    T = B * S
    x = x3d.reshape(T, D)
    mem = mem3d.reshape(B * M, D)
    tgt = tgt3d.reshape(T, D)
    xi, yi = lax.axis_index("x"), lax.axis_index("y")
    chip = 2 * xi + yi

    later = [nm for nm in BIG if nm != "w_in"]
    cast = dict(zip(later, _cast_bf16([a[nm][0] for nm in later])), w_in=a["w_in"][0].astype(BF))

    def gather(names):
        return _Gather([a[nm][0] if nm in ("b_gate", "conv_w") else cast[nm] for nm in names],
                       [BIG_KIND.get(nm, "col") for nm in names])

    first = ("w_in", "b_gate", "conv_w")
    on_in_proj = ("w_conv_out", "w_sgu_out", "w_kv")
    on_conv = ("w_mix_out", "w_q", "w_xo", "w_down")
    full = dict(zip(first, _comm_call(gather(first), "gather_w_in")))
    (p, h1), got = _in_proj(x, a["norm_mix"], full["w_in"], comm=gather(on_in_proj))
    full.update(zip(on_in_proj, got))
    (c,), got = _conv_fwd(p, full["conv_w"], a["conv_b"], B, S, comm=gather(on_conv))
    full.update(zip(on_conv, got))

    sgu_b = a["sgu_b"][0]
    bz = jnp.repeat(jnp.transpose(sgu_b), LANES, axis=1)
    prm = dict(w_co=full["w_conv_out"], w_so=full["w_sgu_out"], w_mo=full["w_mix_out"], w_q=full["w_q"],
               w_xo=full["w_xo"], w_down=full["w_down"],
               la_g=a["conv_ln_g"], la_b=a["conv_ln_b"], ls_g=a["sgu_ln_g"], ls_b=a["sgu_ln_b"],
               sgu_w=a["sgu_w"][0], bz=bz, b_gate=full["b_gate"], g2=a["norm_xattn"], g3=a["norm_ffn"],
               gf=a["norm_final"].reshape(1, D))

    (merged, s_a, sg), got = _branch_fwd(c, p, prm, comm=gather(("w_gu",)))
    prm["w_gu"] = got[0]
    mem_n, kv = _kv_fwd(mem, a["norm_mem"], full["w_kv"], B, M)
    x1, x2, h2, o = _attn_fwd(x, merged, kv, prm, S, M)
    dx2, dx3, dgu, h3, f, lsum, d_g3, d_gf = _ffn_loss(x2, tgt, prm)
    loss = lax.psum(0.5 * jnp.sum(lsum) / D, ("x", "y", "c"))

    size_of = {nm: a[nm].shape[1] if BIG_KIND[nm] == "row" else a[nm].shape[2] for nm in BIG}
    landed = {}

    def scatter(names):
        return _Scatter([gw[nm][1] for nm in names], [BIG_KIND[nm] for nm in names], [size_of[nm] for nm in names])

    gw = {}
    gw["w_down"] = _mm_tn(f, dx3, "dw_down")
    gw["w_gu"] = _mm_tn(h3, dgu, "dw_gu")
    (dx1, dmerged, dq, dkv, d_g2), got = _attn_bwd(x1, kv, dx2, prm, S, M, comm=scatter(("w_gu",)))
    landed["w_gu"] = got[0]
    gw["w_xo"] = _mm_tn(o, dx2, "dw_xo")
    gw["w_q"] = _mm_tn(h2, dq, "dw_q")
    gw["w_kv"] = _mm_tn(mem_n, dkv, "dw_kv")
    d_gm = _kv_bwd(mem, a["norm_mem"], full["w_kv"], dkv, B, M)
    gw["w_mix_out"] = _mm_tn(merged, dx1, "dw_mix_out")
    group = ("w_down", "w_xo", "w_q", "w_kv", "w_mix_out")
    (dc, dpb, dya, dyb, d_wm, d_bz, d_lag, d_lab, d_lsg, d_lsb, d_bg), got = _branch_bwd(
        c, p, dmerged, prm, comm=scatter(group))
    landed.update(zip(group, got))
    gw["w_conv_out"] = _mm_tn(s_a, dya, "dw_conv_out")
    gw["w_sgu_out"] = _mm_tn(sg, dyb, "dw_sgu_out")
    group = ("w_conv_out", "w_sgu_out")
    (dav, dag, d_cw, d_cb), got = _conv_bwd(p, dc, full["conv_w"], B, S, comm=scatter(group))
    landed.update(zip(group, got))
    dp = [dav, dag, dpb]

    chip_arr = jnp.reshape(chip, (1,)).astype(jnp.int32)
    early = [nm for nm in BIG if nm != "w_in"]
    part = {nm: _sum_landed(gw[nm][0], landed[nm], BIG_KIND[nm], chip_arr, "sum_" + nm) for nm in early}
    G = SGU_GROUPS
    d_sb = jnp.transpose(d_bz.reshape(LANES, G, LANES).sum(axis=-1))
    vec_g = dict(norm_mix=jnp.zeros((1, D), F32), conv_b=d_cb, conv_ln_g=d_lag, conv_ln_b=d_lab, sgu_ln_g=d_lsg,
                 sgu_ln_b=d_lsb, norm_xattn=d_g2, norm_mem=d_gm, norm_ffn=d_g3, norm_final=d_gf)
    g_vec = jnp.concatenate([_pad_rows(vec_g[nm]) for nm in VECS] + [_pad_rows(d_bg), _pad_rows(d_cw)], axis=0)
    g_w = jnp.concatenate([d_wm.reshape(G * LANES, LANES), d_sb], axis=0)
    gw["w_in"], got = _mm_tn(h1, dp, "dw_in", comm=_Both(_Swap([part[nm] for nm in early]), _Spread([g_vec, g_w])))
    other = dict(zip(early, got[:len(early)]))
    land_vec, land_w = got[len(early):]
    kind_in, size_in = BIG_KIND["w_in"], size_of["w_in"]
    send, recv, g_thru, land_thru, token = _scatter_start(gw["w_in"][1], kind_in, size_in, "scatter_w_in_start")
    (grad_x, d_g1), _ = _in_proj_bwd(x, dx1, dp, a["norm_mix"] + token[:1, :1], full["w_in"])

    out = {}

    def update(nm):
        res = _update([part[nm], other[nm]], a[nm][0], a["m_" + nm][0], a["v_" + nm][0], "upd_" + nm)
        out[nm] = [r[None] for r in res]

    for nm in early:
        update(nm)
    after = [grad_x] + [out[nm][1] for nm in early]
    land_in = _scatter_wait(send, recv, g_thru, land_thru, after, kind_in, size_in, "scatter_w_in_wait")
    part["w_in"] = _sum_landed(gw["w_in"][0], land_in, kind_in, chip_arr, "sum_w_in")
    g1 = _pad_rows(d_g1)
    other["w_in"], land_g1 = _comm_call(_Both(_Swap([part["w_in"]]), _Spread([g1])), "swap_w_in")
    update("w_in")
    for nm, res in _update_small(land_vec, land_w, land_g1, chip_arr, a).items():
        out[nm] = [r.reshape(a[nm].shape) for r in res]
    return loss, grad_x.reshape(B, S, D), out


WEIGHTS = ("norm_mix", "w_in", "b_gate", "conv_w", "conv_b", "conv_ln_g", "conv_ln_b", "w_conv_out", "sgu_ln_g",
           "sgu_ln_b", "sgu_w", "sgu_b", "w_sgu_out", "w_mix_out", "norm_xattn", "norm_mem", "w_q", "w_kv", "w_xo",
           "norm_ffn", "w_gu", "w_down", "norm_final")


def kernel(x, mem, norm_mix, w_in, b_gate, conv_w, conv_b, conv_ln_g, conv_ln_b, w_conv_out, sgu_ln_g, sgu_ln_b, sgu_w, sgu_b, w_sgu_out, w_mix_out, norm_xattn, norm_mem, w_q, w_kv, w_xo, norm_ffn, w_gu, w_down, norm_final, loss_target, m_norm_mix, m_w_in, m_b_gate, m_conv_w, m_conv_b, m_conv_ln_g, m_conv_ln_b, m_w_conv_out, m_sgu_ln_g, m_sgu_ln_b, m_sgu_w, m_sgu_b, m_w_sgu_out, m_w_mix_out, m_norm_xattn, m_norm_mem, m_w_q, m_w_kv, m_w_xo, m_norm_ffn, m_w_gu, m_w_down, m_norm_final, v_norm_mix, v_w_in, v_b_gate, v_conv_w, v_conv_b, v_conv_ln_g, v_conv_ln_b, v_w_conv_out, v_sgu_ln_g, v_sgu_ln_b, v_sgu_w, v_sgu_b, v_w_sgu_out, v_w_mix_out, v_norm_xattn, v_norm_mem, v_w_q, v_w_kv, v_w_xo, v_norm_ffn, v_w_gu, v_w_down, v_norm_final):
    a = dict(locals())
    loss, grad_x, out = _step(a)
    res = [loss, grad_x]
    for q in range(4):
        res += [out[nm][q] for nm in WEIGHTS]
    return tuple(res)
```
